```python
import functools
import jax, jax.numpy as jnp
from jax import lax
import numpy as np

D_MODEL = 1024
BATCH = 4
SEQ = 4096
DEPTH = 1
DEC_BATCH = 32
DEC_SEQ = 1
PAST_LEN = 16384
PAGE_SIZE = 128

PLE_DIM = 256
D_FF = 2816
EPS = 1e-6
H_GLA = 4
DK_GLA = 64
DV_GLA = 128
GLA_RANK = 16
GLA_GATE_TEMP = 16.0
GLA_CHUNK = 64
H_NSA = 8
G_NSA = 2
R_NSA = H_NSA // G_NSA
D_NSA = 64
CMP_LEN = 32
CMP_STRIDE = 16
CMP_HIDDEN = 128
SEL_BLOCK = 64
N_SELECT = 16
WINDOW = 512
Q_BLOCK = 128
FORCE_SCORE = 1e4
NEG_INF = -1e30
ATTN_SCALE = D_NSA ** -0.5
ROPE_THETA = 500000.0
ROPE_DIM = D_NSA // 4
MIX_WIDTH = H_GLA * DV_GLA + H_NSA * D_NSA
IN_SPLITS = (H_GLA * DK_GLA, H_GLA * DK_GLA, H_GLA * DV_GLA, H_GLA * DV_GLA, GLA_RANK,
             H_NSA * D_NSA, 6 * G_NSA * D_NSA, 3 * H_NSA)
IN_COLS = sum(IN_SPLITS)

kernel_name = 'hymba_gla_nsa_macaron_decoder_step'


def rmsnorm(x, g):
    xf = x.astype(jnp.float32)
    y = xf * lax.rsqrt(jnp.mean(xf * xf, axis=-1, keepdims=True) + EPS)
    return (y * g.astype(jnp.float32)).astype(x.dtype)


def swiglu(x, w_in, w_out):
    g, u = jnp.split(x @ w_in, 2, axis=-1)
    return (jax.nn.silu(g) * u) @ w_out


def partial_rope(x, pos):
    half = ROPE_DIM // 2
    inv_freq = ROPE_THETA ** (-jnp.arange(half, dtype=jnp.float32) * 2.0 / ROPE_DIM)
    ang = pos.astype(jnp.float32)[:, None] * inv_freq[None, :]
    cos = jnp.cos(ang)[:, None, :]
    sin = jnp.sin(ang)[:, None, :]
    xf = x.astype(jnp.float32)
    x1, x2, rest = xf[..., :half], xf[..., half:ROPE_DIM], xf[..., ROPE_DIM:]
    out = jnp.concatenate([x1 * cos - x2 * sin, x2 * cos + x1 * sin, rest], axis=-1)
    return out.astype(x.dtype)


def masked_softmax(s, valid):
    return jax.nn.softmax(jnp.where(valid, s, NEG_INF), axis=-1) * valid


def split_cols(z):
    outs, off = [], 0
    for n in IN_SPLITS:
        outs.append(z[..., off:off + n])
        off += n
    return outs


def project_mixers(n, pos, w_in, w_a2, b_a):
    B, T, _ = n.shape
    q_g, k_g, v_g, r_g, a_lr, q_n, kv_n, gate_n = split_cols(n @ w_in)
    q_g = q_g.reshape(B, T, H_GLA, DK_GLA) * DK_GLA ** -0.5
    k_g = k_g.reshape(B, T, H_GLA, DK_GLA)
    v_g = v_g.reshape(B, T, H_GLA, DV_GLA)
    log_a = jax.nn.log_sigmoid((a_lr @ w_a2 + b_a).astype(jnp.float32)) / GLA_GATE_TEMP
    log_a = log_a.reshape(B, T, H_GLA, DK_GLA)
    q_n = q_n.reshape(B, T, H_NSA, D_NSA)
    kv = kv_n.reshape(B, T, 6, G_NSA, D_NSA)
    k_sel = partial_rope(kv[:, :, 2], pos)
    k_win = partial_rope(kv[:, :, 4], pos)
    nsa_rows = jnp.stack([kv[:, :, 0], kv[:, :, 1], k_sel, kv[:, :, 3]], axis=2)
    win_rows = jnp.stack([k_win, kv[:, :, 5]], axis=2)
    gates = jax.nn.sigmoid(gate_n.reshape(B, T, H_NSA, 3).astype(jnp.float32))
    return (q_g, k_g, v_g, log_a, r_g), (q_n, gates), nsa_rows, win_rows


def gla_scan(q, k, v, log_a, s0, chunk):
    B, T, H, _ = q.shape
    DV = v.shape[-1]
    nc = T // chunk

    def to_chunks(t):
        return t.astype(jnp.float32).reshape(B, nc, chunk, H, t.shape[-1]).transpose(1, 0, 3, 2, 4)

    tri = jnp.tril(jnp.ones((chunk, chunk), dtype=bool))

    def step(S, inp):
        qc, kc, vc, ac = inp
        b = jnp.cumsum(ac, axis=2)
        o_inter = jnp.einsum('bhtd,bhde->bhte', qc * jnp.exp(b), S)
        rel = jnp.where(tri[None, None, :, :, None], b[:, :, :, None, :] - b[:, :, None, :, :], -jnp.inf)
        attn = jnp.einsum('bhtd,bhtsd,bhsd->bhts', qc, jnp.exp(rel), kc)
        o_intra = jnp.einsum('bhts,bhse->bhte', attn, vc)
        b_last = b[:, :, -1:, :]
        S_new = jnp.exp(b_last[:, :, 0, :, None]) * S + jnp.einsum('bhsd,bhse->bhde', kc * jnp.exp(b_last - b), vc)
        return S_new, o_inter + o_intra

    S_fin, o = lax.scan(step, s0, (to_chunks(q), to_chunks(k), to_chunks(v), to_chunks(log_a)))
    o = o.transpose(1, 0, 3, 2, 4).reshape(B, T, H, DV)
    return o, S_fin


def gla_output(o, r, gain):
    B, T = o.shape[:2]
    o = rmsnorm(o, gain).reshape(B, T, H_GLA * DV_GLA)
    return (o * jax.nn.silu(r.astype(jnp.float32))).astype(r.dtype)


def compress(k, pos_emb, w1, w2):
    B, Tk = k.shape[:2]
    nc = (Tk - CMP_LEN) // CMP_STRIDE + 1
    idx = jnp.arange(nc)[:, None] * CMP_STRIDE + jnp.arange(CMP_LEN)[None, :]
    blocks = k[:, idx] + pos_emb[None, None, :, None, :]
    flat = blocks.transpose(0, 1, 3, 2, 4).reshape(B, nc, G_NSA, CMP_LEN * D_NSA)
    return jax.nn.gelu(flat @ w1) @ w2


def to_blocks(k):
    B, Tk = k.shape[:2]
    ns = -(-Tk // SEL_BLOCK)
    k = jnp.pad(k, ((0, 0), (0, ns * SEL_BLOCK - Tk), (0, 0), (0, 0)))
    return k.reshape(B, ns, SEL_BLOCK, G_NSA, D_NSA).transpose(0, 3, 1, 2, 4)


def overlap_matrix(nc, ns):
    start = jnp.arange(nc) * CMP_STRIDE
    sstart = jnp.arange(ns) * SEL_BLOCK
    m = (start[:, None] < sstart[None, :] + SEL_BLOCK) & (start[:, None] + CMP_LEN > sstart[None, :])
    return m.astype(jnp.float32)


def nsa_compressed(q, pos_q, kcmp, vcmp):
    s = jnp.einsum('bqgrd,bngd->bqgrn', q, kcmp).astype(jnp.float32) * ATTN_SCALE
    end = jnp.arange(kcmp.shape[1]) * CMP_STRIDE + CMP_LEN - 1
    valid = (end[None, :] <= pos_q[:, None])[None, :, None, None, :]
    p = masked_softmax(s, valid)
    return jnp.einsum('bqgrn,bngd->bqgrd', p, vcmp.astype(jnp.float32)), p


def nsa_selected(q_rot, pos_q, p_cmp, overlap, kb, vb):
    B, Tq = q_rot.shape[:2]
    ns = kb.shape[2]
    imp = jnp.einsum('bqgrn,ns->bqgs', p_cmp, overlap)
    blk = jnp.arange(ns, dtype=jnp.int32)[None, :]
    cur = (pos_q // SEL_BLOCK)[:, None]
    forced = (blk == 0) | (blk == cur) | (blk == cur - 1)
    causal = blk * SEL_BLOCK <= pos_q[:, None]
    imp = jnp.where(forced[None, :, None, :], FORCE_SCORE, jnp.where(causal[None, :, None, :], imp, -1.0))
    _, idx = lax.top_k(imp, min(N_SELECT, ns))
    kk = idx.shape[-1]
    idx_t = jnp.swapaxes(idx, 1, 2)
    bi = jnp.arange(B)[:, None, None, None]
    gi = jnp.arange(G_NSA)[None, :, None, None]
    kg = kb[bi, gi, idx_t]
    vg = vb[bi, gi, idx_t]
    s = jnp.einsum('bqgrd,bgqksd->bqgrks', q_rot, kg).astype(jnp.float32) * ATTN_SCALE
    kpos = idx[..., None] * SEL_BLOCK + jnp.arange(SEL_BLOCK, dtype=jnp.int32)
    valid = (kpos <= pos_q[None, :, None, None, None]).reshape(B, Tq, G_NSA, 1, kk * SEL_BLOCK)
    shp = s.shape
    p = masked_softmax(s.reshape(B, Tq, G_NSA, R_NSA, kk * SEL_BLOCK), valid).reshape(shp)
    return jnp.einsum('bqgrks,bgqksd->bqgrd', p, vg.astype(jnp.float32))


def nsa_window(q_rot, pos_q, kw, vw, pos_kw):
    s = jnp.einsum('bqgrd,bkgd->bqgrk', q_rot, kw).astype(jnp.float32) * ATTN_SCALE
    diff = pos_q[:, None] - pos_kw[None, :]
    valid = (diff >= 0) & (diff < WINDOW) & (pos_kw[None, :] >= 0)
    p = masked_softmax(s, valid[None, :, None, None, :])
    return jnp.einsum('bqgrk,bkgd->bqgrd', p, vw.astype(jnp.float32))


def nsa_mix(q, gates, pos_q, kcmp, vcmp, overlap, kb, vb, kw, vw, pos_kw):
    B, Tq = q.shape[:2]
    q_raw = q.reshape(B, Tq, G_NSA, R_NSA, D_NSA)
    q_rot = partial_rope(q, pos_q).reshape(B, Tq, G_NSA, R_NSA, D_NSA)
    o_cmp, p_cmp = nsa_compressed(q_raw, pos_q, kcmp, vcmp)
    o_sel = nsa_selected(q_rot, pos_q, p_cmp, overlap, kb, vb)
    o_win = nsa_window(q_rot, pos_q, kw, vw, pos_kw)
    g = gates.reshape(B, Tq, G_NSA, R_NSA, 3)
    o = g[..., 0:1] * o_cmp + g[..., 1:2] * o_sel + g[..., 2:3] * o_win
    return o.reshape(B, Tq, H_NSA * D_NSA).astype(q.dtype)


def mixer_prompt(n, mix_w):
    (w_in, w_a2, b_a, gla_norm, pos_k, w_k1, w_k2, pos_v, w_v1, w_v2) = mix_w
    B, T, _ = n.shape
    pos = jnp.arange(T, dtype=jnp.int32)
    (q_g, k_g, v_g, log_a, r_g), (q_n, gates), nsa_rows, win_rows = project_mixers(n, pos, w_in, w_a2, b_a)
    s0 = jnp.zeros((B, H_GLA, DK_GLA, DV_GLA), jnp.float32)
    o_g, s_fin = gla_scan(q_g, k_g, v_g, log_a, s0, GLA_CHUNK)
    o_gla = gla_output(o_g, r_g, gla_norm)
    kcmp = compress(nsa_rows[:, :, 0], pos_k, w_k1, w_k2)
    vcmp = compress(nsa_rows[:, :, 1], pos_v, w_v1, w_v2)
    kb = to_blocks(nsa_rows[:, :, 2])
    vb = to_blocks(nsa_rows[:, :, 3])
    overlap = overlap_matrix(kcmp.shape[1], kb.shape[2])
    kw_pad = jnp.pad(win_rows, ((0, 0), (WINDOW, 0), (0, 0), (0, 0), (0, 0)))
    nqb = T // Q_BLOCK
    qb = q_n.reshape(B, nqb, Q_BLOCK, H_NSA, D_NSA).swapaxes(0, 1)
    gb = gates.reshape(B, nqb, Q_BLOCK, H_NSA, 3).swapaxes(0, 1)
    starts = jnp.arange(nqb, dtype=jnp.int32) * Q_BLOCK

    def query_block(args):
        q_b, g_b, qs = args
        pos_q = qs + jnp.arange(Q_BLOCK, dtype=jnp.int32)
        kw_b = lax.dynamic_slice_in_dim(kw_pad, qs, WINDOW + Q_BLOCK, axis=1)
        pos_kw = qs - WINDOW + jnp.arange(WINDOW + Q_BLOCK, dtype=jnp.int32)
        return nsa_mix(q_b, g_b, pos_q, kcmp, vcmp, overlap, kb, vb, kw_b[:, :, 0], kw_b[:, :, 1], pos_kw)

    o_nsa = lax.map(query_block, (qb, gb, starts))
    o_nsa = o_nsa.swapaxes(0, 1).reshape(B, T, H_NSA * D_NSA)
    mix = jnp.concatenate([o_gla, o_nsa.astype(o_gla.dtype)], axis=-1)
    return mix, nsa_rows, win_rows[:, T - min(WINDOW, T):], s_fin.astype(n.dtype)


def mixer_sample(n, cache_l, win_l, gla_l, page_table, mix_w):
    (w_in, w_a2, b_a, gla_norm, pos_k, w_k1, w_k2, pos_v, w_v1, w_v2) = mix_w
    B, T, _ = n.shape
    past_len = page_table.shape[1] * cache_l.shape[1]
    pos = past_len + jnp.arange(T, dtype=jnp.int32)
    (q_g, k_g, v_g, log_a, r_g), (q_n, gates), nsa_rows, win_rows = project_mixers(n, pos, w_in, w_a2, b_a)
    o_g, s_fin = gla_scan(q_g, k_g, v_g, log_a, gla_l.astype(jnp.float32), T)
    o_gla = gla_output(o_g, r_g, gla_norm)
    past = cache_l[page_table].reshape(B, past_len, 4, G_NSA, D_NSA)
    full = jnp.concatenate([past, nsa_rows.astype(past.dtype)], axis=1)
    kcmp = compress(full[:, :, 0], pos_k, w_k1, w_k2)
    vcmp = compress(full[:, :, 1], pos_v, w_v1, w_v2)
    kb = to_blocks(full[:, :, 2])
    vb = to_blocks(full[:, :, 3])
    overlap = overlap_matrix(kcmp.shape[1], kb.shape[2])
    wb = win_l.shape[1]
    kw = jnp.concatenate([win_l, win_rows.astype(win_l.dtype)], axis=1)
    pos_kw = past_len - wb + jnp.arange(wb + T, dtype=jnp.int32)
    o_nsa = nsa_mix(q_n, gates, pos, kcmp, vcmp, overlap, kb, vb, kw[:, :, 0], kw[:, :, 1], pos_kw)
    keep = min(WINDOW, wb + T)
    mix = jnp.concatenate([o_gla, o_nsa.astype(o_gla.dtype)], axis=-1)
    return mix, nsa_rows, kw[:, wb + T - keep:], s_fin.astype(gla_l.dtype)


def trunk_layer(x, p, mixer, f1_pre, f1_post, f1_in, f1_out, m_pre, m_post, w_out,
                f2_pre, f2_post, f2_in, f2_out, ple_pre, ple_gate, ple_proj, ple_post):
    h = x + 0.5 * rmsnorm(swiglu(rmsnorm(x, f1_pre), f1_in, f1_out), f1_post)
    mix, nsa_rows, win_state, gla_state = mixer(rmsnorm(h, m_pre))
    h = h + rmsnorm(mix @ w_out, m_post)
    h = h + 0.5 * rmsnorm(swiglu(rmsnorm(h, f2_pre), f2_in, f2_out), f2_post)
    gate = jax.nn.sigmoid(rmsnorm(h, ple_pre) @ ple_gate)
    h = h + rmsnorm(gate * (p @ ple_proj), ple_post)
    return h, nsa_rows, win_state, gla_state


def setup_inputs(seed: int = 0) -> dict:
    key = jax.random.key(seed)
    ks = iter(jax.random.split(key, 48))
    f32 = jnp.float32

    def nrm(shape, scale):
        return jax.random.normal(next(ks), shape, f32) * scale

    def gain(n):
        return 1.0 + 0.05 * jax.random.normal(next(ks), (DEPTH, n), f32)

    n_pages = PAST_LEN // PAGE_SIZE
    n_used = DEC_BATCH * n_pages
    n_pool = n_used + (n_used + 3) // 4
    win_buf = min(WINDOW, PAST_LEN)
    page_table = jax.random.permutation(next(ks), n_pool)[:n_used].reshape(DEC_BATCH, n_pages).astype(jnp.int32)
    return {
        'x_prompt': nrm((BATCH, SEQ, D_MODEL), 1.0),
        'x_sample': nrm((DEC_BATCH, DEC_SEQ, D_MODEL), 1.0),
        'cache_nsa': nrm((DEPTH, n_pool, PAGE_SIZE, 4, G_NSA, D_NSA), 1.0),
        'state_win': nrm((DEPTH, DEC_BATCH, win_buf, 2, G_NSA, D_NSA), 1.0),
        'state_gla': nrm((DEPTH, DEC_BATCH, H_GLA, DK_GLA, DV_GLA), 1.0),
        'page_table': page_table,
        'p_prompt': nrm((DEPTH, BATCH, SEQ, PLE_DIM), 1.0),
        'p_sample': nrm((DEPTH, DEC_BATCH, DEC_SEQ, PLE_DIM), 1.0),
        'ffn1_norm_pre': gain(D_MODEL),
        'ffn1_norm_post': gain(D_MODEL),
        'ffn1_w_in': nrm((DEPTH, D_MODEL, 2 * D_FF), D_MODEL ** -0.5),
        'ffn1_w_out': nrm((DEPTH, D_FF, D_MODEL), D_FF ** -0.5),
        'mix_norm_pre': gain(D_MODEL),
        'mix_norm_post': gain(D_MODEL),
        'w_mix_in': nrm((DEPTH, D_MODEL, IN_COLS), D_MODEL ** -0.5),
        'w_gla_a2': nrm((DEPTH, GLA_RANK, H_GLA * DK_GLA), GLA_RANK ** -0.5),
        'b_gla_a': nrm((DEPTH, H_GLA * DK_GLA), 0.1),
        'gla_out_norm': gain(DV_GLA),
        'cmp_pos_k': nrm((DEPTH, CMP_LEN, D_NSA), 0.5),
        'w_cmp_k1': nrm((DEPTH, CMP_LEN * D_NSA, CMP_HIDDEN), (CMP_LEN * D_NSA) ** -0.5),
        'w_cmp_k2': nrm((DEPTH, CMP_HIDDEN, D_NSA), CMP_HIDDEN ** -0.5),
        'cmp_pos_v': nrm((DEPTH, CMP_LEN, D_NSA), 0.5),
        'w_cmp_v1': nrm((DEPTH, CMP_LEN * D_NSA, CMP_HIDDEN), (CMP_LEN * D_NSA) ** -0.5),
        'w_cmp_v2': nrm((DEPTH, CMP_HIDDEN, D_NSA), CMP_HIDDEN ** -0.5),
        'w_mix_out': nrm((DEPTH, MIX_WIDTH, D_MODEL), MIX_WIDTH ** -0.5),
        'ffn2_norm_pre': gain(D_MODEL),
        'ffn2_norm_post': gain(D_MODEL),
        'ffn2_w_in': nrm((DEPTH, D_MODEL, 2 * D_FF), D_MODEL ** -0.5),
        'ffn2_w_out': nrm((DEPTH, D_FF, D_MODEL), D_FF ** -0.5),
        'ple_norm_pre': gain(D_MODEL),
        'ple_w_gate': nrm((DEPTH, D_MODEL, D_MODEL), D_MODEL ** -0.5),
        'ple_w_proj': nrm((DEPTH, PLE_DIM, D_MODEL), PLE_DIM ** -0.5),
        'ple_norm_post': gain(D_MODEL),
    }


def reference(x_prompt, x_sample, cache_nsa, state_win, state_gla, page_table, p_prompt, p_sample,
              ffn1_norm_pre, ffn1_norm_post, ffn1_w_in, ffn1_w_out,
              mix_norm_pre, mix_norm_post, w_mix_in, w_gla_a2, b_gla_a, gla_out_norm,
              cmp_pos_k, w_cmp_k1, w_cmp_k2, cmp_pos_v, w_cmp_v1, w_cmp_v2, w_mix_out,
              ffn2_norm_pre, ffn2_norm_post, ffn2_w_in, ffn2_w_out,
              ple_norm_pre, ple_w_gate, ple_w_proj, ple_norm_post):
    hp, hs = x_prompt, x_sample
    nsa_p, win_p, gla_p, nsa_s, win_s, gla_s = [], [], [], [], [], []
    for i in range(DEPTH):
        mix_w = (w_mix_in[i], w_gla_a2[i], b_gla_a[i], gla_out_norm[i], cmp_pos_k[i], w_cmp_k1[i],
                 w_cmp_k2[i], cmp_pos_v[i], w_cmp_v1[i], w_cmp_v2[i])
        shared = (ffn1_norm_pre[i], ffn1_norm_post[i], ffn1_w_in[i], ffn1_w_out[i],
                  mix_norm_pre[i], mix_norm_post[i], w_mix_out[i],
                  ffn2_norm_pre[i], ffn2_norm_post[i], ffn2_w_in[i], ffn2_w_out[i],
                  ple_norm_pre[i], ple_w_gate[i], ple_w_proj[i], ple_norm_post[i])
        hp, r_p, w_p, s_p = trunk_layer(hp, p_prompt[i], functools.partial(mixer_prompt, mix_w=mix_w), *shared)
        mix_s = functools.partial(mixer_sample, cache_l=cache_nsa[i], win_l=state_win[i], gla_l=state_gla[i],
                                  page_table=page_table, mix_w=mix_w)
        hs, r_s, w_s, s_s = trunk_layer(hs, p_sample[i], mix_s, *shared)
        nsa_p.append(r_p); win_p.append(w_p); gla_p.append(s_p)
        nsa_s.append(r_s); win_s.append(w_s); gla_s.append(s_s)
    return (hp, hs, jnp.stack(nsa_p), jnp.stack(win_p), jnp.stack(gla_p),
            jnp.stack(nsa_s), jnp.stack(win_s), jnp.stack(gla_s))
```

```python
import functools

import numpy as np
import jax
import jax.numpy as jnp
from jax import lax
from jax.experimental import pallas as pl
from jax.experimental.pallas import tpu as pltpu

F32 = jnp.float32
BF16 = jnp.bfloat16

D_MODEL = 1024
PLE_DIM = 256
D_FF = 2816
EPS = 1e-6
H_GLA = 4
DK_GLA = 64
DV_GLA = 128
GLA_RANK = 16
GLA_GATE_TEMP = 16.0
GLA_CHUNK = 64
H_NSA = 8
G_NSA = 2
R_NSA = H_NSA // G_NSA
D_NSA = 64
CMP_LEN = 32
CMP_STRIDE = 16
CMP_HIDDEN = 128
SEL_BLOCK = 64
N_SELECT = 16
WINDOW = 512
FORCE_SCORE = 1e4
NEG_INF = -1e30
ATTN_SCALE = D_NSA ** -0.5
ROPE_THETA = 500000.0
ROPE_DIM = D_NSA // 4
ROPE_HALF = ROPE_DIM // 2
IN_SPLITS = (H_GLA * DK_GLA, H_GLA * DK_GLA, H_GLA * DV_GLA, H_GLA * DV_GLA, GLA_RANK,
             H_NSA * D_NSA, 6 * G_NSA * D_NSA, 3 * H_NSA)

LANES = 128
SUBLANES = 8
VMEM_LIMIT = 56 * 1024 * 1024

ROW_TILE = 512
FF_CHUNK = 256
Q_TILE = 128
K_TILE = 128
PAGES_PER_STEP = 32
ROWS_PER_PAGE = 128
N_GLA_LEVELS = 6
PAD_BLOCKS = 384


def _params(*sem):
    return pltpu.CompilerParams(dimension_semantics=sem, vmem_limit_bytes=VMEM_LIMIT)


def _rms(x, g):
    return x * lax.rsqrt(jnp.mean(x * x, axis=-1, keepdims=True) + EPS) * g


def _dot(a, b):
    return jnp.dot(a, b, preferred_element_type=F32)


def _dot_nt(a, b):
    return lax.dot_general(a, b, (((1,), (1,)), ((), ())), preferred_element_type=F32)


def _dot_tn(a, b):
    return lax.dot_general(a, b, (((0,), (0,)), ((), ())), preferred_element_type=F32)


def _split_bf16(x, n):
    parts = []
    r = x
    for _ in range(n):
        p = r.astype(BF16)
        parts.append(p)
        r = r - p.astype(F32)
    return parts


def _dot01_left(m01, x, n):
    out = None
    for p in _split_bf16(x, n):
        t = _dot(m01, p)
        out = t if out is None else out + t
    return out


def _dot01_right(x, m01, n):
    out = None
    for p in _split_bf16(x, n):
        t = _dot(p, m01)
        out = t if out is None else out + t
    return out


def _masked_softmax_rows(s, valid):
    sm = jnp.where(valid, s, NEG_INF)
    m = jnp.max(sm, axis=0, keepdims=True)
    e = jnp.exp(sm - m)
    return jnp.where(valid, e / jnp.sum(e, axis=0, keepdims=True), 0.0)


def _masked_softmax_lanes(s, valid):
    sm = jnp.where(valid, s, NEG_INF)
    m = jnp.max(sm, axis=-1, keepdims=True)
    e = jnp.exp(sm - m)
    return jnp.where(valid, e / jnp.sum(e, axis=-1, keepdims=True), 0.0)


def _ffn_kernel(x_ref, gpre_ref, wg_ref, wu_ref, wo_ref, gpost_ref, o_ref):
    x = x_ref[...]
    xn = _rms(x, gpre_ref[...]).astype(BF16)
    acc = jnp.zeros(x.shape, F32)
    for c in range(D_FF // FF_CHUNK):
        sl = slice(c * FF_CHUNK, (c + 1) * FF_CHUNK)
        g = _dot(xn, wg_ref[:, sl])
        u = _dot(xn, wu_ref[:, sl])
        a = (jax.nn.silu(g) * u).astype(BF16)
        acc = acc + _dot(a, wo_ref[sl, :])
    o_ref[...] = x + 0.5 * _rms(acc, gpost_ref[...])


def _ffn(x, gpre, w_in, w_out, gpost, tm):
    m = x.shape[0]
    const = lambda i: (0, 0)
    return pl.pallas_call(
        _ffn_kernel,
        grid=(m // tm,),
        in_specs=[
            pl.BlockSpec((tm, D_MODEL), lambda i: (i, 0)),
            pl.BlockSpec((1, D_MODEL), const),
            pl.BlockSpec((D_MODEL, D_FF), const),
            pl.BlockSpec((D_MODEL, D_FF), lambda i: (0, 1)),
            pl.BlockSpec((D_FF, D_MODEL), const),
            pl.BlockSpec((1, D_MODEL), const),
        ],
        out_specs=pl.BlockSpec((tm, D_MODEL), lambda i: (i, 0)),
        out_shape=jax.ShapeDtypeStruct((m, D_MODEL), F32),
        compiler_params=_params("parallel"),
        name="ffn",
    )(x, gpre, w_in, w_in, w_out, gpost)


N_GLA_COLS = 2 * H_GLA * DK_GLA + 2 * H_GLA * DV_GLA
N_KV_COLS = 6 * G_NSA * D_NSA
WN_COLS = N_GLA_COLS + LANES + N_KV_COLS
N_Q_COLS = H_NSA * D_NSA
N_GATE_ROWS = 32
WT_ROWS = N_Q_COLS + 2 * LANES + N_GATE_ROWS


def _rope_rows(x, cos, sin):
    out = []
    for h in range(H_NSA):
        b = h * D_NSA
        x1 = x[b:b + ROPE_HALF]
        x2 = x[b + ROPE_HALF:b + ROPE_DIM]
        out += [x1 * cos - x2 * sin, x2 * cos + x1 * sin, x[b + ROPE_DIM:b + D_NSA]]
    return jnp.concatenate(out, axis=0)


def _rope_lanes(x, c, s1, s2):
    return x * c + pltpu.roll(x, LANES - ROPE_HALF, 1) * s1 + pltpu.roll(x, ROPE_HALF, 1) * s2


def _proj_kernel(h_ref, g_ref, wn_ref, wt_ref, wa2_ref, ba_ref, rc_ref, rs1_ref, rs2_ref, cos_ref, sin_ref,
                 qk_ref, v_ref, r_ref, la_ref, nsa_ref, win_ref,
                 qraw_ref, qrot_ref, vsel_ref, vwin_ref, gate_ref):
    xn = _rms(h_ref[...], g_ref[...]).astype(BF16)
    nqk = 2 * H_GLA * DK_GLA
    nv = H_GLA * DV_GLA
    z = _dot(xn, wn_ref[:, 0:nqk])
    qk_ref[:, 0:nqk // 2] = z[:, 0:nqk // 2] * (DK_GLA ** -0.5)
    qk_ref[:, nqk // 2:nqk] = z[:, nqk // 2:nqk]
    v_ref[...] = _dot(xn, wn_ref[:, nqk:nqk + nv])
    r_ref[...] = _dot(xn, wn_ref[:, nqk + nv:N_GLA_COLS])
    a_lr = _dot(xn, wn_ref[:, N_GLA_COLS:N_GLA_COLS + LANES])
    xa = _dot(a_lr.astype(BF16), wa2_ref[...]) + ba_ref[...]
    la_ref[...] = (jnp.minimum(xa, 0.0) - jnp.log1p(jnp.exp(-jnp.abs(xa)))) * (1.0 / GLA_GATE_TEMP)
    kv0 = N_GLA_COLS + LANES
    kv = _dot(xn, wn_ref[:, kv0:kv0 + N_KV_COLS])
    rc, rs1, rs2 = rc_ref[...], rs1_ref[...], rs2_ref[...]
    nsa_ref[:, 0:256] = kv[:, 0:256]
    nsa_ref[:, 256:384] = _rope_lanes(kv[:, 256:384], rc, rs1, rs2)
    nsa_ref[:, 384:512] = kv[:, 384:512]
    win_ref[:, 0:128] = _rope_lanes(kv[:, 512:640], rc, rs1, rs2)
    win_ref[:, 128:256] = kv[:, 640:768]
    zt = _dot_nt(wt_ref[...], xn)
    q = zt[0:N_Q_COLS] * ATTN_SCALE
    qraw_ref[...] = q.astype(BF16)
    qrot_ref[...] = _rope_rows(q, cos_ref[...], sin_ref[...]).astype(BF16)
    vsel_ref[...] = zt[N_Q_COLS:N_Q_COLS + LANES].astype(BF16)
    vwin_ref[...] = zt[N_Q_COLS + LANES:N_Q_COLS + 2 * LANES].astype(BF16)
    gate_ref[...] = jax.nn.sigmoid(zt[N_Q_COLS + 2 * LANES:WT_ROWS])


def _proj(h, gain, wn, wt, wa2, ba, tabs, tm, tiles_per_seq):
    m = h.shape[0]
    rc, rs1, rs2, cos_t, sin_t = tabs
    const = lambda i: (0, 0)
    row = lambda i: (i, 0)
    col = lambda i: (0, i)
    tab_row = lambda i: (i % tiles_per_seq, 0)
    tab_col = lambda i: (0, i % tiles_per_seq)
    outs = [
        (2 * H_GLA * DK_GLA, F32), (H_GLA * DV_GLA, F32), (H_GLA * DV_GLA, F32), (H_GLA * DK_GLA, F32),
        (4 * G_NSA * D_NSA, F32), (2 * G_NSA * D_NSA, F32),
    ]
    outs_t = [(N_Q_COLS, BF16), (N_Q_COLS, BF16), (LANES, BF16), (LANES, BF16), (N_GATE_ROWS, F32)]
    return pl.pallas_call(
        _proj_kernel,
        grid=(m // tm,),
        in_specs=[
            pl.BlockSpec((tm, D_MODEL), row),
            pl.BlockSpec((1, D_MODEL), const),
            pl.BlockSpec((D_MODEL, WN_COLS), const),
            pl.BlockSpec((WT_ROWS, D_MODEL), const),
            pl.BlockSpec((LANES, H_GLA * DK_GLA), const),
            pl.BlockSpec((1, H_GLA * DK_GLA), const),
            pl.BlockSpec((tm, LANES), tab_row),
            pl.BlockSpec((tm, LANES), tab_row),
            pl.BlockSpec((tm, LANES), tab_row),
            pl.BlockSpec((ROPE_HALF, tm), tab_col),
            pl.BlockSpec((ROPE_HALF, tm), tab_col),
        ],
        out_specs=[pl.BlockSpec((tm, n), row) for n, _ in outs] + [pl.BlockSpec((n, tm), col) for n, _ in outs_t],
        out_shape=[jax.ShapeDtypeStruct((m, n), d) for n, d in outs]
        + [jax.ShapeDtypeStruct((n, m), d) for n, d in outs_t],
        compiler_params=_params("parallel"),
        name="mixer_proj",
    )(h, gain, wn, wt, wa2, ba, rc, rs1, rs2, cos_t, sin_t)


def _rope_tables(pos):
    inv_freq = ROPE_THETA ** (-jnp.arange(ROPE_HALF, dtype=F32) * 2.0 / ROPE_DIM)
    ang = pos.astype(F32)[:, None] * inv_freq[None, :]
    cos, sin = jnp.cos(ang), jnp.sin(ang)
    n = pos.shape[0]
    one = jnp.ones((n, D_NSA - ROPE_DIM), F32)
    zero = jnp.zeros((n, D_NSA - ROPE_DIM), F32)
    zh = jnp.zeros((n, ROPE_HALF), F32)
    c = jnp.concatenate([cos, cos, one], axis=1)
    s1 = jnp.concatenate([-sin, zh, zero], axis=1)
    s2 = jnp.concatenate([zh, sin, zero], axis=1)
    dup = lambda t: jnp.concatenate([t, t], axis=1)
    return dup(c), dup(s1), dup(s2), cos.T, sin.T


def _gla_constants():
    c = GLA_CHUNK
    t = np.arange(c)
    low = (t[None, :] <= t[:, None]).astype(np.float32)
    mats, masks = [low], []
    for lev in range(1, N_GLA_LEVELS + 1):
        seg = (2 * c) >> lev
        half = seg // 2
        mid = (t // seg) * seg + half
        mats.append(low[mid])
        same = (t[:, None] // seg) == (t[None, :] // seg)
        masks.append((same & ((t[:, None] % seg) >= half) & ((t[None, :] % seg) < half)).astype(np.float32))
    masks.append(np.eye(c, dtype=np.float32))
    return np.concatenate(mats, axis=0), np.stack(masks)


def _gla_kernel(qk_ref, la_ref, v_ref, r_ref, gain_ref, big_ref, mask_ref, o_ref, sfin_ref, st_ref):
    c = pl.program_id(1)
    ch = GLA_CHUNK

    @pl.when(c == 0)
    def _():
        st_ref[...] = jnp.zeros(st_ref.shape, F32)

    lane = lax.broadcasted_iota(jnp.int32, (1, LANES), 1)
    head_mask = [jnp.where(lane < DK_GLA, 1.0, 0.0), jnp.where(lane >= DK_GLA, 1.0, 0.0)]
    big = big_ref[...]
    gain = gain_ref[...]
    nq = H_GLA * DK_GLA
    for p in range(H_GLA // 2):
        sl = slice(p * LANES, (p + 1) * LANES)
        q = qk_ref[:, sl]
        k = qk_ref[:, nq + p * LANES:nq + (p + 1) * LANES]
        allb = _dot01_left(big, la_ref[:, sl], 3)
        b = allb[0:ch]
        b_last = b[ch - 1:ch]
        attn = [jnp.zeros((ch, ch), F32), jnp.zeros((ch, ch), F32)]
        for lev in range(N_GLA_LEVELS + 1):
            if lev < N_GLA_LEVELS:
                ref = allb[(lev + 1) * ch:(lev + 2) * ch]
                ql = q * jnp.exp(jnp.minimum(b - ref, 0.0))
                kl = k * jnp.exp(jnp.minimum(ref - b, 0.0))
            else:
                ql, kl = q, k
            qq = jnp.concatenate([ql * head_mask[0], ql * head_mask[1]], axis=0).astype(BF16)
            s = _dot_nt(qq, kl.astype(BF16))
            mk = mask_ref[lev]
            attn[0] = attn[0] + mk * s[0:ch]
            attn[1] = attn[1] + mk * s[ch:2 * ch]
        q0 = q * jnp.exp(b)
        k_hat = (k * jnp.exp(b_last - b)).astype(BF16)
        decay = jnp.exp(b_last)
        for hh in range(2):
            h = 2 * p + hh
            hs = slice(h * DV_GLA, (h + 1) * DV_GLA)
            vh = v_ref[:, hs].astype(BF16)
            st = st_ref[h]
            o = _dot(attn[hh].astype(BF16), vh) + _dot_nt((q0 * head_mask[hh]).astype(BF16), st.astype(BF16))
            st_ref[h] = st * decay + _dot_tn(vh, k_hat)
            o_ref[:, hs] = _rms(o, gain) * jax.nn.silu(r_ref[:, hs])

    @pl.when(c == pl.num_programs(1) - 1)
    def _():
        sfin_ref[0] = st_ref[...]


def _gla_prompt(qk, la, v, r, gain, nb, t):
    big, masks = _gla_constants()
    nc = t // GLA_CHUNK
    row = lambda b, c: (b * nc + c, 0)
    const2 = lambda b, c: (0, 0)
    m = nb * t
    return pl.pallas_call(
        _gla_kernel,
        grid=(nb, nc),
        in_specs=[
            pl.BlockSpec((GLA_CHUNK, 2 * H_GLA * DK_GLA), row),
            pl.BlockSpec((GLA_CHUNK, H_GLA * DK_GLA), row),
            pl.BlockSpec((GLA_CHUNK, H_GLA * DV_GLA), row),
            pl.BlockSpec((GLA_CHUNK, H_GLA * DV_GLA), row),
            pl.BlockSpec((1, DV_GLA), const2),
            pl.BlockSpec(big.shape, const2),
            pl.BlockSpec(masks.shape, lambda b, c: (0, 0, 0)),
        ],
        out_specs=[
            pl.BlockSpec((GLA_CHUNK, H_GLA * DV_GLA), row),
            pl.BlockSpec((1, H_GLA, DV_GLA, LANES), lambda b, c: (b, 0, 0, 0)),
        ],
        out_shape=[
            jax.ShapeDtypeStruct((m, H_GLA * DV_GLA), F32),
            jax.ShapeDtypeStruct((nb, H_GLA, DV_GLA, LANES), F32),
        ],
        scratch_shapes=[pltpu.VMEM((H_GLA, DV_GLA, LANES), F32)],
        compiler_params=_params("parallel", "arbitrary"),
        name="gla_scan",
    )(qk, la, v, r, gain, jnp.asarray(big, BF16), jnp.asarray(masks, F32))


def _gla_step_kernel(q_ref, k_ref, a_ref, v_ref, r_ref, s_ref, gain_ref, o_ref, sn_ref):
    for h in range(H_GLA):
        s_new = jnp.exp(a_ref[0, h]) * s_ref[0, h] + k_ref[0, h] * v_ref[0, h]
        sn_ref[0, h] = s_new
        o = jnp.sum(q_ref[0, h] * s_new, axis=0, keepdims=True)
        o_ref[0, h] = _rms(o, gain_ref[...]) * jax.nn.silu(r_ref[0, h])


def _gla_step(qcol, kcol, acol, vrow, rrow, state, gain):
    nb = state.shape[0]
    col = pl.BlockSpec((1, H_GLA, DK_GLA, 1), lambda b: (b, 0, 0, 0))
    rowspec = pl.BlockSpec((1, H_GLA, 1, DV_GLA), lambda b: (b, 0, 0, 0))
    stspec = pl.BlockSpec((1, H_GLA, DK_GLA, DV_GLA), lambda b: (b, 0, 0, 0))
    return pl.pallas_call(
        _gla_step_kernel,
        grid=(nb,),
        in_specs=[col, col, col, rowspec, rowspec, stspec, pl.BlockSpec((1, DV_GLA), lambda b: (0, 0))],
        out_specs=[rowspec, stspec],
        out_shape=[jax.ShapeDtypeStruct((nb, H_GLA, 1, DV_GLA), F32),
                   jax.ShapeDtypeStruct((nb, H_GLA, DK_GLA, DV_GLA), F32)],
        compiler_params=_params("parallel"),
        name="gla_step",
    )(qcol, kcol, acol, vrow, rrow, state, gain)


HALF_ROWS = CMP_STRIDE
CHUNKS_PER_PAGE = ROWS_PER_PAGE // HALF_ROWS


def _cmp_partial_kernel(tbl_ref, *refs, pps):
    del tbl_ref
    pages_k = refs[:pps]
    pages_v = refs[pps:2 * pps]
    wk_ref, wv_ref, abk_ref, abv_ref, xk_ref, xv_ref = refs[2 * pps:]
    for i in range(pps):
        rows = slice(i * CHUNKS_PER_PAGE, (i + 1) * CHUNKS_PER_PAGE)
        for l in range(HALF_ROWS):
            lanes = slice(l * LANES, (l + 1) * LANES)
            src = pl.ds(l, CHUNKS_PER_PAGE, stride=HALF_ROWS)
            xk_ref[rows, lanes] = pages_k[i][0, src, :]
            xv_ref[rows, lanes] = pages_v[i][0, src, :]
    abk_ref[0] = _dot(xk_ref[...].astype(BF16), wk_ref[...])
    abv_ref[0] = _dot(xv_ref[...].astype(BF16), wv_ref[...])


def _cmp_partial(rows3, table, wk, wv):
    nb, npages = table.shape
    pps = min(PAGES_PER_STEP, npages)
    nsteps = npages // pps
    nch = npages * CHUNKS_PER_PAGE
    step_chunks = pps * CHUNKS_PER_PAGE
    page_specs = [
        pl.BlockSpec((1, ROWS_PER_PAGE, LANES),
                     functools.partial(lambda b, s, tbl, i, kv: (tbl[b, s * pps + i], 0, kv), i=i, kv=kv))
        for kv in range(2) for i in range(pps)
    ]
    wspec = pl.BlockSpec(wk.shape, lambda b, s, tbl: (0, 0))
    ospec = pl.BlockSpec((1, step_chunks, 4 * CMP_HIDDEN), lambda b, s, tbl: (b, s, 0))
    grid_spec = pltpu.PrefetchScalarGridSpec(
        num_scalar_prefetch=1,
        grid=(nb, nsteps),
        in_specs=page_specs + [wspec, wspec],
        out_specs=[ospec, ospec],
        scratch_shapes=[pltpu.VMEM((step_chunks, HALF_ROWS * LANES), F32),
                        pltpu.VMEM((step_chunks, HALF_ROWS * LANES), F32)],
    )
    osh = jax.ShapeDtypeStruct((nb, nch, 4 * CMP_HIDDEN), F32)
    return pl.pallas_call(
        functools.partial(_cmp_partial_kernel, pps=pps),
        grid_spec=grid_spec,
        out_shape=[osh, osh],
        compiler_params=_params("parallel", "parallel"),
        name="cmp_partial",
    )(table, *([rows3] * (2 * pps)), wk, wv)


def _cmp_finish_kernel(abk_ref, abv_ref, posk_ref, w1k_ref, w2k_ref, posv_ref, w1v_ref, w2v_ref, w2vt_ref,
                       kc_ref, vc_ref, vct_ref):
    nch = abk_ref.shape[1]
    pk = _dot(posk_ref[...], w1k_ref[...])[0:1]
    pv = _dot(posv_ref[...], w1v_ref[...])[0:1]
    abk = abk_ref[0]
    abv = abv_ref[0]
    kc = None
    vc = None
    for g in range(G_NSA):
        o = g * 2 * CMP_HIDDEN
        hk = jax.nn.gelu(abk[:, o:o + CMP_HIDDEN] + pltpu.roll(abk[:, o + CMP_HIDDEN:o + 2 * CMP_HIDDEN], nch - 1, 0) + pk)
        hv = jax.nn.gelu(abv[:, o:o + CMP_HIDDEN] + pltpu.roll(abv[:, o + CMP_HIDDEN:o + 2 * CMP_HIDDEN], nch - 1, 0) + pv)
        hk = hk.astype(BF16)
        hv = hv.astype(BF16)
        tk = _dot(hk, w2k_ref[g])
        tv = _dot(hv, w2v_ref[g])
        kc = tk if kc is None else kc + tk
        vc = tv if vc is None else vc + tv
        vct_ref[0, g * D_NSA:(g + 1) * D_NSA, :] = _dot_nt(w2vt_ref[...], hv).astype(BF16)
    kc_ref[0] = kc.astype(BF16)
    vc_ref[0] = vc.astype(BF16)


def _cmp_finish(abk, abv, posk, w1k, w2k_pad, posv, w1v, w2v_pad, w2vt):
    nb, nch, _ = abk.shape
    ab = pl.BlockSpec((1, nch, 4 * CMP_HIDDEN), lambda b: (b, 0, 0))
    c2 = lambda b: (0, 0)
    c3 = lambda b: (0, 0, 0)
    return pl.pallas_call(
        _cmp_finish_kernel,
        grid=(nb,),
        in_specs=[ab, ab,
                  pl.BlockSpec(posk.shape, c2), pl.BlockSpec(w1k.shape, c2), pl.BlockSpec(w2k_pad.shape, c3),
                  pl.BlockSpec(posv.shape, c2), pl.BlockSpec(w1v.shape, c2), pl.BlockSpec(w2v_pad.shape, c3),
                  pl.BlockSpec(w2vt.shape, c2)],
        out_specs=[pl.BlockSpec((1, nch, LANES), lambda b: (b, 0, 0)),
                   pl.BlockSpec((1, nch, LANES), lambda b: (b, 0, 0)),
                   pl.BlockSpec((1, LANES, nch), lambda b: (b, 0, 0))],
        out_shape=[jax.ShapeDtypeStruct((nb, nch, LANES), BF16),
                   jax.ShapeDtypeStruct((nb, nch, LANES), BF16),
                   jax.ShapeDtypeStruct((nb, LANES, nch), BF16)],
        compiler_params=_params("parallel"),
        name="cmp_finish",
    )(abk, abv, posk, w1k, w2k_pad, posv, w1v, w2v_pad, w2vt)


def _cmp_weights(pos, w1, w2):
    w = w1.reshape(2, HALF_ROWS, D_NSA, CMP_HIDDEN)
    z = jnp.zeros_like(w[0])
    blocks = []
    for g in range(G_NSA):
        cols = []
        for g2 in range(G_NSA):
            for half in range(2):
                cols.append(w[half] if g2 == g else z)
        blocks.append(jnp.concatenate(cols, axis=-1))
    wbig = jnp.stack(blocks, axis=1).reshape(HALF_ROWS * LANES, 4 * CMP_HIDDEN).astype(BF16)
    pos8 = jnp.broadcast_to(pos.reshape(1, CMP_LEN * D_NSA), (SUBLANES, CMP_LEN * D_NSA)).astype(BF16)
    z2 = jnp.zeros_like(w2)
    w2pad = jnp.stack([jnp.concatenate([w2, z2], axis=1), jnp.concatenate([z2, w2], axis=1)]).astype(BF16)
    return wbig, pos8, w1.astype(BF16), w2pad, w2.T.astype(BF16)


def _overlap_t(nc_pad, ns_pad, nc, ns):
    i = np.arange(nc_pad)[None, :] * CMP_STRIDE
    j = np.arange(ns_pad)[:, None] * SEL_BLOCK
    m = (i < j + SEL_BLOCK) & (i + CMP_LEN > j) & (np.arange(nc_pad)[None, :] < nc) & (np.arange(ns_pad)[:, None] < ns)
    return jnp.asarray(m.astype(np.float32), BF16)


def _rank_desc(x):
    nrow = x.shape[0]
    nblk = nrow // SUBLANES
    blocks = [x[v * SUBLANES:(v + 1) * SUBLANES] for v in range(nblk)]
    cnt = [jnp.zeros((SUBLANES, x.shape[1]), F32) for _ in range(nblk)]
    sub = lax.broadcasted_iota(jnp.int32, (SUBLANES, x.shape[1]), 0)
    for jp in range(nrow):
        row = x[jp:jp + 1]
        vb = jp // SUBLANES
        for v in range(nblk):
            ge = jnp.where(row >= blocks[v], 1.0, 0.0)
            gt = jnp.where(row > blocks[v], 1.0, 0.0)
            if v > vb:
                cnt[v] = cnt[v] + ge
            elif v < vb:
                cnt[v] = cnt[v] + gt
            else:
                cnt[v] = cnt[v] + jnp.where(sub > (jp % SUBLANES), ge, gt)
    return jnp.concatenate(cnt, axis=0)


def _nsa_prompt_kernel(qraw_ref, qrot_ref, gate_ref, kc_ref, vct_ref, ksel_ref, vselt_ref, kwin_ref, vwint_ref,
                       ovl_ref, o_ref):
    qb = pl.program_id(1)
    nlane = R_NSA * Q_TILE
    tpos = qb * Q_TILE + lax.broadcasted_iota(jnp.int32, (1, nlane), 1) % Q_TILE
    tpos1 = tpos[:, 0:Q_TILE]
    ncp = kc_ref.shape[1]
    nsb = ovl_ref.shape[0]
    zeros_q = jnp.zeros((D_NSA, nlane), BF16)
    heads = []
    for g in range(G_NSA):
        hs = [g * R_NSA + r for r in range(R_NSA)]
        qraw = jnp.concatenate([qraw_ref[h * D_NSA:(h + 1) * D_NSA, :] for h in hs], axis=1)
        qrot = jnp.concatenate([qrot_ref[h * D_NSA:(h + 1) * D_NSA, :] for h in hs], axis=1)
        pad = (lambda a: jnp.concatenate([a, zeros_q], axis=0)) if g == 0 else (lambda a: jnp.concatenate([zeros_q, a], axis=0))
        gs = slice(g * D_NSA, (g + 1) * D_NSA)

        sc = _dot(kc_ref[0], pad(qraw))
        ci = lax.broadcasted_iota(jnp.int32, (ncp, 1), 0)
        p_cmp = _masked_softmax_rows(sc, ci * CMP_STRIDE + (CMP_LEN - 1) <= tpos)
        o_cmp = _dot(vct_ref[0, gs, :], p_cmp.astype(BF16))
        psum = p_cmp[:, 0:Q_TILE]
        for r in range(1, R_NSA):
            psum = psum + p_cmp[:, r * Q_TILE:(r + 1) * Q_TILE]
        imp = _dot01_left(ovl_ref[...], psum, 2)
        bj = lax.broadcasted_iota(jnp.int32, (nsb, 1), 0)
        cur = tpos1 // SEL_BLOCK
        forced = (bj == 0) | (bj == cur) | (bj == cur - 1)
        imp = jnp.where(forced, FORCE_SCORE, jnp.where(bj * SEL_BLOCK <= tpos1, imp, -1.0))
        bias = jnp.where(_rank_desc(imp) < N_SELECT, 0.0, NEG_INF).astype(BF16)
        bias = jnp.concatenate([bias] * R_NSA, axis=1)
        if nsb < D_NSA:
            bias = jnp.concatenate([bias, jnp.zeros((D_NSA - nsb, nlane), BF16)], axis=0)
        qaug = jnp.concatenate([qrot, bias], axis=0) if g == 0 else jnp.concatenate([bias, qrot], axis=0)

        krow = lax.broadcasted_iota(jnp.int32, (K_TILE, 1), 0)
        klane = lax.broadcasted_iota(jnp.int32, (1, LANES), 1)

        def sel_step(c, carry):
            m, l, acc = carry
            start = pl.multiple_of(c * K_TILE, K_TILE)
            kt = ksel_ref[pl.ds(start, K_TILE), :]
            blk = c * (K_TILE // SEL_BLOCK) + krow // SEL_BLOCK
            if g == 0:
                kaug = jnp.where(klane < D_NSA, kt, jnp.where(klane - D_NSA == blk, 1.0, 0.0))
            else:
                kaug = jnp.where(klane >= D_NSA, kt, jnp.where(klane == blk, 1.0, 0.0))
            s = _dot(kaug.astype(BF16), qaug)
            valid = (start + krow <= tpos) & (s > 0.5 * NEG_INF)
            sm = jnp.where(valid, s, NEG_INF)
            m_new = jnp.maximum(m, jnp.max(sm, axis=0, keepdims=True))
            alpha = jnp.exp(m - m_new)
            e = jnp.where(valid, jnp.exp(sm - m_new), 0.0)
            l = alpha * l + jnp.sum(e, axis=0, keepdims=True)
            vt = vselt_ref[gs, pl.ds(start, K_TILE)]
            acc = alpha * acc + _dot(vt, e.astype(BF16))
            return m_new, l, acc

        init = (jnp.full((1, nlane), NEG_INF, F32), jnp.zeros((1, nlane), F32), jnp.zeros((D_NSA, nlane), F32))
        _, l_sel, acc_sel = lax.fori_loop(0, qb + 1, sel_step, init)
        o_sel = acc_sel / l_sel

        band = WINDOW + Q_TILE
        wstart = pl.multiple_of(jnp.maximum(qb * Q_TILE - WINDOW, 0), Q_TILE)
        sw = _dot(kwin_ref[pl.ds(wstart, band), :].astype(BF16), pad(qrot))
        diff = tpos - (wstart + lax.broadcasted_iota(jnp.int32, (band, 1), 0))
        p_win = _masked_softmax_rows(sw, (diff >= 0) & (diff < WINDOW))
        o_win = _dot(vwint_ref[gs, pl.ds(wstart, band)], p_win.astype(BF16))

        for r in range(R_NSA):
            h = hs[r]
            ls = slice(r * Q_TILE, (r + 1) * Q_TILE)
            heads.append(gate_ref[3 * h:3 * h + 1, :] * o_cmp[:, ls]
                         + gate_ref[3 * h + 1:3 * h + 2, :] * o_sel[:, ls]
                         + gate_ref[3 * h + 2:3 * h + 3, :] * o_win[:, ls])
    o_ref[...] = jnp.concatenate(heads, axis=0).T


def _nsa_prompt(qraw_t, qrot_t, gate_t, kc, vct, nsa_rows, vsel_t, win_rows, vwin_t, nb, t):
    nqb = t // Q_TILE
    ncp = kc.shape[1]
    ovl = _overlap_t(ncp, t // SEL_BLOCK, (t - CMP_LEN) // CMP_STRIDE + 1, t // SEL_BLOCK)
    qcol = lambda b, q: (0, b * nqb + q)
    m = nb * t
    return pl.pallas_call(
        _nsa_prompt_kernel,
        grid=(nb, nqb),
        in_specs=[
            pl.BlockSpec((N_Q_COLS, Q_TILE), qcol),
            pl.BlockSpec((N_Q_COLS, Q_TILE), qcol),
            pl.BlockSpec((N_GATE_ROWS, Q_TILE), qcol),
            pl.BlockSpec((1, ncp, LANES), lambda b, q: (b, 0, 0)),
            pl.BlockSpec((1, LANES, ncp), lambda b, q: (b, 0, 0)),
            pl.BlockSpec((t, LANES), lambda b, q: (b, 2)),
            pl.BlockSpec((LANES, t), lambda b, q: (0, b)),
            pl.BlockSpec((t, LANES), lambda b, q: (b, 0)),
            pl.BlockSpec((LANES, t), lambda b, q: (0, b)),
            pl.BlockSpec(ovl.shape, lambda b, q: (0, 0)),
        ],
        out_specs=pl.BlockSpec((Q_TILE, N_Q_COLS), lambda b, q: (b * nqb + q, 0)),
        out_shape=jax.ShapeDtypeStruct((m, N_Q_COLS), F32),
        compiler_params=_params("parallel", "parallel"),
        name="nsa_prompt",
    )(qraw_t, qrot_t, gate_t, kc, vct, nsa_rows, vsel_t, win_rows, vwin_t, ovl)


def _nsa_decode_select_kernel(qt_ref, kc_ref, vct_ref, ovl_ref, grp_ref, idx_ref, ocmp_ref, *, pos, n_blocks):
    ncp = kc_ref.shape[1]
    sc = _dot(kc_ref[0], qt_ref[0])
    ci = lax.broadcasted_iota(jnp.int32, (ncp, 1), 0)
    p = _masked_softmax_rows(sc, ci * CMP_STRIDE + (CMP_LEN - 1) <= pos)
    ocmp_ref[0] = _dot(vct_ref[0], p.astype(BF16))
    psum = _dot01_right(p, grp_ref[...], 2)
    imp_t = _dot01_left(ovl_ref[...], psum, 2)
    imp_r = imp_t.T
    cur = pos // SEL_BLOCK

    def finish(v, j):
        forced = (j == 0) | (j == cur) | (j == cur - 1)
        v = jnp.where(forced, FORCE_SCORE, jnp.where(j * SEL_BLOCK <= pos, v, -1.0))
        return jnp.where(j < n_blocks, v, -3e38)

    jc = lax.broadcasted_iota(jnp.int32, (PAD_BLOCKS, 1), 0)
    jr = lax.broadcasted_iota(jnp.int32, (1, PAD_BLOCKS), 1)
    kk = lax.broadcasted_iota(jnp.int32, (N_SELECT, 1), 0).astype(F32)
    lane = lax.broadcasted_iota(jnp.int32, (N_SELECT, LANES), 1)
    out = jnp.zeros((N_SELECT, LANES), F32)
    for g in range(G_NSA):
        c0 = g * R_NSA
        col = finish(imp_t[:, c0:c0 + 1], jc)
        row = finish(imp_r[c0:c0 + 1, :], jr)
        beats = (col > row) | ((col == row) & (jc < jr))
        rank = jnp.sum(jnp.where(beats, 1.0, 0.0), axis=0, keepdims=True)
        hit = rank == kk
        idx = jnp.sum(jnp.where(hit, jr.astype(F32), 0.0), axis=1, keepdims=True)
        out = jnp.where(lane == g, idx, out)
    idx_ref[0] = out.astype(jnp.int32)


def _nsa_decode_select(q_t, kc, vct, pos, n_blocks):
    nb, ncp, _ = kc.shape
    nc = (pos + 1 - CMP_LEN) // CMP_STRIDE + 1
    ovl = _overlap_t(ncp, PAD_BLOCKS, nc, n_blocks)
    hh = np.arange(LANES)
    grp = ((hh[:, None] // R_NSA) == (hh[None, :] // R_NSA)) & (hh[:, None] < H_NSA) & (hh[None, :] < H_NSA)
    grp = jnp.asarray(grp.astype(np.float32), BF16)
    return pl.pallas_call(
        functools.partial(_nsa_decode_select_kernel, pos=pos, n_blocks=n_blocks),
        grid=(nb,),
        in_specs=[
            pl.BlockSpec((1, LANES, LANES), lambda b: (b, 0, 0)),
            pl.BlockSpec((1, ncp, LANES), lambda b: (b, 0, 0)),
            pl.BlockSpec((1, LANES, ncp), lambda b: (b, 0, 0)),
            pl.BlockSpec(ovl.shape, lambda b: (0, 0)),
            pl.BlockSpec(grp.shape, lambda b: (0, 0)),
        ],
        out_specs=[pl.BlockSpec((1, N_SELECT, LANES), lambda b: (b, 0, 0)),
                   pl.BlockSpec((1, LANES, LANES), lambda b: (b, 0, 0))],
        out_shape=[jax.ShapeDtypeStruct((nb, N_SELECT, LANES), jnp.int32),
                   jax.ShapeDtypeStruct((nb, LANES, LANES), F32)],
        compiler_params=_params("parallel"),
        name="nsa_decode_select",
    )(q_t, kc, vct, ovl, grp)


def _nsa_decode_attend_kernel(idx_ref, tbl_ref, *refs, n_past_blocks, win_buf):
    del tbl_ref
    nblk = G_NSA * N_SELECT
    blocks = refs[:nblk]
    q_ref, win_ref, nsa_new_ref, win_new_ref, gate_ref, ocmp_ref, o_ref = refs[nblk:]
    b = pl.program_id(0)
    q = q_ref[0]
    qf = q.astype(F32)
    rowg = lax.broadcasted_iota(jnp.int32, (H_NSA, 1), 0) // R_NSA
    ks_new = nsa_new_ref[0, :, 2 * LANES:3 * LANES]
    vs_new = nsa_new_ref[0, :, 3 * LANES:4 * LANES]
    s_new = jnp.sum(qf * ks_new, axis=1, keepdims=True)
    colblk = lax.broadcasted_iota(jnp.int32, (1, N_SELECT * SEL_BLOCK), 1) // SEL_BLOCK
    o_sel = None
    for g in range(G_NSA):
        ks = jnp.concatenate([blocks[g * N_SELECT + k][0, :, 0:LANES] for k in range(N_SELECT)], axis=0)
        vs = jnp.concatenate([blocks[g * N_SELECT + k][0, :, LANES:2 * LANES] for k in range(N_SELECT)], axis=0)
        s = _dot_nt(q, ks.astype(BF16))
        valid = jnp.zeros(colblk.shape, jnp.int32)
        for k in range(N_SELECT):
            ok = jnp.where(idx_ref[b, g, k] < n_past_blocks, 1, 0)
            valid = jnp.where(colblk == k, ok, valid)
        valid = valid > 0
        sm = jnp.where(valid, s, NEG_INF)
        m = jnp.maximum(jnp.max(sm, axis=1, keepdims=True), s_new)
        e = jnp.where(valid, jnp.exp(sm - m), 0.0)
        e_new = jnp.exp(s_new - m)
        l = jnp.sum(e, axis=1, keepdims=True) + e_new
        og = (_dot(e.astype(BF16), vs.astype(BF16)) + e_new * vs_new) / l
        o_sel = og if o_sel is None else jnp.where(rowg == g, og, o_sel)

    kw = win_ref[0, :, 0:LANES]
    vw = win_ref[0, :, LANES:2 * LANES]
    kw_new = win_new_ref[0, :, 0:LANES]
    vw_new = win_new_ref[0, :, LANES:2 * LANES]
    sw = _dot_nt(q, kw.astype(BF16))
    sw_new = jnp.sum(qf * kw_new, axis=1, keepdims=True)
    diff = win_buf - lax.broadcasted_iota(jnp.int32, (1, win_buf), 1)
    validw = (diff >= 0) & (diff < WINDOW)
    smw = jnp.where(validw, sw, NEG_INF)
    mw = jnp.maximum(jnp.max(smw, axis=1, keepdims=True), sw_new)
    ew = jnp.where(validw, jnp.exp(smw - mw), 0.0)
    ew_new = jnp.exp(sw_new - mw)
    lw = jnp.sum(ew, axis=1, keepdims=True) + ew_new
    o_win = (_dot(ew.astype(BF16), vw.astype(BF16)) + ew_new * vw_new) / lw

    gt = gate_ref[0]
    o_ref[0] = gt[:, 0:1] * ocmp_ref[0] + gt[:, 1:2] * o_sel + gt[:, 2:3] * o_win


def _nsa_decode_attend(idx, table, cache_half, q2, win3, nsa_new, win_new, gates, ocmp, n_past_blocks):
    nb = q2.shape[0]
    win_buf = win3.shape[1]
    halves = ROWS_PER_PAGE // SEL_BLOCK

    def blk_map(b, idx_ref, tbl_ref, g, k):
        j = jnp.minimum(idx_ref[b, g, k], n_past_blocks - 1)
        return (tbl_ref[b, j // halves] * halves + j % halves, 0, 1)

    blk_specs = [pl.BlockSpec((1, SEL_BLOCK, 2 * LANES), functools.partial(blk_map, g=g, k=k))
                 for g in range(G_NSA) for k in range(N_SELECT)]
    per_b = lambda shape: pl.BlockSpec((1,) + shape, lambda b, i, t: (b, 0, 0))
    grid_spec = pltpu.PrefetchScalarGridSpec(
        num_scalar_prefetch=2,
        grid=(nb,),
        in_specs=blk_specs + [per_b((H_NSA, LANES)), per_b((win_buf, 2 * LANES)), per_b((1, 4 * LANES)),
                              per_b((1, 2 * LANES)), per_b((H_NSA, LANES)), per_b((H_NSA, LANES))],
        out_specs=per_b((H_NSA, LANES)),
    )
    return pl.pallas_call(
        functools.partial(_nsa_decode_attend_kernel, n_past_blocks=n_past_blocks, win_buf=win_buf),
        grid_spec=grid_spec,
        out_shape=jax.ShapeDtypeStruct((nb, H_NSA, LANES), F32),
        compiler_params=_params("parallel"),
        name="nsa_decode_attend",
    )(idx, table, *([cache_half] * (G_NSA * N_SELECT)), q2, win3, nsa_new, win_new, gates, ocmp)


def _outproj_kernel(h_ref, a_ref, b_ref, wa_ref, wb_ref, g_ref, o_ref):
    y = _dot(a_ref[...].astype(BF16), wa_ref[...]) + _dot(b_ref[...].astype(BF16), wb_ref[...])
    o_ref[...] = h_ref[...] + _rms(y, g_ref[...])


def _outproj(h, a, b, wa, wb, gain, tm):
    m = h.shape[0]
    const = lambda i: (0, 0)
    row = lambda i: (i, 0)
    return pl.pallas_call(
        _outproj_kernel,
        grid=(m // tm,),
        in_specs=[pl.BlockSpec((tm, D_MODEL), row), pl.BlockSpec((tm, a.shape[1]), row),
                  pl.BlockSpec((tm, b.shape[1]), row), pl.BlockSpec(wa.shape, const), pl.BlockSpec(wb.shape, const),
                  pl.BlockSpec((1, D_MODEL), const)],
        out_specs=pl.BlockSpec((tm, D_MODEL), row),
        out_shape=jax.ShapeDtypeStruct((m, D_MODEL), F32),
        compiler_params=_params("parallel"),
        name="mixer_out",
    )(h, a, b, wa, wb, gain)


def _ple_kernel(h_ref, p_ref, gpre_ref, wg_ref, wp_ref, gpost_ref, o_ref):
    h = h_ref[...]
    gate = jax.nn.sigmoid(_dot(_rms(h, gpre_ref[...]).astype(BF16), wg_ref[...]))
    o_ref[...] = h + _rms(gate * _dot(p_ref[...].astype(BF16), wp_ref[...]), gpost_ref[...])


def _ple(h, p, gpre, wg, wp, gpost, tm):
    m = h.shape[0]
    const = lambda i: (0, 0)
    row = lambda i: (i, 0)
    return pl.pallas_call(
        _ple_kernel,
        grid=(m // tm,),
        in_specs=[pl.BlockSpec((tm, D_MODEL), row), pl.BlockSpec((tm, PLE_DIM), row),
                  pl.BlockSpec((1, D_MODEL), const), pl.BlockSpec(wg.shape, const), pl.BlockSpec(wp.shape, const),
                  pl.BlockSpec((1, D_MODEL), const)],
        out_specs=pl.BlockSpec((tm, D_MODEL), row),
        out_shape=jax.ShapeDtypeStruct((m, D_MODEL), F32),
        compiler_params=_params("parallel"),
        name="ple",
    )(h, p, gpre, wg, wp, gpost)


def _split_in_cols(w):
    outs, off = [], 0
    for n in IN_SPLITS:
        outs.append(w[:, off:off + n])
        off += n
    return outs


def _prep_mixer_weights(w_in, w_a2, b_a):
    q_g, k_g, v_g, r_g, a_lr, q_n, kv_n, gate_n = _split_in_cols(w_in)
    pad_cols = lambda w, n: jnp.pad(w, ((0, 0), (0, n - w.shape[1])))
    wn = jnp.concatenate([q_g, k_g, v_g, r_g, pad_cols(a_lr, LANES), kv_n], axis=1).astype(BF16)
    gd = G_NSA * D_NSA
    v_sel = kv_n[:, 3 * gd:4 * gd]
    v_win = kv_n[:, 5 * gd:6 * gd]
    wt = jnp.concatenate([q_n, v_sel, v_win, pad_cols(gate_n, N_GATE_ROWS)], axis=1).T.astype(BF16)
    wa2 = jnp.pad(w_a2, ((0, LANES - GLA_RANK), (0, 0))).astype(BF16)
    return wn, wt, wa2, b_a.reshape(1, -1)


def _row_tile(m):
    return ROW_TILE if m % ROW_TILE == 0 else m


def _layer(x2, p2, mixer, w):
    tm = _row_tile(x2.shape[0])
    h = _ffn(x2, w["f1_pre"], w["f1_in"], w["f1_out"], w["f1_post"], tm)
    o_gla, o_nsa, extras = mixer(h)
    h = _outproj(h, o_gla, o_nsa, w["wo_gla"], w["wo_nsa"], w["m_post"], tm)
    h = _ffn(h, w["f2_pre"], w["f2_in"], w["f2_out"], w["f2_post"], tm)
    h = _ple(h, p2, w["ple_pre"], w["ple_gate"], w["ple_proj"], w["ple_post"], tm)
    return h, extras


def _mixer_prompt(h, w, nb, t):
    tm = _row_tile(h.shape[0])
    tabs = _rope_tables(jnp.arange(t, dtype=jnp.int32))
    (qk, v, r, la, nsa, win, qraw_t, qrot_t, vsel_t, vwin_t, gate_t) = _proj(
        h, w["m_pre"], w["wn"], w["wt"], w["wa2"], w["ba"], tabs, tm, t // tm)
    o_gla, s_fin = _gla_prompt(qk, la, v, r, w["gla_gain"], nb, t)
    pages_per_seq = t // ROWS_PER_PAGE
    table = jnp.arange(nb * pages_per_seq, dtype=jnp.int32).reshape(nb, pages_per_seq)
    abk, abv = _cmp_partial(nsa.reshape(nb * pages_per_seq, ROWS_PER_PAGE, 4 * LANES), table, w["cmp_wk"], w["cmp_wv"])
    kc, _, vct = _cmp_finish(abk, abv, w["cmp_posk"], w["cmp_w1k"], w["cmp_w2k"],
                             w["cmp_posv"], w["cmp_w1v"], w["cmp_w2v"], w["cmp_w2vt"])
    o_nsa = _nsa_prompt(qraw_t, qrot_t, gate_t, kc, vct, nsa, vsel_t, win, vwin_t, nb, t)
    s_t = s_fin.reshape(nb, H_GLA, DV_GLA, 2, DK_GLA)
    s_own = jnp.stack([s_t[:, hh, :, hh % 2, :] for hh in range(H_GLA)], axis=1)
    gla_state = jnp.swapaxes(s_own, -1, -2)
    keep = min(WINDOW, t)
    nsa_rows = nsa.reshape(nb, t, 4, G_NSA, D_NSA)
    win_rows = win.reshape(nb, t, 2, G_NSA, D_NSA)[:, t - keep:]
    return o_gla, o_nsa, (nsa_rows, win_rows, gla_state)


def _mixer_sample(h, w, cache_l, win_l, gla_l, page_table):
    nb = h.shape[0]
    n_pages = page_table.shape[1]
    past_len = n_pages * cache_l.shape[1]
    pos = past_len
    tabs = _rope_tables(jnp.full((nb,), pos, dtype=jnp.int32))
    (qk, v, r, la, nsa, win, qraw_t, qrot_t, _, _, gate_t) = _proj(
        h, w["m_pre"], w["wn"], w["wt"], w["wa2"], w["ba"], tabs, nb, 1)

    nq = H_GLA * DK_GLA
    col = lambda a: a.reshape(nb, H_GLA, DK_GLA, 1)
    rowv = lambda a: a.reshape(nb, H_GLA, 1, DV_GLA)
    o_gla, gla_state = _gla_step(col(qk[:, :nq]), col(qk[:, nq:]), col(la), rowv(v), rowv(r),
                                 gla_l.astype(F32), w["gla_gain"])
    o_gla = o_gla.reshape(nb, H_GLA * DV_GLA)

    cache3 = cache_l.reshape(cache_l.shape[0], ROWS_PER_PAGE, 4 * LANES)
    abk, abv = _cmp_partial(cache3, page_table, w["cmp_wk"], w["cmp_wv"])
    kc, _, vct = _cmp_finish(abk, abv, w["cmp_posk"], w["cmp_w1k"], w["cmp_w2k"],
                             w["cmp_posv"], w["cmp_w1v"], w["cmp_w2v"], w["cmp_w2vt"])

    hg = (jnp.arange(H_NSA) // R_NSA)[None, :, None]

    def group_pad(q_t):
        q8 = q_t.T.reshape(nb, H_NSA, D_NSA)
        return jnp.concatenate([jnp.where(hg == 0, q8, 0), jnp.where(hg == 1, q8, 0)], axis=-1)

    q2_raw = group_pad(qraw_t)
    q2_rot = group_pad(qrot_t)
    q2_raw_t = jnp.pad(jnp.swapaxes(q2_raw, 1, 2), ((0, 0), (0, 0), (0, LANES - H_NSA)))
    n_blocks = -(-(past_len + 1) // SEL_BLOCK)
    idx_pad, ocmp_t = _nsa_decode_select(q2_raw_t, kc, vct, pos, n_blocks)
    idx = jnp.stack([idx_pad[:, :, g] for g in range(G_NSA)], axis=1)
    ocmp = jnp.swapaxes(ocmp_t, 1, 2)[:, :H_NSA, :]
    gates = jnp.pad(gate_t[:3 * H_NSA].T.reshape(nb, H_NSA, 3), ((0, 0), (0, 0), (0, LANES - 3)))
    wb = win_l.shape[1]
    cache_half = cache_l.reshape(cache_l.shape[0] * (ROWS_PER_PAGE // SEL_BLOCK), SEL_BLOCK, 4 * LANES)
    o8 = _nsa_decode_attend(idx, page_table, cache_half, q2_rot, win_l.reshape(nb, wb, 2 * LANES),
                            nsa.reshape(nb, 1, 4 * LANES), win.reshape(nb, 1, 2 * LANES), gates, ocmp,
                            past_len // SEL_BLOCK)
    o8 = o8.reshape(nb, H_NSA, G_NSA, D_NSA)
    o_nsa = jnp.concatenate([o8[:, :R_NSA, 0], o8[:, R_NSA:, 1]], axis=1).reshape(nb, H_NSA * D_NSA)

    nsa_rows = nsa.reshape(nb, 1, 4, G_NSA, D_NSA)
    win_new = win.reshape(nb, 1, 2, G_NSA, D_NSA)
    kw = jnp.concatenate([win_l, win_new.astype(win_l.dtype)], axis=1)
    keep = min(WINDOW, wb + 1)
    return o_gla, o_nsa, (nsa_rows, kw[:, wb + 1 - keep:], gla_state.astype(gla_l.dtype))


def kernel(x_prompt, x_sample, cache_nsa, state_win, state_gla, page_table, p_prompt, p_sample,
           ffn1_norm_pre, ffn1_norm_post, ffn1_w_in, ffn1_w_out,
           mix_norm_pre, mix_norm_post, w_mix_in, w_gla_a2, b_gla_a, gla_out_norm,
           cmp_pos_k, w_cmp_k1, w_cmp_k2, cmp_pos_v, w_cmp_v1, w_cmp_v2, w_mix_out,
           ffn2_norm_pre, ffn2_norm_post, ffn2_w_in, ffn2_w_out,
           ple_norm_pre, ple_w_gate, ple_w_proj, ple_norm_post):
    nb, t, _ = x_prompt.shape
    ns = x_sample.shape[0]
    depth = ffn1_w_in.shape[0]
    hp = x_prompt.reshape(nb * t, D_MODEL)
    hs = x_sample.reshape(ns, D_MODEL)
    outs = [[] for _ in range(6)]
    for i in range(depth):
        wn, wt, wa2, ba = _prep_mixer_weights(w_mix_in[i], w_gla_a2[i], b_gla_a[i])
        cmp_wk, cmp_posk, cmp_w1k, cmp_w2k, _ = _cmp_weights(cmp_pos_k[i], w_cmp_k1[i], w_cmp_k2[i])
        cmp_wv, cmp_posv, cmp_w1v, cmp_w2v, cmp_w2vt = _cmp_weights(cmp_pos_v[i], w_cmp_v1[i], w_cmp_v2[i])
        gla_w = H_GLA * DV_GLA
        w = dict(
            f1_pre=ffn1_norm_pre[i][None], f1_post=ffn1_norm_post[i][None],
            f1_in=ffn1_w_in[i].astype(BF16), f1_out=ffn1_w_out[i].astype(BF16),
            m_pre=mix_norm_pre[i][None], m_post=mix_norm_post[i][None],
            wn=wn, wt=wt, wa2=wa2, ba=ba, gla_gain=gla_out_norm[i][None],
            cmp_wk=cmp_wk, cmp_posk=cmp_posk, cmp_w1k=cmp_w1k, cmp_w2k=cmp_w2k,
            cmp_wv=cmp_wv, cmp_posv=cmp_posv, cmp_w1v=cmp_w1v, cmp_w2v=cmp_w2v, cmp_w2vt=cmp_w2vt,
            wo_gla=w_mix_out[i][:gla_w].astype(BF16), wo_nsa=w_mix_out[i][gla_w:].astype(BF16),
            f2_pre=ffn2_norm_pre[i][None], f2_post=ffn2_norm_post[i][None],
            f2_in=ffn2_w_in[i].astype(BF16), f2_out=ffn2_w_out[i].astype(BF16),
            ple_pre=ple_norm_pre[i][None], ple_post=ple_norm_post[i][None],
            ple_gate=ple_w_gate[i].astype(BF16), ple_proj=ple_w_proj[i].astype(BF16),
        )
        hp, (r_p, w_p, s_p) = _layer(hp, p_prompt[i].reshape(nb * t, PLE_DIM),
                                     functools.partial(_mixer_prompt, w=w, nb=nb, t=t), w)
        hs, (r_s, w_s, s_s) = _layer(hs, p_sample[i].reshape(ns, PLE_DIM),
                                     functools.partial(_mixer_sample, w=w, cache_l=cache_nsa[i], win_l=state_win[i],
                                                       gla_l=state_gla[i], page_table=page_table), w)
        for lst, val in zip(outs, (r_p, w_p, s_p, r_s, w_s, s_s)):
            lst.append(val)
    return (hp.reshape(nb, t, D_MODEL), hs.reshape(ns, 1, D_MODEL), *[jnp.stack(o) for o in outs])
```

```python
import functools

import numpy as np
import jax
import jax.numpy as jnp
from jax import lax
from jax.experimental import pallas as pl
from jax.experimental.pallas import tpu as pltpu

F32 = jnp.float32
BF16 = jnp.bfloat16

D_MODEL = 1024
PLE_DIM = 256
D_FF = 2816
EPS = 1e-6
H_GLA = 4
DK_GLA = 64
DV_GLA = 128
GLA_RANK = 16
GLA_GATE_TEMP = 16.0
GLA_CHUNK = 64
H_NSA = 8
G_NSA = 2
R_NSA = H_NSA // G_NSA
D_NSA = 64
CMP_LEN = 32
CMP_STRIDE = 16
CMP_HIDDEN = 128
SEL_BLOCK = 64
N_SELECT = 16
WINDOW = 512
FORCE_SCORE = 1e4
NEG_INF = -1e30
ATTN_SCALE = D_NSA ** -0.5
ROPE_THETA = 500000.0
ROPE_DIM = D_NSA // 4
ROPE_HALF = ROPE_DIM // 2
IN_SPLITS = (H_GLA * DK_GLA, H_GLA * DK_GLA, H_GLA * DV_GLA, H_GLA * DV_GLA, GLA_RANK,
             H_NSA * D_NSA, 6 * G_NSA * D_NSA, 3 * H_NSA)

LANES = 128
SUBLANES = 8
VMEM_LIMIT = 56 * 1024 * 1024

ROW_TILE = 512
FF_CHUNK = 256
Q_TILE = 128
K_TILE = 128
PAGES_PER_STEP = 32
ROWS_PER_PAGE = 128
N_GLA_LEVELS = 6
PAD_BLOCKS = 384


def _params(*sem):
    return pltpu.CompilerParams(dimension_semantics=sem, vmem_limit_bytes=VMEM_LIMIT)


def _rms(x, g):
    return x * lax.rsqrt(jnp.mean(x * x, axis=-1, keepdims=True) + EPS) * g


def _dot(a, b):
    return jnp.dot(a, b, preferred_element_type=F32)


def _dot_nt(a, b):
    return lax.dot_general(a, b, (((1,), (1,)), ((), ())), preferred_element_type=F32)


def _dot_tn(a, b):
    return lax.dot_general(a, b, (((0,), (0,)), ((), ())), preferred_element_type=F32)


def _split_bf16(x, n):
    parts = []
    r = x
    for _ in range(n):
        p = r.astype(BF16)
        parts.append(p)
        r = r - p.astype(F32)
    return parts


def _dot01_left(m01, x, n):
    out = None
    for p in _split_bf16(x, n):
        t = _dot(m01, p)
        out = t if out is None else out + t
    return out


def _dot01_right(x, m01, n):
    out = None
    for p in _split_bf16(x, n):
        t = _dot(p, m01)
        out = t if out is None else out + t
    return out


def _masked_softmax_rows(s, valid):
    sm = jnp.where(valid, s, NEG_INF)
    m = jnp.max(sm, axis=0, keepdims=True)
    e = jnp.exp(sm - m)
    return jnp.where(valid, e / jnp.sum(e, axis=0, keepdims=True), 0.0)


def _masked_softmax_lanes(s, valid):
    sm = jnp.where(valid, s, NEG_INF)
    m = jnp.max(sm, axis=-1, keepdims=True)
    e = jnp.exp(sm - m)
    return jnp.where(valid, e / jnp.sum(e, axis=-1, keepdims=True), 0.0)


def _ffn_kernel(x_ref, gpre_ref, wg_ref, wu_ref, wo_ref, gpost_ref, o_ref):
    x = x_ref[...]
    xn = _rms(x, gpre_ref[...]).astype(BF16)
    acc = jnp.zeros(x.shape, F32)
    for c in range(D_FF // FF_CHUNK):
        sl = slice(c * FF_CHUNK, (c + 1) * FF_CHUNK)
        g = _dot(xn, wg_ref[:, sl])
        u = _dot(xn, wu_ref[:, sl])
        a = (jax.nn.silu(g) * u).astype(BF16)
        acc = acc + _dot(a, wo_ref[sl, :])
    o_ref[...] = x + 0.5 * _rms(acc, gpost_ref[...])


def _ffn(x, gpre, w_in, w_out, gpost, tm):
    m = x.shape[0]
    const = lambda i: (0, 0)
    return pl.pallas_call(
        _ffn_kernel,
        grid=(m // tm,),
        in_specs=[
            pl.BlockSpec((tm, D_MODEL), lambda i: (i, 0)),
            pl.BlockSpec((1, D_MODEL), const),
            pl.BlockSpec((D_MODEL, D_FF), const),
            pl.BlockSpec((D_MODEL, D_FF), lambda i: (0, 1)),
            pl.BlockSpec((D_FF, D_MODEL), const),
            pl.BlockSpec((1, D_MODEL), const),
        ],
        out_specs=pl.BlockSpec((tm, D_MODEL), lambda i: (i, 0)),
        out_shape=jax.ShapeDtypeStruct((m, D_MODEL), F32),
        compiler_params=_params("parallel"),
        name="ffn",
    )(x, gpre, w_in, w_in, w_out, gpost)


N_GLA_COLS = 2 * H_GLA * DK_GLA + 2 * H_GLA * DV_GLA
N_KV_COLS = 6 * G_NSA * D_NSA
WN_COLS = N_GLA_COLS + LANES + N_KV_COLS
N_Q_COLS = H_NSA * D_NSA
N_GATE_ROWS = 32
WT_ROWS = N_Q_COLS + 2 * LANES + N_GATE_ROWS


def _rope_rows(x, cos, sin):
    out = []
    for h in range(H_NSA):
        b = h * D_NSA
        x1 = x[b:b + ROPE_HALF]
        x2 = x[b + ROPE_HALF:b + ROPE_DIM]
        out += [x1 * cos - x2 * sin, x2 * cos + x1 * sin, x[b + ROPE_DIM:b + D_NSA]]
    return jnp.concatenate(out, axis=0)


def _rope_lanes(x, c, s1, s2):
    return x * c + pltpu.roll(x, LANES - ROPE_HALF, 1) * s1 + pltpu.roll(x, ROPE_HALF, 1) * s2


def _proj_kernel(h_ref, g_ref, wn_ref, wt_ref, wa2_ref, ba_ref, rc_ref, rs1_ref, rs2_ref, cos_ref, sin_ref,
                 qk_ref, v_ref, r_ref, la_ref, nsa_ref, win_ref,
                 qraw_ref, qrot_ref, vsel_ref, vwin_ref, gate_ref):
    xn = _rms(h_ref[...], g_ref[...]).astype(BF16)
    nqk = 2 * H_GLA * DK_GLA
    nv = H_GLA * DV_GLA
    z = _dot(xn, wn_ref[:, 0:nqk])
    qk_ref[:, 0:nqk // 2] = z[:, 0:nqk // 2] * (DK_GLA ** -0.5)
    qk_ref[:, nqk // 2:nqk] = z[:, nqk // 2:nqk]
    v_ref[...] = _dot(xn, wn_ref[:, nqk:nqk + nv])
    r_ref[...] = _dot(xn, wn_ref[:, nqk + nv:N_GLA_COLS])
    a_lr = _dot(xn, wn_ref[:, N_GLA_COLS:N_GLA_COLS + LANES])
    xa = _dot(a_lr.astype(BF16), wa2_ref[...]) + ba_ref[...]
    la_ref[...] = (jnp.minimum(xa, 0.0) - jnp.log1p(jnp.exp(-jnp.abs(xa)))) * (1.0 / GLA_GATE_TEMP)
    kv0 = N_GLA_COLS + LANES
    kv = _dot(xn, wn_ref[:, kv0:kv0 + N_KV_COLS])
    rc, rs1, rs2 = rc_ref[...], rs1_ref[...], rs2_ref[...]
    nsa_ref[:, 0:256] = kv[:, 0:256]
    nsa_ref[:, 256:384] = _rope_lanes(kv[:, 256:384], rc, rs1, rs2)
    nsa_ref[:, 384:512] = kv[:, 384:512]
    win_ref[:, 0:128] = _rope_lanes(kv[:, 512:640], rc, rs1, rs2)
    win_ref[:, 128:256] = kv[:, 640:768]
    zt = _dot_nt(wt_ref[...], xn)
    q = zt[0:N_Q_COLS] * ATTN_SCALE
    qraw_ref[...] = q.astype(BF16)
    qrot_ref[...] = _rope_rows(q, cos_ref[...], sin_ref[...]).astype(BF16)
    vsel_ref[...] = zt[N_Q_COLS:N_Q_COLS + LANES].astype(BF16)
    vwin_ref[...] = zt[N_Q_COLS + LANES:N_Q_COLS + 2 * LANES].astype(BF16)
    gate_ref[...] = jax.nn.sigmoid(zt[N_Q_COLS + 2 * LANES:WT_ROWS])


def _proj(h, gain, wn, wt, wa2, ba, tabs, tm, tiles_per_seq):
    m = h.shape[0]
    rc, rs1, rs2, cos_t, sin_t = tabs
    const = lambda i: (0, 0)
    row = lambda i: (i, 0)
    col = lambda i: (0, i)
    tab_row = lambda i: (i % tiles_per_seq, 0)
    tab_col = lambda i: (0, i % tiles_per_seq)
    outs = [
        (2 * H_GLA * DK_GLA, F32), (H_GLA * DV_GLA, F32), (H_GLA * DV_GLA, F32), (H_GLA * DK_GLA, F32),
        (4 * G_NSA * D_NSA, F32), (2 * G_NSA * D_NSA, F32),
    ]
    outs_t = [(N_Q_COLS, BF16), (N_Q_COLS, BF16), (LANES, BF16), (LANES, BF16), (N_GATE_ROWS, F32)]
    return pl.pallas_call(
        _proj_kernel,
        grid=(m // tm,),
        in_specs=[
            pl.BlockSpec((tm, D_MODEL), row),
            pl.BlockSpec((1, D_MODEL), const),
            pl.BlockSpec((D_MODEL, WN_COLS), const),
            pl.BlockSpec((WT_ROWS, D_MODEL), const),
            pl.BlockSpec((LANES, H_GLA * DK_GLA), const),
            pl.BlockSpec((1, H_GLA * DK_GLA), const),
            pl.BlockSpec((tm, LANES), tab_row),
            pl.BlockSpec((tm, LANES), tab_row),
            pl.BlockSpec((tm, LANES), tab_row),
            pl.BlockSpec((ROPE_HALF, tm), tab_col),
            pl.BlockSpec((ROPE_HALF, tm), tab_col),
        ],
        out_specs=[pl.BlockSpec((tm, n), row) for n, _ in outs] + [pl.BlockSpec((n, tm), col) for n, _ in outs_t],
        out_shape=[jax.ShapeDtypeStruct((m, n), d) for n, d in outs]
        + [jax.ShapeDtypeStruct((n, m), d) for n, d in outs_t],
        compiler_params=_params("parallel"),
        name="mixer_proj",
    )(h, gain, wn, wt, wa2, ba, rc, rs1, rs2, cos_t, sin_t)


def _rope_tables(pos):
    inv_freq = ROPE_THETA ** (-jnp.arange(ROPE_HALF, dtype=F32) * 2.0 / ROPE_DIM)
    ang = pos.astype(F32)[:, None] * inv_freq[None, :]
    cos, sin = jnp.cos(ang), jnp.sin(ang)
    n = pos.shape[0]
    one = jnp.ones((n, D_NSA - ROPE_DIM), F32)
    zero = jnp.zeros((n, D_NSA - ROPE_DIM), F32)
    zh = jnp.zeros((n, ROPE_HALF), F32)
    c = jnp.concatenate([cos, cos, one], axis=1)
    s1 = jnp.concatenate([-sin, zh, zero], axis=1)
    s2 = jnp.concatenate([zh, sin, zero], axis=1)
    dup = lambda t: jnp.concatenate([t, t], axis=1)
    return dup(c), dup(s1), dup(s2), cos.T, sin.T


def _gla_constants():
    c = GLA_CHUNK
    t = np.arange(c)
    low = (t[None, :] <= t[:, None]).astype(np.float32)
    mats, masks = [low], []
    for lev in range(1, N_GLA_LEVELS + 1):
        seg = (2 * c) >> lev
        half = seg // 2
        mid = (t // seg) * seg + half
        mats.append(low[mid])
        same = (t[:, None] // seg) == (t[None, :] // seg)
        masks.append((same & ((t[:, None] % seg) >= half) & ((t[None, :] % seg) < half)).astype(np.float32))
    masks.append(np.eye(c, dtype=np.float32))
    return np.concatenate(mats, axis=0), np.stack(masks)


def _gla_kernel(qk_ref, la_ref, v_ref, r_ref, gain_ref, big_ref, mask_ref, o_ref, sfin_ref, st_ref):
    c = pl.program_id(1)
    ch = GLA_CHUNK

    @pl.when(c == 0)
    def _():
        st_ref[...] = jnp.zeros(st_ref.shape, F32)

    lane = lax.broadcasted_iota(jnp.int32, (1, LANES), 1)
    head_mask = [jnp.where(lane < DK_GLA, 1.0, 0.0), jnp.where(lane >= DK_GLA, 1.0, 0.0)]
    big = big_ref[...]
    gain = gain_ref[...]
    nq = H_GLA * DK_GLA
    for p in range(H_GLA // 2):
        sl = slice(p * LANES, (p + 1) * LANES)
        q = qk_ref[:, sl]
        k = qk_ref[:, nq + p * LANES:nq + (p + 1) * LANES]
        allb = _dot01_left(big, la_ref[:, sl], 3)
        b = allb[0:ch]
        b_last = b[ch - 1:ch]
        attn = [jnp.zeros((ch, ch), F32), jnp.zeros((ch, ch), F32)]
        for lev in range(N_GLA_LEVELS + 1):
            if lev < N_GLA_LEVELS:
                ref = allb[(lev + 1) * ch:(lev + 2) * ch]
                ql = q * jnp.exp(jnp.minimum(b - ref, 0.0))
                kl = k * jnp.exp(jnp.minimum(ref - b, 0.0))
            else:
                ql, kl = q, k
            qq = jnp.concatenate([ql * head_mask[0], ql * head_mask[1]], axis=0).astype(BF16)
            s = _dot_nt(qq, kl.astype(BF16))
            mk = mask_ref[lev]
            attn[0] = attn[0] + mk * s[0:ch]
            attn[1] = attn[1] + mk * s[ch:2 * ch]
        q0 = q * jnp.exp(b)
        k_hat = (k * jnp.exp(b_last - b)).astype(BF16)
        decay = jnp.exp(b_last)
        for hh in range(2):
            h = 2 * p + hh
            hs = slice(h * DV_GLA, (h + 1) * DV_GLA)
            vh = v_ref[:, hs].astype(BF16)
            st = st_ref[h]
            o = _dot(attn[hh].astype(BF16), vh) + _dot_nt((q0 * head_mask[hh]).astype(BF16), st.astype(BF16))
            st_ref[h] = st * decay + _dot_tn(vh, k_hat)
            o_ref[:, hs] = _rms(o, gain) * jax.nn.silu(r_ref[:, hs])

    @pl.when(c == pl.num_programs(1) - 1)
    def _():
        sfin_ref[0] = st_ref[...]


def _gla_prompt(qk, la, v, r, gain, nb, t):
    big, masks = _gla_constants()
    nc = t // GLA_CHUNK
    row = lambda b, c: (b * nc + c, 0)
    const2 = lambda b, c: (0, 0)
    m = nb * t
    return pl.pallas_call(
        _gla_kernel,
        grid=(nb, nc),
        in_specs=[
            pl.BlockSpec((GLA_CHUNK, 2 * H_GLA * DK_GLA), row),
            pl.BlockSpec((GLA_CHUNK, H_GLA * DK_GLA), row),
            pl.BlockSpec((GLA_CHUNK, H_GLA * DV_GLA), row),
            pl.BlockSpec((GLA_CHUNK, H_GLA * DV_GLA), row),
            pl.BlockSpec((1, DV_GLA), const2),
            pl.BlockSpec(big.shape, const2),
            pl.BlockSpec(masks.shape, lambda b, c: (0, 0, 0)),
        ],
        out_specs=[
            pl.BlockSpec((GLA_CHUNK, H_GLA * DV_GLA), row),
            pl.BlockSpec((1, H_GLA, DV_GLA, LANES), lambda b, c: (b, 0, 0, 0)),
        ],
        out_shape=[
            jax.ShapeDtypeStruct((m, H_GLA * DV_GLA), F32),
            jax.ShapeDtypeStruct((nb, H_GLA, DV_GLA, LANES), F32),
        ],
        scratch_shapes=[pltpu.VMEM((H_GLA, DV_GLA, LANES), F32)],
        compiler_params=_params("parallel", "arbitrary"),
        name="gla_scan",
    )(qk, la, v, r, gain, jnp.asarray(big, BF16), jnp.asarray(masks, F32))


def _gla_step_kernel(q_ref, k_ref, a_ref, v_ref, r_ref, s_ref, gain_ref, o_ref, sn_ref):
    for h in range(H_GLA):
        s_new = jnp.exp(a_ref[0, h]) * s_ref[0, h] + k_ref[0, h] * v_ref[0, h]
        sn_ref[0, h] = s_new
        o = jnp.sum(q_ref[0, h] * s_new, axis=0, keepdims=True)
        o_ref[0, h] = _rms(o, gain_ref[...]) * jax.nn.silu(r_ref[0, h])


def _gla_step(qcol, kcol, acol, vrow, rrow, state, gain):
    nb = state.shape[0]
    col = pl.BlockSpec((1, H_GLA, DK_GLA, 1), lambda b: (b, 0, 0, 0))
    rowspec = pl.BlockSpec((1, H_GLA, 1, DV_GLA), lambda b: (b, 0, 0, 0))
    stspec = pl.BlockSpec((1, H_GLA, DK_GLA, DV_GLA), lambda b: (b, 0, 0, 0))
    return pl.pallas_call(
        _gla_step_kernel,
        grid=(nb,),
        in_specs=[col, col, col, rowspec, rowspec, stspec, pl.BlockSpec((1, DV_GLA), lambda b: (0, 0))],
        out_specs=[rowspec, stspec],
        out_shape=[jax.ShapeDtypeStruct((nb, H_GLA, 1, DV_GLA), F32),
                   jax.ShapeDtypeStruct((nb, H_GLA, DK_GLA, DV_GLA), F32)],
        compiler_params=_params("parallel"),
        name="gla_step",
    )(qcol, kcol, acol, vrow, rrow, state, gain)


HALF_ROWS = CMP_STRIDE
CHUNKS_PER_PAGE = ROWS_PER_PAGE // HALF_ROWS


def _cmp_partial_kernel(tbl_ref, *refs, pps, feature_major):
    del tbl_ref
    pages_k = refs[:pps]
    pages_v = refs[pps:2 * pps]
    wk_ref, wv_ref, abk_ref, abv_ref, xk_ref, xv_ref = refs[2 * pps:2 * pps + 6]
    if feature_major:
        tk_ref, tv_ref = refs[2 * pps + 6:]
        for i in range(pps):
            tk_ref[i] = pages_k[i][0].T
            tv_ref[i] = pages_v[i][0].T
    for i in range(pps):
        rows = slice(i * CHUNKS_PER_PAGE, (i + 1) * CHUNKS_PER_PAGE)
        for l in range(HALF_ROWS):
            lanes = slice(l * LANES, (l + 1) * LANES)
            src = pl.ds(l, CHUNKS_PER_PAGE, stride=HALF_ROWS)
            if feature_major:
                xk_ref[rows, lanes] = tk_ref[i, src, :]
                xv_ref[rows, lanes] = tv_ref[i, src, :]
            else:
                xk_ref[rows, lanes] = pages_k[i][0, src, :]
                xv_ref[rows, lanes] = pages_v[i][0, src, :]
    abk_ref[0] = _dot(xk_ref[...].astype(BF16), wk_ref[...])
    abv_ref[0] = _dot(xv_ref[...].astype(BF16), wv_ref[...])


def _cmp_partial(pages, table, wk, wv, feature_major):
    nb, npages = table.shape
    pps = min(PAGES_PER_STEP, npages)
    nsteps = npages // pps
    nch = npages * CHUNKS_PER_PAGE
    step_chunks = pps * CHUNKS_PER_PAGE

    def page_map(b, s, tbl, i, kv):
        page = tbl[b, s * pps + i]
        return (page, kv, 0) if feature_major else (page, 0, kv)

    page_specs = [pl.BlockSpec((1, ROWS_PER_PAGE, LANES), functools.partial(page_map, i=i, kv=kv))
                  for kv in range(2) for i in range(pps)]
    wspec = pl.BlockSpec(wk.shape, lambda b, s, tbl: (0, 0))
    ospec = pl.BlockSpec((1, step_chunks, 4 * CMP_HIDDEN), lambda b, s, tbl: (b, s, 0))
    scratch = [pltpu.VMEM((step_chunks, HALF_ROWS * LANES), F32), pltpu.VMEM((step_chunks, HALF_ROWS * LANES), F32)]
    if feature_major:
        scratch += [pltpu.VMEM((pps, ROWS_PER_PAGE, LANES), F32), pltpu.VMEM((pps, ROWS_PER_PAGE, LANES), F32)]
    grid_spec = pltpu.PrefetchScalarGridSpec(
        num_scalar_prefetch=1,
        grid=(nb, nsteps),
        in_specs=page_specs + [wspec, wspec],
        out_specs=[ospec, ospec],
        scratch_shapes=scratch,
    )
    osh = jax.ShapeDtypeStruct((nb, nch, 4 * CMP_HIDDEN), F32)
    return pl.pallas_call(
        functools.partial(_cmp_partial_kernel, pps=pps, feature_major=feature_major),
        grid_spec=grid_spec,
        out_shape=[osh, osh],
        compiler_params=_params("parallel", "parallel"),
        name="cmp_partial",
    )(table, *([pages] * (2 * pps)), wk, wv)


def _cmp_finish_kernel(abk_ref, abv_ref, posk_ref, w1k_ref, w2k_ref, posv_ref, w1v_ref, w2v_ref, w2vt_ref,
                       kc_ref, vc_ref, vct_ref):
    nch = abk_ref.shape[1]
    pk = _dot(posk_ref[...], w1k_ref[...])[0:1]
    pv = _dot(posv_ref[...], w1v_ref[...])[0:1]
    abk = abk_ref[0]
    abv = abv_ref[0]
    kc = None
    vc = None
    for g in range(G_NSA):
        o = g * 2 * CMP_HIDDEN
        hk = jax.nn.gelu(abk[:, o:o + CMP_HIDDEN] + pltpu.roll(abk[:, o + CMP_HIDDEN:o + 2 * CMP_HIDDEN], nch - 1, 0) + pk)
        hv = jax.nn.gelu(abv[:, o:o + CMP_HIDDEN] + pltpu.roll(abv[:, o + CMP_HIDDEN:o + 2 * CMP_HIDDEN], nch - 1, 0) + pv)
        hk = hk.astype(BF16)
        hv = hv.astype(BF16)
        tk = _dot(hk, w2k_ref[g])
        tv = _dot(hv, w2v_ref[g])
        kc = tk if kc is None else kc + tk
        vc = tv if vc is None else vc + tv
        vct_ref[0, g * D_NSA:(g + 1) * D_NSA, :] = _dot_nt(w2vt_ref[...], hv).astype(BF16)
    kc_ref[0] = kc.astype(BF16)
    vc_ref[0] = vc.astype(BF16)


def _cmp_finish(abk, abv, posk, w1k, w2k_pad, posv, w1v, w2v_pad, w2vt):
    nb, nch, _ = abk.shape
    ab = pl.BlockSpec((1, nch, 4 * CMP_HIDDEN), lambda b: (b, 0, 0))
    c2 = lambda b: (0, 0)
    c3 = lambda b: (0, 0, 0)
    return pl.pallas_call(
        _cmp_finish_kernel,
        grid=(nb,),
        in_specs=[ab, ab,
                  pl.BlockSpec(posk.shape, c2), pl.BlockSpec(w1k.shape, c2), pl.BlockSpec(w2k_pad.shape, c3),
                  pl.BlockSpec(posv.shape, c2), pl.BlockSpec(w1v.shape, c2), pl.BlockSpec(w2v_pad.shape, c3),
                  pl.BlockSpec(w2vt.shape, c2)],
        out_specs=[pl.BlockSpec((1, nch, LANES), lambda b: (b, 0, 0)),
                   pl.BlockSpec((1, nch, LANES), lambda b: (b, 0, 0)),
                   pl.BlockSpec((1, LANES, nch), lambda b: (b, 0, 0))],
        out_shape=[jax.ShapeDtypeStruct((nb, nch, LANES), BF16),
                   jax.ShapeDtypeStruct((nb, nch, LANES), BF16),
                   jax.ShapeDtypeStruct((nb, LANES, nch), BF16)],
        compiler_params=_params("parallel"),
        name="cmp_finish",
    )(abk, abv, posk, w1k, w2k_pad, posv, w1v, w2v_pad, w2vt)


def _cmp_weights(pos, w1, w2):
    w = w1.reshape(2, HALF_ROWS, D_NSA, CMP_HIDDEN)
    z = jnp.zeros_like(w[0])
    blocks = []
    for g in range(G_NSA):
        cols = []
        for g2 in range(G_NSA):
            for half in range(2):
                cols.append(w[half] if g2 == g else z)
        blocks.append(jnp.concatenate(cols, axis=-1))
    wbig = jnp.stack(blocks, axis=1).reshape(HALF_ROWS * LANES, 4 * CMP_HIDDEN).astype(BF16)
    pos8 = jnp.broadcast_to(pos.reshape(1, CMP_LEN * D_NSA), (SUBLANES, CMP_LEN * D_NSA)).astype(BF16)
    z2 = jnp.zeros_like(w2)
    w2pad = jnp.stack([jnp.concatenate([w2, z2], axis=1), jnp.concatenate([z2, w2], axis=1)]).astype(BF16)
    return wbig, pos8, w1.astype(BF16), w2pad, w2.T.astype(BF16)


def _overlap_t(nc_pad, ns_pad, nc, ns):
    i = np.arange(nc_pad)[None, :] * CMP_STRIDE
    j = np.arange(ns_pad)[:, None] * SEL_BLOCK
    m = (i < j + SEL_BLOCK) & (i + CMP_LEN > j) & (np.arange(nc_pad)[None, :] < nc) & (np.arange(ns_pad)[:, None] < ns)
    return jnp.asarray(m.astype(np.float32), BF16)


def _rank_desc(x):
    nrow = x.shape[0]
    nblk = nrow // SUBLANES
    blocks = [x[v * SUBLANES:(v + 1) * SUBLANES] for v in range(nblk)]
    cnt = [jnp.zeros((SUBLANES, x.shape[1]), F32) for _ in range(nblk)]
    sub = lax.broadcasted_iota(jnp.int32, (SUBLANES, x.shape[1]), 0)
    for jp in range(nrow):
        row = x[jp:jp + 1]
        vb = jp // SUBLANES
        for v in range(nblk):
            ge = jnp.where(row >= blocks[v], 1.0, 0.0)
            gt = jnp.where(row > blocks[v], 1.0, 0.0)
            if v > vb:
                cnt[v] = cnt[v] + ge
            elif v < vb:
                cnt[v] = cnt[v] + gt
            else:
                cnt[v] = cnt[v] + jnp.where(sub > (jp % SUBLANES), ge, gt)
    return jnp.concatenate(cnt, axis=0)


def _nsa_prompt_kernel(qraw_ref, qrot_ref, gate_ref, kc_ref, vct_ref, ksel_ref, vselt_ref, kwin_ref, vwint_ref,
                       ovl_ref, o_ref):
    qb = pl.program_id(1)
    nlane = R_NSA * Q_TILE
    tpos = qb * Q_TILE + lax.broadcasted_iota(jnp.int32, (1, nlane), 1) % Q_TILE
    tpos1 = tpos[:, 0:Q_TILE]
    ncp = kc_ref.shape[1]
    nsb = ovl_ref.shape[0]
    zeros_q = jnp.zeros((D_NSA, nlane), BF16)
    heads = []
    for g in range(G_NSA):
        hs = [g * R_NSA + r for r in range(R_NSA)]
        qraw = jnp.concatenate([qraw_ref[h * D_NSA:(h + 1) * D_NSA, :] for h in hs], axis=1)
        qrot = jnp.concatenate([qrot_ref[h * D_NSA:(h + 1) * D_NSA, :] for h in hs], axis=1)
        pad = (lambda a: jnp.concatenate([a, zeros_q], axis=0)) if g == 0 else (lambda a: jnp.concatenate([zeros_q, a], axis=0))
        gs = slice(g * D_NSA, (g + 1) * D_NSA)

        sc = _dot(kc_ref[0], pad(qraw))
        ci = lax.broadcasted_iota(jnp.int32, (ncp, 1), 0)
        p_cmp = _masked_softmax_rows(sc, ci * CMP_STRIDE + (CMP_LEN - 1) <= tpos)
        o_cmp = _dot(vct_ref[0, gs, :], p_cmp.astype(BF16))
        psum = p_cmp[:, 0:Q_TILE]
        for r in range(1, R_NSA):
            psum = psum + p_cmp[:, r * Q_TILE:(r + 1) * Q_TILE]
        imp = _dot01_left(ovl_ref[...], psum, 2)
        bj = lax.broadcasted_iota(jnp.int32, (nsb, 1), 0)
        cur = tpos1 // SEL_BLOCK
        forced = (bj == 0) | (bj == cur) | (bj == cur - 1)
        imp = jnp.where(forced, FORCE_SCORE, jnp.where(bj * SEL_BLOCK <= tpos1, imp, -1.0))
        bias = jnp.where(_rank_desc(imp) < N_SELECT, 0.0, NEG_INF).astype(BF16)
        bias = jnp.concatenate([bias] * R_NSA, axis=1)
        if nsb < D_NSA:
            bias = jnp.concatenate([bias, jnp.zeros((D_NSA - nsb, nlane), BF16)], axis=0)
        qaug = jnp.concatenate([qrot, bias], axis=0) if g == 0 else jnp.concatenate([bias, qrot], axis=0)

        krow = lax.broadcasted_iota(jnp.int32, (K_TILE, 1), 0)
        klane = lax.broadcasted_iota(jnp.int32, (1, LANES), 1)

        def sel_step(c, carry):
            m, l, acc = carry
            start = pl.multiple_of(c * K_TILE, K_TILE)
            kt = ksel_ref[pl.ds(start, K_TILE), :]
            blk = c * (K_TILE // SEL_BLOCK) + krow // SEL_BLOCK
            if g == 0:
                kaug = jnp.where(klane < D_NSA, kt, jnp.where(klane - D_NSA == blk, 1.0, 0.0))
            else:
                kaug = jnp.where(klane >= D_NSA, kt, jnp.where(klane == blk, 1.0, 0.0))
            s = _dot(kaug.astype(BF16), qaug)
            valid = (start + krow <= tpos) & (s > 0.5 * NEG_INF)
            sm = jnp.where(valid, s, NEG_INF)
            m_new = jnp.maximum(m, jnp.max(sm, axis=0, keepdims=True))
            alpha = jnp.exp(m - m_new)
            e = jnp.where(valid, jnp.exp(sm - m_new), 0.0)
            l = alpha * l + jnp.sum(e, axis=0, keepdims=True)
            vt = vselt_ref[gs, pl.ds(start, K_TILE)]
            acc = alpha * acc + _dot(vt, e.astype(BF16))
            return m_new, l, acc

        init = (jnp.full((1, nlane), NEG_INF, F32), jnp.zeros((1, nlane), F32), jnp.zeros((D_NSA, nlane), F32))
        _, l_sel, acc_sel = lax.fori_loop(0, qb + 1, sel_step, init)
        o_sel = acc_sel / l_sel

        band = WINDOW + Q_TILE
        wstart = pl.multiple_of(jnp.maximum(qb * Q_TILE - WINDOW, 0), Q_TILE)
        sw = _dot(kwin_ref[pl.ds(wstart, band), :].astype(BF16), pad(qrot))
        diff = tpos - (wstart + lax.broadcasted_iota(jnp.int32, (band, 1), 0))
        p_win = _masked_softmax_rows(sw, (diff >= 0) & (diff < WINDOW))
        o_win = _dot(vwint_ref[gs, pl.ds(wstart, band)], p_win.astype(BF16))

        for r in range(R_NSA):
            h = hs[r]
            ls = slice(r * Q_TILE, (r + 1) * Q_TILE)
            heads.append(gate_ref[3 * h:3 * h + 1, :] * o_cmp[:, ls]
                         + gate_ref[3 * h + 1:3 * h + 2, :] * o_sel[:, ls]
                         + gate_ref[3 * h + 2:3 * h + 3, :] * o_win[:, ls])
    o_ref[...] = jnp.concatenate(heads, axis=0).T


def _nsa_prompt(qraw_t, qrot_t, gate_t, kc, vct, nsa_rows, vsel_t, win_rows, vwin_t, nb, t):
    nqb = t // Q_TILE
    ncp = kc.shape[1]
    ovl = _overlap_t(ncp, t // SEL_BLOCK, (t - CMP_LEN) // CMP_STRIDE + 1, t // SEL_BLOCK)
    qcol = lambda b, q: (0, b * nqb + q)
    m = nb * t
    return pl.pallas_call(
        _nsa_prompt_kernel,
        grid=(nb, nqb),
        in_specs=[
            pl.BlockSpec((N_Q_COLS, Q_TILE), qcol),
            pl.BlockSpec((N_Q_COLS, Q_TILE), qcol),
            pl.BlockSpec((N_GATE_ROWS, Q_TILE), qcol),
            pl.BlockSpec((1, ncp, LANES), lambda b, q: (b, 0, 0)),
            pl.BlockSpec((1, LANES, ncp), lambda b, q: (b, 0, 0)),
            pl.BlockSpec((t, LANES), lambda b, q: (b, 2)),
            pl.BlockSpec((LANES, t), lambda b, q: (0, b)),
            pl.BlockSpec((t, LANES), lambda b, q: (b, 0)),
            pl.BlockSpec((LANES, t), lambda b, q: (0, b)),
            pl.BlockSpec(ovl.shape, lambda b, q: (0, 0)),
        ],
        out_specs=pl.BlockSpec((Q_TILE, N_Q_COLS), lambda b, q: (b * nqb + q, 0)),
        out_shape=jax.ShapeDtypeStruct((m, N_Q_COLS), F32),
        compiler_params=_params("parallel", "parallel"),
        name="nsa_prompt",
    )(qraw_t, qrot_t, gate_t, kc, vct, nsa_rows, vsel_t, win_rows, vwin_t, ovl)


def _nsa_decode_select_kernel(qt_ref, kc_ref, vct_ref, ovl_ref, grp_ref, idx_ref, ocmp_ref, *, pos, n_blocks):
    ncp = kc_ref.shape[1]
    sc = _dot(kc_ref[0], qt_ref[0])
    ci = lax.broadcasted_iota(jnp.int32, (ncp, 1), 0)
    p = _masked_softmax_rows(sc, ci * CMP_STRIDE + (CMP_LEN - 1) <= pos)
    ocmp_ref[0] = _dot(vct_ref[0], p.astype(BF16))
    psum = _dot01_right(p, grp_ref[...], 2)
    imp_t = _dot01_left(ovl_ref[...], psum, 2)
    imp_r = imp_t.T
    cur = pos // SEL_BLOCK

    def finish(v, j):
        forced = (j == 0) | (j == cur) | (j == cur - 1)
        v = jnp.where(forced, FORCE_SCORE, jnp.where(j * SEL_BLOCK <= pos, v, -1.0))
        return jnp.where(j < n_blocks, v, -3e38)

    jc = lax.broadcasted_iota(jnp.int32, (PAD_BLOCKS, 1), 0)
    jr = lax.broadcasted_iota(jnp.int32, (1, PAD_BLOCKS), 1)
    kk = lax.broadcasted_iota(jnp.int32, (N_SELECT, 1), 0).astype(F32)
    lane = lax.broadcasted_iota(jnp.int32, (N_SELECT, LANES), 1)
    out = jnp.zeros((N_SELECT, LANES), F32)
    for g in range(G_NSA):
        c0 = g * R_NSA
        col = finish(imp_t[:, c0:c0 + 1], jc)
        row = finish(imp_r[c0:c0 + 1, :], jr)
        beats = (col > row) | ((col == row) & (jc < jr))
        rank = jnp.sum(jnp.where(beats, 1.0, 0.0), axis=0, keepdims=True)
        hit = rank == kk
        idx = jnp.sum(jnp.where(hit, jr.astype(F32), 0.0), axis=1, keepdims=True)
        out = jnp.where(lane == g, idx, out)
    idx_ref[0] = out.astype(jnp.int32)


def _nsa_decode_select(q_t, kc, vct, pos, n_blocks):
    nb, ncp, _ = kc.shape
    nc = (pos + 1 - CMP_LEN) // CMP_STRIDE + 1
    ovl = _overlap_t(ncp, PAD_BLOCKS, nc, n_blocks)
    hh = np.arange(LANES)
    grp = ((hh[:, None] // R_NSA) == (hh[None, :] // R_NSA)) & (hh[:, None] < H_NSA) & (hh[None, :] < H_NSA)
    grp = jnp.asarray(grp.astype(np.float32), BF16)
    return pl.pallas_call(
        functools.partial(_nsa_decode_select_kernel, pos=pos, n_blocks=n_blocks),
        grid=(nb,),
        in_specs=[
            pl.BlockSpec((1, LANES, LANES), lambda b: (b, 0, 0)),
            pl.BlockSpec((1, ncp, LANES), lambda b: (b, 0, 0)),
            pl.BlockSpec((1, LANES, ncp), lambda b: (b, 0, 0)),
            pl.BlockSpec(ovl.shape, lambda b: (0, 0)),
            pl.BlockSpec(grp.shape, lambda b: (0, 0)),
        ],
        out_specs=[pl.BlockSpec((1, N_SELECT, LANES), lambda b: (b, 0, 0)),
                   pl.BlockSpec((1, LANES, LANES), lambda b: (b, 0, 0))],
        out_shape=[jax.ShapeDtypeStruct((nb, N_SELECT, LANES), jnp.int32),
                   jax.ShapeDtypeStruct((nb, LANES, LANES), F32)],
        compiler_params=_params("parallel"),
        name="nsa_decode_select",
    )(q_t, kc, vct, ovl, grp)


def _nsa_decode_attend_kernel(idx_ref, tbl_ref, *refs, n_past_blocks, win_buf):
    del tbl_ref
    nblk = G_NSA * N_SELECT
    blocks = refs[:nblk]
    q_ref, win_ref, nsa_new_ref, win_new_ref, gate_ref, ocmp_ref, o_ref = refs[nblk:]
    b = pl.program_id(0)
    q = q_ref[0]
    qf = q.astype(F32)
    rowg = lax.broadcasted_iota(jnp.int32, (H_NSA, 1), 0) // R_NSA
    ks_new = nsa_new_ref[0, :, 2 * LANES:3 * LANES]
    vs_new = nsa_new_ref[0, :, 3 * LANES:4 * LANES]
    s_new = jnp.sum(qf * ks_new, axis=1, keepdims=True)
    col = lax.broadcasted_iota(jnp.int32, (1, N_SELECT * ROWS_PER_PAGE), 1)
    colpage = col // ROWS_PER_PAGE
    colhalf = (col % ROWS_PER_PAGE) // SEL_BLOCK
    halves = ROWS_PER_PAGE // SEL_BLOCK
    o_sel = None
    for g in range(G_NSA):
        ks_t = jnp.concatenate([blocks[g * N_SELECT + k][0, 0:LANES, :] for k in range(N_SELECT)], axis=1)
        vs_t = jnp.concatenate([blocks[g * N_SELECT + k][0, LANES:2 * LANES, :] for k in range(N_SELECT)], axis=1)
        s = _dot(q, ks_t.astype(BF16))
        want = jnp.full(col.shape, -1, jnp.int32)
        for k in range(N_SELECT):
            j = idx_ref[b, g, k]
            half = jnp.where(j < n_past_blocks, j % halves, -1)
            want = jnp.where(colpage == k, half, want)
        valid = colhalf == want
        sm = jnp.where(valid, s, NEG_INF)
        m = jnp.maximum(jnp.max(sm, axis=1, keepdims=True), s_new)
        e = jnp.where(valid, jnp.exp(sm - m), 0.0)
        e_new = jnp.exp(s_new - m)
        l = jnp.sum(e, axis=1, keepdims=True) + e_new
        og = (_dot_nt(e.astype(BF16), vs_t.astype(BF16)) + e_new * vs_new) / l
        o_sel = og if o_sel is None else jnp.where(rowg == g, og, o_sel)

    kw_t = win_ref[0, 0:LANES, :]
    vw_t = win_ref[0, LANES:2 * LANES, :]
    kw_new = win_new_ref[0, :, 0:LANES]
    vw_new = win_new_ref[0, :, LANES:2 * LANES]
    sw = _dot(q, kw_t.astype(BF16))
    sw_new = jnp.sum(qf * kw_new, axis=1, keepdims=True)
    diff = win_buf - lax.broadcasted_iota(jnp.int32, (1, win_buf), 1)
    validw = (diff >= 0) & (diff < WINDOW)
    smw = jnp.where(validw, sw, NEG_INF)
    mw = jnp.maximum(jnp.max(smw, axis=1, keepdims=True), sw_new)
    ew = jnp.where(validw, jnp.exp(smw - mw), 0.0)
    ew_new = jnp.exp(sw_new - mw)
    lw = jnp.sum(ew, axis=1, keepdims=True) + ew_new
    o_win = (_dot_nt(ew.astype(BF16), vw_t.astype(BF16)) + ew_new * vw_new) / lw

    gt = gate_ref[0]
    o_ref[0] = gt[:, 0:1] * ocmp_ref[0] + gt[:, 1:2] * o_sel + gt[:, 2:3] * o_win


def _nsa_decode_attend(idx, table, cache_t, q2, win_t, nsa_new, win_new, gates, ocmp, n_past_blocks):
    nb = q2.shape[0]
    win_buf = win_t.shape[2]
    halves = ROWS_PER_PAGE // SEL_BLOCK

    def blk_map(b, idx_ref, tbl_ref, g, k):
        j = jnp.minimum(idx_ref[b, g, k], n_past_blocks - 1)
        return (tbl_ref[b, j // halves], 1, 0)

    blk_specs = [pl.BlockSpec((1, 2 * LANES, ROWS_PER_PAGE), functools.partial(blk_map, g=g, k=k))
                 for g in range(G_NSA) for k in range(N_SELECT)]
    per_b = lambda shape: pl.BlockSpec((1,) + shape, lambda b, i, t: (b, 0, 0))
    grid_spec = pltpu.PrefetchScalarGridSpec(
        num_scalar_prefetch=2,
        grid=(nb,),
        in_specs=blk_specs + [per_b((H_NSA, LANES)), per_b((2 * LANES, win_buf)), per_b((1, 4 * LANES)),
                              per_b((1, 2 * LANES)), per_b((H_NSA, LANES)), per_b((H_NSA, LANES))],
        out_specs=per_b((H_NSA, LANES)),
    )
    return pl.pallas_call(
        functools.partial(_nsa_decode_attend_kernel, n_past_blocks=n_past_blocks, win_buf=win_buf),
        grid_spec=grid_spec,
        out_shape=jax.ShapeDtypeStruct((nb, H_NSA, LANES), F32),
        compiler_params=_params("parallel"),
        name="nsa_decode_attend",
    )(idx, table, *([cache_t] * (G_NSA * N_SELECT)), q2, win_t, nsa_new, win_new, gates, ocmp)


def _outproj_kernel(h_ref, a_ref, b_ref, wa_ref, wb_ref, g_ref, o_ref):
    y = _dot(a_ref[...].astype(BF16), wa_ref[...]) + _dot(b_ref[...].astype(BF16), wb_ref[...])
    o_ref[...] = h_ref[...] + _rms(y, g_ref[...])


def _outproj(h, a, b, wa, wb, gain, tm):
    m = h.shape[0]
    const = lambda i: (0, 0)
    row = lambda i: (i, 0)
    return pl.pallas_call(
        _outproj_kernel,
        grid=(m // tm,),
        in_specs=[pl.BlockSpec((tm, D_MODEL), row), pl.BlockSpec((tm, a.shape[1]), row),
                  pl.BlockSpec((tm, b.shape[1]), row), pl.BlockSpec(wa.shape, const), pl.BlockSpec(wb.shape, const),
                  pl.BlockSpec((1, D_MODEL), const)],
        out_specs=pl.BlockSpec((tm, D_MODEL), row),
        out_shape=jax.ShapeDtypeStruct((m, D_MODEL), F32),
        compiler_params=_params("parallel"),
        name="mixer_out",
    )(h, a, b, wa, wb, gain)


def _ple_kernel(h_ref, p_ref, gpre_ref, wg_ref, wp_ref, gpost_ref, o_ref):
    h = h_ref[...]
    gate = jax.nn.sigmoid(_dot(_rms(h, gpre_ref[...]).astype(BF16), wg_ref[...]))
    o_ref[...] = h + _rms(gate * _dot(p_ref[...].astype(BF16), wp_ref[...]), gpost_ref[...])


def _ple(h, p, gpre, wg, wp, gpost, tm):
    m = h.shape[0]
    const = lambda i: (0, 0)
    row = lambda i: (i, 0)
    return pl.pallas_call(
        _ple_kernel,
        grid=(m // tm,),
        in_specs=[pl.BlockSpec((tm, D_MODEL), row), pl.BlockSpec((tm, PLE_DIM), row),
                  pl.BlockSpec((1, D_MODEL), const), pl.BlockSpec(wg.shape, const), pl.BlockSpec(wp.shape, const),
                  pl.BlockSpec((1, D_MODEL), const)],
        out_specs=pl.BlockSpec((tm, D_MODEL), row),
        out_shape=jax.ShapeDtypeStruct((m, D_MODEL), F32),
        compiler_params=_params("parallel"),
        name="ple",
    )(h, p, gpre, wg, wp, gpost)


def _split_in_cols(w):
    outs, off = [], 0
    for n in IN_SPLITS:
        outs.append(w[:, off:off + n])
        off += n
    return outs


def _prep_mixer_weights(w_in, w_a2, b_a):
    q_g, k_g, v_g, r_g, a_lr, q_n, kv_n, gate_n = _split_in_cols(w_in)
    pad_cols = lambda w, n: jnp.pad(w, ((0, 0), (0, n - w.shape[1])))
    wn = jnp.concatenate([q_g, k_g, v_g, r_g, pad_cols(a_lr, LANES), kv_n], axis=1).astype(BF16)
    gd = G_NSA * D_NSA
    v_sel = kv_n[:, 3 * gd:4 * gd]
    v_win = kv_n[:, 5 * gd:6 * gd]
    wt = jnp.concatenate([q_n, v_sel, v_win, pad_cols(gate_n, N_GATE_ROWS)], axis=1).T.astype(BF16)
    wa2 = jnp.pad(w_a2, ((0, LANES - GLA_RANK), (0, 0))).astype(BF16)
    return wn, wt, wa2, b_a.reshape(1, -1)


def _row_tile(m):
    return ROW_TILE if m % ROW_TILE == 0 else m


def _layer(x2, p2, mixer, w):
    tm = _row_tile(x2.shape[0])
    h = _ffn(x2, w["f1_pre"], w["f1_in"], w["f1_out"], w["f1_post"], tm)
    o_gla, o_nsa, extras = mixer(h)
    h = _outproj(h, o_gla, o_nsa, w["wo_gla"], w["wo_nsa"], w["m_post"], tm)
    h = _ffn(h, w["f2_pre"], w["f2_in"], w["f2_out"], w["f2_post"], tm)
    h = _ple(h, p2, w["ple_pre"], w["ple_gate"], w["ple_proj"], w["ple_post"], tm)
    return h, extras


def _mixer_prompt(h, w, nb, t):
    tm = _row_tile(h.shape[0])
    tabs = _rope_tables(jnp.arange(t, dtype=jnp.int32))
    (qk, v, r, la, nsa, win, qraw_t, qrot_t, vsel_t, vwin_t, gate_t) = _proj(
        h, w["m_pre"], w["wn"], w["wt"], w["wa2"], w["ba"], tabs, tm, t // tm)
    o_gla, s_fin = _gla_prompt(qk, la, v, r, w["gla_gain"], nb, t)
    pages_per_seq = t // ROWS_PER_PAGE
    table = jnp.arange(nb * pages_per_seq, dtype=jnp.int32).reshape(nb, pages_per_seq)
    abk, abv = _cmp_partial(nsa.reshape(nb * pages_per_seq, ROWS_PER_PAGE, 4 * LANES), table,
                            w["cmp_wk"], w["cmp_wv"], False)
    kc, _, vct = _cmp_finish(abk, abv, w["cmp_posk"], w["cmp_w1k"], w["cmp_w2k"],
                             w["cmp_posv"], w["cmp_w1v"], w["cmp_w2v"], w["cmp_w2vt"])
    o_nsa = _nsa_prompt(qraw_t, qrot_t, gate_t, kc, vct, nsa, vsel_t, win, vwin_t, nb, t)
    s_t = s_fin.reshape(nb, H_GLA, DV_GLA, 2, DK_GLA)
    s_own = jnp.stack([s_t[:, hh, :, hh % 2, :] for hh in range(H_GLA)], axis=1)
    gla_state = jnp.swapaxes(s_own, -1, -2)
    keep = min(WINDOW, t)
    nsa_rows = nsa.reshape(nb, t, 4, G_NSA, D_NSA)
    win_rows = win.reshape(nb, t, 2, G_NSA, D_NSA)[:, t - keep:]
    return o_gla, o_nsa, (nsa_rows, win_rows, gla_state)


def _mixer_sample(h, w, cache_l, win_l, gla_l, page_table):
    nb = h.shape[0]
    n_pages = page_table.shape[1]
    past_len = n_pages * cache_l.shape[1]
    pos = past_len
    tabs = _rope_tables(jnp.full((nb,), pos, dtype=jnp.int32))
    (qk, v, r, la, nsa, win, qraw_t, qrot_t, _, _, gate_t) = _proj(
        h, w["m_pre"], w["wn"], w["wt"], w["wa2"], w["ba"], tabs, nb, 1)

    nq = H_GLA * DK_GLA
    col = lambda a: a.reshape(nb, H_GLA, DK_GLA, 1)
    rowv = lambda a: a.reshape(nb, H_GLA, 1, DV_GLA)
    o_gla, gla_state = _gla_step(col(qk[:, :nq]), col(qk[:, nq:]), col(la), rowv(v), rowv(r),
                                 gla_l.astype(F32), w["gla_gain"])
    o_gla = o_gla.reshape(nb, H_GLA * DV_GLA)

    cache_t = jnp.transpose(cache_l.reshape(cache_l.shape[0], ROWS_PER_PAGE, 4 * LANES), (0, 2, 1))
    abk, abv = _cmp_partial(cache_t, page_table, w["cmp_wk"], w["cmp_wv"], True)
    kc, _, vct = _cmp_finish(abk, abv, w["cmp_posk"], w["cmp_w1k"], w["cmp_w2k"],
                             w["cmp_posv"], w["cmp_w1v"], w["cmp_w2v"], w["cmp_w2vt"])

    hg = (jnp.arange(H_NSA) // R_NSA)[None, :, None]

    def group_pad(q_t):
        q8 = q_t.T.reshape(nb, H_NSA, D_NSA)
        return jnp.concatenate([jnp.where(hg == 0, q8, 0), jnp.where(hg == 1, q8, 0)], axis=-1)

    q2_raw = group_pad(qraw_t)
    q2_rot = group_pad(qrot_t)
    q2_raw_t = jnp.pad(jnp.swapaxes(q2_raw, 1, 2), ((0, 0), (0, 0), (0, LANES - H_NSA)))
    n_blocks = -(-(past_len + 1) // SEL_BLOCK)
    idx_pad, ocmp_t = _nsa_decode_select(q2_raw_t, kc, vct, pos, n_blocks)
    idx = jnp.stack([idx_pad[:, :, g] for g in range(G_NSA)], axis=1)
    ocmp = jnp.swapaxes(ocmp_t, 1, 2)[:, :H_NSA, :]
    gates = jnp.pad(gate_t[:3 * H_NSA].T.reshape(nb, H_NSA, 3), ((0, 0), (0, 0), (0, LANES - 3)))
    wb = win_l.shape[1]
    win_t = jnp.transpose(win_l.reshape(nb, wb, 2 * LANES), (0, 2, 1))
    o8 = _nsa_decode_attend(idx, page_table, cache_t, q2_rot, win_t,
                            nsa.reshape(nb, 1, 4 * LANES), win.reshape(nb, 1, 2 * LANES), gates, ocmp,
                            past_len // SEL_BLOCK)
    o8 = o8.reshape(nb, H_NSA, G_NSA, D_NSA)
    o_nsa = jnp.concatenate([o8[:, :R_NSA, 0], o8[:, R_NSA:, 1]], axis=1).reshape(nb, H_NSA * D_NSA)

    nsa_rows = nsa.reshape(nb, 1, 4, G_NSA, D_NSA)
    win_new = win.reshape(nb, 1, 2, G_NSA, D_NSA)
    kw = jnp.concatenate([win_l, win_new.astype(win_l.dtype)], axis=1)
    keep = min(WINDOW, wb + 1)
    return o_gla, o_nsa, (nsa_rows, kw[:, wb + 1 - keep:], gla_state.astype(gla_l.dtype))


def kernel(x_prompt, x_sample, cache_nsa, state_win, state_gla, page_table, p_prompt, p_sample,
           ffn1_norm_pre, ffn1_norm_post, ffn1_w_in, ffn1_w_out,
           mix_norm_pre, mix_norm_post, w_mix_in, w_gla_a2, b_gla_a, gla_out_norm,
           cmp_pos_k, w_cmp_k1, w_cmp_k2, cmp_pos_v, w_cmp_v1, w_cmp_v2, w_mix_out,
           ffn2_norm_pre, ffn2_norm_post, ffn2_w_in, ffn2_w_out,
           ple_norm_pre, ple_w_gate, ple_w_proj, ple_norm_post):
    nb, t, _ = x_prompt.shape
    ns = x_sample.shape[0]
    depth = ffn1_w_in.shape[0]
    hp = x_prompt.reshape(nb * t, D_MODEL)
    hs = x_sample.reshape(ns, D_MODEL)
    outs = [[] for _ in range(6)]
    for i in range(depth):
        wn, wt, wa2, ba = _prep_mixer_weights(w_mix_in[i], w_gla_a2[i], b_gla_a[i])
        cmp_wk, cmp_posk, cmp_w1k, cmp_w2k, _ = _cmp_weights(cmp_pos_k[i], w_cmp_k1[i], w_cmp_k2[i])
        cmp_wv, cmp_posv, cmp_w1v, cmp_w2v, cmp_w2vt = _cmp_weights(cmp_pos_v[i], w_cmp_v1[i], w_cmp_v2[i])
        gla_w = H_GLA * DV_GLA
        w = dict(
            f1_pre=ffn1_norm_pre[i][None], f1_post=ffn1_norm_post[i][None],
            f1_in=ffn1_w_in[i].astype(BF16), f1_out=ffn1_w_out[i].astype(BF16),
            m_pre=mix_norm_pre[i][None], m_post=mix_norm_post[i][None],
            wn=wn, wt=wt, wa2=wa2, ba=ba, gla_gain=gla_out_norm[i][None],
            cmp_wk=cmp_wk, cmp_posk=cmp_posk, cmp_w1k=cmp_w1k, cmp_w2k=cmp_w2k,
            cmp_wv=cmp_wv, cmp_posv=cmp_posv, cmp_w1v=cmp_w1v, cmp_w2v=cmp_w2v, cmp_w2vt=cmp_w2vt,
            wo_gla=w_mix_out[i][:gla_w].astype(BF16), wo_nsa=w_mix_out[i][gla_w:].astype(BF16),
            f2_pre=ffn2_norm_pre[i][None], f2_post=ffn2_norm_post[i][None],
            f2_in=ffn2_w_in[i].astype(BF16), f2_out=ffn2_w_out[i].astype(BF16),
            ple_pre=ple_norm_pre[i][None], ple_post=ple_norm_post[i][None],
            ple_gate=ple_w_gate[i].astype(BF16), ple_proj=ple_w_proj[i].astype(BF16),
        )
        hp, (r_p, w_p, s_p) = _layer(hp, p_prompt[i].reshape(nb * t, PLE_DIM),
                                     functools.partial(_mixer_prompt, w=w, nb=nb, t=t), w)
        hs, (r_s, w_s, s_s) = _layer(hs, p_sample[i].reshape(ns, PLE_DIM),
                                     functools.partial(_mixer_sample, w=w, cache_l=cache_nsa[i], win_l=state_win[i],
                                                       gla_l=state_gla[i], page_table=page_table), w)
        for lst, val in zip(outs, (r_p, w_p, s_p, r_s, w_s, s_s)):
            lst.append(val)
    return (hp.reshape(nb, t, D_MODEL), hs.reshape(ns, 1, D_MODEL), *[jnp.stack(o) for o in outs])
```

```python
import functools

import numpy as np
import jax
import jax.numpy as jnp
from jax import lax
from jax.experimental import pallas as pl
from jax.experimental.pallas import tpu as pltpu

F32 = jnp.float32
BF16 = jnp.bfloat16

D_MODEL = 1024
PLE_DIM = 256
D_FF = 2816
EPS = 1e-6
H_GLA = 4
DK_GLA = 64
DV_GLA = 128
GLA_RANK = 16
GLA_GATE_TEMP = 16.0
GLA_CHUNK = 64
H_NSA = 8
G_NSA = 2
R_NSA = H_NSA // G_NSA
D_NSA = 64
CMP_LEN = 32
CMP_STRIDE = 16
CMP_HIDDEN = 128
SEL_BLOCK = 64
N_SELECT = 16
WINDOW = 512
FORCE_SCORE = 1e4
NEG_INF = -1e30
M_INIT = -1e29
ATTN_SCALE = D_NSA ** -0.5
ROPE_THETA = 500000.0
ROPE_DIM = D_NSA // 4
ROPE_HALF = ROPE_DIM // 2
IN_SPLITS = (H_GLA * DK_GLA, H_GLA * DK_GLA, H_GLA * DV_GLA, H_GLA * DV_GLA, GLA_RANK,
             H_NSA * D_NSA, 6 * G_NSA * D_NSA, 3 * H_NSA)

LANES = 128
SUBLANES = 8
VMEM_LIMIT = 56 * 1024 * 1024

ROW_TILE = 512
FF_CHUNK = 256
Q_TILE = 128
K_TILE = 512
PAGES_PER_STEP = 32
ROWS_PER_PAGE = 128
N_GLA_LEVELS = 6
PAD_BLOCKS = 384


def _params(*sem):
    return pltpu.CompilerParams(dimension_semantics=sem, vmem_limit_bytes=VMEM_LIMIT)


def _rms(x, g):
    return x * lax.rsqrt(jnp.mean(x * x, axis=-1, keepdims=True) + EPS) * g


def _dot(a, b):
    return jnp.dot(a, b, preferred_element_type=F32)


def _dot_nt(a, b):
    return lax.dot_general(a, b, (((1,), (1,)), ((), ())), preferred_element_type=F32)


def _dot_tn(a, b):
    return lax.dot_general(a, b, (((0,), (0,)), ((), ())), preferred_element_type=F32)


def _split_bf16(x, n):
    parts = []
    r = x
    for _ in range(n):
        p = r.astype(BF16)
        parts.append(p)
        r = r - p.astype(F32)
    return parts


def _dot01_left(m01, x, n):
    out = None
    for p in _split_bf16(x, n):
        t = _dot(m01, p)
        out = t if out is None else out + t
    return out


def _dot01_right(x, m01, n):
    out = None
    for p in _split_bf16(x, n):
        t = _dot(p, m01)
        out = t if out is None else out + t
    return out


def _masked_softmax_rows(s, valid):
    sm = jnp.where(valid, s, NEG_INF)
    m = jnp.max(sm, axis=0, keepdims=True)
    e = jnp.exp(sm - m)
    return jnp.where(valid, e / jnp.sum(e, axis=0, keepdims=True), 0.0)


def _masked_softmax_lanes(s, valid):
    sm = jnp.where(valid, s, NEG_INF)
    m = jnp.max(sm, axis=-1, keepdims=True)
    e = jnp.exp(sm - m)
    return jnp.where(valid, e / jnp.sum(e, axis=-1, keepdims=True), 0.0)


def _ffn_kernel(x_ref, gpre_ref, wg_ref, wu_ref, wo_ref, gpost_ref, o_ref):
    x = x_ref[...]
    xn = _rms(x, gpre_ref[...]).astype(BF16)
    acc = jnp.zeros(x.shape, F32)
    for c in range(D_FF // FF_CHUNK):
        sl = slice(c * FF_CHUNK, (c + 1) * FF_CHUNK)
        g = _dot(xn, wg_ref[:, sl])
        u = _dot(xn, wu_ref[:, sl])
        a = (jax.nn.silu(g) * u).astype(BF16)
        acc = acc + _dot(a, wo_ref[sl, :])
    o_ref[...] = x + 0.5 * _rms(acc, gpost_ref[...])


def _ffn(x, gpre, w_in, w_out, gpost, tm):
    m = x.shape[0]
    const = lambda i: (0, 0)
    return pl.pallas_call(
        _ffn_kernel,
        grid=(m // tm,),
        in_specs=[
            pl.BlockSpec((tm, D_MODEL), lambda i: (i, 0)),
            pl.BlockSpec((1, D_MODEL), const),
            pl.BlockSpec((D_MODEL, D_FF), const),
            pl.BlockSpec((D_MODEL, D_FF), lambda i: (0, 1)),
            pl.BlockSpec((D_FF, D_MODEL), const),
            pl.BlockSpec((1, D_MODEL), const),
        ],
        out_specs=pl.BlockSpec((tm, D_MODEL), lambda i: (i, 0)),
        out_shape=jax.ShapeDtypeStruct((m, D_MODEL), F32),
        compiler_params=_params("parallel"),
        name="ffn",
    )(x, gpre, w_in, w_in, w_out, gpost)


N_GLA_COLS = 2 * H_GLA * DK_GLA + 2 * H_GLA * DV_GLA
N_KV_COLS = 6 * G_NSA * D_NSA
WN_COLS = N_GLA_COLS + 3 * LANES
N_Q_COLS = H_NSA * D_NSA
N_GATE_ROWS = 32
WT_ROWS = N_Q_COLS + N_KV_COLS + N_GATE_ROWS
N_CACHE_FEATS = 4 * G_NSA * D_NSA
N_WIN_FEATS = 2 * G_NSA * D_NSA


def _rope_rows(x, cos, sin):
    out = []
    for h in range(x.shape[0] // D_NSA):
        b = h * D_NSA
        x1 = x[b:b + ROPE_HALF]
        x2 = x[b + ROPE_HALF:b + ROPE_DIM]
        out += [x1 * cos - x2 * sin, x2 * cos + x1 * sin, x[b + ROPE_DIM:b + D_NSA]]
    return jnp.concatenate(out, axis=0)


def _rope_lanes(x, c, s1, s2):
    return x * c + pltpu.roll(x, LANES - ROPE_HALF, 1) * s1 + pltpu.roll(x, ROPE_HALF, 1) * s2


def _proj_kernel(h_ref, g_ref, wn_ref, wt_ref, wa2_ref, ba_ref, rc_ref, rs1_ref, rs2_ref, cos_ref, sin_ref,
                 qk_ref, v_ref, r_ref, la_ref, ksel_ref, kwin_ref,
                 qraw_ref, qrot_ref, gate_ref, nsat_ref, wint_ref):
    xn = _rms(h_ref[...], g_ref[...]).astype(BF16)
    nqk = 2 * H_GLA * DK_GLA
    nv = H_GLA * DV_GLA
    z = _dot(xn, wn_ref[:, 0:nqk])
    qk_ref[:, 0:nqk // 2] = z[:, 0:nqk // 2] * (DK_GLA ** -0.5)
    qk_ref[:, nqk // 2:nqk] = z[:, nqk // 2:nqk]
    v_ref[...] = _dot(xn, wn_ref[:, nqk:nqk + nv])
    r_ref[...] = _dot(xn, wn_ref[:, nqk + nv:N_GLA_COLS])
    a_lr = _dot(xn, wn_ref[:, N_GLA_COLS:N_GLA_COLS + LANES])
    xa = _dot(a_lr.astype(BF16), wa2_ref[...]) + ba_ref[...]
    la_ref[...] = (jnp.minimum(xa, 0.0) - jnp.log1p(jnp.exp(-jnp.abs(xa)))) * (1.0 / GLA_GATE_TEMP)
    k0 = N_GLA_COLS + LANES
    rc, rs1, rs2 = rc_ref[...], rs1_ref[...], rs2_ref[...]
    ksel_ref[...] = _rope_lanes(_dot(xn, wn_ref[:, k0:k0 + LANES]), rc, rs1, rs2)
    kwin_ref[...] = _rope_lanes(_dot(xn, wn_ref[:, k0 + LANES:k0 + 2 * LANES]), rc, rs1, rs2)
    zt = _dot_nt(wt_ref[...], xn)
    cos, sin = cos_ref[...], sin_ref[...]
    q = zt[0:N_Q_COLS] * ATTN_SCALE
    qraw_ref[...] = q.astype(BF16)
    qrot_ref[...] = _rope_rows(q, cos, sin).astype(BF16)
    kv = zt[N_Q_COLS:N_Q_COLS + N_KV_COLS]
    nsat_ref[0, 0:256, :] = kv[0:256]
    nsat_ref[0, 256:384, :] = _rope_rows(kv[256:384], cos, sin)
    nsat_ref[0, 384:512, :] = kv[384:512]
    wint_ref[0, 0:128, :] = _rope_rows(kv[512:640], cos, sin)
    wint_ref[0, 128:256, :] = kv[640:768]
    gate_ref[...] = jax.nn.sigmoid(zt[N_Q_COLS + N_KV_COLS:WT_ROWS])


def _proj(h, gain, wn, wt, wa2, ba, tabs, tm, tiles_per_seq):
    m = h.shape[0]
    rc, rs1, rs2, cos_t, sin_t = tabs
    const = lambda i: (0, 0)
    row = lambda i: (i, 0)
    col = lambda i: (0, i)
    tab_row = lambda i: (i % tiles_per_seq, 0)
    tab_col = lambda i: (0, i % tiles_per_seq)
    outs = [
        (2 * H_GLA * DK_GLA, F32), (H_GLA * DV_GLA, F32), (H_GLA * DV_GLA, F32), (H_GLA * DK_GLA, F32),
        (LANES, F32), (LANES, F32),
    ]
    outs_t = [(N_Q_COLS, BF16), (N_Q_COLS, BF16), (N_GATE_ROWS, F32)]
    nseq = m // (tm * tiles_per_seq)
    seq_len = tm * tiles_per_seq
    outs_seq = [N_CACHE_FEATS, N_WIN_FEATS]
    seq_map = lambda i: (i // tiles_per_seq, 0, i % tiles_per_seq)
    return pl.pallas_call(
        _proj_kernel,
        grid=(m // tm,),
        in_specs=[
            pl.BlockSpec((tm, D_MODEL), row),
            pl.BlockSpec((1, D_MODEL), const),
            pl.BlockSpec((D_MODEL, WN_COLS), const),
            pl.BlockSpec((WT_ROWS, D_MODEL), const),
            pl.BlockSpec((LANES, H_GLA * DK_GLA), const),
            pl.BlockSpec((1, H_GLA * DK_GLA), const),
            pl.BlockSpec((tm, LANES), tab_row),
            pl.BlockSpec((tm, LANES), tab_row),
            pl.BlockSpec((tm, LANES), tab_row),
            pl.BlockSpec((ROPE_HALF, tm), tab_col),
            pl.BlockSpec((ROPE_HALF, tm), tab_col),
        ],
        out_specs=[pl.BlockSpec((tm, n), row) for n, _ in outs] + [pl.BlockSpec((n, tm), col) for n, _ in outs_t]
        + [pl.BlockSpec((1, n, tm), seq_map) for n in outs_seq],
        out_shape=[jax.ShapeDtypeStruct((m, n), d) for n, d in outs]
        + [jax.ShapeDtypeStruct((n, m), d) for n, d in outs_t]
        + [jax.ShapeDtypeStruct((nseq, n, seq_len), F32) for n in outs_seq],
        compiler_params=_params("parallel"),
        name="mixer_proj",
    )(h, gain, wn, wt, wa2, ba, rc, rs1, rs2, cos_t, sin_t)


def _rope_tables(pos):
    inv_freq = ROPE_THETA ** (-jnp.arange(ROPE_HALF, dtype=F32) * 2.0 / ROPE_DIM)
    ang = pos.astype(F32)[:, None] * inv_freq[None, :]
    cos, sin = jnp.cos(ang), jnp.sin(ang)
    n = pos.shape[0]
    one = jnp.ones((n, D_NSA - ROPE_DIM), F32)
    zero = jnp.zeros((n, D_NSA - ROPE_DIM), F32)
    zh = jnp.zeros((n, ROPE_HALF), F32)
    c = jnp.concatenate([cos, cos, one], axis=1)
    s1 = jnp.concatenate([-sin, zh, zero], axis=1)
    s2 = jnp.concatenate([zh, sin, zero], axis=1)
    dup = lambda t: jnp.concatenate([t, t], axis=1)
    return dup(c), dup(s1), dup(s2), cos.T, sin.T


def _gla_constants():
    c = GLA_CHUNK
    t = np.arange(c)
    low = (t[None, :] <= t[:, None]).astype(np.float32)
    mats, masks = [low], []
    for lev in range(1, N_GLA_LEVELS + 1):
        seg = (2 * c) >> lev
        half = seg // 2
        mid = (t // seg) * seg + half
        mats.append(low[mid])
        same = (t[:, None] // seg) == (t[None, :] // seg)
        masks.append((same & ((t[:, None] % seg) >= half) & ((t[None, :] % seg) < half)).astype(np.float32))
    masks.append(np.eye(c, dtype=np.float32))
    return np.concatenate(mats, axis=0), np.stack(masks)


def _gla_kernel(qk_ref, la_ref, v_ref, r_ref, gain_ref, big_ref, mask_ref, o_ref, sfin_ref, st_ref):
    c = pl.program_id(1)
    ch = GLA_CHUNK

    @pl.when(c == 0)
    def _():
        st_ref[...] = jnp.zeros(st_ref.shape, F32)

    lane = lax.broadcasted_iota(jnp.int32, (1, LANES), 1)
    head_mask = [jnp.where(lane < DK_GLA, 1.0, 0.0), jnp.where(lane >= DK_GLA, 1.0, 0.0)]
    big = big_ref[...]
    gain = gain_ref[...]
    nq = H_GLA * DK_GLA
    for p in range(H_GLA // 2):
        sl = slice(p * LANES, (p + 1) * LANES)
        q = qk_ref[:, sl]
        k = qk_ref[:, nq + p * LANES:nq + (p + 1) * LANES]
        allb = _dot01_left(big, la_ref[:, sl], 3)
        b = allb[0:ch]
        b_last = b[ch - 1:ch]
        attn = [jnp.zeros((ch, ch), F32), jnp.zeros((ch, ch), F32)]
        for lev in range(N_GLA_LEVELS + 1):
            if lev < N_GLA_LEVELS:
                ref = allb[(lev + 1) * ch:(lev + 2) * ch]
                ql = q * jnp.exp(jnp.minimum(b - ref, 0.0))
                kl = k * jnp.exp(jnp.minimum(ref - b, 0.0))
            else:
                ql, kl = q, k
            qq = jnp.concatenate([ql * head_mask[0], ql * head_mask[1]], axis=0).astype(BF16)
            s = _dot_nt(qq, kl.astype(BF16))
            mk = mask_ref[lev]
            attn[0] = attn[0] + mk * s[0:ch]
            attn[1] = attn[1] + mk * s[ch:2 * ch]
        q0 = q * jnp.exp(b)
        k_hat = (k * jnp.exp(b_last - b)).astype(BF16)
        decay = jnp.exp(b_last)
        for hh in range(2):
            h = 2 * p + hh
            hs = slice(h * DV_GLA, (h + 1) * DV_GLA)
            vh = v_ref[:, hs].astype(BF16)
            st = st_ref[h]
            o = _dot(attn[hh].astype(BF16), vh) + _dot_nt((q0 * head_mask[hh]).astype(BF16), st.astype(BF16))
            st_ref[h] = st * decay + _dot_tn(vh, k_hat)
            o_ref[:, hs] = _rms(o, gain) * jax.nn.silu(r_ref[:, hs])

    @pl.when(c == pl.num_programs(1) - 1)
    def _():
        sfin_ref[0] = st_ref[...]


def _gla_prompt(qk, la, v, r, gain, nb, t):
    big, masks = _gla_constants()
    nc = t // GLA_CHUNK
    row = lambda b, c: (b * nc + c, 0)
    const2 = lambda b, c: (0, 0)
    m = nb * t
    return pl.pallas_call(
        _gla_kernel,
        grid=(nb, nc),
        in_specs=[
            pl.BlockSpec((GLA_CHUNK, 2 * H_GLA * DK_GLA), row),
            pl.BlockSpec((GLA_CHUNK, H_GLA * DK_GLA), row),
            pl.BlockSpec((GLA_CHUNK, H_GLA * DV_GLA), row),
            pl.BlockSpec((GLA_CHUNK, H_GLA * DV_GLA), row),
            pl.BlockSpec((1, DV_GLA), const2),
            pl.BlockSpec(big.shape, const2),
            pl.BlockSpec(masks.shape, lambda b, c: (0, 0, 0)),
        ],
        out_specs=[
            pl.BlockSpec((GLA_CHUNK, H_GLA * DV_GLA), row),
            pl.BlockSpec((1, H_GLA, DV_GLA, LANES), lambda b, c: (b, 0, 0, 0)),
        ],
        out_shape=[
            jax.ShapeDtypeStruct((m, H_GLA * DV_GLA), F32),
            jax.ShapeDtypeStruct((nb, H_GLA, DV_GLA, LANES), F32),
        ],
        scratch_shapes=[pltpu.VMEM((H_GLA, DV_GLA, LANES), F32)],
        compiler_params=_params("parallel", "arbitrary"),
        name="gla_scan",
    )(qk, la, v, r, gain, jnp.asarray(big, BF16), jnp.asarray(masks, F32))


def _gla_step_kernel(q_ref, k_ref, a_ref, v_ref, r_ref, s_ref, gain_ref, o_ref, sn_ref):
    for h in range(H_GLA):
        s_new = jnp.exp(a_ref[0, h]) * s_ref[0, h] + k_ref[0, h] * v_ref[0, h]
        sn_ref[0, h] = s_new
        o = jnp.sum(q_ref[0, h] * s_new, axis=0, keepdims=True)
        o_ref[0, h] = _rms(o, gain_ref[...]) * jax.nn.silu(r_ref[0, h])


def _gla_step(qcol, kcol, acol, vrow, rrow, state, gain):
    nb = state.shape[0]
    col = pl.BlockSpec((1, H_GLA, DK_GLA, 1), lambda b: (b, 0, 0, 0))
    rowspec = pl.BlockSpec((1, H_GLA, 1, DV_GLA), lambda b: (b, 0, 0, 0))
    stspec = pl.BlockSpec((1, H_GLA, DK_GLA, DV_GLA), lambda b: (b, 0, 0, 0))
    return pl.pallas_call(
        _gla_step_kernel,
        grid=(nb,),
        in_specs=[col, col, col, rowspec, rowspec, stspec, pl.BlockSpec((1, DV_GLA), lambda b: (0, 0))],
        out_specs=[rowspec, stspec],
        out_shape=[jax.ShapeDtypeStruct((nb, H_GLA, 1, DV_GLA), F32),
                   jax.ShapeDtypeStruct((nb, H_GLA, DK_GLA, DV_GLA), F32)],
        compiler_params=_params("parallel"),
        name="gla_step",
    )(qcol, kcol, acol, vrow, rrow, state, gain)


HALF_ROWS = CMP_STRIDE
CHUNKS_PER_PAGE = ROWS_PER_PAGE // HALF_ROWS


def _cmp_partial_kernel(tbl_ref, *refs, pps):
    del tbl_ref
    pages_k = refs[:pps]
    pages_v = refs[pps:2 * pps]
    wk_ref, wv_ref, abk_ref, abv_ref, tk_ref, tv_ref = refs[2 * pps:]
    for i in range(pps):
        tk_ref[i] = pages_k[i][0].T
        tv_ref[i] = pages_v[i][0].T
    lpair = 2 * LANES
    for t_ref, w_ref, ab_ref in ((tk_ref, wk_ref, abk_ref), (tv_ref, wv_ref, abv_ref)):
        acc = None
        for lp in range(HALF_ROWS // 2):
            cols = [jnp.concatenate([t_ref[i, pl.ds(l, CHUNKS_PER_PAGE, stride=HALF_ROWS), :] for i in range(pps)], axis=0)
                    for l in (2 * lp, 2 * lp + 1)]
            x = jnp.concatenate(cols, axis=1).astype(BF16)
            part = _dot(x, w_ref[lp * lpair:(lp + 1) * lpair, :])
            acc = part if acc is None else acc + part
        ab_ref[0] = acc


def _cmp_partial(pages_t, table, wk, wv, paged):
    nb, npages = table.shape
    pps = min(PAGES_PER_STEP, npages)
    nsteps = npages // pps
    nch = npages * CHUNKS_PER_PAGE
    step_chunks = pps * CHUNKS_PER_PAGE

    def page_map(b, s, tbl, i, kv):
        return (tbl[b, s * pps + i], kv, 0) if paged else (b, kv, s * pps + i)

    page_specs = [pl.BlockSpec((1, LANES, ROWS_PER_PAGE), functools.partial(page_map, i=i, kv=kv))
                  for kv in range(2) for i in range(pps)]
    wspec = pl.BlockSpec(wk.shape, lambda b, s, tbl: (0, 0))
    ospec = pl.BlockSpec((1, step_chunks, 4 * CMP_HIDDEN), lambda b, s, tbl: (b, s, 0))
    grid_spec = pltpu.PrefetchScalarGridSpec(
        num_scalar_prefetch=1,
        grid=(nb, nsteps),
        in_specs=page_specs + [wspec, wspec],
        out_specs=[ospec, ospec],
        scratch_shapes=[pltpu.VMEM((pps, ROWS_PER_PAGE, LANES), F32), pltpu.VMEM((pps, ROWS_PER_PAGE, LANES), F32)],
    )
    osh = jax.ShapeDtypeStruct((nb, nch, 4 * CMP_HIDDEN), F32)
    return pl.pallas_call(
        functools.partial(_cmp_partial_kernel, pps=pps),
        grid_spec=grid_spec,
        out_shape=[osh, osh],
        compiler_params=_params("parallel", "parallel"),
        name="cmp_partial",
    )(table, *([pages_t] * (2 * pps)), wk, wv)


def _cmp_finish_kernel(abk_ref, abv_ref, posk_ref, w1k_ref, w2k_ref, posv_ref, w1v_ref, w2v_ref, w2vt_ref,
                       kc_ref, vc_ref, vct_ref):
    nch = abk_ref.shape[1]
    pk = _dot(posk_ref[...], w1k_ref[...])[0:1]
    pv = _dot(posv_ref[...], w1v_ref[...])[0:1]
    abk = abk_ref[0]
    abv = abv_ref[0]
    kc = None
    vc = None
    for g in range(G_NSA):
        o = g * 2 * CMP_HIDDEN
        hk = jax.nn.gelu(abk[:, o:o + CMP_HIDDEN] + pltpu.roll(abk[:, o + CMP_HIDDEN:o + 2 * CMP_HIDDEN], nch - 1, 0) + pk)
        hv = jax.nn.gelu(abv[:, o:o + CMP_HIDDEN] + pltpu.roll(abv[:, o + CMP_HIDDEN:o + 2 * CMP_HIDDEN], nch - 1, 0) + pv)
        hk = hk.astype(BF16)
        hv = hv.astype(BF16)
        tk = _dot(hk, w2k_ref[g])
        tv = _dot(hv, w2v_ref[g])
        kc = tk if kc is None else kc + tk
        vc = tv if vc is None else vc + tv
        vct_ref[0, g * D_NSA:(g + 1) * D_NSA, :] = _dot_nt(w2vt_ref[...], hv).astype(BF16)
    kc_ref[0] = kc.astype(BF16)
    vc_ref[0] = vc.astype(BF16)


def _cmp_finish(abk, abv, posk, w1k, w2k_pad, posv, w1v, w2v_pad, w2vt):
    nb, nch, _ = abk.shape
    ab = pl.BlockSpec((1, nch, 4 * CMP_HIDDEN), lambda b: (b, 0, 0))
    c2 = lambda b: (0, 0)
    c3 = lambda b: (0, 0, 0)
    return pl.pallas_call(
        _cmp_finish_kernel,
        grid=(nb,),
        in_specs=[ab, ab,
                  pl.BlockSpec(posk.shape, c2), pl.BlockSpec(w1k.shape, c2), pl.BlockSpec(w2k_pad.shape, c3),
                  pl.BlockSpec(posv.shape, c2), pl.BlockSpec(w1v.shape, c2), pl.BlockSpec(w2v_pad.shape, c3),
                  pl.BlockSpec(w2vt.shape, c2)],
        out_specs=[pl.BlockSpec((1, nch, LANES), lambda b: (b, 0, 0)),
                   pl.BlockSpec((1, nch, LANES), lambda b: (b, 0, 0)),
                   pl.BlockSpec((1, LANES, nch), lambda b: (b, 0, 0))],
        out_shape=[jax.ShapeDtypeStruct((nb, nch, LANES), BF16),
                   jax.ShapeDtypeStruct((nb, nch, LANES), BF16),
                   jax.ShapeDtypeStruct((nb, LANES, nch), BF16)],
        compiler_params=_params("parallel"),
        name="cmp_finish",
    )(abk, abv, posk, w1k, w2k_pad, posv, w1v, w2v_pad, w2vt)


def _cmp_weights(pos, w1, w2):
    w = w1.reshape(2, HALF_ROWS, D_NSA, CMP_HIDDEN)
    z = jnp.zeros_like(w[0])
    blocks = []
    for g in range(G_NSA):
        cols = []
        for g2 in range(G_NSA):
            for half in range(2):
                cols.append(w[half] if g2 == g else z)
        blocks.append(jnp.concatenate(cols, axis=-1))
    wbig = jnp.stack(blocks, axis=1).reshape(HALF_ROWS * LANES, 4 * CMP_HIDDEN).astype(BF16)
    pos8 = jnp.broadcast_to(pos.reshape(1, CMP_LEN * D_NSA), (SUBLANES, CMP_LEN * D_NSA)).astype(BF16)
    z2 = jnp.zeros_like(w2)
    w2pad = jnp.stack([jnp.concatenate([w2, z2], axis=1), jnp.concatenate([z2, w2], axis=1)]).astype(BF16)
    return wbig, pos8, w1.astype(BF16), w2pad, w2.T.astype(BF16)


def _overlap_t(nc_pad, ns_pad, nc, ns):
    i = np.arange(nc_pad)[None, :] * CMP_STRIDE
    j = np.arange(ns_pad)[:, None] * SEL_BLOCK
    m = (i < j + SEL_BLOCK) & (i + CMP_LEN > j) & (np.arange(nc_pad)[None, :] < nc) & (np.arange(ns_pad)[:, None] < ns)
    return jnp.asarray(m.astype(np.float32), BF16)


def _rank_desc(x):
    nrow = x.shape[0]
    nblk = nrow // SUBLANES
    blocks = [x[v * SUBLANES:(v + 1) * SUBLANES] for v in range(nblk)]
    cnt = [jnp.zeros((SUBLANES, x.shape[1]), F32) for _ in range(nblk)]
    sub = lax.broadcasted_iota(jnp.int32, (SUBLANES, x.shape[1]), 0)
    for jp in range(nrow):
        row = x[jp:jp + 1]
        vb = jp // SUBLANES
        for v in range(nblk):
            ge = jnp.where(row >= blocks[v], 1.0, 0.0)
            gt = jnp.where(row > blocks[v], 1.0, 0.0)
            if v > vb:
                cnt[v] = cnt[v] + ge
            elif v < vb:
                cnt[v] = cnt[v] + gt
            else:
                cnt[v] = cnt[v] + jnp.where(sub > (jp % SUBLANES), ge, gt)
    return jnp.concatenate(cnt, axis=0)


def _nsa_prompt_kernel(qraw_ref, qrot_ref, gate_ref, kc_ref, vct_ref, ksel_ref, vselt_ref, kwin_ref, vwint_ref,
                       ovl_ref, o_ref):
    qb = pl.program_id(1)
    nlane = R_NSA * Q_TILE
    tpos = qb * Q_TILE + lax.broadcasted_iota(jnp.int32, (1, nlane), 1) % Q_TILE
    tpos1 = tpos[:, 0:Q_TILE]
    ncp = kc_ref.shape[1]
    nsb = ovl_ref.shape[0]
    zeros_q = jnp.zeros((D_NSA, nlane), BF16)
    heads = []
    for g in range(G_NSA):
        hs = [g * R_NSA + r for r in range(R_NSA)]
        qraw = jnp.concatenate([qraw_ref[h * D_NSA:(h + 1) * D_NSA, :] for h in hs], axis=1)
        qrot = jnp.concatenate([qrot_ref[h * D_NSA:(h + 1) * D_NSA, :] for h in hs], axis=1)
        pad = (lambda a: jnp.concatenate([a, zeros_q], axis=0)) if g == 0 else (lambda a: jnp.concatenate([zeros_q, a], axis=0))
        gs = slice(g * D_NSA, (g + 1) * D_NSA)

        sc = _dot(kc_ref[0], pad(qraw))
        ci = lax.broadcasted_iota(jnp.int32, (ncp, 1), 0)
        p_cmp = _masked_softmax_rows(sc, ci * CMP_STRIDE + (CMP_LEN - 1) <= tpos)
        o_cmp = _dot(vct_ref[0, gs, :], p_cmp.astype(BF16))
        psum = p_cmp[:, 0:Q_TILE]
        for r in range(1, R_NSA):
            psum = psum + p_cmp[:, r * Q_TILE:(r + 1) * Q_TILE]
        imp = _dot01_left(ovl_ref[...], psum, 2)
        bj = lax.broadcasted_iota(jnp.int32, (nsb, 1), 0)
        cur = tpos1 // SEL_BLOCK
        forced = (bj == 0) | (bj == cur) | (bj == cur - 1)
        imp = jnp.where(forced, FORCE_SCORE, jnp.where(bj * SEL_BLOCK <= tpos1, imp, -1.0))
        bias = jnp.where(_rank_desc(imp) < N_SELECT, 0.0, NEG_INF).astype(BF16)
        bias = jnp.concatenate([bias] * R_NSA, axis=1)
        if nsb < D_NSA:
            bias = jnp.concatenate([bias, jnp.zeros((D_NSA - nsb, nlane), BF16)], axis=0)
        qaug = jnp.concatenate([qrot, bias], axis=0) if g == 0 else jnp.concatenate([bias, qrot], axis=0)

        krow = lax.broadcasted_iota(jnp.int32, (K_TILE, 1), 0)
        klane = lax.broadcasted_iota(jnp.int32, (1, LANES), 1)

        def sel_step(c, carry, causal):
            m, l, acc = carry
            start = pl.multiple_of(c * K_TILE, K_TILE)
            kt = ksel_ref[pl.ds(start, K_TILE), :]
            blk = c * (K_TILE // SEL_BLOCK) + krow // SEL_BLOCK
            if g == 0:
                kaug = jnp.where(klane < D_NSA, kt, jnp.where(klane - D_NSA == blk, 1.0, 0.0))
            else:
                kaug = jnp.where(klane >= D_NSA, kt, jnp.where(klane == blk, 1.0, 0.0))
            s = _dot(kaug.astype(BF16), qaug)
            if causal:
                s = jnp.where(start + krow <= tpos, s, NEG_INF)
            m_new = jnp.maximum(m, jnp.max(s, axis=0, keepdims=True))
            alpha = jnp.exp(m - m_new)
            e = jnp.exp(s - m_new)
            l = alpha * l + jnp.sum(e, axis=0, keepdims=True)
            vt = vselt_ref[0, gs, pl.ds(start, K_TILE)].astype(BF16)
            acc = alpha * acc + _dot(vt, e.astype(BF16))
            return m_new, l, acc

        n_full = (qb * Q_TILE) // K_TILE
        init = (jnp.full((1, nlane), M_INIT, F32), jnp.zeros((1, nlane), F32), jnp.zeros((D_NSA, nlane), F32))
        carry = lax.fori_loop(0, n_full, functools.partial(sel_step, causal=False), init)
        _, l_sel, acc_sel = sel_step(n_full, carry, True)
        o_sel = acc_sel / l_sel

        band = WINDOW + Q_TILE
        wstart = pl.multiple_of(jnp.maximum(qb * Q_TILE - WINDOW, 0), Q_TILE)
        sw = _dot(kwin_ref[pl.ds(wstart, band), :].astype(BF16), pad(qrot))
        diff = tpos - (wstart + lax.broadcasted_iota(jnp.int32, (band, 1), 0))
        p_win = _masked_softmax_rows(sw, (diff >= 0) & (diff < WINDOW))
        o_win = _dot(vwint_ref[0, gs, pl.ds(wstart, band)].astype(BF16), p_win.astype(BF16))

        for r in range(R_NSA):
            h = hs[r]
            ls = slice(r * Q_TILE, (r + 1) * Q_TILE)
            heads.append(gate_ref[3 * h:3 * h + 1, :] * o_cmp[:, ls]
                         + gate_ref[3 * h + 1:3 * h + 2, :] * o_sel[:, ls]
                         + gate_ref[3 * h + 2:3 * h + 3, :] * o_win[:, ls])
    o_ref[...] = jnp.concatenate(heads, axis=0).T


def _nsa_prompt(qraw_t, qrot_t, gate_t, kc, vct, ksel, nsa_t, kwin, win_t, nb, t):
    nqb = t // Q_TILE
    ncp = kc.shape[1]
    ovl = _overlap_t(ncp, t // SEL_BLOCK, (t - CMP_LEN) // CMP_STRIDE + 1, t // SEL_BLOCK)
    qcol = lambda b, q: (0, b * nqb + q)
    m = nb * t
    return pl.pallas_call(
        _nsa_prompt_kernel,
        grid=(nb, nqb),
        in_specs=[
            pl.BlockSpec((N_Q_COLS, Q_TILE), qcol),
            pl.BlockSpec((N_Q_COLS, Q_TILE), qcol),
            pl.BlockSpec((N_GATE_ROWS, Q_TILE), qcol),
            pl.BlockSpec((1, ncp, LANES), lambda b, q: (b, 0, 0)),
            pl.BlockSpec((1, LANES, ncp), lambda b, q: (b, 0, 0)),
            pl.BlockSpec((t, LANES), lambda b, q: (b, 0)),
            pl.BlockSpec((1, LANES, t), lambda b, q: (b, 3, 0)),
            pl.BlockSpec((t, LANES), lambda b, q: (b, 0)),
            pl.BlockSpec((1, LANES, t), lambda b, q: (b, 1, 0)),
            pl.BlockSpec(ovl.shape, lambda b, q: (0, 0)),
        ],
        out_specs=pl.BlockSpec((Q_TILE, N_Q_COLS), lambda b, q: (b * nqb + q, 0)),
        out_shape=jax.ShapeDtypeStruct((m, N_Q_COLS), F32),
        compiler_params=_params("parallel", "parallel"),
        name="nsa_prompt",
    )(qraw_t, qrot_t, gate_t, kc, vct, ksel, nsa_t, kwin, win_t, ovl)


def _nsa_decode_select_kernel(qt_ref, kc_ref, vct_ref, ovl_ref, grp_ref, idx_ref, ocmp_ref, *, pos, n_blocks):
    ncp = kc_ref.shape[1]
    sc = _dot(kc_ref[0], qt_ref[0])
    ci = lax.broadcasted_iota(jnp.int32, (ncp, 1), 0)
    p = _masked_softmax_rows(sc, ci * CMP_STRIDE + (CMP_LEN - 1) <= pos)
    ocmp_ref[0] = _dot(vct_ref[0], p.astype(BF16))
    psum = _dot01_right(p, grp_ref[...], 2)
    imp_t = _dot01_left(ovl_ref[...], psum, 2)
    imp_r = imp_t.T
    cur = pos // SEL_BLOCK

    def finish(v, j):
        forced = (j == 0) | (j == cur) | (j == cur - 1)
        v = jnp.where(forced, FORCE_SCORE, jnp.where(j * SEL_BLOCK <= pos, v, -1.0))
        return jnp.where(j < n_blocks, v, -3e38)

    jc = lax.broadcasted_iota(jnp.int32, (PAD_BLOCKS, 1), 0)
    jr = lax.broadcasted_iota(jnp.int32, (1, PAD_BLOCKS), 1)
    kk = lax.broadcasted_iota(jnp.int32, (N_SELECT, 1), 0).astype(F32)
    lane = lax.broadcasted_iota(jnp.int32, (N_SELECT, LANES), 1)
    out = jnp.zeros((N_SELECT, LANES), F32)
    for g in range(G_NSA):
        c0 = g * R_NSA
        col = finish(imp_t[:, c0:c0 + 1], jc)
        row = finish(imp_r[c0:c0 + 1, :], jr)
        beats = (col > row) | ((col == row) & (jc < jr))
        rank = jnp.sum(jnp.where(beats, 1.0, 0.0), axis=0, keepdims=True)
        hit = rank == kk
        idx = jnp.sum(jnp.where(hit, jr.astype(F32), 0.0), axis=1, keepdims=True)
        out = jnp.where(lane == g, idx, out)
    idx_ref[0] = out.astype(jnp.int32)


def _nsa_decode_select(q_t, kc, vct, pos, n_blocks):
    nb, ncp, _ = kc.shape
    nc = (pos + 1 - CMP_LEN) // CMP_STRIDE + 1
    ovl = _overlap_t(ncp, PAD_BLOCKS, nc, n_blocks)
    hh = np.arange(LANES)
    grp = ((hh[:, None] // R_NSA) == (hh[None, :] // R_NSA)) & (hh[:, None] < H_NSA) & (hh[None, :] < H_NSA)
    grp = jnp.asarray(grp.astype(np.float32), BF16)
    return pl.pallas_call(
        functools.partial(_nsa_decode_select_kernel, pos=pos, n_blocks=n_blocks),
        grid=(nb,),
        in_specs=[
            pl.BlockSpec((1, LANES, LANES), lambda b: (b, 0, 0)),
            pl.BlockSpec((1, ncp, LANES), lambda b: (b, 0, 0)),
            pl.BlockSpec((1, LANES, ncp), lambda b: (b, 0, 0)),
            pl.BlockSpec(ovl.shape, lambda b: (0, 0)),
            pl.BlockSpec(grp.shape, lambda b: (0, 0)),
        ],
        out_specs=[pl.BlockSpec((1, N_SELECT, LANES), lambda b: (b, 0, 0)),
                   pl.BlockSpec((1, LANES, LANES), lambda b: (b, 0, 0))],
        out_shape=[jax.ShapeDtypeStruct((nb, N_SELECT, LANES), jnp.int32),
                   jax.ShapeDtypeStruct((nb, LANES, LANES), F32)],
        compiler_params=_params("parallel"),
        name="nsa_decode_select",
    )(q_t, kc, vct, ovl, grp)


def _nsa_decode_attend_kernel(idx_ref, tbl_ref, *refs, n_past_blocks, win_buf):
    del tbl_ref
    nblk = G_NSA * N_SELECT
    blocks = refs[:nblk]
    q_ref, win_ref, nsa_new_ref, win_new_ref, gate_ref, ocmp_ref, o_ref = refs[nblk:]
    b = pl.program_id(0)
    q = q_ref[0]
    qf = q.astype(F32)
    rowg = lax.broadcasted_iota(jnp.int32, (H_NSA, 1), 0) // R_NSA
    ks_new = nsa_new_ref[0, :, 2 * LANES:3 * LANES]
    vs_new = nsa_new_ref[0, :, 3 * LANES:4 * LANES]
    s_new = jnp.sum(qf * ks_new, axis=1, keepdims=True)
    col = lax.broadcasted_iota(jnp.int32, (1, N_SELECT * ROWS_PER_PAGE), 1)
    colpage = col // ROWS_PER_PAGE
    colhalf = (col % ROWS_PER_PAGE) // SEL_BLOCK
    halves = ROWS_PER_PAGE // SEL_BLOCK
    o_sel = None
    for g in range(G_NSA):
        ks_t = jnp.concatenate([blocks[g * N_SELECT + k][0, 0:LANES, :] for k in range(N_SELECT)], axis=1)
        vs_t = jnp.concatenate([blocks[g * N_SELECT + k][0, LANES:2 * LANES, :] for k in range(N_SELECT)], axis=1)
        s = _dot(q, ks_t.astype(BF16))
        want = jnp.full(col.shape, -1, jnp.int32)
        for k in range(N_SELECT):
            j = idx_ref[b, g, k]
            half = jnp.where(j < n_past_blocks, j % halves, -1)
            want = jnp.where(colpage == k, half, want)
        valid = colhalf == want
        sm = jnp.where(valid, s, NEG_INF)
        m = jnp.maximum(jnp.max(sm, axis=1, keepdims=True), s_new)
        e = jnp.where(valid, jnp.exp(sm - m), 0.0)
        e_new = jnp.exp(s_new - m)
        l = jnp.sum(e, axis=1, keepdims=True) + e_new
        og = (_dot_nt(e.astype(BF16), vs_t.astype(BF16)) + e_new * vs_new) / l
        o_sel = og if o_sel is None else jnp.where(rowg == g, og, o_sel)

    kw_t = win_ref[0, 0:LANES, :]
    vw_t = win_ref[0, LANES:2 * LANES, :]
    kw_new = win_new_ref[0, :, 0:LANES]
    vw_new = win_new_ref[0, :, LANES:2 * LANES]
    sw = _dot(q, kw_t.astype(BF16))
    sw_new = jnp.sum(qf * kw_new, axis=1, keepdims=True)
    diff = win_buf - lax.broadcasted_iota(jnp.int32, (1, win_buf), 1)
    validw = (diff >= 0) & (diff < WINDOW)
    smw = jnp.where(validw, sw, NEG_INF)
    mw = jnp.maximum(jnp.max(smw, axis=1, keepdims=True), sw_new)
    ew = jnp.where(validw, jnp.exp(smw - mw), 0.0)
    ew_new = jnp.exp(sw_new - mw)
    lw = jnp.sum(ew, axis=1, keepdims=True) + ew_new
    o_win = (_dot_nt(ew.astype(BF16), vw_t.astype(BF16)) + ew_new * vw_new) / lw

    gt = gate_ref[0]
    o_ref[0] = gt[:, 0:1] * ocmp_ref[0] + gt[:, 1:2] * o_sel + gt[:, 2:3] * o_win


def _nsa_decode_attend(idx, table, cache_t, q2, win_t, nsa_new, win_new, gates, ocmp, n_past_blocks):
    nb = q2.shape[0]
    win_buf = win_t.shape[2]
    halves = ROWS_PER_PAGE // SEL_BLOCK

    def blk_map(b, idx_ref, tbl_ref, g, k):
        j = jnp.minimum(idx_ref[b, g, k], n_past_blocks - 1)
        return (tbl_ref[b, j // halves], 1, 0)

    blk_specs = [pl.BlockSpec((1, 2 * LANES, ROWS_PER_PAGE), functools.partial(blk_map, g=g, k=k))
                 for g in range(G_NSA) for k in range(N_SELECT)]
    per_b = lambda shape: pl.BlockSpec((1,) + shape, lambda b, i, t: (b, 0, 0))
    grid_spec = pltpu.PrefetchScalarGridSpec(
        num_scalar_prefetch=2,
        grid=(nb,),
        in_specs=blk_specs + [per_b((H_NSA, LANES)), per_b((2 * LANES, win_buf)), per_b((1, 4 * LANES)),
                              per_b((1, 2 * LANES)), per_b((H_NSA, LANES)), per_b((H_NSA, LANES))],
        out_specs=per_b((H_NSA, LANES)),
    )
    return pl.pallas_call(
        functools.partial(_nsa_decode_attend_kernel, n_past_blocks=n_past_blocks, win_buf=win_buf),
        grid_spec=grid_spec,
        out_shape=jax.ShapeDtypeStruct((nb, H_NSA, LANES), F32),
        compiler_params=_params("parallel"),
        name="nsa_decode_attend",
    )(idx, table, *([cache_t] * (G_NSA * N_SELECT)), q2, win_t, nsa_new, win_new, gates, ocmp)


def _outproj_kernel(h_ref, a_ref, b_ref, wa_ref, wb_ref, g_ref, o_ref):
    y = _dot(a_ref[...].astype(BF16), wa_ref[...]) + _dot(b_ref[...].astype(BF16), wb_ref[...])
    o_ref[...] = h_ref[...] + _rms(y, g_ref[...])


def _outproj(h, a, b, wa, wb, gain, tm):
    m = h.shape[0]
    const = lambda i: (0, 0)
    row = lambda i: (i, 0)
    return pl.pallas_call(
        _outproj_kernel,
        grid=(m // tm,),
        in_specs=[pl.BlockSpec((tm, D_MODEL), row), pl.BlockSpec((tm, a.shape[1]), row),
                  pl.BlockSpec((tm, b.shape[1]), row), pl.BlockSpec(wa.shape, const), pl.BlockSpec(wb.shape, const),
                  pl.BlockSpec((1, D_MODEL), const)],
        out_specs=pl.BlockSpec((tm, D_MODEL), row),
        out_shape=jax.ShapeDtypeStruct((m, D_MODEL), F32),
        compiler_params=_params("parallel"),
        name="mixer_out",
    )(h, a, b, wa, wb, gain)


def _ple_kernel(h_ref, p_ref, gpre_ref, wg_ref, wp_ref, gpost_ref, o_ref):
    h = h_ref[...]
    gate = jax.nn.sigmoid(_dot(_rms(h, gpre_ref[...]).astype(BF16), wg_ref[...]))
    o_ref[...] = h + _rms(gate * _dot(p_ref[...].astype(BF16), wp_ref[...]), gpost_ref[...])


def _ple(h, p, gpre, wg, wp, gpost, tm):
    m = h.shape[0]
    const = lambda i: (0, 0)
    row = lambda i: (i, 0)
    return pl.pallas_call(
        _ple_kernel,
        grid=(m // tm,),
        in_specs=[pl.BlockSpec((tm, D_MODEL), row), pl.BlockSpec((tm, PLE_DIM), row),
                  pl.BlockSpec((1, D_MODEL), const), pl.BlockSpec(wg.shape, const), pl.BlockSpec(wp.shape, const),
                  pl.BlockSpec((1, D_MODEL), const)],
        out_specs=pl.BlockSpec((tm, D_MODEL), row),
        out_shape=jax.ShapeDtypeStruct((m, D_MODEL), F32),
        compiler_params=_params("parallel"),
        name="ple",
    )(h, p, gpre, wg, wp, gpost)


def _split_in_cols(w):
    outs, off = [], 0
    for n in IN_SPLITS:
        outs.append(w[:, off:off + n])
        off += n
    return outs


def _prep_mixer_weights(w_in, w_a2, b_a):
    q_g, k_g, v_g, r_g, a_lr, q_n, kv_n, gate_n = _split_in_cols(w_in)
    pad_cols = lambda w, n: jnp.pad(w, ((0, 0), (0, n - w.shape[1])))
    gd = G_NSA * D_NSA
    k_sel = kv_n[:, 2 * gd:3 * gd]
    k_win = kv_n[:, 4 * gd:5 * gd]
    wn = jnp.concatenate([q_g, k_g, v_g, r_g, pad_cols(a_lr, LANES), k_sel, k_win], axis=1).astype(BF16)
    wt = jnp.concatenate([q_n, kv_n, pad_cols(gate_n, N_GATE_ROWS)], axis=1).T.astype(BF16)
    wa2 = jnp.pad(w_a2, ((0, LANES - GLA_RANK), (0, 0))).astype(BF16)
    return wn, wt, wa2, b_a.reshape(1, -1)


def _row_tile(m):
    return ROW_TILE if m % ROW_TILE == 0 else m


def _layer(x2, p2, mixer, w):
    tm = _row_tile(x2.shape[0])
    h = _ffn(x2, w["f1_pre"], w["f1_in"], w["f1_out"], w["f1_post"], tm)
    o_gla, o_nsa, extras = mixer(h)
    h = _outproj(h, o_gla, o_nsa, w["wo_gla"], w["wo_nsa"], w["m_post"], tm)
    h = _ffn(h, w["f2_pre"], w["f2_in"], w["f2_out"], w["f2_post"], tm)
    h = _ple(h, p2, w["ple_pre"], w["ple_gate"], w["ple_proj"], w["ple_post"], tm)
    return h, extras


def _mixer_prompt(h, w, nb, t):
    tm = _row_tile(h.shape[0])
    tabs = _rope_tables(jnp.arange(t, dtype=jnp.int32))
    (qk, v, r, la, ksel, kwin, qraw_t, qrot_t, gate_t, nsa_t, win_t) = _proj(
        h, w["m_pre"], w["wn"], w["wt"], w["wa2"], w["ba"], tabs, tm, t // tm)
    o_gla, s_fin = _gla_prompt(qk, la, v, r, w["gla_gain"], nb, t)
    table = jnp.zeros((nb, t // ROWS_PER_PAGE), jnp.int32)
    abk, abv = _cmp_partial(nsa_t, table, w["cmp_wk"], w["cmp_wv"], False)
    kc, _, vct = _cmp_finish(abk, abv, w["cmp_posk"], w["cmp_w1k"], w["cmp_w2k"],
                             w["cmp_posv"], w["cmp_w1v"], w["cmp_w2v"], w["cmp_w2vt"])
    o_nsa = _nsa_prompt(qraw_t, qrot_t, gate_t, kc, vct, ksel, nsa_t, kwin, win_t, nb, t)
    s_t = s_fin.reshape(nb, H_GLA, DV_GLA, 2, DK_GLA)
    s_own = jnp.stack([s_t[:, hh, :, hh % 2, :] for hh in range(H_GLA)], axis=1)
    gla_state = jnp.swapaxes(s_own, -1, -2)
    keep = min(WINDOW, t)
    rows_first = lambda a, n: jnp.transpose(a.reshape(nb, n, G_NSA, D_NSA, a.shape[-1]), (0, 4, 1, 2, 3))
    nsa_rows = rows_first(nsa_t, 4)
    win_rows = rows_first(win_t[:, :, t - keep:], 2)
    return o_gla, o_nsa, (nsa_rows, win_rows, gla_state)


def _mixer_sample(h, w, cache_l, win_l, gla_l, page_table):
    nb = h.shape[0]
    n_pages = page_table.shape[1]
    past_len = n_pages * cache_l.shape[1]
    pos = past_len
    tabs = _rope_tables(jnp.full((nb,), pos, dtype=jnp.int32))
    (qk, v, r, la, _, _, qraw_t, qrot_t, gate_t, nsa_new_t, win_new_t) = _proj(
        h, w["m_pre"], w["wn"], w["wt"], w["wa2"], w["ba"], tabs, nb, 1)
    nsa = nsa_new_t[0].T
    win = win_new_t[0].T

    nq = H_GLA * DK_GLA
    col = lambda a: a.reshape(nb, H_GLA, DK_GLA, 1)
    rowv = lambda a: a.reshape(nb, H_GLA, 1, DV_GLA)
    o_gla, gla_state = _gla_step(col(qk[:, :nq]), col(qk[:, nq:]), col(la), rowv(v), rowv(r),
                                 gla_l.astype(F32), w["gla_gain"])
    o_gla = o_gla.reshape(nb, H_GLA * DV_GLA)

    cache_t = jnp.transpose(cache_l.reshape(cache_l.shape[0], ROWS_PER_PAGE, 4 * LANES), (0, 2, 1))
    abk, abv = _cmp_partial(cache_t, page_table, w["cmp_wk"], w["cmp_wv"], True)
    kc, _, vct = _cmp_finish(abk, abv, w["cmp_posk"], w["cmp_w1k"], w["cmp_w2k"],
                             w["cmp_posv"], w["cmp_w1v"], w["cmp_w2v"], w["cmp_w2vt"])

    hg = (jnp.arange(H_NSA) // R_NSA)[None, :, None]

    def group_pad(q_t):
        q8 = q_t.T.reshape(nb, H_NSA, D_NSA)
        return jnp.concatenate([jnp.where(hg == 0, q8, 0), jnp.where(hg == 1, q8, 0)], axis=-1)

    q2_raw = group_pad(qraw_t)
    q2_rot = group_pad(qrot_t)
    q2_raw_t = jnp.pad(jnp.swapaxes(q2_raw, 1, 2), ((0, 0), (0, 0), (0, LANES - H_NSA)))
    n_blocks = -(-(past_len + 1) // SEL_BLOCK)
    idx_pad, ocmp_t = _nsa_decode_select(q2_raw_t, kc, vct, pos, n_blocks)
    idx = jnp.stack([idx_pad[:, :, g] for g in range(G_NSA)], axis=1)
    ocmp = jnp.swapaxes(ocmp_t, 1, 2)[:, :H_NSA, :]
    gates = jnp.pad(gate_t[:3 * H_NSA].T.reshape(nb, H_NSA, 3), ((0, 0), (0, 0), (0, LANES - 3)))
    wb = win_l.shape[1]
    win_buf_t = jnp.transpose(win_l.reshape(nb, wb, 2 * LANES), (0, 2, 1))
    o8 = _nsa_decode_attend(idx, page_table, cache_t, q2_rot, win_buf_t,
                            nsa.reshape(nb, 1, 4 * LANES), win.reshape(nb, 1, 2 * LANES), gates, ocmp,
                            past_len // SEL_BLOCK)
    o8 = o8.reshape(nb, H_NSA, G_NSA, D_NSA)
    o_nsa = jnp.concatenate([o8[:, :R_NSA, 0], o8[:, R_NSA:, 1]], axis=1).reshape(nb, H_NSA * D_NSA)

    nsa_rows = nsa.reshape(nb, 1, 4, G_NSA, D_NSA)
    win_new = win.reshape(nb, 1, 2, G_NSA, D_NSA)
    kw = jnp.concatenate([win_l, win_new.astype(win_l.dtype)], axis=1)
    keep = min(WINDOW, wb + 1)
    return o_gla, o_nsa, (nsa_rows, kw[:, wb + 1 - keep:], gla_state.astype(gla_l.dtype))


def kernel(x_prompt, x_sample, cache_nsa, state_win, state_gla, page_table, p_prompt, p_sample,
           ffn1_norm_pre, ffn1_norm_post, ffn1_w_in, ffn1_w_out,
           mix_norm_pre, mix_norm_post, w_mix_in, w_gla_a2, b_gla_a, gla_out_norm,
           cmp_pos_k, w_cmp_k1, w_cmp_k2, cmp_pos_v, w_cmp_v1, w_cmp_v2, w_mix_out,
           ffn2_norm_pre, ffn2_norm_post, ffn2_w_in, ffn2_w_out,
           ple_norm_pre, ple_w_gate, ple_w_proj, ple_norm_post):
    nb, t, _ = x_prompt.shape
    ns = x_sample.shape[0]
    depth = ffn1_w_in.shape[0]
    hp = x_prompt.reshape(nb * t, D_MODEL)
    hs = x_sample.reshape(ns, D_MODEL)
    outs = [[] for _ in range(6)]
    for i in range(depth):
        wn, wt, wa2, ba = _prep_mixer_weights(w_mix_in[i], w_gla_a2[i], b_gla_a[i])
        cmp_wk, cmp_posk, cmp_w1k, cmp_w2k, _ = _cmp_weights(cmp_pos_k[i], w_cmp_k1[i], w_cmp_k2[i])
        cmp_wv, cmp_posv, cmp_w1v, cmp_w2v, cmp_w2vt = _cmp_weights(cmp_pos_v[i], w_cmp_v1[i], w_cmp_v2[i])
        gla_w = H_GLA * DV_GLA
        w = dict(
            f1_pre=ffn1_norm_pre[i][None], f1_post=ffn1_norm_post[i][None],
            f1_in=ffn1_w_in[i].astype(BF16), f1_out=ffn1_w_out[i].astype(BF16),
            m_pre=mix_norm_pre[i][None], m_post=mix_norm_post[i][None],
            wn=wn, wt=wt, wa2=wa2, ba=ba, gla_gain=gla_out_norm[i][None],
            cmp_wk=cmp_wk, cmp_posk=cmp_posk, cmp_w1k=cmp_w1k, cmp_w2k=cmp_w2k,
            cmp_wv=cmp_wv, cmp_posv=cmp_posv, cmp_w1v=cmp_w1v, cmp_w2v=cmp_w2v, cmp_w2vt=cmp_w2vt,
            wo_gla=w_mix_out[i][:gla_w].astype(BF16), wo_nsa=w_mix_out[i][gla_w:].astype(BF16),
            f2_pre=ffn2_norm_pre[i][None], f2_post=ffn2_norm_post[i][None],
            f2_in=ffn2_w_in[i].astype(BF16), f2_out=ffn2_w_out[i].astype(BF16),
            ple_pre=ple_norm_pre[i][None], ple_post=ple_norm_post[i][None],
            ple_gate=ple_w_gate[i].astype(BF16), ple_proj=ple_w_proj[i].astype(BF16),
        )
        hp, (r_p, w_p, s_p) = _layer(hp, p_prompt[i].reshape(nb * t, PLE_DIM),
                                     functools.partial(_mixer_prompt, w=w, nb=nb, t=t), w)
        hs, (r_s, w_s, s_s) = _layer(hs, p_sample[i].reshape(ns, PLE_DIM),
                                     functools.partial(_mixer_sample, w=w, cache_l=cache_nsa[i], win_l=state_win[i],
                                                       gla_l=state_gla[i], page_table=page_table), w)
        for lst, val in zip(outs, (r_p, w_p, s_p, r_s, w_s, s_s)):
            lst.append(val)
    return (hp.reshape(nb, t, D_MODEL), hs.reshape(ns, 1, D_MODEL), *[jnp.stack(o) for o in outs])
```

```python
import functools

import numpy as np
import jax
import jax.numpy as jnp
from jax import lax
from jax.experimental import pallas as pl
from jax.experimental.pallas import tpu as pltpu

F32 = jnp.float32
BF16 = jnp.bfloat16

D_MODEL = 1024
PLE_DIM = 256
D_FF = 2816
EPS = 1e-6
H_GLA = 4
DK_GLA = 64
DV_GLA = 128
GLA_RANK = 16
GLA_GATE_TEMP = 16.0
GLA_CHUNK = 64
H_NSA = 8
G_NSA = 2
R_NSA = H_NSA // G_NSA
D_NSA = 64
CMP_LEN = 32
CMP_STRIDE = 16
CMP_HIDDEN = 128
SEL_BLOCK = 64
N_SELECT = 16
WINDOW = 512
FORCE_SCORE = 1e4
NEG_INF = -1e30
M_INIT = -1e29
ATTN_SCALE = D_NSA ** -0.5
LOG2E = 1.4426950408889634
QK_SCALE = ATTN_SCALE * LOG2E
ROPE_THETA = 500000.0
ROPE_DIM = D_NSA // 4
ROPE_HALF = ROPE_DIM // 2
IN_SPLITS = (H_GLA * DK_GLA, H_GLA * DK_GLA, H_GLA * DV_GLA, H_GLA * DV_GLA, GLA_RANK,
             H_NSA * D_NSA, 6 * G_NSA * D_NSA, 3 * H_NSA)

LANES = 128
SUBLANES = 8
VMEM_LIMIT = 56 * 1024 * 1024

ROW_TILE = 512
FF_CHUNK = 256
Q_TILE = 128
K_TILE = 512
V_PAD_ROWS = 16
SEL_COLS = 256
PAGES_PER_STEP = 32
ROWS_PER_PAGE = 128
N_GLA_LEVELS = 6
GLA_CHUNKS_PER_STEP = 2
PAD_BLOCKS = 384


def _params(*sem):
    return pltpu.CompilerParams(dimension_semantics=sem, vmem_limit_bytes=VMEM_LIMIT)


def _rms(x, g):
    return x * lax.rsqrt(jnp.mean(x * x, axis=-1, keepdims=True) + EPS) * g


def _dot(a, b):
    return jnp.dot(a, b, preferred_element_type=F32)


def _dot_nt(a, b):
    return lax.dot_general(a, b, (((1,), (1,)), ((), ())), preferred_element_type=F32)


def _dot_tn(a, b):
    return lax.dot_general(a, b, (((0,), (0,)), ((), ())), preferred_element_type=F32)


def _split_bf16(x, n):
    parts = []
    r = x
    for _ in range(n):
        p = r.astype(BF16)
        parts.append(p)
        r = r - p.astype(F32)
    return parts


def _dot01_left(m01, x, n):
    out = None
    for p in _split_bf16(x, n):
        t = _dot(m01, p)
        out = t if out is None else out + t
    return out


def _dot01_right(x, m01, n):
    out = None
    for p in _split_bf16(x, n):
        t = _dot(p, m01)
        out = t if out is None else out + t
    return out


def _masked_softmax2_rows(s, valid):
    sm = jnp.where(valid, s, NEG_INF)
    m = jnp.max(sm, axis=0, keepdims=True)
    e = jnp.exp2(sm - m)
    return jnp.where(valid, e / jnp.sum(e, axis=0, keepdims=True), 0.0)


def _ffn_kernel(x_ref, gpre_ref, wg_ref, wu_ref, wo_ref, gpost_ref, o_ref):
    x = x_ref[...]
    xn = _rms(x, gpre_ref[...]).astype(BF16)
    acc = jnp.zeros(x.shape, F32)
    for c in range(D_FF // FF_CHUNK):
        sl = slice(c * FF_CHUNK, (c + 1) * FF_CHUNK)
        g = _dot(xn, wg_ref[:, sl])
        u = _dot(xn, wu_ref[:, sl])
        a = (jax.nn.silu(g) * u).astype(BF16)
        acc = acc + _dot(a, wo_ref[sl, :])
    o_ref[...] = x + 0.5 * _rms(acc, gpost_ref[...])


def _ffn(x, gpre, w_in, w_out, gpost, tm):
    m = x.shape[0]
    const = lambda i: (0, 0)
    return pl.pallas_call(
        _ffn_kernel,
        grid=(m // tm,),
        in_specs=[
            pl.BlockSpec((tm, D_MODEL), lambda i: (i, 0)),
            pl.BlockSpec((1, D_MODEL), const),
            pl.BlockSpec((D_MODEL, D_FF), const),
            pl.BlockSpec((D_MODEL, D_FF), lambda i: (0, 1)),
            pl.BlockSpec((D_FF, D_MODEL), const),
            pl.BlockSpec((1, D_MODEL), const),
        ],
        out_specs=pl.BlockSpec((tm, D_MODEL), lambda i: (i, 0)),
        out_shape=jax.ShapeDtypeStruct((m, D_MODEL), F32),
        compiler_params=_params("parallel"),
        name="ffn",
    )(x, gpre, w_in, w_in, w_out, gpost)


N_GLA_COLS = 2 * H_GLA * DK_GLA + 2 * H_GLA * DV_GLA
N_KV_COLS = 6 * G_NSA * D_NSA
WN_COLS = N_GLA_COLS + 3 * LANES
N_Q_COLS = H_NSA * D_NSA
N_GATE_ROWS = 32
WT_ROWS = N_Q_COLS + N_KV_COLS + N_GATE_ROWS
N_CACHE_FEATS = 4 * G_NSA * D_NSA
N_WIN_FEATS = 2 * G_NSA * D_NSA


def _rope_rows(x, cos, sin):
    out = []
    for h in range(x.shape[0] // D_NSA):
        b = h * D_NSA
        x1 = x[b:b + ROPE_HALF]
        x2 = x[b + ROPE_HALF:b + ROPE_DIM]
        out += [x1 * cos - x2 * sin, x2 * cos + x1 * sin, x[b + ROPE_DIM:b + D_NSA]]
    return jnp.concatenate(out, axis=0)


def _rope_lanes(x, c, s1, s2):
    return x * c + pltpu.roll(x, LANES - ROPE_HALF, 1) * s1 + pltpu.roll(x, ROPE_HALF, 1) * s2


def _proj_kernel(h_ref, g_ref, wn_ref, wt_ref, wa2_ref, ba_ref, rc_ref, rs1_ref, rs2_ref, cos_ref, sin_ref,
                 qk_ref, v_ref, r_ref, la_ref, ksel_ref, kwin_ref,
                 qraw_ref, qrot_ref, gate_ref, nsat_ref, wint_ref):
    xn = _rms(h_ref[...], g_ref[...]).astype(BF16)
    nqk = 2 * H_GLA * DK_GLA
    nv = H_GLA * DV_GLA
    z = _dot(xn, wn_ref[:, 0:nqk])
    qk_ref[:, 0:nqk // 2] = z[:, 0:nqk // 2] * (DK_GLA ** -0.5)
    qk_ref[:, nqk // 2:nqk] = z[:, nqk // 2:nqk]
    v_ref[...] = _dot(xn, wn_ref[:, nqk:nqk + nv])
    r_ref[...] = _dot(xn, wn_ref[:, nqk + nv:N_GLA_COLS])
    a_lr = _dot(xn, wn_ref[:, N_GLA_COLS:N_GLA_COLS + LANES])
    xa = _dot(a_lr.astype(BF16), wa2_ref[...]) + ba_ref[...]
    la_ref[...] = (jnp.minimum(xa, 0.0) - jnp.log1p(jnp.exp(-jnp.abs(xa)))) * (1.0 / GLA_GATE_TEMP)
    k0 = N_GLA_COLS + LANES
    rc, rs1, rs2 = rc_ref[...], rs1_ref[...], rs2_ref[...]
    ksel_ref[...] = _rope_lanes(_dot(xn, wn_ref[:, k0:k0 + LANES]), rc, rs1, rs2)
    kwin_ref[...] = _rope_lanes(_dot(xn, wn_ref[:, k0 + LANES:k0 + 2 * LANES]), rc, rs1, rs2)
    zt = _dot_nt(wt_ref[...], xn)
    cos, sin = cos_ref[...], sin_ref[...]
    q = zt[0:N_Q_COLS] * QK_SCALE
    qraw_ref[...] = q.astype(BF16)
    qrot_ref[...] = _rope_rows(q, cos, sin).astype(BF16)
    kv = zt[N_Q_COLS:N_Q_COLS + N_KV_COLS]
    nsat_ref[0, 0:256, :] = kv[0:256]
    nsat_ref[0, 256:384, :] = _rope_rows(kv[256:384], cos, sin)
    nsat_ref[0, 384:512, :] = kv[384:512]
    wint_ref[0, 0:128, :] = _rope_rows(kv[512:640], cos, sin)
    wint_ref[0, 128:256, :] = kv[640:768]
    gate_ref[...] = jax.nn.sigmoid(zt[N_Q_COLS + N_KV_COLS:WT_ROWS])


def _proj(h, gain, wn, wt, wa2, ba, tabs, tm, tiles_per_seq):
    m = h.shape[0]
    rc, rs1, rs2, cos_t, sin_t = tabs
    const = lambda i: (0, 0)
    row = lambda i: (i, 0)
    col = lambda i: (0, i)
    tab_row = lambda i: (i % tiles_per_seq, 0)
    tab_col = lambda i: (0, i % tiles_per_seq)
    outs = [
        (2 * H_GLA * DK_GLA, F32), (H_GLA * DV_GLA, F32), (H_GLA * DV_GLA, F32), (H_GLA * DK_GLA, F32),
        (LANES, F32), (LANES, F32),
    ]
    outs_t = [(N_Q_COLS, BF16), (N_Q_COLS, BF16), (N_GATE_ROWS, F32)]
    nseq = m // (tm * tiles_per_seq)
    seq_len = tm * tiles_per_seq
    outs_seq = [N_CACHE_FEATS, N_WIN_FEATS]
    seq_map = lambda i: (i // tiles_per_seq, 0, i % tiles_per_seq)
    return pl.pallas_call(
        _proj_kernel,
        grid=(m // tm,),
        in_specs=[
            pl.BlockSpec((tm, D_MODEL), row),
            pl.BlockSpec((1, D_MODEL), const),
            pl.BlockSpec((D_MODEL, WN_COLS), const),
            pl.BlockSpec((WT_ROWS, D_MODEL), const),
            pl.BlockSpec((LANES, H_GLA * DK_GLA), const),
            pl.BlockSpec((1, H_GLA * DK_GLA), const),
            pl.BlockSpec((tm, LANES), tab_row),
            pl.BlockSpec((tm, LANES), tab_row),
            pl.BlockSpec((tm, LANES), tab_row),
            pl.BlockSpec((ROPE_HALF, tm), tab_col),
            pl.BlockSpec((ROPE_HALF, tm), tab_col),
        ],
        out_specs=[pl.BlockSpec((tm, n), row) for n, _ in outs] + [pl.BlockSpec((n, tm), col) for n, _ in outs_t]
        + [pl.BlockSpec((1, n, tm), seq_map) for n in outs_seq],
        out_shape=[jax.ShapeDtypeStruct((m, n), d) for n, d in outs]
        + [jax.ShapeDtypeStruct((n, m), d) for n, d in outs_t]
        + [jax.ShapeDtypeStruct((nseq, n, seq_len), F32) for n in outs_seq],
        compiler_params=_params("parallel"),
        name="mixer_proj",
    )(h, gain, wn, wt, wa2, ba, rc, rs1, rs2, cos_t, sin_t)


def _rope_tables(pos):
    inv_freq = ROPE_THETA ** (-jnp.arange(ROPE_HALF, dtype=F32) * 2.0 / ROPE_DIM)
    ang = pos.astype(F32)[:, None] * inv_freq[None, :]
    cos, sin = jnp.cos(ang), jnp.sin(ang)
    n = pos.shape[0]
    one = jnp.ones((n, D_NSA - ROPE_DIM), F32)
    zero = jnp.zeros((n, D_NSA - ROPE_DIM), F32)
    zh = jnp.zeros((n, ROPE_HALF), F32)
    c = jnp.concatenate([cos, cos, one], axis=1)
    s1 = jnp.concatenate([-sin, zh, zero], axis=1)
    s2 = jnp.concatenate([zh, sin, zero], axis=1)
    dup = lambda t: jnp.concatenate([t, t], axis=1)
    return dup(c), dup(s1), dup(s2), cos.T, sin.T


def _gla_constants():
    c = GLA_CHUNK
    t = np.arange(c)
    low = (t[None, :] <= t[:, None]).astype(np.float32)
    mats, masks = [low], []
    for lev in range(1, N_GLA_LEVELS + 1):
        seg = (2 * c) >> lev
        half = seg // 2
        mid = (t // seg) * seg + half
        mats.append(low[mid])
        same = (t[:, None] // seg) == (t[None, :] // seg)
        masks.append((same & ((t[:, None] % seg) >= half) & ((t[None, :] % seg) < half)).astype(np.float32))
    masks.append(np.eye(c, dtype=np.float32))
    return np.concatenate(mats, axis=0), np.stack(masks)


def _gla_kernel(qk_ref, la_ref, v_ref, r_ref, gain_ref, big_ref, mask_ref, o_ref, sfin_ref, st_ref):
    c = pl.program_id(1)
    ch = GLA_CHUNK

    @pl.when(c == 0)
    def _():
        st_ref[...] = jnp.zeros(st_ref.shape, F32)

    lane = lax.broadcasted_iota(jnp.int32, (1, LANES), 1)
    head_mask = [jnp.where(lane < DK_GLA, 1.0, 0.0), jnp.where(lane >= DK_GLA, 1.0, 0.0)]
    big = big_ref[...]
    gain = gain_ref[...]
    nq = H_GLA * DK_GLA
    n_chunks = qk_ref.shape[0] // ch
    units = [(ci, p) for ci in range(n_chunks) for p in range(H_GLA // 2)]

    allbs = {}
    for ci, p in units:
        rows = slice(ci * ch, (ci + 1) * ch)
        allbs[ci, p] = _dot01_left(big, la_ref[rows, p * LANES:(p + 1) * LANES], 3)
    intra = {}
    for ci, p in units:
        rows = slice(ci * ch, (ci + 1) * ch)
        q = qk_ref[rows, p * LANES:(p + 1) * LANES]
        k = qk_ref[rows, nq + p * LANES:nq + (p + 1) * LANES]
        allb = allbs[ci, p]
        b = allb[0:ch]
        b_last = b[ch - 1:ch]
        attn = [jnp.zeros((ch, ch), F32), jnp.zeros((ch, ch), F32)]
        for lev in range(N_GLA_LEVELS + 1):
            if lev < N_GLA_LEVELS:
                ref = allb[(lev + 1) * ch:(lev + 2) * ch]
                ql = q * jnp.exp(jnp.minimum(b - ref, 0.0))
                kl = k * jnp.exp(jnp.minimum(ref - b, 0.0))
            else:
                ql, kl = q, k
            qq = jnp.concatenate([ql * head_mask[0], ql * head_mask[1]], axis=0).astype(BF16)
            s = _dot_nt(qq, kl.astype(BF16))
            mk = mask_ref[lev]
            attn[0] = attn[0] + mk * s[0:ch]
            attn[1] = attn[1] + mk * s[ch:2 * ch]
        q0 = q * jnp.exp(b)
        k_hat = (k * jnp.exp(b_last - b)).astype(BF16)
        for hh in range(2):
            hs = slice((2 * p + hh) * DV_GLA, (2 * p + hh + 1) * DV_GLA)
            vh = v_ref[rows, hs].astype(BF16)
            intra[ci, 2 * p + hh] = (_dot(attn[hh].astype(BF16), vh), (q0 * head_mask[hh]).astype(BF16),
                                     _dot_tn(vh, k_hat), jnp.exp(b_last))
    for ci in range(n_chunks):
        rows = slice(ci * ch, (ci + 1) * ch)
        for h in range(H_GLA):
            hs = slice(h * DV_GLA, (h + 1) * DV_GLA)
            o_intra, q0h, kv, decay = intra[ci, h]
            st = st_ref[h]
            o = o_intra + _dot_nt(q0h, st.astype(BF16))
            st_ref[h] = st * decay + kv
            o_ref[rows, hs] = _rms(o, gain) * jax.nn.silu(r_ref[rows, hs])

    @pl.when(c == pl.num_programs(1) - 1)
    def _():
        sfin_ref[0] = st_ref[...]


def _gla_prompt(qk, la, v, r, gain, nb, t):
    big, masks = _gla_constants()
    rows = GLA_CHUNK * GLA_CHUNKS_PER_STEP
    nc = t // rows
    row = lambda b, c: (b * nc + c, 0)
    const2 = lambda b, c: (0, 0)
    m = nb * t
    return pl.pallas_call(
        _gla_kernel,
        grid=(nb, nc),
        in_specs=[
            pl.BlockSpec((rows, 2 * H_GLA * DK_GLA), row),
            pl.BlockSpec((rows, H_GLA * DK_GLA), row),
            pl.BlockSpec((rows, H_GLA * DV_GLA), row),
            pl.BlockSpec((rows, H_GLA * DV_GLA), row),
            pl.BlockSpec((1, DV_GLA), const2),
            pl.BlockSpec(big.shape, const2),
            pl.BlockSpec(masks.shape, lambda b, c: (0, 0, 0)),
        ],
        out_specs=[
            pl.BlockSpec((rows, H_GLA * DV_GLA), row),
            pl.BlockSpec((1, H_GLA, DV_GLA, LANES), lambda b, c: (b, 0, 0, 0)),
        ],
        out_shape=[
            jax.ShapeDtypeStruct((m, H_GLA * DV_GLA), F32),
            jax.ShapeDtypeStruct((nb, H_GLA, DV_GLA, LANES), F32),
        ],
        scratch_shapes=[pltpu.VMEM((H_GLA, DV_GLA, LANES), F32)],
        compiler_params=_params("parallel", "arbitrary"),
        name="gla_scan",
    )(qk, la, v, r, gain, jnp.asarray(big, BF16), jnp.asarray(masks, F32))


def _gla_step_kernel(q_ref, k_ref, a_ref, v_ref, r_ref, s_ref, gain_ref, o_ref, sn_ref):
    for h in range(H_GLA):
        s_new = jnp.exp(a_ref[0, h]) * s_ref[0, h] + k_ref[0, h] * v_ref[0, h]
        sn_ref[0, h] = s_new
        o = jnp.sum(q_ref[0, h] * s_new, axis=0, keepdims=True)
        o_ref[0, h] = _rms(o, gain_ref[...]) * jax.nn.silu(r_ref[0, h])


def _gla_step(qcol, kcol, acol, vrow, rrow, state, gain):
    nb = state.shape[0]
    col = pl.BlockSpec((1, H_GLA, DK_GLA, 1), lambda b: (b, 0, 0, 0))
    rowspec = pl.BlockSpec((1, H_GLA, 1, DV_GLA), lambda b: (b, 0, 0, 0))
    stspec = pl.BlockSpec((1, H_GLA, DK_GLA, DV_GLA), lambda b: (b, 0, 0, 0))
    return pl.pallas_call(
        _gla_step_kernel,
        grid=(nb,),
        in_specs=[col, col, col, rowspec, rowspec, stspec, pl.BlockSpec((1, DV_GLA), lambda b: (0, 0))],
        out_specs=[rowspec, stspec],
        out_shape=[jax.ShapeDtypeStruct((nb, H_GLA, 1, DV_GLA), F32),
                   jax.ShapeDtypeStruct((nb, H_GLA, DK_GLA, DV_GLA), F32)],
        compiler_params=_params("parallel"),
        name="gla_step",
    )(qcol, kcol, acol, vrow, rrow, state, gain)


HALF_ROWS = CMP_STRIDE
CHUNKS_PER_PAGE = ROWS_PER_PAGE // HALF_ROWS


def _cmp_partial_kernel(tbl_ref, *refs, pps):
    del tbl_ref
    pages = refs[:pps]
    wk_ref, wv_ref, abk_ref, abv_ref, tk_ref, tv_ref = refs[pps:]
    for i in range(pps):
        tk_ref[i] = pages[i][0, 0:LANES, :].T
        tv_ref[i] = pages[i][0, LANES:2 * LANES, :].T
    lpair = 2 * LANES
    for t_ref, w_ref, ab_ref in ((tk_ref, wk_ref, abk_ref), (tv_ref, wv_ref, abv_ref)):
        acc = None
        for lp in range(HALF_ROWS // 2):
            cols = [jnp.concatenate([t_ref[i, pl.ds(l, CHUNKS_PER_PAGE, stride=HALF_ROWS), :] for i in range(pps)], axis=0)
                    for l in (2 * lp, 2 * lp + 1)]
            x = jnp.concatenate(cols, axis=1).astype(BF16)
            part = _dot(x, w_ref[lp * lpair:(lp + 1) * lpair, :])
            acc = part if acc is None else acc + part
        ab_ref[0] = acc


def _cmp_partial(pages_t, table, wk, wv, paged):
    nb, npages = table.shape
    pps = min(PAGES_PER_STEP, npages)
    nsteps = npages // pps
    nch = npages * CHUNKS_PER_PAGE
    step_chunks = pps * CHUNKS_PER_PAGE

    def page_map(b, s, tbl, i):
        return (tbl[b, s * pps + i], 0, 0) if paged else (b, 0, s * pps + i)

    page_specs = [pl.BlockSpec((1, 2 * LANES, ROWS_PER_PAGE), functools.partial(page_map, i=i)) for i in range(pps)]
    wspec = pl.BlockSpec(wk.shape, lambda b, s, tbl: (0, 0))
    ospec = pl.BlockSpec((1, step_chunks, 4 * CMP_HIDDEN), lambda b, s, tbl: (b, s, 0))
    grid_spec = pltpu.PrefetchScalarGridSpec(
        num_scalar_prefetch=1,
        grid=(nb, nsteps),
        in_specs=page_specs + [wspec, wspec],
        out_specs=[ospec, ospec],
        scratch_shapes=[pltpu.VMEM((pps, ROWS_PER_PAGE, LANES), F32), pltpu.VMEM((pps, ROWS_PER_PAGE, LANES), F32)],
    )
    osh = jax.ShapeDtypeStruct((nb, nch, 4 * CMP_HIDDEN), F32)
    return pl.pallas_call(
        functools.partial(_cmp_partial_kernel, pps=pps),
        grid_spec=grid_spec,
        out_shape=[osh, osh],
        compiler_params=_params("parallel", "parallel"),
        name="cmp_partial",
    )(table, *([pages_t] * pps), wk, wv)


def _cmp_finish_kernel(abk_ref, abv_ref, posk_ref, w1k_ref, w2k_ref, posv_ref, w1v_ref, w2v_ref, w2vt_ref,
                       kc_ref, vc_ref, vct_ref):
    nch = abk_ref.shape[1]
    pk = _dot(posk_ref[...], w1k_ref[...])[0:1]
    pv = _dot(posv_ref[...], w1v_ref[...])[0:1]
    abk = abk_ref[0]
    abv = abv_ref[0]
    kc = None
    vc = None
    for g in range(G_NSA):
        o = g * 2 * CMP_HIDDEN
        hk = jax.nn.gelu(abk[:, o:o + CMP_HIDDEN] + pltpu.roll(abk[:, o + CMP_HIDDEN:o + 2 * CMP_HIDDEN], nch - 1, 0) + pk)
        hv = jax.nn.gelu(abv[:, o:o + CMP_HIDDEN] + pltpu.roll(abv[:, o + CMP_HIDDEN:o + 2 * CMP_HIDDEN], nch - 1, 0) + pv)
        hk = hk.astype(BF16)
        hv = hv.astype(BF16)
        tk = _dot(hk, w2k_ref[g])
        tv = _dot(hv, w2v_ref[g])
        kc = tk if kc is None else kc + tk
        vc = tv if vc is None else vc + tv
        vct_ref[0, g * D_NSA:(g + 1) * D_NSA, :] = _dot_nt(w2vt_ref[...], hv).astype(BF16)
    kc_ref[0] = kc.astype(BF16)
    vc_ref[0] = vc.astype(BF16)


def _cmp_finish(abk, abv, posk, w1k, w2k_pad, posv, w1v, w2v_pad, w2vt):
    nb, nch, _ = abk.shape
    ab = pl.BlockSpec((1, nch, 4 * CMP_HIDDEN), lambda b: (b, 0, 0))
    c2 = lambda b: (0, 0)
    c3 = lambda b: (0, 0, 0)
    return pl.pallas_call(
        _cmp_finish_kernel,
        grid=(nb,),
        in_specs=[ab, ab,
                  pl.BlockSpec(posk.shape, c2), pl.BlockSpec(w1k.shape, c2), pl.BlockSpec(w2k_pad.shape, c3),
                  pl.BlockSpec(posv.shape, c2), pl.BlockSpec(w1v.shape, c2), pl.BlockSpec(w2v_pad.shape, c3),
                  pl.BlockSpec(w2vt.shape, c2)],
        out_specs=[pl.BlockSpec((1, nch, LANES), lambda b: (b, 0, 0)),
                   pl.BlockSpec((1, nch, LANES), lambda b: (b, 0, 0)),
                   pl.BlockSpec((1, LANES, nch), lambda b: (b, 0, 0))],
        out_shape=[jax.ShapeDtypeStruct((nb, nch, LANES), BF16),
                   jax.ShapeDtypeStruct((nb, nch, LANES), BF16),
                   jax.ShapeDtypeStruct((nb, LANES, nch), BF16)],
        compiler_params=_params("parallel"),
        name="cmp_finish",
    )(abk, abv, posk, w1k, w2k_pad, posv, w1v, w2v_pad, w2vt)


def _cmp_weights(pos, w1, w2):
    w = w1.reshape(2, HALF_ROWS, D_NSA, CMP_HIDDEN)
    z = jnp.zeros_like(w[0])
    blocks = []
    for g in range(G_NSA):
        cols = []
        for g2 in range(G_NSA):
            for half in range(2):
                cols.append(w[half] if g2 == g else z)
        blocks.append(jnp.concatenate(cols, axis=-1))
    wbig = jnp.stack(blocks, axis=1).reshape(HALF_ROWS * LANES, 4 * CMP_HIDDEN).astype(BF16)
    pos8 = jnp.broadcast_to(pos.reshape(1, CMP_LEN * D_NSA), (SUBLANES, CMP_LEN * D_NSA)).astype(BF16)
    z2 = jnp.zeros_like(w2)
    w2pad = jnp.stack([jnp.concatenate([w2, z2], axis=1), jnp.concatenate([z2, w2], axis=1)]).astype(BF16)
    return wbig, pos8, w1.astype(BF16), w2pad, w2.T.astype(BF16)


def _overlap_t(nc_pad, ns_pad, nc, ns):
    i = np.arange(nc_pad)[None, :] * CMP_STRIDE
    j = np.arange(ns_pad)[:, None] * SEL_BLOCK
    m = (i < j + SEL_BLOCK) & (i + CMP_LEN > j) & (np.arange(nc_pad)[None, :] < nc) & (np.arange(ns_pad)[:, None] < ns)
    return jnp.asarray(m.astype(np.float32), BF16)


def _rank_desc(x):
    nrow = x.shape[0]
    nblk = nrow // SUBLANES
    blocks = [x[v * SUBLANES:(v + 1) * SUBLANES] for v in range(nblk)]
    cnt = [jnp.zeros((SUBLANES, x.shape[1]), F32) for _ in range(nblk)]
    sub = lax.broadcasted_iota(jnp.int32, (SUBLANES, x.shape[1]), 0)
    for jp in range(nrow):
        row = x[jp:jp + 1]
        vb = jp // SUBLANES
        for v in range(nblk):
            ge = jnp.where(row >= blocks[v], 1.0, 0.0)
            gt = jnp.where(row > blocks[v], 1.0, 0.0)
            if v > vb:
                cnt[v] = cnt[v] + ge
            elif v < vb:
                cnt[v] = cnt[v] + gt
            else:
                cnt[v] = cnt[v] + jnp.where(sub > (jp % SUBLANES), ge, gt)
    return jnp.concatenate(cnt, axis=0)


def _nsa_prompt_kernel(qraw_ref, qrot_ref, gate_ref, kc_ref, vct_ref, ksel_ref, vselt_ref, kwin_ref, vwint_ref,
                       ovl_ref, o_ref, kaug_ref, kwb_ref, vsa_ref, vwa_ref):
    qb = pl.program_id(1)
    nlane = R_NSA * Q_TILE
    tpos = qb * Q_TILE + lax.broadcasted_iota(jnp.int32, (1, nlane), 1) % Q_TILE
    tpos1 = tpos[:, 0:Q_TILE]
    ncp = kc_ref.shape[1]
    nsb = ovl_ref.shape[0]
    seq = ksel_ref.shape[0]
    zeros_q = jnp.zeros((D_NSA, nlane), BF16)

    @pl.when(qb == 0)
    def _():
        ks = ksel_ref[...]
        lane = lax.broadcasted_iota(jnp.int32, (1, LANES), 1)
        blk = lax.broadcasted_iota(jnp.int32, (seq, 1), 0) // SEL_BLOCK
        kaug_ref[0] = jnp.where(lane < D_NSA, ks, jnp.where(lane - D_NSA == blk, 1.0, 0.0)).astype(BF16)
        kaug_ref[1] = jnp.where(lane >= D_NSA, ks, jnp.where(lane == blk, 1.0, 0.0)).astype(BF16)
        kwb_ref[...] = kwin_ref[...].astype(BF16)
        ones_row = jnp.where(lax.broadcasted_iota(jnp.int32, (V_PAD_ROWS, seq), 0) == 0, 1.0, 0.0)
        for g in range(G_NSA):
            gs = slice(g * D_NSA, (g + 1) * D_NSA)
            vsa_ref[g] = jnp.concatenate([vselt_ref[0, gs, :], ones_row], axis=0).astype(BF16)
            vwa_ref[g] = jnp.concatenate([vwint_ref[0, gs, :], ones_row], axis=0).astype(BF16)

    def group_q(ref, g):
        q = jnp.concatenate([ref[h * D_NSA:(h + 1) * D_NSA, :] for h in range(g * R_NSA, (g + 1) * R_NSA)], axis=1)
        return q, (jnp.concatenate([q, zeros_q], axis=0) if g == 0 else jnp.concatenate([zeros_q, q], axis=0))

    cmp_scores = [_dot(kc_ref[0], group_q(qraw_ref, g)[1]) for g in range(G_NSA)]
    o_cmps, qaugs, qpads = [], [], []
    for g in range(G_NSA):
        qrot, qpad = group_q(qrot_ref, g)
        gs = slice(g * D_NSA, (g + 1) * D_NSA)

        sc = cmp_scores[g]
        ci = lax.broadcasted_iota(jnp.int32, (ncp, 1), 0)
        p_cmp = _masked_softmax2_rows(sc, ci * CMP_STRIDE + (CMP_LEN - 1) <= tpos)
        o_cmp = _dot(vct_ref[0, gs, :], p_cmp.astype(BF16))
        psum = p_cmp[:, 0:Q_TILE]
        for r in range(1, R_NSA):
            psum = psum + p_cmp[:, r * Q_TILE:(r + 1) * Q_TILE]
        imp = _dot01_left(ovl_ref[...], psum, 2)
        bj = lax.broadcasted_iota(jnp.int32, (nsb, 1), 0)
        cur = tpos1 // SEL_BLOCK
        forced = (bj == 0) | (bj == cur) | (bj == cur - 1)
        imp = jnp.where(forced, FORCE_SCORE, jnp.where(bj * SEL_BLOCK <= tpos1, imp, -1.0))
        bias = jnp.where(_rank_desc(imp) < N_SELECT, 0.0, NEG_INF).astype(BF16)
        bias = jnp.concatenate([bias] * R_NSA, axis=1)
        if nsb < D_NSA:
            bias = jnp.concatenate([bias, jnp.zeros((D_NSA - nsb, nlane), BF16)], axis=0)
        o_cmps.append(o_cmp)
        qaugs.append(jnp.concatenate([qrot, bias], axis=0) if g == 0 else jnp.concatenate([bias, qrot], axis=0))
        qpads.append(qpad)

    krow = lax.broadcasted_iota(jnp.int32, (K_TILE, 1), 0)
    ncol = nlane // SEL_COLS
    pairs = [(g, slice(j * SEL_COLS, (j + 1) * SEL_COLS)) for g in range(G_NSA) for j in range(ncol)]

    def sel_step(c, carry, causal):
        start = pl.multiple_of(c * K_TILE, K_TILE)
        out = []
        scores = [_dot(kaug_ref[g, pl.ds(start, K_TILE), :], qaugs[g][:, cols]) for g, cols in pairs]
        for (g, cols), (m, acc), s in zip(pairs, carry, scores):
            if causal:
                s = jnp.where(start + krow <= tpos[:, cols], s, NEG_INF)
            m_new = jnp.maximum(m, jnp.max(s, axis=0, keepdims=True))
            e = jnp.exp2(s - m_new).astype(BF16)
            acc = jnp.exp2(m - m_new) * acc + _dot(vsa_ref[g, :, pl.ds(start, K_TILE)], e)
            out.append((m_new, acc))
        return tuple(out)

    n_full = (qb * Q_TILE) // K_TILE
    init = tuple((jnp.full((1, SEL_COLS), M_INIT, F32), jnp.zeros((D_NSA + V_PAD_ROWS, SEL_COLS), F32))
                 for _ in pairs)
    carry = lax.fori_loop(0, n_full, functools.partial(sel_step, causal=False), init)

    band = WINDOW + Q_TILE
    wstart = pl.multiple_of(jnp.maximum(qb * Q_TILE - WINDOW, 0), Q_TILE)
    win_scores = [_dot(kwb_ref[pl.ds(wstart, band), :], qpads[g]) for g in range(G_NSA)]

    carry = sel_step(n_full, carry, True)
    o_sels = []
    for g in range(G_NSA):
        acc_sel = jnp.concatenate([carry[g * ncol + j][1] for j in range(ncol)], axis=1)
        o_sels.append(acc_sel[0:D_NSA] / acc_sel[D_NSA:D_NSA + 1])

    diff = tpos - (wstart + lax.broadcasted_iota(jnp.int32, (band, 1), 0))
    in_window = (diff >= 0) & (diff < WINDOW)
    heads = []
    for g in range(G_NSA):
        sw = jnp.where(in_window, win_scores[g], NEG_INF)
        e_win = jnp.exp2(sw - jnp.max(sw, axis=0, keepdims=True)).astype(BF16)
        acc_win = _dot(vwa_ref[g, :, pl.ds(wstart, band)], e_win)
        o_win = acc_win[0:D_NSA] / acc_win[D_NSA:D_NSA + 1]
        for r in range(R_NSA):
            h = g * R_NSA + r
            ls = slice(r * Q_TILE, (r + 1) * Q_TILE)
            heads.append(gate_ref[3 * h:3 * h + 1, :] * o_cmps[g][:, ls]
                         + gate_ref[3 * h + 1:3 * h + 2, :] * o_sels[g][:, ls]
                         + gate_ref[3 * h + 2:3 * h + 3, :] * o_win[:, ls])
    o_ref[...] = jnp.concatenate(heads, axis=0).T


def _nsa_prompt(qraw_t, qrot_t, gate_t, kc, vct, ksel, nsa_t, kwin, win_t, nb, t):
    nqb = t // Q_TILE
    ncp = kc.shape[1]
    ovl = _overlap_t(ncp, t // SEL_BLOCK, (t - CMP_LEN) // CMP_STRIDE + 1, t // SEL_BLOCK)
    qcol = lambda b, q: (0, b * nqb + q)
    m = nb * t
    return pl.pallas_call(
        _nsa_prompt_kernel,
        grid=(nb, nqb),
        in_specs=[
            pl.BlockSpec((N_Q_COLS, Q_TILE), qcol),
            pl.BlockSpec((N_Q_COLS, Q_TILE), qcol),
            pl.BlockSpec((N_GATE_ROWS, Q_TILE), qcol),
            pl.BlockSpec((1, ncp, LANES), lambda b, q: (b, 0, 0)),
            pl.BlockSpec((1, LANES, ncp), lambda b, q: (b, 0, 0)),
            pl.BlockSpec((t, LANES), lambda b, q: (b, 0)),
            pl.BlockSpec((1, LANES, t), lambda b, q: (b, 3, 0)),
            pl.BlockSpec((t, LANES), lambda b, q: (b, 0)),
            pl.BlockSpec((1, LANES, t), lambda b, q: (b, 1, 0)),
            pl.BlockSpec(ovl.shape, lambda b, q: (0, 0)),
        ],
        out_specs=pl.BlockSpec((Q_TILE, N_Q_COLS), lambda b, q: (b * nqb + q, 0)),
        out_shape=jax.ShapeDtypeStruct((m, N_Q_COLS), F32),
        scratch_shapes=[pltpu.VMEM((G_NSA, t, LANES), BF16), pltpu.VMEM((t, LANES), BF16),
                        pltpu.VMEM((G_NSA, D_NSA + V_PAD_ROWS, t), BF16),
                        pltpu.VMEM((G_NSA, D_NSA + V_PAD_ROWS, t), BF16)],
        compiler_params=_params("parallel", "arbitrary"),
        name="nsa_prompt",
    )(qraw_t, qrot_t, gate_t, kc, vct, ksel, nsa_t, kwin, win_t, ovl)


def _nsa_decode_select_kernel(qt_ref, kc_ref, vct_ref, ovl_ref, grp_ref, idx_ref, ocmp_ref, *, pos, n_blocks):
    ncp = kc_ref.shape[1]
    sc = _dot(kc_ref[0], qt_ref[0])
    ci = lax.broadcasted_iota(jnp.int32, (ncp, 1), 0)
    p = _masked_softmax2_rows(sc, ci * CMP_STRIDE + (CMP_LEN - 1) <= pos)
    ocmp_ref[0] = _dot(vct_ref[0], p.astype(BF16))
    psum = _dot01_right(p, grp_ref[...], 2)
    imp_t = _dot01_left(ovl_ref[...], psum, 2)
    imp_r = imp_t.T
    cur = pos // SEL_BLOCK

    def finish(v, j):
        forced = (j == 0) | (j == cur) | (j == cur - 1)
        v = jnp.where(forced, FORCE_SCORE, jnp.where(j * SEL_BLOCK <= pos, v, -1.0))
        return jnp.where(j < n_blocks, v, -3e38)

    jc = lax.broadcasted_iota(jnp.int32, (PAD_BLOCKS, 1), 0)
    jr = lax.broadcasted_iota(jnp.int32, (1, PAD_BLOCKS), 1)
    kk = lax.broadcasted_iota(jnp.int32, (N_SELECT, 1), 0).astype(F32)
    lane = lax.broadcasted_iota(jnp.int32, (N_SELECT, LANES), 1)
    out = jnp.zeros((N_SELECT, LANES), F32)
    for g in range(G_NSA):
        c0 = g * R_NSA
        col = finish(imp_t[:, c0:c0 + 1], jc)
        row = finish(imp_r[c0:c0 + 1, :], jr)
        beats = (col > row) | ((col == row) & (jc < jr))
        rank = jnp.sum(jnp.where(beats, 1.0, 0.0), axis=0, keepdims=True)
        hit = rank == kk
        idx = jnp.sum(jnp.where(hit, jr.astype(F32), 0.0), axis=1, keepdims=True)
        out = jnp.where(lane == g, idx, out)
    idx_ref[0] = out.astype(jnp.int32)


def _nsa_decode_select(q_t, kc, vct, pos, n_blocks):
    nb, ncp, _ = kc.shape
    nc = (pos + 1 - CMP_LEN) // CMP_STRIDE + 1
    ovl = _overlap_t(ncp, PAD_BLOCKS, nc, n_blocks)
    hh = np.arange(LANES)
    grp = ((hh[:, None] // R_NSA) == (hh[None, :] // R_NSA)) & (hh[:, None] < H_NSA) & (hh[None, :] < H_NSA)
    grp = jnp.asarray(grp.astype(np.float32), BF16)
    return pl.pallas_call(
        functools.partial(_nsa_decode_select_kernel, pos=pos, n_blocks=n_blocks),
        grid=(nb,),
        in_specs=[
            pl.BlockSpec((1, LANES, LANES), lambda b: (b, 0, 0)),
            pl.BlockSpec((1, ncp, LANES), lambda b: (b, 0, 0)),
            pl.BlockSpec((1, LANES, ncp), lambda b: (b, 0, 0)),
            pl.BlockSpec(ovl.shape, lambda b: (0, 0)),
            pl.BlockSpec(grp.shape, lambda b: (0, 0)),
        ],
        out_specs=[pl.BlockSpec((1, N_SELECT, LANES), lambda b: (b, 0, 0)),
                   pl.BlockSpec((1, LANES, LANES), lambda b: (b, 0, 0))],
        out_shape=[jax.ShapeDtypeStruct((nb, N_SELECT, LANES), jnp.int32),
                   jax.ShapeDtypeStruct((nb, LANES, LANES), F32)],
        compiler_params=_params("parallel"),
        name="nsa_decode_select",
    )(q_t, kc, vct, ovl, grp)


def _nsa_decode_attend_kernel(idx_ref, tbl_ref, *refs, n_past_blocks, win_buf):
    del tbl_ref
    nblk = G_NSA * N_SELECT
    blocks = refs[:nblk]
    q_ref, win_ref, nsa_new_ref, win_new_ref, gate_ref, ocmp_ref, o_ref = refs[nblk:]
    b = pl.program_id(0)
    q = q_ref[0]
    qf = q.astype(F32)
    rowg = lax.broadcasted_iota(jnp.int32, (H_NSA, 1), 0) // R_NSA
    ks_new = nsa_new_ref[0, :, 2 * LANES:3 * LANES]
    vs_new = nsa_new_ref[0, :, 3 * LANES:4 * LANES]
    s_new = jnp.sum(qf * ks_new, axis=1, keepdims=True)
    col = lax.broadcasted_iota(jnp.int32, (1, N_SELECT * ROWS_PER_PAGE), 1)
    colpage = col // ROWS_PER_PAGE
    colhalf = (col % ROWS_PER_PAGE) // SEL_BLOCK
    halves = ROWS_PER_PAGE // SEL_BLOCK
    o_sel = None
    for g in range(G_NSA):
        ks_t = jnp.concatenate([blocks[g * N_SELECT + k][0, 0:LANES, :] for k in range(N_SELECT)], axis=1)
        vs_t = jnp.concatenate([blocks[g * N_SELECT + k][0, LANES:2 * LANES, :] for k in range(N_SELECT)], axis=1)
        s = _dot(q, ks_t.astype(BF16))
        want = jnp.full(col.shape, -1, jnp.int32)
        for k in range(N_SELECT):
            j = idx_ref[b, g, k]
            half = jnp.where(j < n_past_blocks, j % halves, -1)
            want = jnp.where(colpage == k, half, want)
        valid = colhalf == want
        sm = jnp.where(valid, s, NEG_INF)
        m = jnp.maximum(jnp.max(sm, axis=1, keepdims=True), s_new)
        e = jnp.where(valid, jnp.exp2(sm - m), 0.0)
        e_new = jnp.exp2(s_new - m)
        l = jnp.sum(e, axis=1, keepdims=True) + e_new
        og = (_dot_nt(e.astype(BF16), vs_t.astype(BF16)) + e_new * vs_new) / l
        o_sel = og if o_sel is None else jnp.where(rowg == g, og, o_sel)

    kw_t = win_ref[0, 0:LANES, :]
    vw_t = win_ref[0, LANES:2 * LANES, :]
    kw_new = win_new_ref[0, :, 0:LANES]
    vw_new = win_new_ref[0, :, LANES:2 * LANES]
    sw = _dot(q, kw_t.astype(BF16))
    sw_new = jnp.sum(qf * kw_new, axis=1, keepdims=True)
    diff = win_buf - lax.broadcasted_iota(jnp.int32, (1, win_buf), 1)
    validw = (diff >= 0) & (diff < WINDOW)
    smw = jnp.where(validw, sw, NEG_INF)
    mw = jnp.maximum(jnp.max(smw, axis=1, keepdims=True), sw_new)
    ew = jnp.where(validw, jnp.exp2(smw - mw), 0.0)
    ew_new = jnp.exp2(sw_new - mw)
    lw = jnp.sum(ew, axis=1, keepdims=True) + ew_new
    o_win = (_dot_nt(ew.astype(BF16), vw_t.astype(BF16)) + ew_new * vw_new) / lw

    gt = gate_ref[0]
    o_ref[0] = gt[:, 0:1] * ocmp_ref[0] + gt[:, 1:2] * o_sel + gt[:, 2:3] * o_win


def _nsa_decode_attend(idx, table, cache_t, q2, win_t, nsa_new, win_new, gates, ocmp, n_past_blocks):
    nb = q2.shape[0]
    win_buf = win_t.shape[2]
    halves = ROWS_PER_PAGE // SEL_BLOCK

    def blk_map(b, idx_ref, tbl_ref, g, k):
        j = jnp.minimum(idx_ref[b, g, k], n_past_blocks - 1)
        return (tbl_ref[b, j // halves], 1, 0)

    blk_specs = [pl.BlockSpec((1, 2 * LANES, ROWS_PER_PAGE), functools.partial(blk_map, g=g, k=k))
                 for g in range(G_NSA) for k in range(N_SELECT)]
    per_b = lambda shape: pl.BlockSpec((1,) + shape, lambda b, i, t: (b, 0, 0))
    grid_spec = pltpu.PrefetchScalarGridSpec(
        num_scalar_prefetch=2,
        grid=(nb,),
        in_specs=blk_specs + [per_b((H_NSA, LANES)), per_b((2 * LANES, win_buf)), per_b((1, 4 * LANES)),
                              per_b((1, 2 * LANES)), per_b((H_NSA, LANES)), per_b((H_NSA, LANES))],
        out_specs=per_b((H_NSA, LANES)),
    )
    return pl.pallas_call(
        functools.partial(_nsa_decode_attend_kernel, n_past_blocks=n_past_blocks, win_buf=win_buf),
        grid_spec=grid_spec,
        out_shape=jax.ShapeDtypeStruct((nb, H_NSA, LANES), F32),
        compiler_params=_params("parallel"),
        name="nsa_decode_attend",
    )(idx, table, *([cache_t] * (G_NSA * N_SELECT)), q2, win_t, nsa_new, win_new, gates, ocmp)


def _outproj_kernel(h_ref, a_ref, b_ref, wa_ref, wb_ref, g_ref, o_ref):
    y = _dot(a_ref[...].astype(BF16), wa_ref[...]) + _dot(b_ref[...].astype(BF16), wb_ref[...])
    o_ref[...] = h_ref[...] + _rms(y, g_ref[...])


def _outproj(h, a, b, wa, wb, gain, tm):
    m = h.shape[0]
    const = lambda i: (0, 0)
    row = lambda i: (i, 0)
    return pl.pallas_call(
        _outproj_kernel,
        grid=(m // tm,),
        in_specs=[pl.BlockSpec((tm, D_MODEL), row), pl.BlockSpec((tm, a.shape[1]), row),
                  pl.BlockSpec((tm, b.shape[1]), row), pl.BlockSpec(wa.shape, const), pl.BlockSpec(wb.shape, const),
                  pl.BlockSpec((1, D_MODEL), const)],
        out_specs=pl.BlockSpec((tm, D_MODEL), row),
        out_shape=jax.ShapeDtypeStruct((m, D_MODEL), F32),
        compiler_params=_params("parallel"),
        name="mixer_out",
    )(h, a, b, wa, wb, gain)


def _ple_kernel(h_ref, p_ref, gpre_ref, wg_ref, wp_ref, gpost_ref, o_ref):
    h = h_ref[...]
    gate = jax.nn.sigmoid(_dot(_rms(h, gpre_ref[...]).astype(BF16), wg_ref[...]))
    o_ref[...] = h + _rms(gate * _dot(p_ref[...].astype(BF16), wp_ref[...]), gpost_ref[...])


def _ple(h, p, gpre, wg, wp, gpost, tm):
    m = h.shape[0]
    const = lambda i: (0, 0)
    row = lambda i: (i, 0)
    return pl.pallas_call(
        _ple_kernel,
        grid=(m // tm,),
        in_specs=[pl.BlockSpec((tm, D_MODEL), row), pl.BlockSpec((tm, PLE_DIM), row),
                  pl.BlockSpec((1, D_MODEL), const), pl.BlockSpec(wg.shape, const), pl.BlockSpec(wp.shape, const),
                  pl.BlockSpec((1, D_MODEL), const)],
        out_specs=pl.BlockSpec((tm, D_MODEL), row),
        out_shape=jax.ShapeDtypeStruct((m, D_MODEL), F32),
        compiler_params=_params("parallel"),
        name="ple",
    )(h, p, gpre, wg, wp, gpost)


def _split_in_cols(w):
    outs, off = [], 0
    for n in IN_SPLITS:
        outs.append(w[:, off:off + n])
        off += n
    return outs


def _prep_mixer_weights(w_in, w_a2, b_a):
    q_g, k_g, v_g, r_g, a_lr, q_n, kv_n, gate_n = _split_in_cols(w_in)
    pad_cols = lambda w, n: jnp.pad(w, ((0, 0), (0, n - w.shape[1])))
    gd = G_NSA * D_NSA
    k_sel = kv_n[:, 2 * gd:3 * gd]
    k_win = kv_n[:, 4 * gd:5 * gd]
    wn = jnp.concatenate([q_g, k_g, v_g, r_g, pad_cols(a_lr, LANES), k_sel, k_win], axis=1).astype(BF16)
    wt = jnp.concatenate([q_n, kv_n, pad_cols(gate_n, N_GATE_ROWS)], axis=1).T.astype(BF16)
    wa2 = jnp.pad(w_a2, ((0, LANES - GLA_RANK), (0, 0))).astype(BF16)
    return wn, wt, wa2, b_a.reshape(1, -1)


def _row_tile(m):
    return ROW_TILE if m % ROW_TILE == 0 else m


def _layer(x2, p2, mixer, w):
    tm = _row_tile(x2.shape[0])
    h = _ffn(x2, w["f1_pre"], w["f1_in"], w["f1_out"], w["f1_post"], tm)
    o_gla, o_nsa, extras = mixer(h)
    h = _outproj(h, o_gla, o_nsa, w["wo_gla"], w["wo_nsa"], w["m_post"], tm)
    h = _ffn(h, w["f2_pre"], w["f2_in"], w["f2_out"], w["f2_post"], tm)
    h = _ple(h, p2, w["ple_pre"], w["ple_gate"], w["ple_proj"], w["ple_post"], tm)
    return h, extras


def _mixer_prompt(h, w, nb, t):
    tm = _row_tile(h.shape[0])
    tabs = _rope_tables(jnp.arange(t, dtype=jnp.int32))
    (qk, v, r, la, ksel, kwin, qraw_t, qrot_t, gate_t, nsa_t, win_t) = _proj(
        h, w["m_pre"], w["wn"], w["wt"], w["wa2"], w["ba"], tabs, tm, t // tm)
    o_gla, s_fin = _gla_prompt(qk, la, v, r, w["gla_gain"], nb, t)
    table = jnp.zeros((nb, t // ROWS_PER_PAGE), jnp.int32)
    abk, abv = _cmp_partial(nsa_t, table, w["cmp_wk"], w["cmp_wv"], False)
    kc, _, vct = _cmp_finish(abk, abv, w["cmp_posk"], w["cmp_w1k"], w["cmp_w2k"],
                             w["cmp_posv"], w["cmp_w1v"], w["cmp_w2v"], w["cmp_w2vt"])
    o_nsa = _nsa_prompt(qraw_t, qrot_t, gate_t, kc, vct, ksel, nsa_t, kwin, win_t, nb, t)
    s_t = s_fin.reshape(nb, H_GLA, DV_GLA, 2, DK_GLA)
    s_own = jnp.stack([s_t[:, hh, :, hh % 2, :] for hh in range(H_GLA)], axis=1)
    gla_state = jnp.swapaxes(s_own, -1, -2)
    keep = min(WINDOW, t)
    rows_first = lambda a, n: jnp.transpose(a.reshape(nb, n, G_NSA, D_NSA, a.shape[-1]), (0, 4, 1, 2, 3))
    nsa_rows = rows_first(nsa_t, 4)
    win_rows = rows_first(win_t[:, :, t - keep:], 2)
    return o_gla, o_nsa, (nsa_rows, win_rows, gla_state)


def _mixer_sample(h, w, cache_l, win_l, gla_l, page_table):
    nb = h.shape[0]
    n_pages = page_table.shape[1]
    past_len = n_pages * cache_l.shape[1]
    pos = past_len
    tabs = _rope_tables(jnp.full((nb,), pos, dtype=jnp.int32))
    (qk, v, r, la, _, _, qraw_t, qrot_t, gate_t, nsa_new_t, win_new_t) = _proj(
        h, w["m_pre"], w["wn"], w["wt"], w["wa2"], w["ba"], tabs, nb, 1)
    nsa = nsa_new_t[0].T
    win = win_new_t[0].T

    nq = H_GLA * DK_GLA
    col = lambda a: a.reshape(nb, H_GLA, DK_GLA, 1)
    rowv = lambda a: a.reshape(nb, H_GLA, 1, DV_GLA)
    o_gla, gla_state = _gla_step(col(qk[:, :nq]), col(qk[:, nq:]), col(la), rowv(v), rowv(r),
                                 gla_l.astype(F32), w["gla_gain"])
    o_gla = o_gla.reshape(nb, H_GLA * DV_GLA)

    cache_t = jnp.transpose(cache_l.reshape(cache_l.shape[0], ROWS_PER_PAGE, 4 * LANES), (0, 2, 1))
    abk, abv = _cmp_partial(cache_t, page_table, w["cmp_wk"], w["cmp_wv"], True)
    kc, _, vct = _cmp_finish(abk, abv, w["cmp_posk"], w["cmp_w1k"], w["cmp_w2k"],
                             w["cmp_posv"], w["cmp_w1v"], w["cmp_w2v"], w["cmp_w2vt"])

    hg = (jnp.arange(H_NSA) // R_NSA)[None, :, None]

    def group_pad(q_t):
        q8 = q_t.T.reshape(nb, H_NSA, D_NSA)
        return jnp.concatenate([jnp.where(hg == 0, q8, 0), jnp.where(hg == 1, q8, 0)], axis=-1)

    q2_raw = group_pad(qraw_t)
    q2_rot = group_pad(qrot_t)
    q2_raw_t = jnp.pad(jnp.swapaxes(q2_raw, 1, 2), ((0, 0), (0, 0), (0, LANES - H_NSA)))
    n_blocks = -(-(past_len + 1) // SEL_BLOCK)
    idx_pad, ocmp_t = _nsa_decode_select(q2_raw_t, kc, vct, pos, n_blocks)
    idx = jnp.stack([idx_pad[:, :, g] for g in range(G_NSA)], axis=1)
    ocmp = jnp.swapaxes(ocmp_t, 1, 2)[:, :H_NSA, :]
    gates = jnp.pad(gate_t[:3 * H_NSA].T.reshape(nb, H_NSA, 3), ((0, 0), (0, 0), (0, LANES - 3)))
    wb = win_l.shape[1]
    win_buf_t = jnp.transpose(win_l.reshape(nb, wb, 2 * LANES), (0, 2, 1))
    o8 = _nsa_decode_attend(idx, page_table, cache_t, q2_rot, win_buf_t,
                            nsa.reshape(nb, 1, 4 * LANES), win.reshape(nb, 1, 2 * LANES), gates, ocmp,
                            past_len // SEL_BLOCK)
    o8 = o8.reshape(nb, H_NSA, G_NSA, D_NSA)
    o_nsa = jnp.concatenate([o8[:, :R_NSA, 0], o8[:, R_NSA:, 1]], axis=1).reshape(nb, H_NSA * D_NSA)

    nsa_rows = nsa.reshape(nb, 1, 4, G_NSA, D_NSA)
    win_new = win.reshape(nb, 1, 2, G_NSA, D_NSA)
    kw = jnp.concatenate([win_l, win_new.astype(win_l.dtype)], axis=1)
    keep = min(WINDOW, wb + 1)
    return o_gla, o_nsa, (nsa_rows, kw[:, wb + 1 - keep:], gla_state.astype(gla_l.dtype))


def kernel(x_prompt, x_sample, cache_nsa, state_win, state_gla, page_table, p_prompt, p_sample,
           ffn1_norm_pre, ffn1_norm_post, ffn1_w_in, ffn1_w_out,
           mix_norm_pre, mix_norm_post, w_mix_in, w_gla_a2, b_gla_a, gla_out_norm,
           cmp_pos_k, w_cmp_k1, w_cmp_k2, cmp_pos_v, w_cmp_v1, w_cmp_v2, w_mix_out,
           ffn2_norm_pre, ffn2_norm_post, ffn2_w_in, ffn2_w_out,
           ple_norm_pre, ple_w_gate, ple_w_proj, ple_norm_post):
    nb, t, _ = x_prompt.shape
    ns = x_sample.shape[0]
    depth = ffn1_w_in.shape[0]
    hp = x_prompt.reshape(nb * t, D_MODEL)
    hs = x_sample.reshape(ns, D_MODEL)
    outs = [[] for _ in range(6)]
    for i in range(depth):
        wn, wt, wa2, ba = _prep_mixer_weights(w_mix_in[i], w_gla_a2[i], b_gla_a[i])
        cmp_wk, cmp_posk, cmp_w1k, cmp_w2k, _ = _cmp_weights(cmp_pos_k[i], w_cmp_k1[i], w_cmp_k2[i])
        cmp_wv, cmp_posv, cmp_w1v, cmp_w2v, cmp_w2vt = _cmp_weights(cmp_pos_v[i], w_cmp_v1[i], w_cmp_v2[i])
        gla_w = H_GLA * DV_GLA
        w = dict(
            f1_pre=ffn1_norm_pre[i][None], f1_post=ffn1_norm_post[i][None],
            f1_in=ffn1_w_in[i].astype(BF16), f1_out=ffn1_w_out[i].astype(BF16),
            m_pre=mix_norm_pre[i][None], m_post=mix_norm_post[i][None],
            wn=wn, wt=wt, wa2=wa2, ba=ba, gla_gain=gla_out_norm[i][None],
            cmp_wk=cmp_wk, cmp_posk=cmp_posk, cmp_w1k=cmp_w1k, cmp_w2k=cmp_w2k,
            cmp_wv=cmp_wv, cmp_posv=cmp_posv, cmp_w1v=cmp_w1v, cmp_w2v=cmp_w2v, cmp_w2vt=cmp_w2vt,
            wo_gla=w_mix_out[i][:gla_w].astype(BF16), wo_nsa=w_mix_out[i][gla_w:].astype(BF16),
            f2_pre=ffn2_norm_pre[i][None], f2_post=ffn2_norm_post[i][None],
            f2_in=ffn2_w_in[i].astype(BF16), f2_out=ffn2_w_out[i].astype(BF16),
            ple_pre=ple_norm_pre[i][None], ple_post=ple_norm_post[i][None],
            ple_gate=ple_w_gate[i].astype(BF16), ple_proj=ple_w_proj[i].astype(BF16),
        )
        hp, (r_p, w_p, s_p) = _layer(hp, p_prompt[i].reshape(nb * t, PLE_DIM),
                                     functools.partial(_mixer_prompt, w=w, nb=nb, t=t), w)
        hs, (r_s, w_s, s_s) = _layer(hs, p_sample[i].reshape(ns, PLE_DIM),
                                     functools.partial(_mixer_sample, w=w, cache_l=cache_nsa[i], win_l=state_win[i],
                                                       gla_l=state_gla[i], page_table=page_table), w)
        for lst, val in zip(outs, (r_p, w_p, s_p, r_s, w_s, s_s)):
            lst.append(val)
    return (hp.reshape(nb, t, D_MODEL), hs.reshape(ns, 1, D_MODEL), *[jnp.stack(o) for o in outs])
```

```python
import functools

import numpy as np
import jax
import jax.numpy as jnp
from jax import lax
from jax.experimental import pallas as pl
from jax.experimental.pallas import tpu as pltpu

F32 = jnp.float32
BF16 = jnp.bfloat16

D_MODEL = 1024
PLE_DIM = 256
D_FF = 2816
EPS = 1e-6
H_GLA = 4
DK_GLA = 64
DV_GLA = 128
GLA_RANK = 16
GLA_GATE_TEMP = 16.0
GLA_CHUNK = 64
H_NSA = 8
G_NSA = 2
R_NSA = H_NSA // G_NSA
D_NSA = 64
CMP_LEN = 32
CMP_STRIDE = 16
CMP_HIDDEN = 128
SEL_BLOCK = 64
N_SELECT = 16
WINDOW = 512
FORCE_SCORE = 1e4
NEG_INF = -1e30
M_INIT = -1e29
ATTN_SCALE = D_NSA ** -0.5
LOG2E = 1.4426950408889634
QK_SCALE = ATTN_SCALE * LOG2E
ROPE_THETA = 500000.0
ROPE_DIM = D_NSA // 4
ROPE_HALF = ROPE_DIM // 2
IN_SPLITS = (H_GLA * DK_GLA, H_GLA * DK_GLA, H_GLA * DV_GLA, H_GLA * DV_GLA, GLA_RANK,
             H_NSA * D_NSA, 6 * G_NSA * D_NSA, 3 * H_NSA)

LANES = 128
SUBLANES = 8
VMEM_LIMIT = 56 * 1024 * 1024

ROW_TILE = 512
FF_CHUNK = 256
Q_TILE = 256
K_TILE = 512
V_PAD_ROWS = 16
SEL_COLS = 256
PAGES_PER_STEP = 64
ROWS_PER_PAGE = 128
N_GLA_LEVELS = 6
GLA_CHUNKS_PER_STEP = 2
PAD_BLOCKS = 384


def _params(*sem):
    return pltpu.CompilerParams(dimension_semantics=sem, vmem_limit_bytes=VMEM_LIMIT)


def _rms(x, g):
    return x * lax.rsqrt(jnp.mean(x * x, axis=-1, keepdims=True) + EPS) * g


def _dot(a, b):
    return jnp.dot(a, b, preferred_element_type=F32)


def _dot_nt(a, b):
    return lax.dot_general(a, b, (((1,), (1,)), ((), ())), preferred_element_type=F32)


def _dot_tn(a, b):
    return lax.dot_general(a, b, (((0,), (0,)), ((), ())), preferred_element_type=F32)


def _split_bf16(x, n):
    parts = []
    r = x
    for _ in range(n):
        p = r.astype(BF16)
        parts.append(p)
        r = r - p.astype(F32)
    return parts


def _dot01_left(m01, x, n):
    out = None
    for p in _split_bf16(x, n):
        t = _dot(m01, p)
        out = t if out is None else out + t
    return out


def _dot01_right(x, m01, n):
    out = None
    for p in _split_bf16(x, n):
        t = _dot(p, m01)
        out = t if out is None else out + t
    return out


def _masked_softmax2_rows(s, valid):
    sm = jnp.where(valid, s, NEG_INF)
    m = jnp.max(sm, axis=0, keepdims=True)
    e = jnp.exp2(sm - m)
    return jnp.where(valid, e / jnp.sum(e, axis=0, keepdims=True), 0.0)


def _ffn_kernel(x_ref, gpre_ref, wg_ref, wu_ref, wo_ref, gpost_ref, o_ref):
    x = x_ref[...]
    xn = _rms(x, gpre_ref[...]).astype(BF16)
    acc = jnp.zeros(x.shape, F32)
    for c in range(D_FF // FF_CHUNK):
        sl = slice(c * FF_CHUNK, (c + 1) * FF_CHUNK)
        g = _dot(xn, wg_ref[:, sl])
        u = _dot(xn, wu_ref[:, sl])
        a = (jax.nn.silu(g) * u).astype(BF16)
        acc = acc + _dot(a, wo_ref[sl, :])
    o_ref[...] = x + 0.5 * _rms(acc, gpost_ref[...])


def _ffn(x, gpre, w_in, w_out, gpost, tm):
    m = x.shape[0]
    const = lambda i: (0, 0)
    return pl.pallas_call(
        _ffn_kernel,
        grid=(m // tm,),
        in_specs=[
            pl.BlockSpec((tm, D_MODEL), lambda i: (i, 0)),
            pl.BlockSpec((1, D_MODEL), const),
            pl.BlockSpec((D_MODEL, D_FF), const),
            pl.BlockSpec((D_MODEL, D_FF), lambda i: (0, 1)),
            pl.BlockSpec((D_FF, D_MODEL), const),
            pl.BlockSpec((1, D_MODEL), const),
        ],
        out_specs=pl.BlockSpec((tm, D_MODEL), lambda i: (i, 0)),
        out_shape=jax.ShapeDtypeStruct((m, D_MODEL), F32),
        compiler_params=_params("parallel"),
        name="ffn",
    )(x, gpre, w_in, w_in, w_out, gpost)


N_GLA_COLS = 2 * H_GLA * DK_GLA + 2 * H_GLA * DV_GLA
N_KV_COLS = 6 * G_NSA * D_NSA
WN_COLS = N_GLA_COLS + 3 * LANES
N_Q_COLS = H_NSA * D_NSA
N_GATE_ROWS = 32
WT_ROWS = N_Q_COLS + N_KV_COLS + N_GATE_ROWS
N_CACHE_FEATS = 4 * G_NSA * D_NSA
N_WIN_FEATS = 2 * G_NSA * D_NSA


def _rope_rows(x, cos, sin):
    out = []
    for h in range(x.shape[0] // D_NSA):
        b = h * D_NSA
        x1 = x[b:b + ROPE_HALF]
        x2 = x[b + ROPE_HALF:b + ROPE_DIM]
        out += [x1 * cos - x2 * sin, x2 * cos + x1 * sin, x[b + ROPE_DIM:b + D_NSA]]
    return jnp.concatenate(out, axis=0)


def _rope_lanes(x, c, s1, s2):
    return x * c + pltpu.roll(x, LANES - ROPE_HALF, 1) * s1 + pltpu.roll(x, ROPE_HALF, 1) * s2


def _proj_kernel(h_ref, g_ref, wn_ref, wt_ref, wa2_ref, ba_ref, rc_ref, rs1_ref, rs2_ref, cos_ref, sin_ref,
                 qk_ref, v_ref, r_ref, la_ref, ksel_ref, kwin_ref,
                 qraw_ref, qrot_ref, gate_ref, nsat_ref, wint_ref):
    xn = _rms(h_ref[...], g_ref[...]).astype(BF16)
    nqk = 2 * H_GLA * DK_GLA
    nv = H_GLA * DV_GLA
    z = _dot(xn, wn_ref[:, 0:nqk])
    qk_ref[:, 0:nqk // 2] = z[:, 0:nqk // 2] * (DK_GLA ** -0.5)
    qk_ref[:, nqk // 2:nqk] = z[:, nqk // 2:nqk]
    v_ref[...] = _dot(xn, wn_ref[:, nqk:nqk + nv])
    r_ref[...] = _dot(xn, wn_ref[:, nqk + nv:N_GLA_COLS])
    a_lr = _dot(xn, wn_ref[:, N_GLA_COLS:N_GLA_COLS + LANES])
    xa = _dot(a_lr.astype(BF16), wa2_ref[...]) + ba_ref[...]
    la_ref[...] = (jnp.minimum(xa, 0.0) - jnp.log1p(jnp.exp(-jnp.abs(xa)))) * (1.0 / GLA_GATE_TEMP)
    k0 = N_GLA_COLS + LANES
    rc, rs1, rs2 = rc_ref[...], rs1_ref[...], rs2_ref[...]
    ksel_ref[...] = _rope_lanes(_dot(xn, wn_ref[:, k0:k0 + LANES]), rc, rs1, rs2)
    kwin_ref[...] = _rope_lanes(_dot(xn, wn_ref[:, k0 + LANES:k0 + 2 * LANES]), rc, rs1, rs2)
    zt = _dot_nt(wt_ref[...], xn)
    cos, sin = cos_ref[...], sin_ref[...]
    q = zt[0:N_Q_COLS] * QK_SCALE
    qraw_ref[...] = q.astype(BF16)
    qrot_ref[...] = _rope_rows(q, cos, sin).astype(BF16)
    kv = zt[N_Q_COLS:N_Q_COLS + N_KV_COLS]
    nsat_ref[0, 0:256, :] = kv[0:256]
    nsat_ref[0, 256:384, :] = _rope_rows(kv[256:384], cos, sin)
    nsat_ref[0, 384:512, :] = kv[384:512]
    wint_ref[0, 0:128, :] = _rope_rows(kv[512:640], cos, sin)
    wint_ref[0, 128:256, :] = kv[640:768]
    gate_ref[...] = jax.nn.sigmoid(zt[N_Q_COLS + N_KV_COLS:WT_ROWS])


def _proj(h, gain, wn, wt, wa2, ba, tabs, tm, tiles_per_seq):
    m = h.shape[0]
    rc, rs1, rs2, cos_t, sin_t = tabs
    const = lambda i: (0, 0)
    row = lambda i: (i, 0)
    col = lambda i: (0, i)
    tab_row = lambda i: (i % tiles_per_seq, 0)
    tab_col = lambda i: (0, i % tiles_per_seq)
    outs = [
        (2 * H_GLA * DK_GLA, F32), (H_GLA * DV_GLA, F32), (H_GLA * DV_GLA, F32), (H_GLA * DK_GLA, F32),
        (LANES, F32), (LANES, F32),
    ]
    outs_t = [(N_Q_COLS, BF16), (N_Q_COLS, BF16), (N_GATE_ROWS, F32)]
    nseq = m // (tm * tiles_per_seq)
    seq_len = tm * tiles_per_seq
    outs_seq = [N_CACHE_FEATS, N_WIN_FEATS]
    seq_map = lambda i: (i // tiles_per_seq, 0, i % tiles_per_seq)
    return pl.pallas_call(
        _proj_kernel,
        grid=(m // tm,),
        in_specs=[
            pl.BlockSpec((tm, D_MODEL), row),
            pl.BlockSpec((1, D_MODEL), const),
            pl.BlockSpec((D_MODEL, WN_COLS), const),
            pl.BlockSpec((WT_ROWS, D_MODEL), const),
            pl.BlockSpec((LANES, H_GLA * DK_GLA), const),
            pl.BlockSpec((1, H_GLA * DK_GLA), const),
            pl.BlockSpec((tm, LANES), tab_row),
            pl.BlockSpec((tm, LANES), tab_row),
            pl.BlockSpec((tm, LANES), tab_row),
            pl.BlockSpec((ROPE_HALF, tm), tab_col),
            pl.BlockSpec((ROPE_HALF, tm), tab_col),
        ],
        out_specs=[pl.BlockSpec((tm, n), row) for n, _ in outs] + [pl.BlockSpec((n, tm), col) for n, _ in outs_t]
        + [pl.BlockSpec((1, n, tm), seq_map) for n in outs_seq],
        out_shape=[jax.ShapeDtypeStruct((m, n), d) for n, d in outs]
        + [jax.ShapeDtypeStruct((n, m), d) for n, d in outs_t]
        + [jax.ShapeDtypeStruct((nseq, n, seq_len), F32) for n in outs_seq],
        compiler_params=_params("parallel"),
        name="mixer_proj",
    )(h, gain, wn, wt, wa2, ba, rc, rs1, rs2, cos_t, sin_t)


def _rope_tables(pos):
    pos = np.asarray(pos, np.float64)
    inv_freq = ROPE_THETA ** (-np.arange(ROPE_HALF, dtype=np.float64) * 2.0 / ROPE_DIM)
    ang = pos[:, None] * inv_freq[None, :]
    cos, sin = np.cos(ang), np.sin(ang)
    n = pos.shape[0]
    one = np.ones((n, D_NSA - ROPE_DIM))
    zero = np.zeros((n, D_NSA - ROPE_DIM))
    zh = np.zeros((n, ROPE_HALF))
    c = np.concatenate([cos, cos, one], axis=1)
    s1 = np.concatenate([-sin, zh, zero], axis=1)
    s2 = np.concatenate([zh, sin, zero], axis=1)
    dup = lambda t: np.concatenate([t, t], axis=1)
    return tuple(jnp.asarray(a, F32) for a in (dup(c), dup(s1), dup(s2), cos.T, sin.T))


def _gla_constants():
    c = GLA_CHUNK
    t = np.arange(c)
    low = (t[None, :] <= t[:, None]).astype(np.float32)
    mats, masks = [low], []
    for lev in range(1, N_GLA_LEVELS + 1):
        seg = (2 * c) >> lev
        half = seg // 2
        mid = (t // seg) * seg + half
        mats.append(low[mid])
        same = (t[:, None] // seg) == (t[None, :] // seg)
        masks.append((same & ((t[:, None] % seg) >= half) & ((t[None, :] % seg) < half)).astype(np.float32))
    masks.append(np.eye(c, dtype=np.float32))
    return np.concatenate(mats, axis=0), np.stack(masks)


def _gla_kernel(qk_ref, la_ref, v_ref, r_ref, gain_ref, big_ref, mask_ref, o_ref, sfin_ref, st_ref):
    c = pl.program_id(1)
    ch = GLA_CHUNK

    @pl.when(c == 0)
    def _():
        st_ref[...] = jnp.zeros(st_ref.shape, F32)

    lane = lax.broadcasted_iota(jnp.int32, (1, LANES), 1)
    head_mask = [jnp.where(lane < DK_GLA, 1.0, 0.0), jnp.where(lane >= DK_GLA, 1.0, 0.0)]
    big = big_ref[...]
    gain = gain_ref[...]
    nq = H_GLA * DK_GLA
    n_chunks = qk_ref.shape[0] // ch
    units = [(ci, p) for ci in range(n_chunks) for p in range(H_GLA // 2)]

    allbs = {}
    for ci, p in units:
        rows = slice(ci * ch, (ci + 1) * ch)
        allbs[ci, p] = _dot01_left(big, la_ref[rows, p * LANES:(p + 1) * LANES], 3)
    intra = {}
    for ci, p in units:
        rows = slice(ci * ch, (ci + 1) * ch)
        q = qk_ref[rows, p * LANES:(p + 1) * LANES]
        k = qk_ref[rows, nq + p * LANES:nq + (p + 1) * LANES]
        allb = allbs[ci, p]
        b = allb[0:ch]
        b_last = b[ch - 1:ch]
        attn = [jnp.zeros((ch, ch), F32), jnp.zeros((ch, ch), F32)]
        for lev in range(N_GLA_LEVELS + 1):
            if lev < N_GLA_LEVELS:
                ref = allb[(lev + 1) * ch:(lev + 2) * ch]
                ql = q * jnp.exp(jnp.minimum(b - ref, 0.0))
                kl = k * jnp.exp(jnp.minimum(ref - b, 0.0))
            else:
                ql, kl = q, k
            qq = jnp.concatenate([ql * head_mask[0], ql * head_mask[1]], axis=0).astype(BF16)
            s = _dot_nt(qq, kl.astype(BF16))
            mk = mask_ref[lev]
            attn[0] = attn[0] + mk * s[0:ch]
            attn[1] = attn[1] + mk * s[ch:2 * ch]
        q0 = q * jnp.exp(b)
        k_hat = (k * jnp.exp(b_last - b)).astype(BF16)
        for hh in range(2):
            hs = slice((2 * p + hh) * DV_GLA, (2 * p + hh + 1) * DV_GLA)
            vh = v_ref[rows, hs].astype(BF16)
            intra[ci, 2 * p + hh] = (_dot(attn[hh].astype(BF16), vh), (q0 * head_mask[hh]).astype(BF16),
                                     _dot_tn(vh, k_hat), jnp.exp(b_last))
    for ci in range(n_chunks):
        rows = slice(ci * ch, (ci + 1) * ch)
        for h in range(H_GLA):
            hs = slice(h * DV_GLA, (h + 1) * DV_GLA)
            o_intra, q0h, kv, decay = intra[ci, h]
            st = st_ref[h]
            o = o_intra + _dot_nt(q0h, st.astype(BF16))
            st_ref[h] = st * decay + kv
            o_ref[rows, hs] = _rms(o, gain) * jax.nn.silu(r_ref[rows, hs])

    @pl.when(c == pl.num_programs(1) - 1)
    def _():
        sfin_ref[0] = st_ref[...]


def _gla_prompt(qk, la, v, r, gain, nb, t):
    big, masks = _gla_constants()
    rows = GLA_CHUNK * GLA_CHUNKS_PER_STEP
    nc = t // rows
    row = lambda b, c: (b * nc + c, 0)
    const2 = lambda b, c: (0, 0)
    m = nb * t
    return pl.pallas_call(
        _gla_kernel,
        grid=(nb, nc),
        in_specs=[
            pl.BlockSpec((rows, 2 * H_GLA * DK_GLA), row),
            pl.BlockSpec((rows, H_GLA * DK_GLA), row),
            pl.BlockSpec((rows, H_GLA * DV_GLA), row),
            pl.BlockSpec((rows, H_GLA * DV_GLA), row),
            pl.BlockSpec((1, DV_GLA), const2),
            pl.BlockSpec(big.shape, const2),
            pl.BlockSpec(masks.shape, lambda b, c: (0, 0, 0)),
        ],
        out_specs=[
            pl.BlockSpec((rows, H_GLA * DV_GLA), row),
            pl.BlockSpec((1, H_GLA, DV_GLA, LANES), lambda b, c: (b, 0, 0, 0)),
        ],
        out_shape=[
            jax.ShapeDtypeStruct((m, H_GLA * DV_GLA), F32),
            jax.ShapeDtypeStruct((nb, H_GLA, DV_GLA, LANES), F32),
        ],
        scratch_shapes=[pltpu.VMEM((H_GLA, DV_GLA, LANES), F32)],
        compiler_params=_params("parallel", "arbitrary"),
        name="gla_scan",
    )(qk, la, v, r, gain, jnp.asarray(big, BF16), jnp.asarray(masks, F32))


def _gla_step_kernel(q_ref, k_ref, a_ref, v_ref, r_ref, s_ref, gain_ref, o_ref, sn_ref):
    for h in range(H_GLA):
        s_new = jnp.exp(a_ref[0, h]) * s_ref[0, h] + k_ref[0, h] * v_ref[0, h]
        sn_ref[0, h] = s_new
        o = jnp.sum(q_ref[0, h] * s_new, axis=0, keepdims=True)
        o_ref[0, h] = _rms(o, gain_ref[...]) * jax.nn.silu(r_ref[0, h])


def _gla_step(qcol, kcol, acol, vrow, rrow, state, gain):
    nb = state.shape[0]
    col = pl.BlockSpec((1, H_GLA, DK_GLA, 1), lambda b: (b, 0, 0, 0))
    rowspec = pl.BlockSpec((1, H_GLA, 1, DV_GLA), lambda b: (b, 0, 0, 0))
    stspec = pl.BlockSpec((1, H_GLA, DK_GLA, DV_GLA), lambda b: (b, 0, 0, 0))
    return pl.pallas_call(
        _gla_step_kernel,
        grid=(nb,),
        in_specs=[col, col, col, rowspec, rowspec, stspec, pl.BlockSpec((1, DV_GLA), lambda b: (0, 0))],
        out_specs=[rowspec, stspec],
        out_shape=[jax.ShapeDtypeStruct((nb, H_GLA, 1, DV_GLA), F32),
                   jax.ShapeDtypeStruct((nb, H_GLA, DK_GLA, DV_GLA), F32)],
        compiler_params=_params("parallel"),
        name="gla_step",
    )(qcol, kcol, acol, vrow, rrow, state, gain)


HALF_ROWS = CMP_STRIDE
CHUNKS_PER_PAGE = ROWS_PER_PAGE // HALF_ROWS


def _cmp_partial_kernel(tbl_ref, *refs, pps):
    del tbl_ref
    pages = refs[:pps]
    wk_ref, wv_ref, abk_ref, abv_ref, tk_ref, tv_ref = refs[pps:]
    for i in range(pps):
        tk_ref[i] = pages[i][0, 0:LANES, :].T
        tv_ref[i] = pages[i][0, LANES:2 * LANES, :].T
    lpair = 2 * LANES
    for t_ref, w_ref, ab_ref in ((tk_ref, wk_ref, abk_ref), (tv_ref, wv_ref, abv_ref)):
        acc = None
        for lp in range(HALF_ROWS // 2):
            cols = [jnp.concatenate([t_ref[i, pl.ds(l, CHUNKS_PER_PAGE, stride=HALF_ROWS), :] for i in range(pps)], axis=0)
                    for l in (2 * lp, 2 * lp + 1)]
            x = jnp.concatenate(cols, axis=1).astype(BF16)
            part = _dot(x, w_ref[lp * lpair:(lp + 1) * lpair, :])
            acc = part if acc is None else acc + part
        ab_ref[0] = acc


def _cmp_partial(pages_t, table, wk, wv, paged):
    nb, npages = table.shape
    pps = min(PAGES_PER_STEP, npages)
    nsteps = npages // pps
    nch = npages * CHUNKS_PER_PAGE
    step_chunks = pps * CHUNKS_PER_PAGE

    def page_map(b, s, tbl, i):
        return (tbl[b, s * pps + i], 0, 0) if paged else (b, 0, s * pps + i)

    page_specs = [pl.BlockSpec((1, 2 * LANES, ROWS_PER_PAGE), functools.partial(page_map, i=i)) for i in range(pps)]
    wspec = pl.BlockSpec(wk.shape, lambda b, s, tbl: (0, 0))
    ospec = pl.BlockSpec((1, step_chunks, 4 * CMP_HIDDEN), lambda b, s, tbl: (b, s, 0))
    grid_spec = pltpu.PrefetchScalarGridSpec(
        num_scalar_prefetch=1,
        grid=(nb, nsteps),
        in_specs=page_specs + [wspec, wspec],
        out_specs=[ospec, ospec],
        scratch_shapes=[pltpu.VMEM((pps, ROWS_PER_PAGE, LANES), F32), pltpu.VMEM((pps, ROWS_PER_PAGE, LANES), F32)],
    )
    osh = jax.ShapeDtypeStruct((nb, nch, 4 * CMP_HIDDEN), F32)
    return pl.pallas_call(
        functools.partial(_cmp_partial_kernel, pps=pps),
        grid_spec=grid_spec,
        out_shape=[osh, osh],
        compiler_params=_params("parallel", "parallel"),
        name="cmp_partial",
    )(table, *([pages_t] * pps), wk, wv)


def _cmp_finish_kernel(abk_ref, abv_ref, posk_ref, w1k_ref, w2k_ref, posv_ref, w1v_ref, w2v_ref, w2vt_ref,
                       kc_ref, vc_ref, vct_ref):
    nch = abk_ref.shape[1]
    pk = _dot(posk_ref[...], w1k_ref[...])[0:1]
    pv = _dot(posv_ref[...], w1v_ref[...])[0:1]
    abk = abk_ref[0]
    abv = abv_ref[0]
    kc = None
    vc = None
    for g in range(G_NSA):
        o = g * 2 * CMP_HIDDEN
        hk = jax.nn.gelu(abk[:, o:o + CMP_HIDDEN] + pltpu.roll(abk[:, o + CMP_HIDDEN:o + 2 * CMP_HIDDEN], nch - 1, 0) + pk)
        hv = jax.nn.gelu(abv[:, o:o + CMP_HIDDEN] + pltpu.roll(abv[:, o + CMP_HIDDEN:o + 2 * CMP_HIDDEN], nch - 1, 0) + pv)
        hk = hk.astype(BF16)
        hv = hv.astype(BF16)
        tk = _dot(hk, w2k_ref[g])
        tv = _dot(hv, w2v_ref[g])
        kc = tk if kc is None else kc + tk
        vc = tv if vc is None else vc + tv
        vct_ref[0, g * D_NSA:(g + 1) * D_NSA, :] = _dot_nt(w2vt_ref[...], hv).astype(BF16)
    kc_ref[0] = kc.astype(BF16)
    vc_ref[0] = vc.astype(BF16)


def _cmp_finish(abk, abv, posk, w1k, w2k_pad, posv, w1v, w2v_pad, w2vt):
    nb, nch, _ = abk.shape
    ab = pl.BlockSpec((1, nch, 4 * CMP_HIDDEN), lambda b: (b, 0, 0))
    c2 = lambda b: (0, 0)
    c3 = lambda b: (0, 0, 0)
    return pl.pallas_call(
        _cmp_finish_kernel,
        grid=(nb,),
        in_specs=[ab, ab,
                  pl.BlockSpec(posk.shape, c2), pl.BlockSpec(w1k.shape, c2), pl.BlockSpec(w2k_pad.shape, c3),
                  pl.BlockSpec(posv.shape, c2), pl.BlockSpec(w1v.shape, c2), pl.BlockSpec(w2v_pad.shape, c3),
                  pl.BlockSpec(w2vt.shape, c2)],
        out_specs=[pl.BlockSpec((1, nch, LANES), lambda b: (b, 0, 0)),
                   pl.BlockSpec((1, nch, LANES), lambda b: (b, 0, 0)),
                   pl.BlockSpec((1, LANES, nch), lambda b: (b, 0, 0))],
        out_shape=[jax.ShapeDtypeStruct((nb, nch, LANES), BF16),
                   jax.ShapeDtypeStruct((nb, nch, LANES), BF16),
                   jax.ShapeDtypeStruct((nb, LANES, nch), BF16)],
        compiler_params=_params("parallel"),
        name="cmp_finish",
    )(abk, abv, posk, w1k, w2k_pad, posv, w1v, w2v_pad, w2vt)


def _cmp_weights(pos, w1, w2):
    w = w1.reshape(2, HALF_ROWS, D_NSA, CMP_HIDDEN)
    z = jnp.zeros_like(w[0])
    blocks = []
    for g in range(G_NSA):
        cols = []
        for g2 in range(G_NSA):
            for half in range(2):
                cols.append(w[half] if g2 == g else z)
        blocks.append(jnp.concatenate(cols, axis=-1))
    wbig = jnp.stack(blocks, axis=1).reshape(HALF_ROWS * LANES, 4 * CMP_HIDDEN).astype(BF16)
    pos8 = jnp.broadcast_to(pos.reshape(1, CMP_LEN * D_NSA), (SUBLANES, CMP_LEN * D_NSA)).astype(BF16)
    z2 = jnp.zeros_like(w2)
    w2pad = jnp.stack([jnp.concatenate([w2, z2], axis=1), jnp.concatenate([z2, w2], axis=1)]).astype(BF16)
    return wbig, pos8, w1.astype(BF16), w2pad, w2.T.astype(BF16)


def _overlap_t(nc_pad, ns_pad, nc, ns):
    i = np.arange(nc_pad)[None, :] * CMP_STRIDE
    j = np.arange(ns_pad)[:, None] * SEL_BLOCK
    m = (i < j + SEL_BLOCK) & (i + CMP_LEN > j) & (np.arange(nc_pad)[None, :] < nc) & (np.arange(ns_pad)[:, None] < ns)
    return jnp.asarray(m.astype(np.float32), BF16)


def _rank_desc(x):
    nrow = x.shape[0]
    nblk = nrow // SUBLANES
    blocks = [x[v * SUBLANES:(v + 1) * SUBLANES] for v in range(nblk)]
    cnt = [jnp.zeros((SUBLANES, x.shape[1]), F32) for _ in range(nblk)]
    sub = lax.broadcasted_iota(jnp.int32, (SUBLANES, x.shape[1]), 0)
    for jp in range(nrow):
        row = x[jp:jp + 1]
        vb = jp // SUBLANES
        for v in range(nblk):
            ge = jnp.where(row >= blocks[v], 1.0, 0.0)
            gt = jnp.where(row > blocks[v], 1.0, 0.0)
            if v > vb:
                cnt[v] = cnt[v] + ge
            elif v < vb:
                cnt[v] = cnt[v] + gt
            else:
                cnt[v] = cnt[v] + jnp.where(sub > (jp % SUBLANES), ge, gt)
    return jnp.concatenate(cnt, axis=0)


def _nsa_prompt_kernel(qraw_ref, qrot_ref, gate_ref, kc_ref, vct_ref, ksel_ref, vselt_ref, kwin_ref, vwint_ref,
                       ovl_ref, o_ref, kaug_ref, kwb_ref, vsa_ref, vwa_ref):
    qb = pl.program_id(1)
    nlane = R_NSA * Q_TILE
    tpos = qb * Q_TILE + lax.broadcasted_iota(jnp.int32, (1, nlane), 1) % Q_TILE
    tpos1 = tpos[:, 0:Q_TILE]
    ncp = kc_ref.shape[1]
    nsb = ovl_ref.shape[0]
    seq = ksel_ref.shape[0]
    zeros_q = jnp.zeros((D_NSA, nlane), BF16)

    @pl.when(qb == 0)
    def _():
        ks = ksel_ref[...]
        lane = lax.broadcasted_iota(jnp.int32, (1, LANES), 1)
        blk = lax.broadcasted_iota(jnp.int32, (seq, 1), 0) // SEL_BLOCK
        kaug_ref[0] = jnp.where(lane < D_NSA, ks, jnp.where(lane - D_NSA == blk, 1.0, 0.0)).astype(BF16)
        kaug_ref[1] = jnp.where(lane >= D_NSA, ks, jnp.where(lane == blk, 1.0, 0.0)).astype(BF16)
        kwb_ref[...] = kwin_ref[...].astype(BF16)
        ones_row = jnp.where(lax.broadcasted_iota(jnp.int32, (V_PAD_ROWS, seq), 0) == 0, 1.0, 0.0)
        for g in range(G_NSA):
            gs = slice(g * D_NSA, (g + 1) * D_NSA)
            vsa_ref[g] = jnp.concatenate([vselt_ref[0, gs, :], ones_row], axis=0).astype(BF16)
            vwa_ref[g] = jnp.concatenate([vwint_ref[0, gs, :], ones_row], axis=0).astype(BF16)

    def group_q(ref, g):
        q = jnp.concatenate([ref[h * D_NSA:(h + 1) * D_NSA, :] for h in range(g * R_NSA, (g + 1) * R_NSA)], axis=1)
        return q, (jnp.concatenate([q, zeros_q], axis=0) if g == 0 else jnp.concatenate([zeros_q, q], axis=0))

    cmp_scores = [_dot(kc_ref[0], group_q(qraw_ref, g)[1]) for g in range(G_NSA)]
    o_cmps, qaugs, qpads = [], [], []
    for g in range(G_NSA):
        qrot, qpad = group_q(qrot_ref, g)
        gs = slice(g * D_NSA, (g + 1) * D_NSA)

        sc = cmp_scores[g]
        ci = lax.broadcasted_iota(jnp.int32, (ncp, 1), 0)
        p_cmp = _masked_softmax2_rows(sc, ci * CMP_STRIDE + (CMP_LEN - 1) <= tpos)
        o_cmp = _dot(vct_ref[0, gs, :], p_cmp.astype(BF16))
        psum = p_cmp[:, 0:Q_TILE]
        for r in range(1, R_NSA):
            psum = psum + p_cmp[:, r * Q_TILE:(r + 1) * Q_TILE]
        imp = _dot01_left(ovl_ref[...], psum, 2)
        bj = lax.broadcasted_iota(jnp.int32, (nsb, 1), 0)
        cur = tpos1 // SEL_BLOCK
        forced = (bj == 0) | (bj == cur) | (bj == cur - 1)
        imp = jnp.where(forced, FORCE_SCORE, jnp.where(bj * SEL_BLOCK <= tpos1, imp, -1.0))
        bias = jnp.where(_rank_desc(imp) < N_SELECT, 0.0, NEG_INF).astype(BF16)
        bias = jnp.concatenate([bias] * R_NSA, axis=1)
        if nsb < D_NSA:
            bias = jnp.concatenate([bias, jnp.zeros((D_NSA - nsb, nlane), BF16)], axis=0)
        o_cmps.append(o_cmp)
        qaugs.append(jnp.concatenate([qrot, bias], axis=0) if g == 0 else jnp.concatenate([bias, qrot], axis=0))
        qpads.append(qpad)

    krow = lax.broadcasted_iota(jnp.int32, (K_TILE, 1), 0)
    ncol = nlane // SEL_COLS
    pairs = [(g, slice(j * SEL_COLS, (j + 1) * SEL_COLS)) for g in range(G_NSA) for j in range(ncol)]

    def sel_step(c, carry, causal):
        start = pl.multiple_of(c * K_TILE, K_TILE)
        out = []
        scores = [_dot(kaug_ref[g, pl.ds(start, K_TILE), :], qaugs[g][:, cols]) for g, cols in pairs]
        for (g, cols), (m, acc), s in zip(pairs, carry, scores):
            if causal:
                s = jnp.where(start + krow <= tpos[:, cols], s, NEG_INF)
            m_new = jnp.maximum(m, jnp.max(s, axis=0, keepdims=True))
            e = jnp.exp2(s - m_new).astype(BF16)
            acc = jnp.exp2(m - m_new) * acc + _dot(vsa_ref[g, :, pl.ds(start, K_TILE)], e)
            out.append((m_new, acc))
        return tuple(out)

    n_full = (qb * Q_TILE) // K_TILE
    init = tuple((jnp.full((1, SEL_COLS), M_INIT, F32), jnp.zeros((D_NSA + V_PAD_ROWS, SEL_COLS), F32))
                 for _ in pairs)
    carry = lax.fori_loop(0, n_full, functools.partial(sel_step, causal=False), init)

    band = WINDOW + Q_TILE
    wstart = pl.multiple_of(jnp.maximum(qb * Q_TILE - WINDOW, 0), Q_TILE)
    win_scores = [_dot(kwb_ref[pl.ds(wstart, band), :], qpads[g]) for g in range(G_NSA)]

    carry = sel_step(n_full, carry, True)
    o_sels = []
    for g in range(G_NSA):
        acc_sel = jnp.concatenate([carry[g * ncol + j][1] for j in range(ncol)], axis=1)
        o_sels.append(acc_sel[0:D_NSA] / acc_sel[D_NSA:D_NSA + 1])

    diff = tpos - (wstart + lax.broadcasted_iota(jnp.int32, (band, 1), 0))
    old_edge = diff[0:Q_TILE] < WINDOW
    causal_ok = diff >= 0
    heads = []
    for g in range(G_NSA):
        sw = win_scores[g]
        sw = jnp.concatenate([jnp.where(old_edge, sw[0:Q_TILE], NEG_INF), sw[Q_TILE:]], axis=0)
        sw = jnp.where(causal_ok, sw, NEG_INF)
        e_win = jnp.exp2(sw - jnp.max(sw, axis=0, keepdims=True)).astype(BF16)
        acc_win = _dot(vwa_ref[g, :, pl.ds(wstart, band)], e_win)
        o_win = acc_win[0:D_NSA] / acc_win[D_NSA:D_NSA + 1]
        for r in range(R_NSA):
            h = g * R_NSA + r
            ls = slice(r * Q_TILE, (r + 1) * Q_TILE)
            heads.append(gate_ref[3 * h:3 * h + 1, :] * o_cmps[g][:, ls]
                         + gate_ref[3 * h + 1:3 * h + 2, :] * o_sels[g][:, ls]
                         + gate_ref[3 * h + 2:3 * h + 3, :] * o_win[:, ls])
    o_ref[...] = jnp.concatenate(heads, axis=0).T


def _nsa_prompt(qraw_t, qrot_t, gate_t, kc, vct, ksel, nsa_t, kwin, win_t, nb, t):
    nqb = t // Q_TILE
    ncp = kc.shape[1]
    ovl = _overlap_t(ncp, t // SEL_BLOCK, (t - CMP_LEN) // CMP_STRIDE + 1, t // SEL_BLOCK)
    qcol = lambda b, q: (0, b * nqb + q)
    m = nb * t
    return pl.pallas_call(
        _nsa_prompt_kernel,
        grid=(nb, nqb),
        in_specs=[
            pl.BlockSpec((N_Q_COLS, Q_TILE), qcol),
            pl.BlockSpec((N_Q_COLS, Q_TILE), qcol),
            pl.BlockSpec((N_GATE_ROWS, Q_TILE), qcol),
            pl.BlockSpec((1, ncp, LANES), lambda b, q: (b, 0, 0)),
            pl.BlockSpec((1, LANES, ncp), lambda b, q: (b, 0, 0)),
            pl.BlockSpec((t, LANES), lambda b, q: (b, 0)),
            pl.BlockSpec((1, LANES, t), lambda b, q: (b, 3, 0)),
            pl.BlockSpec((t, LANES), lambda b, q: (b, 0)),
            pl.BlockSpec((1, LANES, t), lambda b, q: (b, 1, 0)),
            pl.BlockSpec(ovl.shape, lambda b, q: (0, 0)),
        ],
        out_specs=pl.BlockSpec((Q_TILE, N_Q_COLS), lambda b, q: (b * nqb + q, 0)),
        out_shape=jax.ShapeDtypeStruct((m, N_Q_COLS), F32),
        scratch_shapes=[pltpu.VMEM((G_NSA, t, LANES), BF16), pltpu.VMEM((t, LANES), BF16),
                        pltpu.VMEM((G_NSA, D_NSA + V_PAD_ROWS, t), BF16),
                        pltpu.VMEM((G_NSA, D_NSA + V_PAD_ROWS, t), BF16)],
        compiler_params=_params("parallel", "arbitrary"),
        name="nsa_prompt",
    )(qraw_t, qrot_t, gate_t, kc, vct, ksel, nsa_t, kwin, win_t, ovl)


def _nsa_decode_select_kernel(qt_ref, kc_ref, vct_ref, ovl_ref, grp_ref, idx_ref, ocmp_ref, *, pos, n_blocks):
    ncp = kc_ref.shape[1]
    sc = _dot(kc_ref[0], qt_ref[0])
    ci = lax.broadcasted_iota(jnp.int32, (ncp, 1), 0)
    p = _masked_softmax2_rows(sc, ci * CMP_STRIDE + (CMP_LEN - 1) <= pos)
    ocmp_ref[0] = _dot(vct_ref[0], p.astype(BF16))
    psum = _dot01_right(p, grp_ref[...], 2)
    imp_t = _dot01_left(ovl_ref[...], psum, 2)
    imp_r = imp_t.T
    cur = pos // SEL_BLOCK

    def finish(v, j):
        forced = (j == 0) | (j == cur) | (j == cur - 1)
        v = jnp.where(forced, FORCE_SCORE, jnp.where(j * SEL_BLOCK <= pos, v, -1.0))
        return jnp.where(j < n_blocks, v, -3e38)

    jc = lax.broadcasted_iota(jnp.int32, (PAD_BLOCKS, 1), 0)
    jr = lax.broadcasted_iota(jnp.int32, (1, PAD_BLOCKS), 1)
    kk = lax.broadcasted_iota(jnp.int32, (N_SELECT, 1), 0).astype(F32)
    lane = lax.broadcasted_iota(jnp.int32, (N_SELECT, LANES), 1)
    out = jnp.zeros((N_SELECT, LANES), F32)
    for g in range(G_NSA):
        c0 = g * R_NSA
        col = finish(imp_t[:, c0:c0 + 1], jc)
        row = finish(imp_r[c0:c0 + 1, :], jr)
        beats = (col > row) | ((col == row) & (jc < jr))
        rank = jnp.sum(jnp.where(beats, 1.0, 0.0), axis=0, keepdims=True)
        hit = rank == kk
        idx = jnp.sum(jnp.where(hit, jr.astype(F32), 0.0), axis=1, keepdims=True)
        out = jnp.where(lane == g, idx, out)
    idx_ref[0] = out.astype(jnp.int32)


def _nsa_decode_select(q_t, kc, vct, pos, n_blocks):
    nb, ncp, _ = kc.shape
    nc = (pos + 1 - CMP_LEN) // CMP_STRIDE + 1
    ovl = _overlap_t(ncp, PAD_BLOCKS, nc, n_blocks)
    hh = np.arange(LANES)
    grp = ((hh[:, None] // R_NSA) == (hh[None, :] // R_NSA)) & (hh[:, None] < H_NSA) & (hh[None, :] < H_NSA)
    grp = jnp.asarray(grp.astype(np.float32), BF16)
    return pl.pallas_call(
        functools.partial(_nsa_decode_select_kernel, pos=pos, n_blocks=n_blocks),
        grid=(nb,),
        in_specs=[
            pl.BlockSpec((1, LANES, LANES), lambda b: (b, 0, 0)),
            pl.BlockSpec((1, ncp, LANES), lambda b: (b, 0, 0)),
            pl.BlockSpec((1, LANES, ncp), lambda b: (b, 0, 0)),
            pl.BlockSpec(ovl.shape, lambda b: (0, 0)),
            pl.BlockSpec(grp.shape, lambda b: (0, 0)),
        ],
        out_specs=[pl.BlockSpec((1, N_SELECT, LANES), lambda b: (b, 0, 0)),
                   pl.BlockSpec((1, LANES, LANES), lambda b: (b, 0, 0))],
        out_shape=[jax.ShapeDtypeStruct((nb, N_SELECT, LANES), jnp.int32),
                   jax.ShapeDtypeStruct((nb, LANES, LANES), F32)],
        compiler_params=_params("parallel"),
        name="nsa_decode_select",
    )(q_t, kc, vct, ovl, grp)


def _nsa_decode_attend_kernel(idx_ref, tbl_ref, *refs, n_past_blocks, win_buf):
    del tbl_ref
    nblk = G_NSA * N_SELECT
    blocks = refs[:nblk]
    q_ref, win_ref, nsa_new_ref, win_new_ref, gate_ref, ocmp_ref, o_ref = refs[nblk:]
    b = pl.program_id(0)
    q = q_ref[0]
    qf = q.astype(F32)
    rowg = lax.broadcasted_iota(jnp.int32, (H_NSA, 1), 0) // R_NSA
    ks_new = nsa_new_ref[0, :, 2 * LANES:3 * LANES]
    vs_new = nsa_new_ref[0, :, 3 * LANES:4 * LANES]
    s_new = jnp.sum(qf * ks_new, axis=1, keepdims=True)
    col = lax.broadcasted_iota(jnp.int32, (1, N_SELECT * ROWS_PER_PAGE), 1)
    colpage = col // ROWS_PER_PAGE
    colhalf = (col % ROWS_PER_PAGE) // SEL_BLOCK
    halves = ROWS_PER_PAGE // SEL_BLOCK
    o_sel = None
    for g in range(G_NSA):
        ks_t = jnp.concatenate([blocks[g * N_SELECT + k][0, 0:LANES, :] for k in range(N_SELECT)], axis=1)
        vs_t = jnp.concatenate([blocks[g * N_SELECT + k][0, LANES:2 * LANES, :] for k in range(N_SELECT)], axis=1)
        s = _dot(q, ks_t.astype(BF16))
        want = jnp.full(col.shape, -1, jnp.int32)
        for k in range(N_SELECT):
            j = idx_ref[b, g, k]
            half = jnp.where(j < n_past_blocks, j % halves, -1)
            want = jnp.where(colpage == k, half, want)
        valid = colhalf == want
        sm = jnp.where(valid, s, NEG_INF)
        m = jnp.maximum(jnp.max(sm, axis=1, keepdims=True), s_new)
        e = jnp.where(valid, jnp.exp2(sm - m), 0.0)
        e_new = jnp.exp2(s_new - m)
        l = jnp.sum(e, axis=1, keepdims=True) + e_new
        og = (_dot_nt(e.astype(BF16), vs_t.astype(BF16)) + e_new * vs_new) / l
        o_sel = og if o_sel is None else jnp.where(rowg == g, og, o_sel)

    kw_t = win_ref[0, 0:LANES, :]
    vw_t = win_ref[0, LANES:2 * LANES, :]
    kw_new = win_new_ref[0, :, 0:LANES]
    vw_new = win_new_ref[0, :, LANES:2 * LANES]
    sw = _dot(q, kw_t.astype(BF16))
    sw_new = jnp.sum(qf * kw_new, axis=1, keepdims=True)
    diff = win_buf - lax.broadcasted_iota(jnp.int32, (1, win_buf), 1)
    validw = (diff >= 0) & (diff < WINDOW)
    smw = jnp.where(validw, sw, NEG_INF)
    mw = jnp.maximum(jnp.max(smw, axis=1, keepdims=True), sw_new)
    ew = jnp.where(validw, jnp.exp2(smw - mw), 0.0)
    ew_new = jnp.exp2(sw_new - mw)
    lw = jnp.sum(ew, axis=1, keepdims=True) + ew_new
    o_win = (_dot_nt(ew.astype(BF16), vw_t.astype(BF16)) + ew_new * vw_new) / lw

    gt = gate_ref[0]
    o_ref[0] = gt[:, 0:1] * ocmp_ref[0] + gt[:, 1:2] * o_sel + gt[:, 2:3] * o_win


def _nsa_decode_attend(idx, table, cache_t, q2, win_t, nsa_new, win_new, gates, ocmp, n_past_blocks):
    nb = q2.shape[0]
    win_buf = win_t.shape[2]
    halves = ROWS_PER_PAGE // SEL_BLOCK

    def blk_map(b, idx_ref, tbl_ref, g, k):
        j = jnp.minimum(idx_ref[b, g, k], n_past_blocks - 1)
        return (tbl_ref[b, j // halves], 1, 0)

    blk_specs = [pl.BlockSpec((1, 2 * LANES, ROWS_PER_PAGE), functools.partial(blk_map, g=g, k=k))
                 for g in range(G_NSA) for k in range(N_SELECT)]
    per_b = lambda shape: pl.BlockSpec((1,) + shape, lambda b, i, t: (b, 0, 0))
    grid_spec = pltpu.PrefetchScalarGridSpec(
        num_scalar_prefetch=2,
        grid=(nb,),
        in_specs=blk_specs + [per_b((H_NSA, LANES)), per_b((2 * LANES, win_buf)), per_b((1, 4 * LANES)),
                              per_b((1, 2 * LANES)), per_b((H_NSA, LANES)), per_b((H_NSA, LANES))],
        out_specs=per_b((H_NSA, LANES)),
    )
    return pl.pallas_call(
        functools.partial(_nsa_decode_attend_kernel, n_past_blocks=n_past_blocks, win_buf=win_buf),
        grid_spec=grid_spec,
        out_shape=jax.ShapeDtypeStruct((nb, H_NSA, LANES), F32),
        compiler_params=_params("parallel"),
        name="nsa_decode_attend",
    )(idx, table, *([cache_t] * (G_NSA * N_SELECT)), q2, win_t, nsa_new, win_new, gates, ocmp)


def _post_mixer_kernel(h_ref, a_ref, b_ref, p_ref, wa_ref, wb_ref, gm_ref,
                       gpre_ref, wg_ref, wu_ref, wo_ref, gpost_ref,
                       ppre_ref, pg_ref, pp_ref, ppost_ref, o_ref):
    y = _dot(a_ref[...].astype(BF16), wa_ref[...]) + _dot(b_ref[...].astype(BF16), wb_ref[...])
    h = h_ref[...] + _rms(y, gm_ref[...])
    xn = _rms(h, gpre_ref[...]).astype(BF16)
    acc = jnp.zeros(h.shape, F32)
    for c in range(D_FF // FF_CHUNK):
        sl = slice(c * FF_CHUNK, (c + 1) * FF_CHUNK)
        g = _dot(xn, wg_ref[:, sl])
        u = _dot(xn, wu_ref[:, sl])
        acc = acc + _dot((jax.nn.silu(g) * u).astype(BF16), wo_ref[sl, :])
    h = h + 0.5 * _rms(acc, gpost_ref[...])
    gate = jax.nn.sigmoid(_dot(_rms(h, ppre_ref[...]).astype(BF16), pg_ref[...]))
    o_ref[...] = h + _rms(gate * _dot(p_ref[...].astype(BF16), pp_ref[...]), ppost_ref[...])


def _post_mixer(h, a, b, p, w, tm):
    m = h.shape[0]
    row = lambda i: (i, 0)
    once = dict(pipeline_mode=pl.Buffered(1))
    const = lambda shape, idx=(0, 0): pl.BlockSpec(shape, lambda i: idx, **once)
    vec = const((1, D_MODEL))
    return pl.pallas_call(
        _post_mixer_kernel,
        grid=(m // tm,),
        in_specs=[pl.BlockSpec((tm, D_MODEL), row), pl.BlockSpec((tm, a.shape[1]), row),
                  pl.BlockSpec((tm, b.shape[1]), row), pl.BlockSpec((tm, PLE_DIM), row),
                  const(w["wo_gla"].shape), const(w["wo_nsa"].shape), vec,
                  vec, const((D_MODEL, D_FF)), const((D_MODEL, D_FF), (0, 1)), const((D_FF, D_MODEL)), vec,
                  vec, const(w["ple_gate"].shape), const(w["ple_proj"].shape), vec],
        out_specs=pl.BlockSpec((tm, D_MODEL), row),
        out_shape=jax.ShapeDtypeStruct((m, D_MODEL), F32),
        compiler_params=_params("parallel"),
        name="post_mixer",
    )(h, a, b, p, w["wo_gla"], w["wo_nsa"], w["m_post"],
      w["f2_pre"], w["f2_in"], w["f2_in"], w["f2_out"], w["f2_post"],
      w["ple_pre"], w["ple_gate"], w["ple_proj"], w["ple_post"])


def _split_in_cols(w):
    outs, off = [], 0
    for n in IN_SPLITS:
        outs.append(w[:, off:off + n])
        off += n
    return outs


def _prep_mixer_weights(w_in, w_a2, b_a):
    q_g, k_g, v_g, r_g, a_lr, q_n, kv_n, gate_n = _split_in_cols(w_in)
    pad_cols = lambda w, n: jnp.pad(w, ((0, 0), (0, n - w.shape[1])))
    gd = G_NSA * D_NSA
    k_sel = kv_n[:, 2 * gd:3 * gd]
    k_win = kv_n[:, 4 * gd:5 * gd]
    wn = jnp.concatenate([q_g, k_g, v_g, r_g, pad_cols(a_lr, LANES), k_sel, k_win], axis=1).astype(BF16)
    wt = jnp.concatenate([q_n, kv_n, pad_cols(gate_n, N_GATE_ROWS)], axis=1).T.astype(BF16)
    wa2 = jnp.pad(w_a2, ((0, LANES - GLA_RANK), (0, 0))).astype(BF16)
    return wn, wt, wa2, b_a.reshape(1, -1)


def _row_tile(m):
    return ROW_TILE if m % ROW_TILE == 0 else m


def _layer(x2, p2, mixer, w):
    tm = _row_tile(x2.shape[0])
    h = _ffn(x2, w["f1_pre"], w["f1_in"], w["f1_out"], w["f1_post"], tm)
    o_gla, o_nsa, extras = mixer(h)
    h = _post_mixer(h, o_gla, o_nsa, p2, w, tm)
    return h, extras


def _mixer_prompt(h, w, nb, t):
    tm = _row_tile(h.shape[0])
    tabs = _rope_tables(np.arange(t))
    (qk, v, r, la, ksel, kwin, qraw_t, qrot_t, gate_t, nsa_t, win_t) = _proj(
        h, w["m_pre"], w["wn"], w["wt"], w["wa2"], w["ba"], tabs, tm, t // tm)
    o_gla, s_fin = _gla_prompt(qk, la, v, r, w["gla_gain"], nb, t)
    table = jnp.zeros((nb, t // ROWS_PER_PAGE), jnp.int32)
    abk, abv = _cmp_partial(nsa_t, table, w["cmp_wk"], w["cmp_wv"], False)
    kc, _, vct = _cmp_finish(abk, abv, w["cmp_posk"], w["cmp_w1k"], w["cmp_w2k"],
                             w["cmp_posv"], w["cmp_w1v"], w["cmp_w2v"], w["cmp_w2vt"])
    o_nsa = _nsa_prompt(qraw_t, qrot_t, gate_t, kc, vct, ksel, nsa_t, kwin, win_t, nb, t)
    s_t = s_fin.reshape(nb, H_GLA, DV_GLA, 2, DK_GLA)
    s_own = jnp.stack([s_t[:, hh, :, hh % 2, :] for hh in range(H_GLA)], axis=1)
    gla_state = jnp.swapaxes(s_own, -1, -2)
    keep = min(WINDOW, t)
    rows_first = lambda a, n: jnp.transpose(a.reshape(nb, n, G_NSA, D_NSA, a.shape[-1]), (0, 4, 1, 2, 3))
    nsa_rows = rows_first(nsa_t, 4)
    win_rows = rows_first(win_t[:, :, t - keep:], 2)
    return o_gla, o_nsa, (nsa_rows, win_rows, gla_state)


def _mixer_sample(h, w, cache_l, win_l, gla_l, page_table):
    nb = h.shape[0]
    n_pages = page_table.shape[1]
    past_len = n_pages * cache_l.shape[1]
    pos = past_len
    tabs = _rope_tables(np.full((nb,), pos))
    (qk, v, r, la, _, _, qraw_t, qrot_t, gate_t, nsa_new_t, win_new_t) = _proj(
        h, w["m_pre"], w["wn"], w["wt"], w["wa2"], w["ba"], tabs, nb, 1)
    nsa = nsa_new_t[0].T
    win = win_new_t[0].T

    nq = H_GLA * DK_GLA
    col = lambda a: a.reshape(nb, H_GLA, DK_GLA, 1)
    rowv = lambda a: a.reshape(nb, H_GLA, 1, DV_GLA)
    o_gla, gla_state = _gla_step(col(qk[:, :nq]), col(qk[:, nq:]), col(la), rowv(v), rowv(r),
                                 gla_l.astype(F32), w["gla_gain"])
    o_gla = o_gla.reshape(nb, H_GLA * DV_GLA)

    cache_t = jnp.transpose(cache_l.reshape(cache_l.shape[0], ROWS_PER_PAGE, 4 * LANES), (0, 2, 1))
    abk, abv = _cmp_partial(cache_t, page_table, w["cmp_wk"], w["cmp_wv"], True)
    kc, _, vct = _cmp_finish(abk, abv, w["cmp_posk"], w["cmp_w1k"], w["cmp_w2k"],
                             w["cmp_posv"], w["cmp_w1v"], w["cmp_w2v"], w["cmp_w2vt"])

    hg = (jnp.arange(H_NSA) // R_NSA)[None, :, None]

    def group_pad(q_t):
        q8 = q_t.T.reshape(nb, H_NSA, D_NSA)
        return jnp.concatenate([jnp.where(hg == 0, q8, 0), jnp.where(hg == 1, q8, 0)], axis=-1)

    q2_raw = group_pad(qraw_t)
    q2_rot = group_pad(qrot_t)
    q2_raw_t = jnp.pad(jnp.swapaxes(q2_raw, 1, 2), ((0, 0), (0, 0), (0, LANES - H_NSA)))
    n_blocks = -(-(past_len + 1) // SEL_BLOCK)
    idx_pad, ocmp_t = _nsa_decode_select(q2_raw_t, kc, vct, pos, n_blocks)
    idx = jnp.stack([idx_pad[:, :, g] for g in range(G_NSA)], axis=1)
    ocmp = jnp.swapaxes(ocmp_t, 1, 2)[:, :H_NSA, :]
    gates = jnp.pad(gate_t[:3 * H_NSA].T.reshape(nb, H_NSA, 3), ((0, 0), (0, 0), (0, LANES - 3)))
    wb = win_l.shape[1]
    win_buf_t = jnp.transpose(win_l.reshape(nb, wb, 2 * LANES), (0, 2, 1))
    o8 = _nsa_decode_attend(idx, page_table, cache_t, q2_rot, win_buf_t,
                            nsa.reshape(nb, 1, 4 * LANES), win.reshape(nb, 1, 2 * LANES), gates, ocmp,
                            past_len // SEL_BLOCK)
    o8 = o8.reshape(nb, H_NSA, G_NSA, D_NSA)
    o_nsa = jnp.concatenate([o8[:, :R_NSA, 0], o8[:, R_NSA:, 1]], axis=1).reshape(nb, H_NSA * D_NSA)

    nsa_rows = nsa.reshape(nb, 1, 4, G_NSA, D_NSA)
    win_new = win.reshape(nb, 1, 2, G_NSA, D_NSA)
    kw = jnp.concatenate([win_l, win_new.astype(win_l.dtype)], axis=1)
    keep = min(WINDOW, wb + 1)
    return o_gla, o_nsa, (nsa_rows, kw[:, wb + 1 - keep:], gla_state.astype(gla_l.dtype))


def kernel(x_prompt, x_sample, cache_nsa, state_win, state_gla, page_table, p_prompt, p_sample,
           ffn1_norm_pre, ffn1_norm_post, ffn1_w_in, ffn1_w_out,
           mix_norm_pre, mix_norm_post, w_mix_in, w_gla_a2, b_gla_a, gla_out_norm,
           cmp_pos_k, w_cmp_k1, w_cmp_k2, cmp_pos_v, w_cmp_v1, w_cmp_v2, w_mix_out,
           ffn2_norm_pre, ffn2_norm_post, ffn2_w_in, ffn2_w_out,
           ple_norm_pre, ple_w_gate, ple_w_proj, ple_norm_post):
    nb, t, _ = x_prompt.shape
    ns = x_sample.shape[0]
    depth = ffn1_w_in.shape[0]
    hp = x_prompt.reshape(nb * t, D_MODEL)
    hs = x_sample.reshape(ns, D_MODEL)
    outs = [[] for _ in range(6)]
    for i in range(depth):
        wn, wt, wa2, ba = _prep_mixer_weights(w_mix_in[i], w_gla_a2[i], b_gla_a[i])
        cmp_wk, cmp_posk, cmp_w1k, cmp_w2k, _ = _cmp_weights(cmp_pos_k[i], w_cmp_k1[i], w_cmp_k2[i])
        cmp_wv, cmp_posv, cmp_w1v, cmp_w2v, cmp_w2vt = _cmp_weights(cmp_pos_v[i], w_cmp_v1[i], w_cmp_v2[i])
        gla_w = H_GLA * DV_GLA
        w = dict(
            f1_pre=ffn1_norm_pre[i][None], f1_post=ffn1_norm_post[i][None],
            f1_in=ffn1_w_in[i].astype(BF16), f1_out=ffn1_w_out[i].astype(BF16),
            m_pre=mix_norm_pre[i][None], m_post=mix_norm_post[i][None],
            wn=wn, wt=wt, wa2=wa2, ba=ba, gla_gain=gla_out_norm[i][None],
            cmp_wk=cmp_wk, cmp_posk=cmp_posk, cmp_w1k=cmp_w1k, cmp_w2k=cmp_w2k,
            cmp_wv=cmp_wv, cmp_posv=cmp_posv, cmp_w1v=cmp_w1v, cmp_w2v=cmp_w2v, cmp_w2vt=cmp_w2vt,
            wo_gla=w_mix_out[i][:gla_w].astype(BF16), wo_nsa=w_mix_out[i][gla_w:].astype(BF16),
            f2_pre=ffn2_norm_pre[i][None], f2_post=ffn2_norm_post[i][None],
            f2_in=ffn2_w_in[i].astype(BF16), f2_out=ffn2_w_out[i].astype(BF16),
            ple_pre=ple_norm_pre[i][None], ple_post=ple_norm_post[i][None],
            ple_gate=ple_w_gate[i].astype(BF16), ple_proj=ple_w_proj[i].astype(BF16),
        )
        hp, (r_p, w_p, s_p) = _layer(hp, p_prompt[i].reshape(nb * t, PLE_DIM),
                                     functools.partial(_mixer_prompt, w=w, nb=nb, t=t), w)
        hs, (r_s, w_s, s_s) = _layer(hs, p_sample[i].reshape(ns, PLE_DIM),
                                     functools.partial(_mixer_sample, w=w, cache_l=cache_nsa[i], win_l=state_win[i],
                                                       gla_l=state_gla[i], page_table=page_table), w)
        for lst, val in zip(outs, (r_p, w_p, s_p, r_s, w_s, s_s)):
            lst.append(val)
    return (hp.reshape(nb, t, D_MODEL), hs.reshape(ns, 1, D_MODEL), *[jnp.stack(o) for o in outs])
```

```python
import functools

import numpy as np
import jax
import jax.numpy as jnp
from jax import lax
from jax.experimental import pallas as pl
from jax.experimental.pallas import tpu as pltpu

F32 = jnp.float32
BF16 = jnp.bfloat16

D_MODEL = 1024
PLE_DIM = 256
D_FF = 2816
EPS = 1e-6
H_GLA = 4
DK_GLA = 64
DV_GLA = 128
GLA_RANK = 16
GLA_GATE_TEMP = 16.0
GLA_CHUNK = 64
H_NSA = 8
G_NSA = 2
R_NSA = H_NSA // G_NSA
D_NSA = 64
CMP_LEN = 32
CMP_STRIDE = 16
CMP_HIDDEN = 128
SEL_BLOCK = 64
N_SELECT = 16
WINDOW = 512
FORCE_SCORE = 1e4
NEG_INF = -1e30
M_INIT = -1e29
ATTN_SCALE = D_NSA ** -0.5
LOG2E = 1.4426950408889634
QK_SCALE = ATTN_SCALE * LOG2E
ROPE_THETA = 500000.0
ROPE_DIM = D_NSA // 4
ROPE_HALF = ROPE_DIM // 2
IN_SPLITS = (H_GLA * DK_GLA, H_GLA * DK_GLA, H_GLA * DV_GLA, H_GLA * DV_GLA, GLA_RANK,
             H_NSA * D_NSA, 6 * G_NSA * D_NSA, 3 * H_NSA)

LANES = 128
SUBLANES = 8
VMEM_LIMIT = 56 * 1024 * 1024

ROW_TILE = 512
FF_CHUNK = 256
Q_TILE = 256
K_TILE = 512
V_PAD_ROWS = 16
SEL_COLS = 256
PAGES_PER_STEP = 64
ROWS_PER_PAGE = 128
N_GLA_LEVELS = 6
GLA_CHUNKS_PER_STEP = 4
GLA_STEP_SEQS = 8
PAD_BLOCKS = 384


def _params(*sem):
    return pltpu.CompilerParams(dimension_semantics=sem, vmem_limit_bytes=VMEM_LIMIT)


def _rms(x, g):
    return x * lax.rsqrt(jnp.mean(x * x, axis=-1, keepdims=True) + EPS) * g


def _dot(a, b):
    return jnp.dot(a, b, preferred_element_type=F32)


def _dot_nt(a, b):
    return lax.dot_general(a, b, (((1,), (1,)), ((), ())), preferred_element_type=F32)


def _dot_tn(a, b):
    return lax.dot_general(a, b, (((0,), (0,)), ((), ())), preferred_element_type=F32)


def _split_bf16(x, n):
    parts = []
    r = x
    for _ in range(n):
        p = r.astype(BF16)
        parts.append(p)
        r = r - p.astype(F32)
    return parts


def _dot01_left(m01, x, n):
    out = None
    for p in _split_bf16(x, n):
        t = _dot(m01, p)
        out = t if out is None else out + t
    return out


def _dot01_right(x, m01, n):
    out = None
    for p in _split_bf16(x, n):
        t = _dot(p, m01)
        out = t if out is None else out + t
    return out


def _masked_softmax2_rows(s, valid):
    sm = jnp.where(valid, s, NEG_INF)
    m = jnp.max(sm, axis=0, keepdims=True)
    e = jnp.exp2(sm - m)
    return jnp.where(valid, e / jnp.sum(e, axis=0, keepdims=True), 0.0)


def _ffn_kernel(x_ref, gpre_ref, wg_ref, wu_ref, wo_ref, gpost_ref, o_ref):
    x = x_ref[...]
    xn = _rms(x, gpre_ref[...]).astype(BF16)
    acc = jnp.zeros(x.shape, F32)
    for c in range(D_FF // FF_CHUNK):
        sl = slice(c * FF_CHUNK, (c + 1) * FF_CHUNK)
        g = _dot(xn, wg_ref[:, sl])
        u = _dot(xn, wu_ref[:, sl])
        a = (jax.nn.silu(g) * u).astype(BF16)
        acc = acc + _dot(a, wo_ref[sl, :])
    o_ref[...] = x + 0.5 * _rms(acc, gpost_ref[...])


def _ffn(x, gpre, w_in, w_out, gpost, tm):
    m = x.shape[0]
    const = lambda i: (0, 0)
    return pl.pallas_call(
        _ffn_kernel,
        grid=(m // tm,),
        in_specs=[
            pl.BlockSpec((tm, D_MODEL), lambda i: (i, 0)),
            pl.BlockSpec((1, D_MODEL), const),
            pl.BlockSpec((D_MODEL, D_FF), const),
            pl.BlockSpec((D_MODEL, D_FF), lambda i: (0, 1)),
            pl.BlockSpec((D_FF, D_MODEL), const),
            pl.BlockSpec((1, D_MODEL), const),
        ],
        out_specs=pl.BlockSpec((tm, D_MODEL), lambda i: (i, 0)),
        out_shape=jax.ShapeDtypeStruct((m, D_MODEL), F32),
        compiler_params=_params("parallel"),
        name="ffn",
    )(x, gpre, w_in, w_in, w_out, gpost)


N_GLA_COLS = 2 * H_GLA * DK_GLA + 2 * H_GLA * DV_GLA
N_KV_COLS = 6 * G_NSA * D_NSA
WN_COLS = N_GLA_COLS + 3 * LANES
N_Q_COLS = H_NSA * D_NSA
N_GATE_ROWS = 32
WT_ROWS = N_Q_COLS + N_KV_COLS + N_GATE_ROWS
N_CACHE_FEATS = 4 * G_NSA * D_NSA
N_WIN_FEATS = 2 * G_NSA * D_NSA


def _rope_rows(x, cos, sin):
    out = []
    for h in range(x.shape[0] // D_NSA):
        b = h * D_NSA
        x1 = x[b:b + ROPE_HALF]
        x2 = x[b + ROPE_HALF:b + ROPE_DIM]
        out += [x1 * cos - x2 * sin, x2 * cos + x1 * sin, x[b + ROPE_DIM:b + D_NSA]]
    return jnp.concatenate(out, axis=0)


def _rope_lanes(x, c, s1, s2):
    return x * c + pltpu.roll(x, LANES - ROPE_HALF, 1) * s1 + pltpu.roll(x, ROPE_HALF, 1) * s2


def _proj_kernel(h_ref, g_ref, wn_ref, wt_ref, wa2_ref, ba_ref, rc_ref, rs1_ref, rs2_ref, cos_ref, sin_ref,
                 qk_ref, v_ref, r_ref, la_ref, ksel_ref, kwin_ref,
                 qraw_ref, qrot_ref, gate_ref, nsat_ref, wint_ref):
    xn = _rms(h_ref[...], g_ref[...]).astype(BF16)
    nqk = 2 * H_GLA * DK_GLA
    nv = H_GLA * DV_GLA
    z = _dot(xn, wn_ref[:, 0:nqk])
    qk_ref[:, 0:nqk // 2] = z[:, 0:nqk // 2] * (DK_GLA ** -0.5)
    qk_ref[:, nqk // 2:nqk] = z[:, nqk // 2:nqk]
    v_ref[...] = _dot(xn, wn_ref[:, nqk:nqk + nv])
    r_ref[...] = _dot(xn, wn_ref[:, nqk + nv:N_GLA_COLS])
    a_lr = _dot(xn, wn_ref[:, N_GLA_COLS:N_GLA_COLS + LANES])
    xa = _dot(a_lr.astype(BF16), wa2_ref[...]) + ba_ref[...]
    la_ref[...] = (jnp.minimum(xa, 0.0) - jnp.log1p(jnp.exp(-jnp.abs(xa)))) * (1.0 / GLA_GATE_TEMP)
    k0 = N_GLA_COLS + LANES
    rc, rs1, rs2 = rc_ref[...], rs1_ref[...], rs2_ref[...]
    ksel_ref[...] = _rope_lanes(_dot(xn, wn_ref[:, k0:k0 + LANES]), rc, rs1, rs2)
    kwin_ref[...] = _rope_lanes(_dot(xn, wn_ref[:, k0 + LANES:k0 + 2 * LANES]), rc, rs1, rs2)
    zt = _dot_nt(wt_ref[...], xn)
    cos, sin = cos_ref[...], sin_ref[...]
    q = zt[0:N_Q_COLS] * QK_SCALE
    qraw_ref[...] = q.astype(BF16)
    qrot_ref[...] = _rope_rows(q, cos, sin).astype(BF16)
    kv = zt[N_Q_COLS:N_Q_COLS + N_KV_COLS]
    nsat_ref[0, 0:256, :] = kv[0:256]
    nsat_ref[0, 256:384, :] = _rope_rows(kv[256:384], cos, sin)
    nsat_ref[0, 384:512, :] = kv[384:512]
    wint_ref[0, 0:128, :] = _rope_rows(kv[512:640], cos, sin)
    wint_ref[0, 128:256, :] = kv[640:768]
    gate_ref[...] = jax.nn.sigmoid(zt[N_Q_COLS + N_KV_COLS:WT_ROWS])


def _proj(h, gain, wn, wt, wa2, ba, tabs, tm, tiles_per_seq):
    m = h.shape[0]
    rc, rs1, rs2, cos_t, sin_t = tabs
    const = lambda i: (0, 0)
    row = lambda i: (i, 0)
    col = lambda i: (0, i)
    tab_row = lambda i: (i % tiles_per_seq, 0)
    tab_col = lambda i: (0, i % tiles_per_seq)
    outs = [
        (2 * H_GLA * DK_GLA, F32), (H_GLA * DV_GLA, F32), (H_GLA * DV_GLA, F32), (H_GLA * DK_GLA, F32),
        (LANES, F32), (LANES, F32),
    ]
    outs_t = [(N_Q_COLS, BF16), (N_Q_COLS, BF16), (N_GATE_ROWS, F32)]
    nseq = m // (tm * tiles_per_seq)
    seq_len = tm * tiles_per_seq
    outs_seq = [N_CACHE_FEATS, N_WIN_FEATS]
    seq_map = lambda i: (i // tiles_per_seq, 0, i % tiles_per_seq)
    return pl.pallas_call(
        _proj_kernel,
        grid=(m // tm,),
        in_specs=[
            pl.BlockSpec((tm, D_MODEL), row),
            pl.BlockSpec((1, D_MODEL), const),
            pl.BlockSpec((D_MODEL, WN_COLS), const),
            pl.BlockSpec((WT_ROWS, D_MODEL), const),
            pl.BlockSpec((LANES, H_GLA * DK_GLA), const),
            pl.BlockSpec((1, H_GLA * DK_GLA), const),
            pl.BlockSpec((tm, LANES), tab_row),
            pl.BlockSpec((tm, LANES), tab_row),
            pl.BlockSpec((tm, LANES), tab_row),
            pl.BlockSpec((ROPE_HALF, tm), tab_col),
            pl.BlockSpec((ROPE_HALF, tm), tab_col),
        ],
        out_specs=[pl.BlockSpec((tm, n), row) for n, _ in outs] + [pl.BlockSpec((n, tm), col) for n, _ in outs_t]
        + [pl.BlockSpec((1, n, tm), seq_map) for n in outs_seq],
        out_shape=[jax.ShapeDtypeStruct((m, n), d) for n, d in outs]
        + [jax.ShapeDtypeStruct((n, m), d) for n, d in outs_t]
        + [jax.ShapeDtypeStruct((nseq, n, seq_len), F32) for n in outs_seq],
        compiler_params=_params("parallel"),
        name="mixer_proj",
    )(h, gain, wn, wt, wa2, ba, rc, rs1, rs2, cos_t, sin_t)


def _rope_tables(pos):
    pos = np.asarray(pos, np.float64)
    inv_freq = ROPE_THETA ** (-np.arange(ROPE_HALF, dtype=np.float64) * 2.0 / ROPE_DIM)
    ang = pos[:, None] * inv_freq[None, :]
    cos, sin = np.cos(ang), np.sin(ang)
    n = pos.shape[0]
    one = np.ones((n, D_NSA - ROPE_DIM))
    zero = np.zeros((n, D_NSA - ROPE_DIM))
    zh = np.zeros((n, ROPE_HALF))
    c = np.concatenate([cos, cos, one], axis=1)
    s1 = np.concatenate([-sin, zh, zero], axis=1)
    s2 = np.concatenate([zh, sin, zero], axis=1)
    dup = lambda t: np.concatenate([t, t], axis=1)
    return tuple(jnp.asarray(a, F32) for a in (dup(c), dup(s1), dup(s2), cos.T, sin.T))


def _gla_constants():
    c = GLA_CHUNK
    t = np.arange(c)
    low = (t[None, :] <= t[:, None]).astype(np.float32)
    mats, masks = [low], []
    for lev in range(1, N_GLA_LEVELS + 1):
        seg = (2 * c) >> lev
        half = seg // 2
        mid = (t // seg) * seg + half
        mats.append(low[mid])
        same = (t[:, None] // seg) == (t[None, :] // seg)
        masks.append((same & ((t[:, None] % seg) >= half) & ((t[None, :] % seg) < half)).astype(np.float32))
    masks.append(np.eye(c, dtype=np.float32))
    return np.concatenate(mats, axis=0), np.stack(masks)


def _gla_kernel(qk_ref, la_ref, v_ref, r_ref, gain_ref, big_ref, mask_ref, o_ref, sfin_ref, st_ref):
    c = pl.program_id(1)
    ch = GLA_CHUNK

    @pl.when(c == 0)
    def _():
        st_ref[...] = jnp.zeros(st_ref.shape, F32)

    lane = lax.broadcasted_iota(jnp.int32, (1, LANES), 1)
    head_mask = [jnp.where(lane < DK_GLA, 1.0, 0.0), jnp.where(lane >= DK_GLA, 1.0, 0.0)]
    big = big_ref[...]
    gain = gain_ref[...]
    nq = H_GLA * DK_GLA
    n_chunks = qk_ref.shape[0] // ch
    units = [(ci, p) for ci in range(n_chunks) for p in range(H_GLA // 2)]

    allbs = {}
    for ci, p in units:
        rows = slice(ci * ch, (ci + 1) * ch)
        allbs[ci, p] = _dot01_left(big, la_ref[rows, p * LANES:(p + 1) * LANES], 3)
    intra = {}
    for ci, p in units:
        rows = slice(ci * ch, (ci + 1) * ch)
        q = qk_ref[rows, p * LANES:(p + 1) * LANES]
        k = qk_ref[rows, nq + p * LANES:nq + (p + 1) * LANES]
        allb = allbs[ci, p]
        b = allb[0:ch]
        b_last = b[ch - 1:ch]
        attn = [jnp.zeros((ch, ch), F32), jnp.zeros((ch, ch), F32)]
        for lev in range(N_GLA_LEVELS + 1):
            if lev < N_GLA_LEVELS:
                ref = allb[(lev + 1) * ch:(lev + 2) * ch]
                ql = q * jnp.exp(jnp.minimum(b - ref, 0.0))
                kl = k * jnp.exp(jnp.minimum(ref - b, 0.0))
            else:
                ql, kl = q, k
            qq = jnp.concatenate([ql * head_mask[0], ql * head_mask[1]], axis=0).astype(BF16)
            s = _dot_nt(qq, kl.astype(BF16))
            mk = mask_ref[lev]
            attn[0] = attn[0] + mk * s[0:ch]
            attn[1] = attn[1] + mk * s[ch:2 * ch]
        q0 = q * jnp.exp(b)
        k_hat = (k * jnp.exp(b_last - b)).astype(BF16)
        for hh in range(2):
            hs = slice((2 * p + hh) * DV_GLA, (2 * p + hh + 1) * DV_GLA)
            vh = v_ref[rows, hs].astype(BF16)
            intra[ci, 2 * p + hh] = (_dot(attn[hh].astype(BF16), vh), (q0 * head_mask[hh]).astype(BF16),
                                     _dot_tn(vh, k_hat), jnp.exp(b_last))
    for ci in range(n_chunks):
        rows = slice(ci * ch, (ci + 1) * ch)
        for h in range(H_GLA):
            hs = slice(h * DV_GLA, (h + 1) * DV_GLA)
            o_intra, q0h, kv, decay = intra[ci, h]
            st = st_ref[h]
            o = o_intra + _dot_nt(q0h, st.astype(BF16))
            st_ref[h] = st * decay + kv
            o_ref[rows, hs] = _rms(o, gain) * jax.nn.silu(r_ref[rows, hs])

    @pl.when(c == pl.num_programs(1) - 1)
    def _():
        sfin_ref[0] = st_ref[...]


def _gla_prompt(qk, la, v, r, gain, nb, t):
    big, masks = _gla_constants()
    rows = GLA_CHUNK * GLA_CHUNKS_PER_STEP
    nc = t // rows
    row = lambda b, c: (b * nc + c, 0)
    const2 = lambda b, c: (0, 0)
    m = nb * t
    return pl.pallas_call(
        _gla_kernel,
        grid=(nb, nc),
        in_specs=[
            pl.BlockSpec((rows, 2 * H_GLA * DK_GLA), row),
            pl.BlockSpec((rows, H_GLA * DK_GLA), row),
            pl.BlockSpec((rows, H_GLA * DV_GLA), row),
            pl.BlockSpec((rows, H_GLA * DV_GLA), row),
            pl.BlockSpec((1, DV_GLA), const2),
            pl.BlockSpec(big.shape, const2),
            pl.BlockSpec(masks.shape, lambda b, c: (0, 0, 0)),
        ],
        out_specs=[
            pl.BlockSpec((rows, H_GLA * DV_GLA), row),
            pl.BlockSpec((1, H_GLA, DV_GLA, LANES), lambda b, c: (b, 0, 0, 0)),
        ],
        out_shape=[
            jax.ShapeDtypeStruct((m, H_GLA * DV_GLA), F32),
            jax.ShapeDtypeStruct((nb, H_GLA, DV_GLA, LANES), F32),
        ],
        scratch_shapes=[pltpu.VMEM((H_GLA, DV_GLA, LANES), F32)],
        compiler_params=_params("parallel", "arbitrary"),
        name="gla_scan",
    )(qk, la, v, r, gain, jnp.asarray(big, BF16), jnp.asarray(masks, F32))


def _gla_step_kernel(q_ref, k_ref, a_ref, v_ref, r_ref, s_ref, gain_ref, o_ref, sn_ref):
    for b in range(q_ref.shape[0]):
        for h in range(H_GLA):
            s_new = jnp.exp(a_ref[b, h]) * s_ref[b, h] + k_ref[b, h] * v_ref[b, h]
            sn_ref[b, h] = s_new
            o = jnp.sum(q_ref[b, h] * s_new, axis=0, keepdims=True)
            o_ref[b, h] = _rms(o, gain_ref[...]) * jax.nn.silu(r_ref[b, h])


def _gla_step(qcol, kcol, acol, vrow, rrow, state, gain):
    nb = state.shape[0]
    sb = GLA_STEP_SEQS if nb % GLA_STEP_SEQS == 0 else nb
    col = pl.BlockSpec((sb, H_GLA, DK_GLA, 1), lambda b: (b, 0, 0, 0))
    rowspec = pl.BlockSpec((sb, H_GLA, 1, DV_GLA), lambda b: (b, 0, 0, 0))
    stspec = pl.BlockSpec((sb, H_GLA, DK_GLA, DV_GLA), lambda b: (b, 0, 0, 0))
    return pl.pallas_call(
        _gla_step_kernel,
        grid=(nb // sb,),
        in_specs=[col, col, col, rowspec, rowspec, stspec, pl.BlockSpec((1, DV_GLA), lambda b: (0, 0))],
        out_specs=[rowspec, stspec],
        out_shape=[jax.ShapeDtypeStruct((nb, H_GLA, 1, DV_GLA), F32),
                   jax.ShapeDtypeStruct((nb, H_GLA, DK_GLA, DV_GLA), F32)],
        compiler_params=_params("parallel"),
        name="gla_step",
    )(qcol, kcol, acol, vrow, rrow, state, gain)


HALF_ROWS = CMP_STRIDE
CHUNKS_PER_PAGE = ROWS_PER_PAGE // HALF_ROWS


def _compress_kernel(tbl_ref, *refs, pps):
    del tbl_ref
    pages = refs[:pps]
    (wk_ref, wv_ref, posk_ref, w1k_ref, w2k_ref, posv_ref, w1v_ref, w2vt_ref,
     kc_ref, vct_ref, tk_ref, tv_ref, abk_ref, abv_ref) = refs[pps:]
    s = pl.program_id(1)
    step_chunks = pps * CHUNKS_PER_PAGE
    for i in range(pps):
        tk_ref[i] = pages[i][0, 0:LANES, :].T
        tv_ref[i] = pages[i][0, LANES:2 * LANES, :].T
    lpair = 2 * LANES
    row0 = pl.multiple_of(s * step_chunks, step_chunks)
    for t_ref, w_ref, ab_ref in ((tk_ref, wk_ref, abk_ref), (tv_ref, wv_ref, abv_ref)):
        acc = None
        for lp in range(HALF_ROWS // 2):
            cols = [jnp.concatenate([t_ref[i, pl.ds(l, CHUNKS_PER_PAGE, stride=HALF_ROWS), :] for i in range(pps)], axis=0)
                    for l in (2 * lp, 2 * lp + 1)]
            x = jnp.concatenate(cols, axis=1).astype(BF16)
            part = _dot(x, w_ref[lp * lpair:(lp + 1) * lpair, :])
            acc = part if acc is None else acc + part
        ab_ref[pl.ds(row0, step_chunks), :] = acc

    @pl.when(s == pl.num_programs(1) - 1)
    def _():
        nch = abk_ref.shape[0]
        pk = _dot(posk_ref[...], w1k_ref[...])[0:1]
        pv = _dot(posv_ref[...], w1v_ref[...])[0:1]
        abk = abk_ref[...]
        abv = abv_ref[...]
        kc = None
        for g in range(G_NSA):
            o = g * 2 * CMP_HIDDEN
            hk = jax.nn.gelu(abk[:, o:o + CMP_HIDDEN] + pltpu.roll(abk[:, o + CMP_HIDDEN:o + 2 * CMP_HIDDEN], nch - 1, 0) + pk)
            hv = jax.nn.gelu(abv[:, o:o + CMP_HIDDEN] + pltpu.roll(abv[:, o + CMP_HIDDEN:o + 2 * CMP_HIDDEN], nch - 1, 0) + pv)
            tk = _dot(hk.astype(BF16), w2k_ref[g])
            kc = tk if kc is None else kc + tk
            vct_ref[0, g * D_NSA:(g + 1) * D_NSA, :] = _dot_nt(w2vt_ref[...], hv.astype(BF16)).astype(BF16)
        kc_ref[0] = kc.astype(BF16)


def _compress(pages_t, table, w, paged):
    nb, npages = table.shape
    pps = min(PAGES_PER_STEP, npages)
    nsteps = npages // pps
    nch = npages * CHUNKS_PER_PAGE

    def page_map(b, s, tbl, i):
        return (tbl[b, s * pps + i], 0, 0) if paged else (b, 0, s * pps + i)

    page_specs = [pl.BlockSpec((1, 2 * LANES, ROWS_PER_PAGE), functools.partial(page_map, i=i)) for i in range(pps)]
    names = ("cmp_wk", "cmp_wv", "cmp_posk", "cmp_w1k", "cmp_w2k", "cmp_posv", "cmp_w1v", "cmp_w2vt")
    consts = [w[n] for n in names]
    const_specs = [pl.BlockSpec(c.shape, functools.partial(lambda b, s, tbl, nd: (0,) * nd, nd=c.ndim)) for c in consts]
    grid_spec = pltpu.PrefetchScalarGridSpec(
        num_scalar_prefetch=1,
        grid=(nb, nsteps),
        in_specs=page_specs + const_specs,
        out_specs=[pl.BlockSpec((1, nch, LANES), lambda b, s, tbl: (b, 0, 0)),
                   pl.BlockSpec((1, LANES, nch), lambda b, s, tbl: (b, 0, 0))],
        scratch_shapes=[pltpu.VMEM((pps, ROWS_PER_PAGE, LANES), F32), pltpu.VMEM((pps, ROWS_PER_PAGE, LANES), F32),
                        pltpu.VMEM((nch, 4 * CMP_HIDDEN), F32), pltpu.VMEM((nch, 4 * CMP_HIDDEN), F32)],
    )
    return pl.pallas_call(
        functools.partial(_compress_kernel, pps=pps),
        grid_spec=grid_spec,
        out_shape=[jax.ShapeDtypeStruct((nb, nch, LANES), BF16),
                   jax.ShapeDtypeStruct((nb, LANES, nch), BF16)],
        compiler_params=_params("parallel", "arbitrary"),
        name="compress",
    )(table, *([pages_t] * pps), *consts)


def _cmp_weights(pos, w1, w2):
    w = w1.reshape(2, HALF_ROWS, D_NSA, CMP_HIDDEN)
    z = jnp.zeros_like(w[0])
    blocks = []
    for g in range(G_NSA):
        cols = []
        for g2 in range(G_NSA):
            for half in range(2):
                cols.append(w[half] if g2 == g else z)
        blocks.append(jnp.concatenate(cols, axis=-1))
    wbig = jnp.stack(blocks, axis=1).reshape(HALF_ROWS * LANES, 4 * CMP_HIDDEN).astype(BF16)
    pos8 = jnp.broadcast_to(pos.reshape(1, CMP_LEN * D_NSA), (SUBLANES, CMP_LEN * D_NSA)).astype(BF16)
    z2 = jnp.zeros_like(w2)
    w2pad = jnp.stack([jnp.concatenate([w2, z2], axis=1), jnp.concatenate([z2, w2], axis=1)]).astype(BF16)
    return wbig, pos8, w1.astype(BF16), w2pad, w2.T.astype(BF16)


def _overlap_t(nc_pad, ns_pad, nc, ns):
    i = np.arange(nc_pad)[None, :] * CMP_STRIDE
    j = np.arange(ns_pad)[:, None] * SEL_BLOCK
    m = (i < j + SEL_BLOCK) & (i + CMP_LEN > j) & (np.arange(nc_pad)[None, :] < nc) & (np.arange(ns_pad)[:, None] < ns)
    return jnp.asarray(m.astype(np.float32), BF16)


def _rank_desc(x):
    nrow = x.shape[0]
    nblk = nrow // SUBLANES
    blocks = [x[v * SUBLANES:(v + 1) * SUBLANES] for v in range(nblk)]
    cnt = [jnp.zeros((SUBLANES, x.shape[1]), F32) for _ in range(nblk)]
    sub = lax.broadcasted_iota(jnp.int32, (SUBLANES, x.shape[1]), 0)
    for jp in range(nrow):
        row = x[jp:jp + 1]
        vb = jp // SUBLANES
        for v in range(nblk):
            ge = jnp.where(row >= blocks[v], 1.0, 0.0)
            gt = jnp.where(row > blocks[v], 1.0, 0.0)
            if v > vb:
                cnt[v] = cnt[v] + ge
            elif v < vb:
                cnt[v] = cnt[v] + gt
            else:
                cnt[v] = cnt[v] + jnp.where(sub > (jp % SUBLANES), ge, gt)
    return jnp.concatenate(cnt, axis=0)


def _nsa_prompt_kernel(qraw_ref, qrot_ref, gate_ref, kc_ref, vct_ref, ksel_ref, vselt_ref, kwin_ref, vwint_ref,
                       ovl_ref, o_ref, kaug_ref, kwb_ref, vsa_ref, vwa_ref):
    qb = pl.program_id(1)
    nlane = R_NSA * Q_TILE
    tpos = qb * Q_TILE + lax.broadcasted_iota(jnp.int32, (1, nlane), 1) % Q_TILE
    tpos1 = tpos[:, 0:Q_TILE]
    ncp = kc_ref.shape[1]
    nsb = ovl_ref.shape[0]
    seq = ksel_ref.shape[0]
    zeros_q = jnp.zeros((D_NSA, nlane), BF16)

    @pl.when(qb == 0)
    def _():
        ks = ksel_ref[...]
        lane = lax.broadcasted_iota(jnp.int32, (1, LANES), 1)
        blk = lax.broadcasted_iota(jnp.int32, (seq, 1), 0) // SEL_BLOCK
        kaug_ref[0] = jnp.where(lane < D_NSA, ks, jnp.where(lane - D_NSA == blk, 1.0, 0.0)).astype(BF16)
        kaug_ref[1] = jnp.where(lane >= D_NSA, ks, jnp.where(lane == blk, 1.0, 0.0)).astype(BF16)
        kwb_ref[...] = kwin_ref[...].astype(BF16)
        ones_row = jnp.where(lax.broadcasted_iota(jnp.int32, (V_PAD_ROWS, seq), 0) == 0, 1.0, 0.0)
        for g in range(G_NSA):
            gs = slice(g * D_NSA, (g + 1) * D_NSA)
            vsa_ref[g] = jnp.concatenate([vselt_ref[0, gs, :], ones_row], axis=0).astype(BF16)
            vwa_ref[g] = jnp.concatenate([vwint_ref[0, gs, :], ones_row], axis=0).astype(BF16)

    def group_q(ref, g):
        q = jnp.concatenate([ref[h * D_NSA:(h + 1) * D_NSA, :] for h in range(g * R_NSA, (g + 1) * R_NSA)], axis=1)
        return q, (jnp.concatenate([q, zeros_q], axis=0) if g == 0 else jnp.concatenate([zeros_q, q], axis=0))

    cmp_scores = [_dot(kc_ref[0], group_q(qraw_ref, g)[1]) for g in range(G_NSA)]
    o_cmps, qaugs, qpads = [], [], []
    for g in range(G_NSA):
        qrot, qpad = group_q(qrot_ref, g)
        gs = slice(g * D_NSA, (g + 1) * D_NSA)

        sc = cmp_scores[g]
        ci = lax.broadcasted_iota(jnp.int32, (ncp, 1), 0)
        p_cmp = _masked_softmax2_rows(sc, ci * CMP_STRIDE + (CMP_LEN - 1) <= tpos)
        o_cmp = _dot(vct_ref[0, gs, :], p_cmp.astype(BF16))
        psum = p_cmp[:, 0:Q_TILE]
        for r in range(1, R_NSA):
            psum = psum + p_cmp[:, r * Q_TILE:(r + 1) * Q_TILE]
        imp = _dot01_left(ovl_ref[...], psum, 2)
        bj = lax.broadcasted_iota(jnp.int32, (nsb, 1), 0)
        cur = tpos1 // SEL_BLOCK
        forced = (bj == 0) | (bj == cur) | (bj == cur - 1)
        imp = jnp.where(forced, FORCE_SCORE, jnp.where(bj * SEL_BLOCK <= tpos1, imp, -1.0))
        bias = jnp.where(_rank_desc(imp) < N_SELECT, 0.0, NEG_INF).astype(BF16)
        bias = jnp.concatenate([bias] * R_NSA, axis=1)
        if nsb < D_NSA:
            bias = jnp.concatenate([bias, jnp.zeros((D_NSA - nsb, nlane), BF16)], axis=0)
        o_cmps.append(o_cmp)
        qaugs.append(jnp.concatenate([qrot, bias], axis=0) if g == 0 else jnp.concatenate([bias, qrot], axis=0))
        qpads.append(qpad)

    krow = lax.broadcasted_iota(jnp.int32, (K_TILE, 1), 0)
    ncol = nlane // SEL_COLS
    pairs = [(g, slice(j * SEL_COLS, (j + 1) * SEL_COLS)) for g in range(G_NSA) for j in range(ncol)]

    def sel_step(c, carry, causal):
        start = pl.multiple_of(c * K_TILE, K_TILE)
        out = []
        scores = [_dot(kaug_ref[g, pl.ds(start, K_TILE), :], qaugs[g][:, cols]) for g, cols in pairs]
        for (g, cols), (m, acc), s in zip(pairs, carry, scores):
            if causal:
                s = jnp.where(start + krow <= tpos[:, cols], s, NEG_INF)
            m_new = jnp.maximum(m, jnp.max(s, axis=0, keepdims=True))
            e = jnp.exp2(s - m_new).astype(BF16)
            acc = jnp.exp2(m - m_new) * acc + _dot(vsa_ref[g, :, pl.ds(start, K_TILE)], e)
            out.append((m_new, acc))
        return tuple(out)

    n_full = (qb * Q_TILE) // K_TILE
    init = tuple((jnp.full((1, SEL_COLS), M_INIT, F32), jnp.zeros((D_NSA + V_PAD_ROWS, SEL_COLS), F32))
                 for _ in pairs)
    carry = lax.fori_loop(0, n_full, functools.partial(sel_step, causal=False), init)

    band = WINDOW + Q_TILE
    wstart = pl.multiple_of(jnp.maximum(qb * Q_TILE - WINDOW, 0), Q_TILE)
    win_scores = [_dot(kwb_ref[pl.ds(wstart, band), :], qpads[g]) for g in range(G_NSA)]

    carry = sel_step(n_full, carry, True)
    o_sels = []
    for g in range(G_NSA):
        acc_sel = jnp.concatenate([carry[g * ncol + j][1] for j in range(ncol)], axis=1)
        o_sels.append(acc_sel[0:D_NSA] / acc_sel[D_NSA:D_NSA + 1])

    diff = tpos - (wstart + lax.broadcasted_iota(jnp.int32, (band, 1), 0))
    old_edge = diff[0:Q_TILE] < WINDOW
    causal_ok = diff >= 0
    heads = []
    for g in range(G_NSA):
        sw = win_scores[g]
        sw = jnp.concatenate([jnp.where(old_edge, sw[0:Q_TILE], NEG_INF), sw[Q_TILE:]], axis=0)
        sw = jnp.where(causal_ok, sw, NEG_INF)
        e_win = jnp.exp2(sw - jnp.max(sw, axis=0, keepdims=True)).astype(BF16)
        acc_win = _dot(vwa_ref[g, :, pl.ds(wstart, band)], e_win)
        o_win = acc_win[0:D_NSA] / acc_win[D_NSA:D_NSA + 1]
        for r in range(R_NSA):
            h = g * R_NSA + r
            ls = slice(r * Q_TILE, (r + 1) * Q_TILE)
            heads.append(gate_ref[3 * h:3 * h + 1, :] * o_cmps[g][:, ls]
                         + gate_ref[3 * h + 1:3 * h + 2, :] * o_sels[g][:, ls]
                         + gate_ref[3 * h + 2:3 * h + 3, :] * o_win[:, ls])
    o_ref[...] = jnp.concatenate(heads, axis=0).T


def _nsa_prompt(qraw_t, qrot_t, gate_t, kc, vct, ksel, nsa_t, kwin, win_t, nb, t):
    nqb = t // Q_TILE
    ncp = kc.shape[1]
    ovl = _overlap_t(ncp, t // SEL_BLOCK, (t - CMP_LEN) // CMP_STRIDE + 1, t // SEL_BLOCK)
    qcol = lambda b, q: (0, b * nqb + q)
    m = nb * t
    return pl.pallas_call(
        _nsa_prompt_kernel,
        grid=(nb, nqb),
        in_specs=[
            pl.BlockSpec((N_Q_COLS, Q_TILE), qcol),
            pl.BlockSpec((N_Q_COLS, Q_TILE), qcol),
            pl.BlockSpec((N_GATE_ROWS, Q_TILE), qcol),
            pl.BlockSpec((1, ncp, LANES), lambda b, q: (b, 0, 0)),
            pl.BlockSpec((1, LANES, ncp), lambda b, q: (b, 0, 0)),
            pl.BlockSpec((t, LANES), lambda b, q: (b, 0)),
            pl.BlockSpec((1, LANES, t), lambda b, q: (b, 3, 0)),
            pl.BlockSpec((t, LANES), lambda b, q: (b, 0)),
            pl.BlockSpec((1, LANES, t), lambda b, q: (b, 1, 0)),
            pl.BlockSpec(ovl.shape, lambda b, q: (0, 0)),
        ],
        out_specs=pl.BlockSpec((Q_TILE, N_Q_COLS), lambda b, q: (b * nqb + q, 0)),
        out_shape=jax.ShapeDtypeStruct((m, N_Q_COLS), F32),
        scratch_shapes=[pltpu.VMEM((G_NSA, t, LANES), BF16), pltpu.VMEM((t, LANES), BF16),
                        pltpu.VMEM((G_NSA, D_NSA + V_PAD_ROWS, t), BF16),
                        pltpu.VMEM((G_NSA, D_NSA + V_PAD_ROWS, t), BF16)],
        compiler_params=_params("parallel", "arbitrary"),
        name="nsa_prompt",
    )(qraw_t, qrot_t, gate_t, kc, vct, ksel, nsa_t, kwin, win_t, ovl)


def _nsa_decode_select_kernel(qt_ref, kc_ref, vct_ref, ovl_ref, grp_ref, idx_ref, ocmp_ref, *, pos, n_blocks):
    ncp = kc_ref.shape[1]
    sc = _dot(kc_ref[0], qt_ref[0])
    ci = lax.broadcasted_iota(jnp.int32, (ncp, 1), 0)
    p = _masked_softmax2_rows(sc, ci * CMP_STRIDE + (CMP_LEN - 1) <= pos)
    ocmp_ref[0] = _dot(vct_ref[0], p.astype(BF16))
    psum = _dot01_right(p, grp_ref[...], 2)
    imp_t = _dot01_left(ovl_ref[...], psum, 2)
    imp_r = imp_t.T
    cur = pos // SEL_BLOCK

    def finish(v, j):
        forced = (j == 0) | (j == cur) | (j == cur - 1)
        v = jnp.where(forced, FORCE_SCORE, jnp.where(j * SEL_BLOCK <= pos, v, -1.0))
        return jnp.where(j < n_blocks, v, -3e38)

    jc = lax.broadcasted_iota(jnp.int32, (PAD_BLOCKS, 1), 0)
    jr = lax.broadcasted_iota(jnp.int32, (1, PAD_BLOCKS), 1)
    kk = lax.broadcasted_iota(jnp.int32, (N_SELECT, 1), 0).astype(F32)
    lane = lax.broadcasted_iota(jnp.int32, (N_SELECT, LANES), 1)
    out = jnp.zeros((N_SELECT, LANES), F32)
    for g in range(G_NSA):
        c0 = g * R_NSA
        col = finish(imp_t[:, c0:c0 + 1], jc)
        row = finish(imp_r[c0:c0 + 1, :], jr)
        beats = (col > row) | ((col == row) & (jc < jr))
        rank = jnp.sum(jnp.where(beats, 1.0, 0.0), axis=0, keepdims=True)
        hit = rank == kk
        idx = jnp.sum(jnp.where(hit, jr.astype(F32), 0.0), axis=1, keepdims=True)
        out = jnp.where(lane == g, idx, out)
    idx_ref[0] = out.astype(jnp.int32)


def _nsa_decode_select(q_t, kc, vct, pos, n_blocks):
    nb, ncp, _ = kc.shape
    nc = (pos + 1 - CMP_LEN) // CMP_STRIDE + 1
    ovl = _overlap_t(ncp, PAD_BLOCKS, nc, n_blocks)
    hh = np.arange(LANES)
    grp = ((hh[:, None] // R_NSA) == (hh[None, :] // R_NSA)) & (hh[:, None] < H_NSA) & (hh[None, :] < H_NSA)
    grp = jnp.asarray(grp.astype(np.float32), BF16)
    return pl.pallas_call(
        functools.partial(_nsa_decode_select_kernel, pos=pos, n_blocks=n_blocks),
        grid=(nb,),
        in_specs=[
            pl.BlockSpec((1, LANES, LANES), lambda b: (b, 0, 0)),
            pl.BlockSpec((1, ncp, LANES), lambda b: (b, 0, 0)),
            pl.BlockSpec((1, LANES, ncp), lambda b: (b, 0, 0)),
            pl.BlockSpec(ovl.shape, lambda b: (0, 0)),
            pl.BlockSpec(grp.shape, lambda b: (0, 0)),
        ],
        out_specs=[pl.BlockSpec((1, N_SELECT, LANES), lambda b: (b, 0, 0)),
                   pl.BlockSpec((1, LANES, LANES), lambda b: (b, 0, 0))],
        out_shape=[jax.ShapeDtypeStruct((nb, N_SELECT, LANES), jnp.int32),
                   jax.ShapeDtypeStruct((nb, LANES, LANES), F32)],
        compiler_params=_params("parallel"),
        name="nsa_decode_select",
    )(q_t, kc, vct, ovl, grp)


def _nsa_decode_attend_kernel(idx_ref, tbl_ref, *refs, n_past_blocks, win_buf):
    del tbl_ref
    nblk = G_NSA * N_SELECT
    blocks = refs[:nblk]
    q_ref, win_ref, nsa_new_ref, win_new_ref, gate_ref, ocmp_ref, o_ref = refs[nblk:]
    b = pl.program_id(0)
    q = q_ref[0]
    qf = q.astype(F32)
    rowg = lax.broadcasted_iota(jnp.int32, (H_NSA, 1), 0) // R_NSA
    ks_new = nsa_new_ref[0, :, 2 * LANES:3 * LANES]
    vs_new = nsa_new_ref[0, :, 3 * LANES:4 * LANES]
    s_new = jnp.sum(qf * ks_new, axis=1, keepdims=True)
    col = lax.broadcasted_iota(jnp.int32, (1, N_SELECT * ROWS_PER_PAGE), 1)
    colpage = col // ROWS_PER_PAGE
    colhalf = (col % ROWS_PER_PAGE) // SEL_BLOCK
    halves = ROWS_PER_PAGE // SEL_BLOCK
    o_sel = None
    for g in range(G_NSA):
        ks_t = jnp.concatenate([blocks[g * N_SELECT + k][0, 0:LANES, :] for k in range(N_SELECT)], axis=1)
        vs_t = jnp.concatenate([blocks[g * N_SELECT + k][0, LANES:2 * LANES, :] for k in range(N_SELECT)], axis=1)
        s = _dot(q, ks_t.astype(BF16))
        want = jnp.full(col.shape, -1, jnp.int32)
        for k in range(N_SELECT):
            j = idx_ref[b, g, k]
            half = jnp.where(j < n_past_blocks, j % halves, -1)
            want = jnp.where(colpage == k, half, want)
        valid = colhalf == want
        sm = jnp.where(valid, s, NEG_INF)
        m = jnp.maximum(jnp.max(sm, axis=1, keepdims=True), s_new)
        e = jnp.where(valid, jnp.exp2(sm - m), 0.0)
        e_new = jnp.exp2(s_new - m)
        l = jnp.sum(e, axis=1, keepdims=True) + e_new
        og = (_dot_nt(e.astype(BF16), vs_t.astype(BF16)) + e_new * vs_new) / l
        o_sel = og if o_sel is None else jnp.where(rowg == g, og, o_sel)

    kw_t = win_ref[0, 0:LANES, :]
    vw_t = win_ref[0, LANES:2 * LANES, :]
    kw_new = win_new_ref[0, :, 0:LANES]
    vw_new = win_new_ref[0, :, LANES:2 * LANES]
    sw = _dot(q, kw_t.astype(BF16))
    sw_new = jnp.sum(qf * kw_new, axis=1, keepdims=True)
    diff = win_buf - lax.broadcasted_iota(jnp.int32, (1, win_buf), 1)
    validw = (diff >= 0) & (diff < WINDOW)
    smw = jnp.where(validw, sw, NEG_INF)
    mw = jnp.maximum(jnp.max(smw, axis=1, keepdims=True), sw_new)
    ew = jnp.where(validw, jnp.exp2(smw - mw), 0.0)
    ew_new = jnp.exp2(sw_new - mw)
    lw = jnp.sum(ew, axis=1, keepdims=True) + ew_new
    o_win = (_dot_nt(ew.astype(BF16), vw_t.astype(BF16)) + ew_new * vw_new) / lw

    gt = gate_ref[0]
    o_ref[0] = gt[:, 0:1] * ocmp_ref[0] + gt[:, 1:2] * o_sel + gt[:, 2:3] * o_win


def _nsa_decode_attend(idx, table, cache_t, q2, win_t, nsa_new, win_new, gates, ocmp, n_past_blocks):
    nb = q2.shape[0]
    win_buf = win_t.shape[2]
    halves = ROWS_PER_PAGE // SEL_BLOCK

    def blk_map(b, idx_ref, tbl_ref, g, k):
        j = jnp.minimum(idx_ref[b, g, k], n_past_blocks - 1)
        return (tbl_ref[b, j // halves], 1, 0)

    blk_specs = [pl.BlockSpec((1, 2 * LANES, ROWS_PER_PAGE), functools.partial(blk_map, g=g, k=k))
                 for g in range(G_NSA) for k in range(N_SELECT)]
    per_b = lambda shape: pl.BlockSpec((1,) + shape, lambda b, i, t: (b, 0, 0))
    grid_spec = pltpu.PrefetchScalarGridSpec(
        num_scalar_prefetch=2,
        grid=(nb,),
        in_specs=blk_specs + [per_b((H_NSA, LANES)), per_b((2 * LANES, win_buf)), per_b((1, 4 * LANES)),
                              per_b((1, 2 * LANES)), per_b((H_NSA, LANES)), per_b((H_NSA, LANES))],
        out_specs=per_b((H_NSA, LANES)),
    )
    return pl.pallas_call(
        functools.partial(_nsa_decode_attend_kernel, n_past_blocks=n_past_blocks, win_buf=win_buf),
        grid_spec=grid_spec,
        out_shape=jax.ShapeDtypeStruct((nb, H_NSA, LANES), F32),
        compiler_params=_params("parallel"),
        name="nsa_decode_attend",
    )(idx, table, *([cache_t] * (G_NSA * N_SELECT)), q2, win_t, nsa_new, win_new, gates, ocmp)


def _post_mixer_kernel(h_ref, a_ref, b_ref, p_ref, wa_ref, wb_ref, gm_ref,
                       gpre_ref, wg_ref, wu_ref, wo_ref, gpost_ref,
                       ppre_ref, pg_ref, pp_ref, ppost_ref, o_ref):
    y = _dot(a_ref[...].astype(BF16), wa_ref[...]) + _dot(b_ref[...].astype(BF16), wb_ref[...])
    h = h_ref[...] + _rms(y, gm_ref[...])
    xn = _rms(h, gpre_ref[...]).astype(BF16)
    acc = jnp.zeros(h.shape, F32)
    for c in range(D_FF // FF_CHUNK):
        sl = slice(c * FF_CHUNK, (c + 1) * FF_CHUNK)
        g = _dot(xn, wg_ref[:, sl])
        u = _dot(xn, wu_ref[:, sl])
        acc = acc + _dot((jax.nn.silu(g) * u).astype(BF16), wo_ref[sl, :])
    h = h + 0.5 * _rms(acc, gpost_ref[...])
    gate = jax.nn.sigmoid(_dot(_rms(h, ppre_ref[...]).astype(BF16), pg_ref[...]))
    o_ref[...] = h + _rms(gate * _dot(p_ref[...].astype(BF16), pp_ref[...]), ppost_ref[...])


def _post_mixer(h, a, b, p, w, tm):
    m = h.shape[0]
    row = lambda i: (i, 0)
    once = dict(pipeline_mode=pl.Buffered(1))
    const = lambda shape, idx=(0, 0): pl.BlockSpec(shape, lambda i: idx, **once)
    vec = const((1, D_MODEL))
    return pl.pallas_call(
        _post_mixer_kernel,
        grid=(m // tm,),
        in_specs=[pl.BlockSpec((tm, D_MODEL), row), pl.BlockSpec((tm, a.shape[1]), row),
                  pl.BlockSpec((tm, b.shape[1]), row), pl.BlockSpec((tm, PLE_DIM), row),
                  const(w["wo_gla"].shape), const(w["wo_nsa"].shape), vec,
                  vec, const((D_MODEL, D_FF)), const((D_MODEL, D_FF), (0, 1)), const((D_FF, D_MODEL)), vec,
                  vec, const(w["ple_gate"].shape), const(w["ple_proj"].shape), vec],
        out_specs=pl.BlockSpec((tm, D_MODEL), row),
        out_shape=jax.ShapeDtypeStruct((m, D_MODEL), F32),
        compiler_params=_params("parallel"),
        name="post_mixer",
    )(h, a, b, p, w["wo_gla"], w["wo_nsa"], w["m_post"],
      w["f2_pre"], w["f2_in"], w["f2_in"], w["f2_out"], w["f2_post"],
      w["ple_pre"], w["ple_gate"], w["ple_proj"], w["ple_post"])


def _split_in_cols(w):
    outs, off = [], 0
    for n in IN_SPLITS:
        outs.append(w[:, off:off + n])
        off += n
    return outs


def _prep_mixer_weights(w_in, w_a2, b_a):
    q_g, k_g, v_g, r_g, a_lr, q_n, kv_n, gate_n = _split_in_cols(w_in)
    pad_cols = lambda w, n: jnp.pad(w, ((0, 0), (0, n - w.shape[1])))
    gd = G_NSA * D_NSA
    k_sel = kv_n[:, 2 * gd:3 * gd]
    k_win = kv_n[:, 4 * gd:5 * gd]
    wn = jnp.concatenate([q_g, k_g, v_g, r_g, pad_cols(a_lr, LANES), k_sel, k_win], axis=1).astype(BF16)
    wt = jnp.concatenate([q_n, kv_n, pad_cols(gate_n, N_GATE_ROWS)], axis=1).T.astype(BF16)
    wa2 = jnp.pad(w_a2, ((0, LANES - GLA_RANK), (0, 0))).astype(BF16)
    return wn, wt, wa2, b_a.reshape(1, -1)


def _row_tile(m):
    return ROW_TILE if m % ROW_TILE == 0 else m


def _layer(x2, p2, mixer, w):
    tm = _row_tile(x2.shape[0])
    h = _ffn(x2, w["f1_pre"], w["f1_in"], w["f1_out"], w["f1_post"], tm)
    o_gla, o_nsa, extras = mixer(h)
    h = _post_mixer(h, o_gla, o_nsa, p2, w, tm)
    return h, extras


def _mixer_prompt(h, w, nb, t):
    tm = _row_tile(h.shape[0])
    tabs = _rope_tables(np.arange(t))
    (qk, v, r, la, ksel, kwin, qraw_t, qrot_t, gate_t, nsa_t, win_t) = _proj(
        h, w["m_pre"], w["wn"], w["wt"], w["wa2"], w["ba"], tabs, tm, t // tm)
    o_gla, s_fin = _gla_prompt(qk, la, v, r, w["gla_gain"], nb, t)
    table = jnp.zeros((nb, t // ROWS_PER_PAGE), jnp.int32)
    kc, vct = _compress(nsa_t, table, w, False)
    o_nsa = _nsa_prompt(qraw_t, qrot_t, gate_t, kc, vct, ksel, nsa_t, kwin, win_t, nb, t)
    s_t = s_fin.reshape(nb, H_GLA, DV_GLA, 2, DK_GLA)
    s_own = jnp.stack([s_t[:, hh, :, hh % 2, :] for hh in range(H_GLA)], axis=1)
    gla_state = jnp.swapaxes(s_own, -1, -2)
    keep = min(WINDOW, t)
    rows_first = lambda a, n: jnp.transpose(a.reshape(nb, n, G_NSA, D_NSA, a.shape[-1]), (0, 4, 1, 2, 3))
    nsa_rows = rows_first(nsa_t, 4)
    win_rows = rows_first(win_t[:, :, t - keep:], 2)
    return o_gla, o_nsa, (nsa_rows, win_rows, gla_state)


def _mixer_sample(h, w, cache_l, win_l, gla_l, page_table):
    nb = h.shape[0]
    n_pages = page_table.shape[1]
    past_len = n_pages * cache_l.shape[1]
    pos = past_len
    tabs = _rope_tables(np.full((nb,), pos))
    (qk, v, r, la, _, _, qraw_t, qrot_t, gate_t, nsa_new_t, win_new_t) = _proj(
        h, w["m_pre"], w["wn"], w["wt"], w["wa2"], w["ba"], tabs, nb, 1)
    nsa = nsa_new_t[0].T
    win = win_new_t[0].T

    nq = H_GLA * DK_GLA
    col = lambda a: a.reshape(nb, H_GLA, DK_GLA, 1)
    rowv = lambda a: a.reshape(nb, H_GLA, 1, DV_GLA)
    o_gla, gla_state = _gla_step(col(qk[:, :nq]), col(qk[:, nq:]), col(la), rowv(v), rowv(r),
                                 gla_l.astype(F32), w["gla_gain"])
    o_gla = o_gla.reshape(nb, H_GLA * DV_GLA)

    cache_t = jnp.transpose(cache_l.reshape(cache_l.shape[0], ROWS_PER_PAGE, 4 * LANES), (0, 2, 1))
    kc, vct = _compress(cache_t, page_table, w, True)

    hg = (jnp.arange(H_NSA) // R_NSA)[None, :, None]

    def group_pad(q_t):
        q8 = q_t.T.reshape(nb, H_NSA, D_NSA)
        return jnp.concatenate([jnp.where(hg == 0, q8, 0), jnp.where(hg == 1, q8, 0)], axis=-1)

    q2_raw = group_pad(qraw_t)
    q2_rot = group_pad(qrot_t)
    q2_raw_t = jnp.pad(jnp.swapaxes(q2_raw, 1, 2), ((0, 0), (0, 0), (0, LANES - H_NSA)))
    n_blocks = -(-(past_len + 1) // SEL_BLOCK)
    idx_pad, ocmp_t = _nsa_decode_select(q2_raw_t, kc, vct, pos, n_blocks)
    idx = jnp.stack([idx_pad[:, :, g] for g in range(G_NSA)], axis=1)
    ocmp = jnp.swapaxes(ocmp_t, 1, 2)[:, :H_NSA, :]
    gates = jnp.pad(gate_t[:3 * H_NSA].T.reshape(nb, H_NSA, 3), ((0, 0), (0, 0), (0, LANES - 3)))
    wb = win_l.shape[1]
    win_buf_t = jnp.transpose(win_l.reshape(nb, wb, 2 * LANES), (0, 2, 1))
    o8 = _nsa_decode_attend(idx, page_table, cache_t, q2_rot, win_buf_t,
                            nsa.reshape(nb, 1, 4 * LANES), win.reshape(nb, 1, 2 * LANES), gates, ocmp,
                            past_len // SEL_BLOCK)
    o8 = o8.reshape(nb, H_NSA, G_NSA, D_NSA)
    o_nsa = jnp.concatenate([o8[:, :R_NSA, 0], o8[:, R_NSA:, 1]], axis=1).reshape(nb, H_NSA * D_NSA)

    nsa_rows = nsa.reshape(nb, 1, 4, G_NSA, D_NSA)
    win_new = win.reshape(nb, 1, 2, G_NSA, D_NSA)
    kw = jnp.concatenate([win_l, win_new.astype(win_l.dtype)], axis=1)
    keep = min(WINDOW, wb + 1)
    return o_gla, o_nsa, (nsa_rows, kw[:, wb + 1 - keep:], gla_state.astype(gla_l.dtype))


def kernel(x_prompt, x_sample, cache_nsa, state_win, state_gla, page_table, p_prompt, p_sample,
           ffn1_norm_pre, ffn1_norm_post, ffn1_w_in, ffn1_w_out,
           mix_norm_pre, mix_norm_post, w_mix_in, w_gla_a2, b_gla_a, gla_out_norm,
           cmp_pos_k, w_cmp_k1, w_cmp_k2, cmp_pos_v, w_cmp_v1, w_cmp_v2, w_mix_out,
           ffn2_norm_pre, ffn2_norm_post, ffn2_w_in, ffn2_w_out,
           ple_norm_pre, ple_w_gate, ple_w_proj, ple_norm_post):
    nb, t, _ = x_prompt.shape
    ns = x_sample.shape[0]
    depth = ffn1_w_in.shape[0]
    hp = x_prompt.reshape(nb * t, D_MODEL)
    hs = x_sample.reshape(ns, D_MODEL)
    outs = [[] for _ in range(6)]
    for i in range(depth):
        wn, wt, wa2, ba = _prep_mixer_weights(w_mix_in[i], w_gla_a2[i], b_gla_a[i])
        cmp_wk, cmp_posk, cmp_w1k, cmp_w2k, _ = _cmp_weights(cmp_pos_k[i], w_cmp_k1[i], w_cmp_k2[i])
        cmp_wv, cmp_posv, cmp_w1v, _, cmp_w2vt = _cmp_weights(cmp_pos_v[i], w_cmp_v1[i], w_cmp_v2[i])
        gla_w = H_GLA * DV_GLA
        w = dict(
            f1_pre=ffn1_norm_pre[i][None], f1_post=ffn1_norm_post[i][None],
            f1_in=ffn1_w_in[i].astype(BF16), f1_out=ffn1_w_out[i].astype(BF16),
            m_pre=mix_norm_pre[i][None], m_post=mix_norm_post[i][None],
            wn=wn, wt=wt, wa2=wa2, ba=ba, gla_gain=gla_out_norm[i][None],
            cmp_wk=cmp_wk, cmp_posk=cmp_posk, cmp_w1k=cmp_w1k, cmp_w2k=cmp_w2k,
            cmp_wv=cmp_wv, cmp_posv=cmp_posv, cmp_w1v=cmp_w1v, cmp_w2vt=cmp_w2vt,
            wo_gla=w_mix_out[i][:gla_w].astype(BF16), wo_nsa=w_mix_out[i][gla_w:].astype(BF16),
            f2_pre=ffn2_norm_pre[i][None], f2_post=ffn2_norm_post[i][None],
            f2_in=ffn2_w_in[i].astype(BF16), f2_out=ffn2_w_out[i].astype(BF16),
            ple_pre=ple_norm_pre[i][None], ple_post=ple_norm_post[i][None],
            ple_gate=ple_w_gate[i].astype(BF16), ple_proj=ple_w_proj[i].astype(BF16),
        )
        hp, (r_p, w_p, s_p) = _layer(hp, p_prompt[i].reshape(nb * t, PLE_DIM),
                                     functools.partial(_mixer_prompt, w=w, nb=nb, t=t), w)
        hs, (r_s, w_s, s_s) = _layer(hs, p_sample[i].reshape(ns, PLE_DIM),
                                     functools.partial(_mixer_sample, w=w, cache_l=cache_nsa[i], win_l=state_win[i],
                                                       gla_l=state_gla[i], page_table=page_table), w)
        for lst, val in zip(outs, (r_p, w_p, s_p, r_s, w_s, s_s)):
            lst.append(val)
    return (hp.reshape(nb, t, D_MODEL), hs.reshape(ns, 1, D_MODEL), *[jnp.stack(o) for o in outs])
```

```python
import functools

import numpy as np
import jax
import jax.numpy as jnp
from jax import lax
from jax.experimental import pallas as pl
from jax.experimental.pallas import tpu as pltpu

F32 = jnp.float32
BF16 = jnp.bfloat16

D_MODEL = 1024
PLE_DIM = 256
D_FF = 2816
EPS = 1e-6
H_GLA = 4
DK_GLA = 64
DV_GLA = 128
GLA_RANK = 16
GLA_GATE_TEMP = 16.0
GLA_CHUNK = 64
H_NSA = 8
G_NSA = 2
R_NSA = H_NSA // G_NSA
D_NSA = 64
CMP_LEN = 32
CMP_STRIDE = 16
CMP_HIDDEN = 128
SEL_BLOCK = 64
N_SELECT = 16
WINDOW = 512
FORCE_SCORE = 1e4
NEG_INF = -1e30
M_INIT = -1e29
ATTN_SCALE = D_NSA ** -0.5
LOG2E = 1.4426950408889634
QK_SCALE = ATTN_SCALE * LOG2E
ROPE_THETA = 500000.0
ROPE_DIM = D_NSA // 4
ROPE_HALF = ROPE_DIM // 2
IN_SPLITS = (H_GLA * DK_GLA, H_GLA * DK_GLA, H_GLA * DV_GLA, H_GLA * DV_GLA, GLA_RANK,
             H_NSA * D_NSA, 6 * G_NSA * D_NSA, 3 * H_NSA)

LANES = 128
SUBLANES = 8
VMEM_LIMIT = 56 * 1024 * 1024

ROW_TILE = 512
FF_CHUNK = 256
Q_TILE = 256
K_TILE = 512
V_PAD_ROWS = 16
SEL_COLS = 256
PAGES_PER_STEP = 64
ROWS_PER_PAGE = 128
N_GLA_LEVELS = 6
GLA_CHUNKS_PER_STEP = 4
PAD_BLOCKS = 384


def _params(*sem):
    return pltpu.CompilerParams(dimension_semantics=sem, vmem_limit_bytes=VMEM_LIMIT)


def _rms(x, g):
    return x * lax.rsqrt(jnp.mean(x * x, axis=-1, keepdims=True) + EPS) * g


def _dot(a, b):
    return jnp.dot(a, b, preferred_element_type=F32)


def _dot_nt(a, b):
    return lax.dot_general(a, b, (((1,), (1,)), ((), ())), preferred_element_type=F32)


def _dot_tn(a, b):
    return lax.dot_general(a, b, (((0,), (0,)), ((), ())), preferred_element_type=F32)


def _split_bf16(x, n):
    parts = []
    r = x
    for _ in range(n):
        p = r.astype(BF16)
        parts.append(p)
        r = r - p.astype(F32)
    return parts


def _dot01_left(m01, x, n):
    out = None
    for p in _split_bf16(x, n):
        t = _dot(m01, p)
        out = t if out is None else out + t
    return out


def _dot01_right(x, m01, n):
    out = None
    for p in _split_bf16(x, n):
        t = _dot(p, m01)
        out = t if out is None else out + t
    return out


def _masked_softmax2_rows(s, valid):
    sm = jnp.where(valid, s, NEG_INF)
    m = jnp.max(sm, axis=0, keepdims=True)
    e = jnp.exp2(sm - m)
    return jnp.where(valid, e / jnp.sum(e, axis=0, keepdims=True), 0.0)


def _ffn_kernel(x_ref, gpre_ref, wg_ref, wu_ref, wo_ref, gpost_ref, o_ref):
    x = x_ref[...]
    xn = _rms(x, gpre_ref[...]).astype(BF16)
    acc = jnp.zeros(x.shape, F32)
    for c in range(D_FF // FF_CHUNK):
        sl = slice(c * FF_CHUNK, (c + 1) * FF_CHUNK)
        g = _dot(xn, wg_ref[:, sl])
        u = _dot(xn, wu_ref[:, sl])
        a = (jax.nn.silu(g) * u).astype(BF16)
        acc = acc + _dot(a, wo_ref[sl, :])
    o_ref[...] = x + 0.5 * _rms(acc, gpost_ref[...])


def _ffn(x, gpre, w_in, w_out, gpost):
    m = x.shape[0]
    tm = _row_tile(m)
    once = dict(pipeline_mode=pl.Buffered(1))
    const = lambda i: (0, 0)
    return pl.pallas_call(
        _ffn_kernel,
        grid=(m // tm,),
        in_specs=[
            pl.BlockSpec((tm, D_MODEL), lambda i: (i, 0)),
            pl.BlockSpec((1, D_MODEL), const, **once),
            pl.BlockSpec((D_MODEL, D_FF), const, **once),
            pl.BlockSpec((D_MODEL, D_FF), lambda i: (0, 1), **once),
            pl.BlockSpec((D_FF, D_MODEL), const, **once),
            pl.BlockSpec((1, D_MODEL), const, **once),
        ],
        out_specs=pl.BlockSpec((tm, D_MODEL), lambda i: (i, 0)),
        out_shape=jax.ShapeDtypeStruct((m, D_MODEL), F32),
        compiler_params=_params("parallel"),
        name="ffn",
    )(x, gpre, w_in, w_in, w_out, gpost)


N_GLA_COLS = 2 * H_GLA * DK_GLA + 2 * H_GLA * DV_GLA
N_KV_COLS = 6 * G_NSA * D_NSA
WN_COLS = N_GLA_COLS + 3 * LANES
N_Q_COLS = H_NSA * D_NSA
N_GATE_ROWS = 32
WT_ROWS = N_Q_COLS + N_KV_COLS + N_GATE_ROWS
N_CACHE_FEATS = 4 * G_NSA * D_NSA
N_WIN_FEATS = 2 * G_NSA * D_NSA


def _rope_rows(x, cos, sin):
    out = []
    for h in range(x.shape[0] // D_NSA):
        b = h * D_NSA
        x1 = x[b:b + ROPE_HALF]
        x2 = x[b + ROPE_HALF:b + ROPE_DIM]
        out += [x1 * cos - x2 * sin, x2 * cos + x1 * sin, x[b + ROPE_DIM:b + D_NSA]]
    return jnp.concatenate(out, axis=0)


def _rope_lanes(x, c, s1, s2):
    return x * c + pltpu.roll(x, LANES - ROPE_HALF, 1) * s1 + pltpu.roll(x, ROPE_HALF, 1) * s2


def _proj_kernel(h_ref, g_ref, wn_ref, wt_ref, wa2_ref, ba_ref, rc_ref, rs1_ref, rs2_ref, cos_ref, sin_ref,
                 qk_ref, v_ref, r_ref, la_ref, ksel_ref, kwin_ref,
                 qraw_ref, qrot_ref, gate_ref, nsat_ref, wint_ref):
    xn = _rms(h_ref[...], g_ref[...]).astype(BF16)
    nqk = 2 * H_GLA * DK_GLA
    nv = H_GLA * DV_GLA
    a_lr = _dot(xn, wn_ref[:, N_GLA_COLS:N_GLA_COLS + LANES])
    z = _dot(xn, wn_ref[:, 0:nqk])
    qk_ref[:, 0:nqk // 2] = z[:, 0:nqk // 2] * (DK_GLA ** -0.5)
    qk_ref[:, nqk // 2:nqk] = z[:, nqk // 2:nqk]
    v_ref[...] = _dot(xn, wn_ref[:, nqk:nqk + nv])
    r_ref[...] = _dot(xn, wn_ref[:, nqk + nv:N_GLA_COLS])
    k0 = N_GLA_COLS + LANES
    rc, rs1, rs2 = rc_ref[...], rs1_ref[...], rs2_ref[...]
    ksel_ref[...] = _rope_lanes(_dot(xn, wn_ref[:, k0:k0 + LANES]), rc, rs1, rs2)
    kwin_ref[...] = _rope_lanes(_dot(xn, wn_ref[:, k0 + LANES:k0 + 2 * LANES]), rc, rs1, rs2)
    zt = _dot_nt(wt_ref[...], xn)
    cos, sin = cos_ref[...], sin_ref[...]
    q = zt[0:N_Q_COLS] * QK_SCALE
    qraw_ref[...] = q.astype(BF16)
    qrot_ref[...] = _rope_rows(q, cos, sin).astype(BF16)
    kv = zt[N_Q_COLS:N_Q_COLS + N_KV_COLS]
    nsat_ref[0, 0:256, :] = kv[0:256]
    nsat_ref[0, 256:384, :] = _rope_rows(kv[256:384], cos, sin)
    nsat_ref[0, 384:512, :] = kv[384:512]
    wint_ref[0, 0:128, :] = _rope_rows(kv[512:640], cos, sin)
    wint_ref[0, 128:256, :] = kv[640:768]
    gate_ref[...] = jax.nn.sigmoid(zt[N_Q_COLS + N_KV_COLS:WT_ROWS])
    xa = _dot(a_lr.astype(BF16), wa2_ref[...]) + ba_ref[...]
    la_ref[...] = (jnp.minimum(xa, 0.0) - jnp.log1p(jnp.exp(-jnp.abs(xa)))) * (1.0 / GLA_GATE_TEMP)


def _proj(h, gain, wn, wt, wa2, ba, tabs, tm, tiles_per_seq):
    m = h.shape[0]
    rc, rs1, rs2, cos_t, sin_t = tabs
    const = lambda i: (0, 0)
    row = lambda i: (i, 0)
    col = lambda i: (0, i)
    tab_row = lambda i: (i % tiles_per_seq, 0)
    tab_col = lambda i: (0, i % tiles_per_seq)
    outs = [
        (2 * H_GLA * DK_GLA, F32), (H_GLA * DV_GLA, F32), (H_GLA * DV_GLA, F32), (H_GLA * DK_GLA, F32),
        (LANES, F32), (LANES, F32),
    ]
    outs_t = [(N_Q_COLS, BF16), (N_Q_COLS, BF16), (N_GATE_ROWS, F32)]
    nseq = m // (tm * tiles_per_seq)
    seq_len = tm * tiles_per_seq
    outs_seq = [N_CACHE_FEATS, N_WIN_FEATS]
    seq_map = lambda i: (i // tiles_per_seq, 0, i % tiles_per_seq)
    return pl.pallas_call(
        _proj_kernel,
        grid=(m // tm,),
        in_specs=[
            pl.BlockSpec((tm, D_MODEL), row),
            pl.BlockSpec((1, D_MODEL), const),
            pl.BlockSpec((D_MODEL, WN_COLS), const),
            pl.BlockSpec((WT_ROWS, D_MODEL), const),
            pl.BlockSpec((LANES, H_GLA * DK_GLA), const),
            pl.BlockSpec((1, H_GLA * DK_GLA), const),
            pl.BlockSpec((tm, LANES), tab_row),
            pl.BlockSpec((tm, LANES), tab_row),
            pl.BlockSpec((tm, LANES), tab_row),
            pl.BlockSpec((ROPE_HALF, tm), tab_col),
            pl.BlockSpec((ROPE_HALF, tm), tab_col),
        ],
        out_specs=[pl.BlockSpec((tm, n), row) for n, _ in outs] + [pl.BlockSpec((n, tm), col) for n, _ in outs_t]
        + [pl.BlockSpec((1, n, tm), seq_map) for n in outs_seq],
        out_shape=[jax.ShapeDtypeStruct((m, n), d) for n, d in outs]
        + [jax.ShapeDtypeStruct((n, m), d) for n, d in outs_t]
        + [jax.ShapeDtypeStruct((nseq, n, seq_len), F32) for n in outs_seq],
        compiler_params=_params("parallel"),
        name="mixer_proj",
    )(h, gain, wn, wt, wa2, ba, rc, rs1, rs2, cos_t, sin_t)


def _rope_tables(pos):
    pos = np.asarray(pos, np.float64)
    inv_freq = ROPE_THETA ** (-np.arange(ROPE_HALF, dtype=np.float64) * 2.0 / ROPE_DIM)
    ang = pos[:, None] * inv_freq[None, :]
    cos, sin = np.cos(ang), np.sin(ang)
    n = pos.shape[0]
    one = np.ones((n, D_NSA - ROPE_DIM))
    zero = np.zeros((n, D_NSA - ROPE_DIM))
    zh = np.zeros((n, ROPE_HALF))
    c = np.concatenate([cos, cos, one], axis=1)
    s1 = np.concatenate([-sin, zh, zero], axis=1)
    s2 = np.concatenate([zh, sin, zero], axis=1)
    dup = lambda t: np.concatenate([t, t], axis=1)
    return tuple(jnp.asarray(a, F32) for a in (dup(c), dup(s1), dup(s2), cos.T, sin.T))


def _gla_constants():
    c = GLA_CHUNK
    t = np.arange(c)
    low = (t[None, :] <= t[:, None]).astype(np.float32)
    mats, masks = [low], []
    for lev in range(1, N_GLA_LEVELS + 1):
        seg = (2 * c) >> lev
        half = seg // 2
        mid = (t // seg) * seg + half
        mats.append(low[mid])
        same = (t[:, None] // seg) == (t[None, :] // seg)
        masks.append((same & ((t[:, None] % seg) >= half) & ((t[None, :] % seg) < half)).astype(np.float32))
    masks.append(np.eye(c, dtype=np.float32))
    return np.concatenate(mats, axis=0), np.stack(masks)


def _gla_kernel(qk_ref, la_ref, v_ref, r_ref, gain_ref, big_ref, mask_ref, o_ref, sfin_ref, st_ref):
    c = pl.program_id(1)
    ch = GLA_CHUNK

    @pl.when(c == 0)
    def _():
        st_ref[...] = jnp.zeros(st_ref.shape, F32)

    lane = lax.broadcasted_iota(jnp.int32, (1, LANES), 1)
    head_mask = [jnp.where(lane < DK_GLA, 1.0, 0.0), jnp.where(lane >= DK_GLA, 1.0, 0.0)]
    big = big_ref[...]
    gain = gain_ref[...]
    nq = H_GLA * DK_GLA
    n_chunks = qk_ref.shape[0] // ch
    units = [(ci, p) for ci in range(n_chunks) for p in range(H_GLA // 2)]

    allbs = {}
    for ci, p in units:
        rows = slice(ci * ch, (ci + 1) * ch)
        allbs[ci, p] = _dot01_left(big, la_ref[rows, p * LANES:(p + 1) * LANES], 3)
    intra = {}
    for ci, p in units:
        rows = slice(ci * ch, (ci + 1) * ch)
        q = qk_ref[rows, p * LANES:(p + 1) * LANES]
        k = qk_ref[rows, nq + p * LANES:nq + (p + 1) * LANES]
        allb = allbs[ci, p]
        b = allb[0:ch]
        b_last = b[ch - 1:ch]
        attn = [jnp.zeros((ch, ch), F32), jnp.zeros((ch, ch), F32)]
        for lev in range(N_GLA_LEVELS + 1):
            if lev < N_GLA_LEVELS:
                ref = allb[(lev + 1) * ch:(lev + 2) * ch]
                ql = q * jnp.exp(jnp.minimum(b - ref, 0.0))
                kl = k * jnp.exp(jnp.minimum(ref - b, 0.0))
            else:
                ql, kl = q, k
            qq = jnp.concatenate([ql * head_mask[0], ql * head_mask[1]], axis=0).astype(BF16)
            s = _dot_nt(qq, kl.astype(BF16))
            mk = mask_ref[lev]
            attn[0] = attn[0] + mk * s[0:ch]
            attn[1] = attn[1] + mk * s[ch:2 * ch]
        q0 = q * jnp.exp(b)
        k_hat = (k * jnp.exp(b_last - b)).astype(BF16)
        for hh in range(2):
            hs = slice((2 * p + hh) * DV_GLA, (2 * p + hh + 1) * DV_GLA)
            vh = v_ref[rows, hs].astype(BF16)
            intra[ci, 2 * p + hh] = (_dot(attn[hh].astype(BF16), vh), (q0 * head_mask[hh]).astype(BF16),
                                     _dot_tn(vh, k_hat), jnp.exp(b_last))
    for ci in range(n_chunks):
        rows = slice(ci * ch, (ci + 1) * ch)
        for h in range(H_GLA):
            hs = slice(h * DV_GLA, (h + 1) * DV_GLA)
            o_intra, q0h, kv, decay = intra[ci, h]
            st = st_ref[h]
            o = o_intra + _dot_nt(q0h, st.astype(BF16))
            st_ref[h] = st * decay + kv
            o_ref[rows, hs] = _rms(o, gain) * jax.nn.silu(r_ref[rows, hs])

    @pl.when(c == pl.num_programs(1) - 1)
    def _():
        sfin_ref[0] = st_ref[...]


def _gla_prompt(qk, la, v, r, gain, nb, t):
    big, masks = _gla_constants()
    rows = GLA_CHUNK * GLA_CHUNKS_PER_STEP
    nc = t // rows
    row = lambda b, c: (b * nc + c, 0)
    const2 = lambda b, c: (0, 0)
    m = nb * t
    return pl.pallas_call(
        _gla_kernel,
        grid=(nb, nc),
        in_specs=[
            pl.BlockSpec((rows, 2 * H_GLA * DK_GLA), row),
            pl.BlockSpec((rows, H_GLA * DK_GLA), row),
            pl.BlockSpec((rows, H_GLA * DV_GLA), row),
            pl.BlockSpec((rows, H_GLA * DV_GLA), row),
            pl.BlockSpec((1, DV_GLA), const2),
            pl.BlockSpec(big.shape, const2),
            pl.BlockSpec(masks.shape, lambda b, c: (0, 0, 0)),
        ],
        out_specs=[
            pl.BlockSpec((rows, H_GLA * DV_GLA), row),
            pl.BlockSpec((1, H_GLA, DV_GLA, LANES), lambda b, c: (b, 0, 0, 0)),
        ],
        out_shape=[
            jax.ShapeDtypeStruct((m, H_GLA * DV_GLA), F32),
            jax.ShapeDtypeStruct((nb, H_GLA, DV_GLA, LANES), F32),
        ],
        scratch_shapes=[pltpu.VMEM((H_GLA, DV_GLA, LANES), F32)],
        compiler_params=_params("parallel", "arbitrary"),
        name="gla_scan",
    )(qk, la, v, r, gain, jnp.asarray(big, BF16), jnp.asarray(masks, F32))


def _gla_step_kernel(qk_t_ref, la_t_ref, v_ref, r_ref, s_ref, gain_ref, o_ref, sn_ref):
    nq = H_GLA * DK_GLA
    for b in range(s_ref.shape[0]):
        for h in range(H_GLA):
            ks = slice(h * DK_GLA, (h + 1) * DK_GLA)
            vs = slice(h * DV_GLA, (h + 1) * DV_GLA)
            q = qk_t_ref[ks, b:b + 1]
            k = qk_t_ref[nq + h * DK_GLA:nq + (h + 1) * DK_GLA, b:b + 1]
            s_new = jnp.exp(la_t_ref[ks, b:b + 1]) * s_ref[b, h] + k * v_ref[b:b + 1, vs]
            sn_ref[b, h] = s_new
            o = jnp.sum(q * s_new, axis=0, keepdims=True)
            o_ref[b:b + 1, vs] = _rms(o, gain_ref[...]) * jax.nn.silu(r_ref[b:b + 1, vs])


def _gla_step(qk_t, la_t, v, r, state, gain):
    nb = state.shape[0]
    full = lambda a: pl.BlockSpec(a.shape, functools.partial(lambda i, nd: (0,) * nd, nd=a.ndim))
    return pl.pallas_call(
        _gla_step_kernel,
        grid=(1,),
        in_specs=[full(qk_t), full(la_t), full(v), full(r), full(state), full(gain)],
        out_specs=[full(v), full(state)],
        out_shape=[jax.ShapeDtypeStruct((nb, H_GLA * DV_GLA), F32),
                   jax.ShapeDtypeStruct((nb, H_GLA, DK_GLA, DV_GLA), F32)],
        compiler_params=_params("arbitrary"),
        name="gla_step",
    )(qk_t, la_t, v, r, state, gain)


HALF_ROWS = CMP_STRIDE
CHUNKS_PER_PAGE = ROWS_PER_PAGE // HALF_ROWS


def _compress_kernel(tbl_ref, *refs, pps):
    del tbl_ref
    pages = refs[:pps]
    (wk_ref, wv_ref, posk_ref, w1k_ref, w2k_ref, posv_ref, w1v_ref, w2vt_ref,
     kc_ref, vct_ref, tk_ref, tv_ref, abk_ref, abv_ref) = refs[pps:]
    s = pl.program_id(1)
    step_chunks = pps * CHUNKS_PER_PAGE
    for i in range(pps):
        tk_ref[i] = pages[i][0, 0:LANES, :].T
        tv_ref[i] = pages[i][0, LANES:2 * LANES, :].T
    lpair = 2 * LANES
    row0 = pl.multiple_of(s * step_chunks, step_chunks)
    for t_ref, w_ref, ab_ref in ((tk_ref, wk_ref, abk_ref), (tv_ref, wv_ref, abv_ref)):
        acc = None
        for lp in range(HALF_ROWS // 2):
            cols = [jnp.concatenate([t_ref[i, pl.ds(l, CHUNKS_PER_PAGE, stride=HALF_ROWS), :] for i in range(pps)], axis=0)
                    for l in (2 * lp, 2 * lp + 1)]
            x = jnp.concatenate(cols, axis=1).astype(BF16)
            part = _dot(x, w_ref[lp * lpair:(lp + 1) * lpair, :])
            acc = part if acc is None else acc + part
        ab_ref[pl.ds(row0, step_chunks), :] = acc

    @pl.when(s == pl.num_programs(1) - 1)
    def _():
        nch = abk_ref.shape[0]
        pk = _dot(posk_ref[...], w1k_ref[...])[0:1]
        pv = _dot(posv_ref[...], w1v_ref[...])[0:1]
        abk = abk_ref[...]
        abv = abv_ref[...]
        kc = None
        for g in range(G_NSA):
            o = g * 2 * CMP_HIDDEN
            hk = jax.nn.gelu(abk[:, o:o + CMP_HIDDEN] + pltpu.roll(abk[:, o + CMP_HIDDEN:o + 2 * CMP_HIDDEN], nch - 1, 0) + pk)
            hv = jax.nn.gelu(abv[:, o:o + CMP_HIDDEN] + pltpu.roll(abv[:, o + CMP_HIDDEN:o + 2 * CMP_HIDDEN], nch - 1, 0) + pv)
            tk = _dot(hk.astype(BF16), w2k_ref[g])
            kc = tk if kc is None else kc + tk
            vct_ref[0, g * D_NSA:(g + 1) * D_NSA, :] = _dot_nt(w2vt_ref[...], hv.astype(BF16)).astype(BF16)
        kc_ref[0] = kc.astype(BF16)


def _compress(pages_t, table, w, paged):
    nb, npages = table.shape
    pps = min(PAGES_PER_STEP, npages)
    nsteps = npages // pps
    nch = npages * CHUNKS_PER_PAGE

    def page_map(b, s, tbl, i):
        return (tbl[b, s * pps + i], 0, 0) if paged else (b, 0, s * pps + i)

    page_specs = [pl.BlockSpec((1, 2 * LANES, ROWS_PER_PAGE), functools.partial(page_map, i=i)) for i in range(pps)]
    names = ("cmp_wk", "cmp_wv", "cmp_posk", "cmp_w1k", "cmp_w2k", "cmp_posv", "cmp_w1v", "cmp_w2vt")
    consts = [w[n] for n in names]
    const_specs = [pl.BlockSpec(c.shape, functools.partial(lambda b, s, tbl, nd: (0,) * nd, nd=c.ndim)) for c in consts]
    grid_spec = pltpu.PrefetchScalarGridSpec(
        num_scalar_prefetch=1,
        grid=(nb, nsteps),
        in_specs=page_specs + const_specs,
        out_specs=[pl.BlockSpec((1, nch, LANES), lambda b, s, tbl: (b, 0, 0)),
                   pl.BlockSpec((1, LANES, nch), lambda b, s, tbl: (b, 0, 0))],
        scratch_shapes=[pltpu.VMEM((pps, ROWS_PER_PAGE, LANES), F32), pltpu.VMEM((pps, ROWS_PER_PAGE, LANES), F32),
                        pltpu.VMEM((nch, 4 * CMP_HIDDEN), F32), pltpu.VMEM((nch, 4 * CMP_HIDDEN), F32)],
    )
    return pl.pallas_call(
        functools.partial(_compress_kernel, pps=pps),
        grid_spec=grid_spec,
        out_shape=[jax.ShapeDtypeStruct((nb, nch, LANES), BF16),
                   jax.ShapeDtypeStruct((nb, LANES, nch), BF16)],
        compiler_params=_params("parallel", "arbitrary"),
        name="compress",
    )(table, *([pages_t] * pps), *consts)


def _cmp_weights(pos, w1, w2):
    w = w1.reshape(2, HALF_ROWS, D_NSA, CMP_HIDDEN)
    z = jnp.zeros_like(w[0])
    blocks = []
    for g in range(G_NSA):
        cols = []
        for g2 in range(G_NSA):
            for half in range(2):
                cols.append(w[half] if g2 == g else z)
        blocks.append(jnp.concatenate(cols, axis=-1))
    wbig = jnp.stack(blocks, axis=1).reshape(HALF_ROWS * LANES, 4 * CMP_HIDDEN).astype(BF16)
    pos8 = jnp.broadcast_to(pos.reshape(1, CMP_LEN * D_NSA), (SUBLANES, CMP_LEN * D_NSA)).astype(BF16)
    z2 = jnp.zeros_like(w2)
    w2pad = jnp.stack([jnp.concatenate([w2, z2], axis=1), jnp.concatenate([z2, w2], axis=1)]).astype(BF16)
    return wbig, pos8, w1.astype(BF16), w2pad, w2.T.astype(BF16)


def _overlap_t(nc_pad, ns_pad, nc, ns):
    i = np.arange(nc_pad)[None, :] * CMP_STRIDE
    j = np.arange(ns_pad)[:, None] * SEL_BLOCK
    m = (i < j + SEL_BLOCK) & (i + CMP_LEN > j) & (np.arange(nc_pad)[None, :] < nc) & (np.arange(ns_pad)[:, None] < ns)
    return jnp.asarray(m.astype(np.float32), BF16)


def _rank_desc(x):
    nrow = x.shape[0]
    nblk = nrow // SUBLANES
    blocks = [x[v * SUBLANES:(v + 1) * SUBLANES] for v in range(nblk)]
    cnt = [jnp.zeros((SUBLANES, x.shape[1]), F32) for _ in range(nblk)]
    sub = lax.broadcasted_iota(jnp.int32, (SUBLANES, x.shape[1]), 0)
    for jp in range(nrow):
        row = x[jp:jp + 1]
        vb = jp // SUBLANES
        for v in range(nblk):
            ge = jnp.where(row >= blocks[v], 1.0, 0.0)
            gt = jnp.where(row > blocks[v], 1.0, 0.0)
            if v > vb:
                cnt[v] = cnt[v] + ge
            elif v < vb:
                cnt[v] = cnt[v] + gt
            else:
                cnt[v] = cnt[v] + jnp.where(sub > (jp % SUBLANES), ge, gt)
    return jnp.concatenate(cnt, axis=0)


def _nsa_prompt_kernel(qraw_ref, qrot_ref, gate_ref, kc_ref, vct_ref, ksel_ref, vselt_ref, kwin_ref, vwint_ref,
                       ovl_ref, o_ref, kaug_ref, kwb_ref, vsa_ref, vwa_ref):
    qb = pl.program_id(1)
    nlane = R_NSA * Q_TILE
    tpos = qb * Q_TILE + lax.broadcasted_iota(jnp.int32, (1, nlane), 1) % Q_TILE
    tpos1 = tpos[:, 0:Q_TILE]
    ncp = kc_ref.shape[1]
    nsb = ovl_ref.shape[0]
    seq = ksel_ref.shape[0]
    zeros_q = jnp.zeros((D_NSA, nlane), BF16)

    @pl.when(qb == 0)
    def _():
        ks = ksel_ref[...]
        lane = lax.broadcasted_iota(jnp.int32, (1, LANES), 1)
        blk = lax.broadcasted_iota(jnp.int32, (seq, 1), 0) // SEL_BLOCK
        kaug_ref[0] = jnp.where(lane < D_NSA, ks, jnp.where(lane - D_NSA == blk, 1.0, 0.0)).astype(BF16)
        kaug_ref[1] = jnp.where(lane >= D_NSA, ks, jnp.where(lane == blk, 1.0, 0.0)).astype(BF16)
        kwb_ref[...] = kwin_ref[...].astype(BF16)
        ones_row = jnp.where(lax.broadcasted_iota(jnp.int32, (V_PAD_ROWS, seq), 0) == 0, 1.0, 0.0)
        for g in range(G_NSA):
            gs = slice(g * D_NSA, (g + 1) * D_NSA)
            vsa_ref[g] = jnp.concatenate([vselt_ref[0, gs, :], ones_row], axis=0).astype(BF16)
            vwa_ref[g] = jnp.concatenate([vwint_ref[0, gs, :], ones_row], axis=0).astype(BF16)

    def group_q(ref, g):
        q = jnp.concatenate([ref[h * D_NSA:(h + 1) * D_NSA, :] for h in range(g * R_NSA, (g + 1) * R_NSA)], axis=1)
        return q, (jnp.concatenate([q, zeros_q], axis=0) if g == 0 else jnp.concatenate([zeros_q, q], axis=0))

    cmp_scores = [_dot(kc_ref[0], group_q(qraw_ref, g)[1]) for g in range(G_NSA)]
    o_cmps, qaugs, qpads = [], [], []
    for g in range(G_NSA):
        qrot, qpad = group_q(qrot_ref, g)
        gs = slice(g * D_NSA, (g + 1) * D_NSA)

        sc = cmp_scores[g]
        ci = lax.broadcasted_iota(jnp.int32, (ncp, 1), 0)
        p_cmp = _masked_softmax2_rows(sc, ci * CMP_STRIDE + (CMP_LEN - 1) <= tpos)
        o_cmp = _dot(vct_ref[0, gs, :], p_cmp.astype(BF16))
        psum = p_cmp[:, 0:Q_TILE]
        for r in range(1, R_NSA):
            psum = psum + p_cmp[:, r * Q_TILE:(r + 1) * Q_TILE]
        imp = _dot01_left(ovl_ref[...], psum, 2)
        bj = lax.broadcasted_iota(jnp.int32, (nsb, 1), 0)
        cur = tpos1 // SEL_BLOCK
        forced = (bj == 0) | (bj == cur) | (bj == cur - 1)
        imp = jnp.where(forced, FORCE_SCORE, jnp.where(bj * SEL_BLOCK <= tpos1, imp, -1.0))
        bias = jnp.where(_rank_desc(imp) < N_SELECT, 0.0, NEG_INF).astype(BF16)
        bias = jnp.concatenate([bias] * R_NSA, axis=1)
        if nsb < D_NSA:
            bias = jnp.concatenate([bias, jnp.zeros((D_NSA - nsb, nlane), BF16)], axis=0)
        o_cmps.append(o_cmp)
        qaugs.append(jnp.concatenate([qrot, bias], axis=0) if g == 0 else jnp.concatenate([bias, qrot], axis=0))
        qpads.append(qpad)

    ncol = nlane // SEL_COLS
    pairs = [(g, slice(j * SEL_COLS, (j + 1) * SEL_COLS)) for g in range(G_NSA) for j in range(ncol)]

    krow = lax.broadcasted_iota(jnp.int32, (K_TILE, 1), 0)

    def sel_step(c, carry, causal):
        start = pl.multiple_of(c * K_TILE, K_TILE)
        out = []
        scores = [_dot(kaug_ref[g, pl.ds(start, K_TILE), :], qaugs[g][:, cols]) for g, cols in pairs]
        for (g, cols), (m, acc), s in zip(pairs, carry, scores):
            if causal:
                s = jnp.where(start + krow <= tpos[:, cols], s, NEG_INF)
            m_new = jnp.maximum(m, jnp.max(s, axis=0, keepdims=True))
            e = jnp.exp2(s - m_new).astype(BF16)
            acc = jnp.exp2(m - m_new) * acc + _dot(vsa_ref[g, :, pl.ds(start, K_TILE)], e)
            out.append((m_new, acc))
        return tuple(out)

    n_full = (qb * Q_TILE) // K_TILE
    init = tuple((jnp.full((1, SEL_COLS), M_INIT, F32), jnp.zeros((D_NSA + V_PAD_ROWS, SEL_COLS), F32))
                 for _ in pairs)
    carry = lax.fori_loop(0, n_full, functools.partial(sel_step, causal=False), init)

    band = WINDOW + Q_TILE
    wstart = pl.multiple_of(jnp.maximum(qb * Q_TILE - WINDOW, 0), Q_TILE)
    win_scores = [_dot(kwb_ref[pl.ds(wstart, band), :], qpads[g]) for g in range(G_NSA)]

    carry = sel_step(n_full, carry, True)
    o_sels = []
    for g in range(G_NSA):
        acc_sel = jnp.concatenate([carry[g * ncol + j][1] for j in range(ncol)], axis=1)
        o_sels.append(acc_sel[0:D_NSA] / acc_sel[D_NSA:D_NSA + 1])

    diff = tpos - (wstart + lax.broadcasted_iota(jnp.int32, (band, 1), 0))
    old_edge = diff[0:Q_TILE] < WINDOW
    causal_ok = diff >= 0
    heads = []
    for g in range(G_NSA):
        sw = win_scores[g]
        sw = jnp.concatenate([jnp.where(old_edge, sw[0:Q_TILE], NEG_INF), sw[Q_TILE:]], axis=0)
        sw = jnp.where(causal_ok, sw, NEG_INF)
        e_win = jnp.exp2(sw - jnp.max(sw, axis=0, keepdims=True)).astype(BF16)
        acc_win = _dot(vwa_ref[g, :, pl.ds(wstart, band)], e_win)
        o_win = acc_win[0:D_NSA] / acc_win[D_NSA:D_NSA + 1]
        for r in range(R_NSA):
            h = g * R_NSA + r
            ls = slice(r * Q_TILE, (r + 1) * Q_TILE)
            heads.append(gate_ref[3 * h:3 * h + 1, :] * o_cmps[g][:, ls]
                         + gate_ref[3 * h + 1:3 * h + 2, :] * o_sels[g][:, ls]
                         + gate_ref[3 * h + 2:3 * h + 3, :] * o_win[:, ls])
    o_ref[...] = jnp.concatenate(heads, axis=0).T


def _nsa_prompt(qraw_t, qrot_t, gate_t, kc, vct, ksel, nsa_t, kwin, win_t, nb, t):
    nqb = t // Q_TILE
    ncp = kc.shape[1]
    ovl = _overlap_t(ncp, t // SEL_BLOCK, (t - CMP_LEN) // CMP_STRIDE + 1, t // SEL_BLOCK)
    qcol = lambda b, q: (0, b * nqb + q)
    m = nb * t
    return pl.pallas_call(
        _nsa_prompt_kernel,
        grid=(nb, nqb),
        in_specs=[
            pl.BlockSpec((N_Q_COLS, Q_TILE), qcol),
            pl.BlockSpec((N_Q_COLS, Q_TILE), qcol),
            pl.BlockSpec((N_GATE_ROWS, Q_TILE), qcol),
            pl.BlockSpec((1, ncp, LANES), lambda b, q: (b, 0, 0)),
            pl.BlockSpec((1, LANES, ncp), lambda b, q: (b, 0, 0)),
            pl.BlockSpec((t, LANES), lambda b, q: (b, 0)),
            pl.BlockSpec((1, LANES, t), lambda b, q: (b, 3, 0)),
            pl.BlockSpec((t, LANES), lambda b, q: (b, 0)),
            pl.BlockSpec((1, LANES, t), lambda b, q: (b, 1, 0)),
            pl.BlockSpec(ovl.shape, lambda b, q: (0, 0)),
        ],
        out_specs=pl.BlockSpec((Q_TILE, N_Q_COLS), lambda b, q: (b * nqb + q, 0)),
        out_shape=jax.ShapeDtypeStruct((m, N_Q_COLS), F32),
        scratch_shapes=[pltpu.VMEM((G_NSA, t, LANES), BF16), pltpu.VMEM((t, LANES), BF16),
                        pltpu.VMEM((G_NSA, D_NSA + V_PAD_ROWS, t), BF16),
                        pltpu.VMEM((G_NSA, D_NSA + V_PAD_ROWS, t), BF16)],
        compiler_params=_params("parallel", "arbitrary"),
        name="nsa_prompt",
    )(qraw_t, qrot_t, gate_t, kc, vct, ksel, nsa_t, kwin, win_t, ovl)


def _nsa_decode_select_kernel(qt_ref, kc_ref, vct_ref, ovl_ref, grp_ref, idx_ref, ocmp_ref, *, pos, n_blocks):
    ncp = kc_ref.shape[1]
    sc = _dot(kc_ref[0], qt_ref[0])
    ci = lax.broadcasted_iota(jnp.int32, (ncp, 1), 0)
    p = _masked_softmax2_rows(sc, ci * CMP_STRIDE + (CMP_LEN - 1) <= pos)
    ocmp_ref[0] = _dot(vct_ref[0], p.astype(BF16))
    psum = _dot01_right(p, grp_ref[...], 2)
    imp_t = _dot01_left(ovl_ref[...], psum, 2)
    imp_r = imp_t.T
    cur = pos // SEL_BLOCK

    def finish(v, j):
        forced = (j == 0) | (j == cur) | (j == cur - 1)
        v = jnp.where(forced, FORCE_SCORE, jnp.where(j * SEL_BLOCK <= pos, v, -1.0))
        return jnp.where(j < n_blocks, v, -3e38)

    jc = lax.broadcasted_iota(jnp.int32, (PAD_BLOCKS, 1), 0)
    jr = lax.broadcasted_iota(jnp.int32, (1, PAD_BLOCKS), 1)
    kk = lax.broadcasted_iota(jnp.int32, (N_SELECT, 1), 0).astype(F32)
    lane = lax.broadcasted_iota(jnp.int32, (N_SELECT, LANES), 1)
    out = jnp.zeros((N_SELECT, LANES), F32)
    for g in range(G_NSA):
        c0 = g * R_NSA
        col = finish(imp_t[:, c0:c0 + 1], jc)
        row = finish(imp_r[c0:c0 + 1, :], jr)
        beats = (col > row) | ((col == row) & (jc < jr))
        rank = jnp.sum(jnp.where(beats, 1.0, 0.0), axis=0, keepdims=True)
        hit = rank == kk
        idx = jnp.sum(jnp.where(hit, jr.astype(F32), 0.0), axis=1, keepdims=True)
        out = jnp.where(lane == g, idx, out)
    idx_ref[0] = out.astype(jnp.int32)


def _nsa_decode_select(q_t, kc, vct, pos, n_blocks):
    nb, ncp, _ = kc.shape
    nc = (pos + 1 - CMP_LEN) // CMP_STRIDE + 1
    ovl = _overlap_t(ncp, PAD_BLOCKS, nc, n_blocks)
    hh = np.arange(LANES)
    grp = ((hh[:, None] // R_NSA) == (hh[None, :] // R_NSA)) & (hh[:, None] < H_NSA) & (hh[None, :] < H_NSA)
    grp = jnp.asarray(grp.astype(np.float32), BF16)
    return pl.pallas_call(
        functools.partial(_nsa_decode_select_kernel, pos=pos, n_blocks=n_blocks),
        grid=(nb,),
        in_specs=[
            pl.BlockSpec((1, LANES, LANES), lambda b: (b, 0, 0)),
            pl.BlockSpec((1, ncp, LANES), lambda b: (b, 0, 0)),
            pl.BlockSpec((1, LANES, ncp), lambda b: (b, 0, 0)),
            pl.BlockSpec(ovl.shape, lambda b: (0, 0)),
            pl.BlockSpec(grp.shape, lambda b: (0, 0)),
        ],
        out_specs=[pl.BlockSpec((1, N_SELECT, LANES), lambda b: (b, 0, 0)),
                   pl.BlockSpec((1, LANES, LANES), lambda b: (b, 0, 0))],
        out_shape=[jax.ShapeDtypeStruct((nb, N_SELECT, LANES), jnp.int32),
                   jax.ShapeDtypeStruct((nb, LANES, LANES), F32)],
        compiler_params=_params("parallel"),
        name="nsa_decode_select",
    )(q_t, kc, vct, ovl, grp)


def _nsa_decode_attend_kernel(idx_ref, tbl_ref, *refs, n_past_blocks, win_buf):
    del tbl_ref
    nblk = G_NSA * N_SELECT
    blocks = refs[:nblk]
    q_ref, win_ref, nsa_new_ref, win_new_ref, gate_ref, ocmp_ref, o_ref = refs[nblk:]
    b = pl.program_id(0)
    q = q_ref[0]
    qf = q.astype(F32)
    rowg = lax.broadcasted_iota(jnp.int32, (H_NSA, 1), 0) // R_NSA
    ks_new = nsa_new_ref[0, :, 2 * LANES:3 * LANES]
    vs_new = nsa_new_ref[0, :, 3 * LANES:4 * LANES]
    s_new = jnp.sum(qf * ks_new, axis=1, keepdims=True)
    col = lax.broadcasted_iota(jnp.int32, (1, N_SELECT * ROWS_PER_PAGE), 1)
    colpage = col // ROWS_PER_PAGE
    colhalf = (col % ROWS_PER_PAGE) // SEL_BLOCK
    halves = ROWS_PER_PAGE // SEL_BLOCK
    o_sel = None
    for g in range(G_NSA):
        ks_t = jnp.concatenate([blocks[g * N_SELECT + k][0, 0:LANES, :] for k in range(N_SELECT)], axis=1)
        vs_t = jnp.concatenate([blocks[g * N_SELECT + k][0, LANES:2 * LANES, :] for k in range(N_SELECT)], axis=1)
        s = _dot(q, ks_t.astype(BF16))
        want = jnp.full(col.shape, -1, jnp.int32)
        for k in range(N_SELECT):
            j = idx_ref[b, g, k]
            half = jnp.where(j < n_past_blocks, j % halves, -1)
            want = jnp.where(colpage == k, half, want)
        valid = colhalf == want
        sm = jnp.where(valid, s, NEG_INF)
        m = jnp.maximum(jnp.max(sm, axis=1, keepdims=True), s_new)
        e = jnp.where(valid, jnp.exp2(sm - m), 0.0)
        e_new = jnp.exp2(s_new - m)
        l = jnp.sum(e, axis=1, keepdims=True) + e_new
        og = (_dot_nt(e.astype(BF16), vs_t.astype(BF16)) + e_new * vs_new) / l
        o_sel = og if o_sel is None else jnp.where(rowg == g, og, o_sel)

    kw_t = win_ref[0, 0:LANES, :]
    vw_t = win_ref[0, LANES:2 * LANES, :]
    kw_new = win_new_ref[0, :, 0:LANES]
    vw_new = win_new_ref[0, :, LANES:2 * LANES]
    sw = _dot(q, kw_t.astype(BF16))
    sw_new = jnp.sum(qf * kw_new, axis=1, keepdims=True)
    diff = win_buf - lax.broadcasted_iota(jnp.int32, (1, win_buf), 1)
    validw = (diff >= 0) & (diff < WINDOW)
    smw = jnp.where(validw, sw, NEG_INF)
    mw = jnp.maximum(jnp.max(smw, axis=1, keepdims=True), sw_new)
    ew = jnp.where(validw, jnp.exp2(smw - mw), 0.0)
    ew_new = jnp.exp2(sw_new - mw)
    lw = jnp.sum(ew, axis=1, keepdims=True) + ew_new
    o_win = (_dot_nt(ew.astype(BF16), vw_t.astype(BF16)) + ew_new * vw_new) / lw

    gt = gate_ref[0]
    o_ref[0] = gt[:, 0:1] * ocmp_ref[0] + gt[:, 1:2] * o_sel + gt[:, 2:3] * o_win


def _nsa_decode_attend(idx, table, cache_t, q2, win_t, nsa_new, win_new, gates, ocmp, n_past_blocks):
    nb = q2.shape[0]
    win_buf = win_t.shape[2]
    halves = ROWS_PER_PAGE // SEL_BLOCK

    def blk_map(b, idx_ref, tbl_ref, g, k):
        j = jnp.minimum(idx_ref[b, g, k], n_past_blocks - 1)
        return (tbl_ref[b, j // halves], 1, 0)

    blk_specs = [pl.BlockSpec((1, 2 * LANES, ROWS_PER_PAGE), functools.partial(blk_map, g=g, k=k))
                 for g in range(G_NSA) for k in range(N_SELECT)]
    per_b = lambda shape: pl.BlockSpec((1,) + shape, lambda b, i, t: (b, 0, 0))
    grid_spec = pltpu.PrefetchScalarGridSpec(
        num_scalar_prefetch=2,
        grid=(nb,),
        in_specs=blk_specs + [per_b((H_NSA, LANES)), per_b((2 * LANES, win_buf)), per_b((1, 4 * LANES)),
                              per_b((1, 2 * LANES)), per_b((H_NSA, LANES)), per_b((H_NSA, LANES))],
        out_specs=per_b((H_NSA, LANES)),
    )
    return pl.pallas_call(
        functools.partial(_nsa_decode_attend_kernel, n_past_blocks=n_past_blocks, win_buf=win_buf),
        grid_spec=grid_spec,
        out_shape=jax.ShapeDtypeStruct((nb, H_NSA, LANES), F32),
        compiler_params=_params("parallel"),
        name="nsa_decode_attend",
    )(idx, table, *([cache_t] * (G_NSA * N_SELECT)), q2, win_t, nsa_new, win_new, gates, ocmp)


def _post_mixer_kernel(h_ref, a_ref, b_ref, p_ref, wa_ref, wb_ref, gm_ref,
                       gpre_ref, wg_ref, wu_ref, wo_ref, gpost_ref,
                       ppre_ref, pg_ref, pp_ref, ppost_ref, o_ref):
    y = _dot(a_ref[...].astype(BF16), wa_ref[...]) + _dot(b_ref[...].astype(BF16), wb_ref[...])
    h = h_ref[...] + _rms(y, gm_ref[...])
    xn = _rms(h, gpre_ref[...]).astype(BF16)
    acc = jnp.zeros(h.shape, F32)
    for c in range(D_FF // FF_CHUNK):
        sl = slice(c * FF_CHUNK, (c + 1) * FF_CHUNK)
        g = _dot(xn, wg_ref[:, sl])
        u = _dot(xn, wu_ref[:, sl])
        acc = acc + _dot((jax.nn.silu(g) * u).astype(BF16), wo_ref[sl, :])
    h = h + 0.5 * _rms(acc, gpost_ref[...])
    gate = jax.nn.sigmoid(_dot(_rms(h, ppre_ref[...]).astype(BF16), pg_ref[...]))
    o_ref[...] = h + _rms(gate * _dot(p_ref[...].astype(BF16), pp_ref[...]), ppost_ref[...])


def _post_mixer(h, a, b, p, w, tm):
    m = h.shape[0]
    row = lambda i: (i, 0)
    once = dict(pipeline_mode=pl.Buffered(1))
    const = lambda shape, idx=(0, 0): pl.BlockSpec(shape, lambda i: idx, **once)
    vec = const((1, D_MODEL))
    return pl.pallas_call(
        _post_mixer_kernel,
        grid=(m // tm,),
        in_specs=[pl.BlockSpec((tm, D_MODEL), row), pl.BlockSpec((tm, a.shape[1]), row),
                  pl.BlockSpec((tm, b.shape[1]), row), pl.BlockSpec((tm, PLE_DIM), row),
                  const(w["wo_gla"].shape), const(w["wo_nsa"].shape), vec,
                  vec, const((D_MODEL, D_FF)), const((D_MODEL, D_FF), (0, 1)), const((D_FF, D_MODEL)), vec,
                  vec, const(w["ple_gate"].shape), const(w["ple_proj"].shape), vec],
        out_specs=pl.BlockSpec((tm, D_MODEL), row),
        out_shape=jax.ShapeDtypeStruct((m, D_MODEL), F32),
        compiler_params=_params("parallel"),
        name="post_mixer",
    )(h, a, b, p, w["wo_gla"], w["wo_nsa"], w["m_post"],
      w["f2_pre"], w["f2_in"], w["f2_in"], w["f2_out"], w["f2_post"],
      w["ple_pre"], w["ple_gate"], w["ple_proj"], w["ple_post"])


def _split_in_cols(w):
    outs, off = [], 0
    for n in IN_SPLITS:
        outs.append(w[:, off:off + n])
        off += n
    return outs


def _prep_mixer_weights(w_in, w_a2, b_a):
    q_g, k_g, v_g, r_g, a_lr, q_n, kv_n, gate_n = _split_in_cols(w_in)
    pad_cols = lambda w, n: jnp.pad(w, ((0, 0), (0, n - w.shape[1])))
    gd = G_NSA * D_NSA
    k_sel = kv_n[:, 2 * gd:3 * gd]
    k_win = kv_n[:, 4 * gd:5 * gd]
    wn = jnp.concatenate([q_g, k_g, v_g, r_g, pad_cols(a_lr, LANES), k_sel, k_win], axis=1).astype(BF16)
    wt = jnp.concatenate([q_n, kv_n, pad_cols(gate_n, N_GATE_ROWS)], axis=1).T.astype(BF16)
    wa2 = jnp.pad(w_a2, ((0, LANES - GLA_RANK), (0, 0))).astype(BF16)
    return wn, wt, wa2, b_a.reshape(1, -1)


def _row_tile(m):
    return ROW_TILE if m % ROW_TILE == 0 else m


def _layer(x2, p2, mixer, w):
    tm = _row_tile(x2.shape[0])
    h = _ffn(x2, w["f1_pre"], w["f1_in"], w["f1_out"], w["f1_post"])
    o_gla, o_nsa, extras = mixer(h)
    h = _post_mixer(h, o_gla, o_nsa, p2, w, tm)
    return h, extras


def _mixer_prompt(h, w, nb, t):
    tm = _row_tile(h.shape[0])
    tabs = _rope_tables(np.arange(t))
    (qk, v, r, la, ksel, kwin, qraw_t, qrot_t, gate_t, nsa_t, win_t) = _proj(
        h, w["m_pre"], w["wn"], w["wt"], w["wa2"], w["ba"], tabs, tm, t // tm)
    o_gla, s_fin = _gla_prompt(qk, la, v, r, w["gla_gain"], nb, t)
    table = jnp.zeros((nb, t // ROWS_PER_PAGE), jnp.int32)
    kc, vct = _compress(nsa_t, table, w, False)
    o_nsa = _nsa_prompt(qraw_t, qrot_t, gate_t, kc, vct, ksel, nsa_t, kwin, win_t, nb, t)
    s_t = s_fin.reshape(nb, H_GLA, DV_GLA, 2, DK_GLA)
    s_own = jnp.stack([s_t[:, hh, :, hh % 2, :] for hh in range(H_GLA)], axis=1)
    gla_state = jnp.swapaxes(s_own, -1, -2)
    keep = min(WINDOW, t)
    rows_first = lambda a, n: jnp.transpose(a.reshape(nb, n, G_NSA, D_NSA, a.shape[-1]), (0, 4, 1, 2, 3))
    nsa_rows = rows_first(nsa_t, 4)
    win_rows = rows_first(win_t[:, :, t - keep:], 2)
    return o_gla, o_nsa, (nsa_rows, win_rows, gla_state)


def _mixer_sample(h, w, cache_l, win_l, gla_l, page_table):
    nb = h.shape[0]
    n_pages = page_table.shape[1]
    past_len = n_pages * cache_l.shape[1]
    pos = past_len
    tabs = _rope_tables(np.full((nb,), pos))
    (qk, v, r, la, _, _, qraw_t, qrot_t, gate_t, nsa_new_t, win_new_t) = _proj(
        h, w["m_pre"], w["wn"], w["wt"], w["wa2"], w["ba"], tabs, nb, 1)
    nsa = nsa_new_t[0].T
    win = win_new_t[0].T

    o_gla, gla_state = _gla_step(qk.T, la.T, v, r, gla_l.astype(F32), w["gla_gain"])

    cache_t = jnp.transpose(cache_l.reshape(cache_l.shape[0], ROWS_PER_PAGE, 4 * LANES), (0, 2, 1))
    kc, vct = _compress(cache_t, page_table, w, True)

    hg = (jnp.arange(H_NSA) // R_NSA)[None, :, None]

    def group_pad(q_t):
        q8 = q_t.T.reshape(nb, H_NSA, D_NSA)
        return jnp.concatenate([jnp.where(hg == 0, q8, 0), jnp.where(hg == 1, q8, 0)], axis=-1)

    q2_raw = group_pad(qraw_t)
    q2_rot = group_pad(qrot_t)
    q2_raw_t = jnp.pad(jnp.swapaxes(q2_raw, 1, 2), ((0, 0), (0, 0), (0, LANES - H_NSA)))
    n_blocks = -(-(past_len + 1) // SEL_BLOCK)
    idx_pad, ocmp_t = _nsa_decode_select(q2_raw_t, kc, vct, pos, n_blocks)
    idx = jnp.stack([idx_pad[:, :, g] for g in range(G_NSA)], axis=1)
    ocmp = jnp.swapaxes(ocmp_t, 1, 2)[:, :H_NSA, :]
    gates = jnp.pad(gate_t[:3 * H_NSA].T.reshape(nb, H_NSA, 3), ((0, 0), (0, 0), (0, LANES - 3)))
    wb = win_l.shape[1]
    win_buf_t = jnp.transpose(win_l.reshape(nb, wb, 2 * LANES), (0, 2, 1))
    o8 = _nsa_decode_attend(idx, page_table, cache_t, q2_rot, win_buf_t,
                            nsa.reshape(nb, 1, 4 * LANES), win.reshape(nb, 1, 2 * LANES), gates, ocmp,
                            past_len // SEL_BLOCK)
    o8 = o8.reshape(nb, H_NSA, G_NSA, D_NSA)
    o_nsa = jnp.concatenate([o8[:, :R_NSA, 0], o8[:, R_NSA:, 1]], axis=1).reshape(nb, H_NSA * D_NSA)

    nsa_rows = nsa.reshape(nb, 1, 4, G_NSA, D_NSA)
    win_new = win.reshape(nb, 1, 2, G_NSA, D_NSA)
    kw = jnp.concatenate([win_l, win_new.astype(win_l.dtype)], axis=1)
    keep = min(WINDOW, wb + 1)
    return o_gla, o_nsa, (nsa_rows, kw[:, wb + 1 - keep:], gla_state.astype(gla_l.dtype))


def kernel(x_prompt, x_sample, cache_nsa, state_win, state_gla, page_table, p_prompt, p_sample,
           ffn1_norm_pre, ffn1_norm_post, ffn1_w_in, ffn1_w_out,
           mix_norm_pre, mix_norm_post, w_mix_in, w_gla_a2, b_gla_a, gla_out_norm,
           cmp_pos_k, w_cmp_k1, w_cmp_k2, cmp_pos_v, w_cmp_v1, w_cmp_v2, w_mix_out,
           ffn2_norm_pre, ffn2_norm_post, ffn2_w_in, ffn2_w_out,
           ple_norm_pre, ple_w_gate, ple_w_proj, ple_norm_post):
    nb, t, _ = x_prompt.shape
    ns = x_sample.shape[0]
    depth = ffn1_w_in.shape[0]
    hp = x_prompt.reshape(nb * t, D_MODEL)
    hs = x_sample.reshape(ns, D_MODEL)
    outs = [[] for _ in range(6)]
    for i in range(depth):
        wn, wt, wa2, ba = _prep_mixer_weights(w_mix_in[i], w_gla_a2[i], b_gla_a[i])
        cmp_wk, cmp_posk, cmp_w1k, cmp_w2k, _ = _cmp_weights(cmp_pos_k[i], w_cmp_k1[i], w_cmp_k2[i])
        cmp_wv, cmp_posv, cmp_w1v, _, cmp_w2vt = _cmp_weights(cmp_pos_v[i], w_cmp_v1[i], w_cmp_v2[i])
        gla_w = H_GLA * DV_GLA
        w = dict(
            f1_pre=ffn1_norm_pre[i][None], f1_post=ffn1_norm_post[i][None],
            f1_in=ffn1_w_in[i].astype(BF16), f1_out=ffn1_w_out[i].astype(BF16),
            m_pre=mix_norm_pre[i][None], m_post=mix_norm_post[i][None],
            wn=wn, wt=wt, wa2=wa2, ba=ba, gla_gain=gla_out_norm[i][None],
            cmp_wk=cmp_wk, cmp_posk=cmp_posk, cmp_w1k=cmp_w1k, cmp_w2k=cmp_w2k,
            cmp_wv=cmp_wv, cmp_posv=cmp_posv, cmp_w1v=cmp_w1v, cmp_w2vt=cmp_w2vt,
            wo_gla=w_mix_out[i][:gla_w].astype(BF16), wo_nsa=w_mix_out[i][gla_w:].astype(BF16),
            f2_pre=ffn2_norm_pre[i][None], f2_post=ffn2_norm_post[i][None],
            f2_in=ffn2_w_in[i].astype(BF16), f2_out=ffn2_w_out[i].astype(BF16),
            ple_pre=ple_norm_pre[i][None], ple_post=ple_norm_post[i][None],
            ple_gate=ple_w_gate[i].astype(BF16), ple_proj=ple_w_proj[i].astype(BF16),
        )
        hp, (r_p, w_p, s_p) = _layer(hp, p_prompt[i].reshape(nb * t, PLE_DIM),
                                     functools.partial(_mixer_prompt, w=w, nb=nb, t=t), w)
        hs, (r_s, w_s, s_s) = _layer(hs, p_sample[i].reshape(ns, PLE_DIM),
                                     functools.partial(_mixer_sample, w=w, cache_l=cache_nsa[i], win_l=state_win[i],
                                                       gla_l=state_gla[i], page_table=page_table), w)
        for lst, val in zip(outs, (r_p, w_p, s_p, r_s, w_s, s_s)):
            lst.append(val)
    return (hp.reshape(nb, t, D_MODEL), hs.reshape(ns, 1, D_MODEL), *[jnp.stack(o) for o in outs])
```

```python
import functools

import numpy as np
import jax
import jax.numpy as jnp
from jax import lax
from jax.experimental import pallas as pl
from jax.experimental.pallas import tpu as pltpu

F32 = jnp.float32
BF16 = jnp.bfloat16

D_MODEL = 1024
PLE_DIM = 256
D_FF = 2816
EPS = 1e-6
H_GLA = 4
DK_GLA = 64
DV_GLA = 128
GLA_RANK = 16
GLA_GATE_TEMP = 16.0
GLA_CHUNK = 64
H_NSA = 8
G_NSA = 2
R_NSA = H_NSA // G_NSA
D_NSA = 64
CMP_LEN = 32
CMP_STRIDE = 16
CMP_HIDDEN = 128
SEL_BLOCK = 64
N_SELECT = 16
WINDOW = 512
FORCE_SCORE = 1e4
NEG_INF = -1e30
M_INIT = -1e29
ATTN_SCALE = D_NSA ** -0.5
LOG2E = 1.4426950408889634
QK_SCALE = ATTN_SCALE * LOG2E
ROPE_THETA = 500000.0
ROPE_DIM = D_NSA // 4
ROPE_HALF = ROPE_DIM // 2
IN_SPLITS = (H_GLA * DK_GLA, H_GLA * DK_GLA, H_GLA * DV_GLA, H_GLA * DV_GLA, GLA_RANK,
             H_NSA * D_NSA, 6 * G_NSA * D_NSA, 3 * H_NSA)

LANES = 128
SUBLANES = 8
VMEM_LIMIT = 56 * 1024 * 1024

ROW_TILE = 512
FF_CHUNK = 256
Q_TILE = 256
K_TILE = 512
V_PAD_ROWS = 16
SEL_COLS = 256
PAGES_PER_STEP = 64
ROWS_PER_PAGE = 128
N_GLA_LEVELS = 6
GLA_CHUNKS_PER_STEP = 4
PAD_BLOCKS = 384


def _params(*sem):
    return pltpu.CompilerParams(dimension_semantics=sem, vmem_limit_bytes=VMEM_LIMIT)


def _rms(x, g):
    return x * lax.rsqrt(jnp.mean(x * x, axis=-1, keepdims=True) + EPS) * g


def _dot(a, b):
    return jnp.dot(a, b, preferred_element_type=F32)


def _dot_nt(a, b):
    return lax.dot_general(a, b, (((1,), (1,)), ((), ())), preferred_element_type=F32)


def _dot_tn(a, b):
    return lax.dot_general(a, b, (((0,), (0,)), ((), ())), preferred_element_type=F32)


def _split_bf16(x, n):
    parts = []
    r = x
    for _ in range(n):
        p = r.astype(BF16)
        parts.append(p)
        r = r - p.astype(F32)
    return parts


def _dot01_left(m01, x, n):
    out = None
    for p in _split_bf16(x, n):
        t = _dot(m01, p)
        out = t if out is None else out + t
    return out


def _dot01_right(x, m01, n):
    out = None
    for p in _split_bf16(x, n):
        t = _dot(p, m01)
        out = t if out is None else out + t
    return out


def _masked_softmax2_rows(s, valid):
    sm = jnp.where(valid, s, NEG_INF)
    m = jnp.max(sm, axis=0, keepdims=True)
    e = jnp.exp2(sm - m)
    return jnp.where(valid, e / jnp.sum(e, axis=0, keepdims=True), 0.0)


def _ffn_kernel(x_ref, gpre_ref, wg_ref, wu_ref, wo_ref, gpost_ref, o_ref):
    x = x_ref[...]
    xn = _rms(x, gpre_ref[...]).astype(BF16)
    acc = jnp.zeros(x.shape, F32)
    for c in range(D_FF // FF_CHUNK):
        sl = slice(c * FF_CHUNK, (c + 1) * FF_CHUNK)
        g = _dot(xn, wg_ref[:, sl])
        u = _dot(xn, wu_ref[:, sl])
        a = (jax.nn.silu(g) * u).astype(BF16)
        acc = acc + _dot(a, wo_ref[sl, :])
    o_ref[...] = x + 0.5 * _rms(acc, gpost_ref[...])


def _ffn(x, gpre, w_in, w_out, gpost):
    m = x.shape[0]
    tm = _row_tile(m)
    once = dict(pipeline_mode=pl.Buffered(1))
    const = lambda i: (0, 0)
    return pl.pallas_call(
        _ffn_kernel,
        grid=(m // tm,),
        in_specs=[
            pl.BlockSpec((tm, D_MODEL), lambda i: (i, 0)),
            pl.BlockSpec((1, D_MODEL), const, **once),
            pl.BlockSpec((D_MODEL, D_FF), const, **once),
            pl.BlockSpec((D_MODEL, D_FF), lambda i: (0, 1), **once),
            pl.BlockSpec((D_FF, D_MODEL), const, **once),
            pl.BlockSpec((1, D_MODEL), const, **once),
        ],
        out_specs=pl.BlockSpec((tm, D_MODEL), lambda i: (i, 0)),
        out_shape=jax.ShapeDtypeStruct((m, D_MODEL), F32),
        compiler_params=_params("parallel"),
        name="ffn",
    )(x, gpre, w_in, w_in, w_out, gpost)


N_GLA_COLS = 2 * H_GLA * DK_GLA + 2 * H_GLA * DV_GLA
N_KV_COLS = 6 * G_NSA * D_NSA
WN_COLS = N_GLA_COLS + 3 * LANES
N_Q_COLS = H_NSA * D_NSA
N_GATE_ROWS = 32
WT_ROWS = N_Q_COLS + N_KV_COLS + N_GATE_ROWS
N_CACHE_FEATS = 4 * G_NSA * D_NSA
N_WIN_FEATS = 2 * G_NSA * D_NSA


def _rope_rows(x, cos, sin):
    out = []
    for h in range(x.shape[0] // D_NSA):
        b = h * D_NSA
        x1 = x[b:b + ROPE_HALF]
        x2 = x[b + ROPE_HALF:b + ROPE_DIM]
        out += [x1 * cos - x2 * sin, x2 * cos + x1 * sin, x[b + ROPE_DIM:b + D_NSA]]
    return jnp.concatenate(out, axis=0)


def _rope_lanes(x, c, s1, s2):
    return x * c + pltpu.roll(x, LANES - ROPE_HALF, 1) * s1 + pltpu.roll(x, ROPE_HALF, 1) * s2


def _proj_kernel(h_ref, g_ref, wn_ref, wt_ref, wa2_ref, ba_ref, rc_ref, rs1_ref, rs2_ref, cos_ref, sin_ref,
                 qk_ref, v_ref, r_ref, la_ref, ksel_ref, kwin_ref,
                 qraw_ref, qrot_ref, gate_ref, nsat_ref, wint_ref):
    xn = _rms(h_ref[...], g_ref[...]).astype(BF16)
    nqk = 2 * H_GLA * DK_GLA
    nv = H_GLA * DV_GLA
    a_lr = _dot(xn, wn_ref[:, N_GLA_COLS:N_GLA_COLS + LANES])
    z = _dot(xn, wn_ref[:, 0:nqk])
    qk_ref[:, 0:nqk // 2] = z[:, 0:nqk // 2] * (DK_GLA ** -0.5)
    qk_ref[:, nqk // 2:nqk] = z[:, nqk // 2:nqk]
    v_ref[...] = _dot(xn, wn_ref[:, nqk:nqk + nv])
    r_ref[...] = _dot(xn, wn_ref[:, nqk + nv:N_GLA_COLS])
    k0 = N_GLA_COLS + LANES
    rc, rs1, rs2 = rc_ref[...], rs1_ref[...], rs2_ref[...]
    ksel_ref[...] = _rope_lanes(_dot(xn, wn_ref[:, k0:k0 + LANES]), rc, rs1, rs2)
    kwin_ref[...] = _rope_lanes(_dot(xn, wn_ref[:, k0 + LANES:k0 + 2 * LANES]), rc, rs1, rs2)
    zt = _dot_nt(wt_ref[...], xn)
    cos, sin = cos_ref[...], sin_ref[...]
    q = zt[0:N_Q_COLS] * QK_SCALE
    qraw_ref[...] = q.astype(BF16)
    qrot_ref[...] = _rope_rows(q, cos, sin).astype(BF16)
    kv = zt[N_Q_COLS:N_Q_COLS + N_KV_COLS]
    nsat_ref[0, 0:256, :] = kv[0:256]
    nsat_ref[0, 256:384, :] = _rope_rows(kv[256:384], cos, sin)
    nsat_ref[0, 384:512, :] = kv[384:512]
    wint_ref[0, 0:128, :] = _rope_rows(kv[512:640], cos, sin)
    wint_ref[0, 128:256, :] = kv[640:768]
    gate_ref[...] = jax.nn.sigmoid(zt[N_Q_COLS + N_KV_COLS:WT_ROWS])
    xa = _dot(a_lr.astype(BF16), wa2_ref[...]) + ba_ref[...]
    la_ref[...] = (jnp.minimum(xa, 0.0) - jnp.log1p(jnp.exp(-jnp.abs(xa)))) * (1.0 / GLA_GATE_TEMP)


def _proj(h, gain, wn, wt, wa2, ba, tabs, tm, tiles_per_seq):
    m = h.shape[0]
    rc, rs1, rs2, cos_t, sin_t = tabs
    const = lambda i: (0, 0)
    row = lambda i: (i, 0)
    col = lambda i: (0, i)
    tab_row = lambda i: (i % tiles_per_seq, 0)
    tab_col = lambda i: (0, i % tiles_per_seq)
    outs = [
        (2 * H_GLA * DK_GLA, F32), (H_GLA * DV_GLA, F32), (H_GLA * DV_GLA, F32), (H_GLA * DK_GLA, F32),
        (LANES, F32), (LANES, F32),
    ]
    outs_t = [(N_Q_COLS, BF16), (N_Q_COLS, BF16), (N_GATE_ROWS, F32)]
    nseq = m // (tm * tiles_per_seq)
    seq_len = tm * tiles_per_seq
    outs_seq = [N_CACHE_FEATS, N_WIN_FEATS]
    seq_map = lambda i: (i // tiles_per_seq, 0, i % tiles_per_seq)
    return pl.pallas_call(
        _proj_kernel,
        grid=(m // tm,),
        in_specs=[
            pl.BlockSpec((tm, D_MODEL), row),
            pl.BlockSpec((1, D_MODEL), const),
            pl.BlockSpec((D_MODEL, WN_COLS), const),
            pl.BlockSpec((WT_ROWS, D_MODEL), const),
            pl.BlockSpec((LANES, H_GLA * DK_GLA), const),
            pl.BlockSpec((1, H_GLA * DK_GLA), const),
            pl.BlockSpec((tm, LANES), tab_row),
            pl.BlockSpec((tm, LANES), tab_row),
            pl.BlockSpec((tm, LANES), tab_row),
            pl.BlockSpec((ROPE_HALF, tm), tab_col),
            pl.BlockSpec((ROPE_HALF, tm), tab_col),
        ],
        out_specs=[pl.BlockSpec((tm, n), row) for n, _ in outs] + [pl.BlockSpec((n, tm), col) for n, _ in outs_t]
        + [pl.BlockSpec((1, n, tm), seq_map) for n in outs_seq],
        out_shape=[jax.ShapeDtypeStruct((m, n), d) for n, d in outs]
        + [jax.ShapeDtypeStruct((n, m), d) for n, d in outs_t]
        + [jax.ShapeDtypeStruct((nseq, n, seq_len), F32) for n in outs_seq],
        compiler_params=_params("parallel"),
        name="mixer_proj",
    )(h, gain, wn, wt, wa2, ba, rc, rs1, rs2, cos_t, sin_t)


def _rope_tables(pos):
    pos = np.asarray(pos, np.float64)
    inv_freq = ROPE_THETA ** (-np.arange(ROPE_HALF, dtype=np.float64) * 2.0 / ROPE_DIM)
    ang = pos[:, None] * inv_freq[None, :]
    cos, sin = np.cos(ang), np.sin(ang)
    n = pos.shape[0]
    one = np.ones((n, D_NSA - ROPE_DIM))
    zero = np.zeros((n, D_NSA - ROPE_DIM))
    zh = np.zeros((n, ROPE_HALF))
    c = np.concatenate([cos, cos, one], axis=1)
    s1 = np.concatenate([-sin, zh, zero], axis=1)
    s2 = np.concatenate([zh, sin, zero], axis=1)
    dup = lambda t: np.concatenate([t, t], axis=1)
    return tuple(jnp.asarray(a, F32) for a in (dup(c), dup(s1), dup(s2), cos.T, sin.T))


def _gla_constants():
    c = GLA_CHUNK
    t = np.arange(c)
    low = (t[None, :] <= t[:, None]).astype(np.float32)
    mats, masks = [low], []
    for lev in range(1, N_GLA_LEVELS + 1):
        seg = (2 * c) >> lev
        half = seg // 2
        mid = (t // seg) * seg + half
        mats.append(low[mid])
        same = (t[:, None] // seg) == (t[None, :] // seg)
        masks.append((same & ((t[:, None] % seg) >= half) & ((t[None, :] % seg) < half)).astype(np.float32))
    masks.append(np.eye(c, dtype=np.float32))
    return np.concatenate(mats, axis=0), np.stack(masks)


def _gla_kernel(qk_ref, la_ref, v_ref, r_ref, gain_ref, big_ref, mask_ref, o_ref, sfin_ref, st_ref):
    c = pl.program_id(1)
    ch = GLA_CHUNK

    @pl.when(c == 0)
    def _():
        st_ref[...] = jnp.zeros(st_ref.shape, F32)

    lane = lax.broadcasted_iota(jnp.int32, (1, LANES), 1)
    head_mask = [jnp.where(lane < DK_GLA, 1.0, 0.0), jnp.where(lane >= DK_GLA, 1.0, 0.0)]
    big = big_ref[...]
    gain = gain_ref[...]
    nq = H_GLA * DK_GLA
    n_chunks = qk_ref.shape[0] // ch
    units = [(ci, p) for ci in range(n_chunks) for p in range(H_GLA // 2)]

    allbs = {}
    for ci, p in units:
        rows = slice(ci * ch, (ci + 1) * ch)
        allbs[ci, p] = _dot01_left(big, la_ref[rows, p * LANES:(p + 1) * LANES], 3)
    intra = {}
    for ci, p in units:
        rows = slice(ci * ch, (ci + 1) * ch)
        q = qk_ref[rows, p * LANES:(p + 1) * LANES]
        k = qk_ref[rows, nq + p * LANES:nq + (p + 1) * LANES]
        allb = allbs[ci, p]
        b = allb[0:ch]
        b_last = b[ch - 1:ch]
        attn = [jnp.zeros((ch, ch), F32), jnp.zeros((ch, ch), F32)]
        for lev in range(N_GLA_LEVELS + 1):
            if lev < N_GLA_LEVELS:
                ref = allb[(lev + 1) * ch:(lev + 2) * ch]
                ql = q * jnp.exp(jnp.minimum(b - ref, 0.0))
                kl = k * jnp.exp(jnp.minimum(ref - b, 0.0))
            else:
                ql, kl = q, k
            qq = jnp.concatenate([ql * head_mask[0], ql * head_mask[1]], axis=0).astype(BF16)
            s = _dot_nt(qq, kl.astype(BF16))
            mk = mask_ref[lev]
            attn[0] = attn[0] + mk * s[0:ch]
            attn[1] = attn[1] + mk * s[ch:2 * ch]
        q0 = q * jnp.exp(b)
        k_hat = (k * jnp.exp(b_last - b)).astype(BF16)
        for hh in range(2):
            hs = slice((2 * p + hh) * DV_GLA, (2 * p + hh + 1) * DV_GLA)
            vh = v_ref[rows, hs].astype(BF16)
            intra[ci, 2 * p + hh] = (_dot(attn[hh].astype(BF16), vh), (q0 * head_mask[hh]).astype(BF16),
                                     _dot_tn(vh, k_hat), jnp.exp(b_last))
    for ci in range(n_chunks):
        rows = slice(ci * ch, (ci + 1) * ch)
        for h in range(H_GLA):
            hs = slice(h * DV_GLA, (h + 1) * DV_GLA)
            o_intra, q0h, kv, decay = intra[ci, h]
            st = st_ref[h]
            o = o_intra + _dot_nt(q0h, st.astype(BF16))
            st_ref[h] = st * decay + kv
            o_ref[rows, hs] = _rms(o, gain) * jax.nn.silu(r_ref[rows, hs])

    @pl.when(c == pl.num_programs(1) - 1)
    def _():
        sfin_ref[0] = st_ref[...]


def _gla_prompt(qk, la, v, r, gain, nb, t):
    big, masks = _gla_constants()
    rows = GLA_CHUNK * GLA_CHUNKS_PER_STEP
    nc = t // rows
    row = lambda b, c: (b * nc + c, 0)
    const2 = lambda b, c: (0, 0)
    m = nb * t
    return pl.pallas_call(
        _gla_kernel,
        grid=(nb, nc),
        in_specs=[
            pl.BlockSpec((rows, 2 * H_GLA * DK_GLA), row),
            pl.BlockSpec((rows, H_GLA * DK_GLA), row),
            pl.BlockSpec((rows, H_GLA * DV_GLA), row),
            pl.BlockSpec((rows, H_GLA * DV_GLA), row),
            pl.BlockSpec((1, DV_GLA), const2),
            pl.BlockSpec(big.shape, const2),
            pl.BlockSpec(masks.shape, lambda b, c: (0, 0, 0)),
        ],
        out_specs=[
            pl.BlockSpec((rows, H_GLA * DV_GLA), row),
            pl.BlockSpec((1, H_GLA, DV_GLA, LANES), lambda b, c: (b, 0, 0, 0)),
        ],
        out_shape=[
            jax.ShapeDtypeStruct((m, H_GLA * DV_GLA), F32),
            jax.ShapeDtypeStruct((nb, H_GLA, DV_GLA, LANES), F32),
        ],
        scratch_shapes=[pltpu.VMEM((H_GLA, DV_GLA, LANES), F32)],
        compiler_params=_params("parallel", "arbitrary"),
        name="gla_scan",
    )(qk, la, v, r, gain, jnp.asarray(big, BF16), jnp.asarray(masks, F32))


def _gla_step_kernel(qk_t_ref, la_t_ref, v_ref, r_ref, s_ref, gain_ref, o_ref, sn_ref):
    nq = H_GLA * DK_GLA
    for b in range(s_ref.shape[0]):
        for h in range(H_GLA):
            ks = slice(h * DK_GLA, (h + 1) * DK_GLA)
            vs = slice(h * DV_GLA, (h + 1) * DV_GLA)
            q = qk_t_ref[ks, b:b + 1]
            k = qk_t_ref[nq + h * DK_GLA:nq + (h + 1) * DK_GLA, b:b + 1]
            s_new = jnp.exp(la_t_ref[ks, b:b + 1]) * s_ref[b, h] + k * v_ref[b:b + 1, vs]
            sn_ref[b, h] = s_new
            o = jnp.sum(q * s_new, axis=0, keepdims=True)
            o_ref[b:b + 1, vs] = _rms(o, gain_ref[...]) * jax.nn.silu(r_ref[b:b + 1, vs])


def _gla_step(qk_t, la_t, v, r, state, gain):
    nb = state.shape[0]
    full = lambda a: pl.BlockSpec(a.shape, functools.partial(lambda i, nd: (0,) * nd, nd=a.ndim))
    return pl.pallas_call(
        _gla_step_kernel,
        grid=(1,),
        in_specs=[full(qk_t), full(la_t), full(v), full(r), full(state), full(gain)],
        out_specs=[full(v), full(state)],
        out_shape=[jax.ShapeDtypeStruct((nb, H_GLA * DV_GLA), F32),
                   jax.ShapeDtypeStruct((nb, H_GLA, DK_GLA, DV_GLA), F32)],
        compiler_params=_params("arbitrary"),
        name="gla_step",
    )(qk_t, la_t, v, r, state, gain)


HALF_ROWS = CMP_STRIDE
CHUNKS_PER_PAGE = ROWS_PER_PAGE // HALF_ROWS


def _compress_kernel(tbl_ref, pages_hbm, wk_ref, wv_ref, posk_ref, w1k_ref, w2k_ref, posv_ref, w1v_ref, w2vt_ref,
                     kc_ref, vct_ref, buf_ref, sem, tk_ref, tv_ref, abk_ref, abv_ref, *, pps, paged):
    b = pl.program_id(0)
    s = pl.program_id(1)
    nsteps = pl.num_programs(1)
    step = b * nsteps + s
    slot = step % 2
    step_chunks = pps * CHUNKS_PER_PAGE

    def page_copy(seq, grp, i, slot_):
        if paged:
            src = pages_hbm.at[tbl_ref[seq, grp * pps + i], pl.ds(0, 2 * LANES), :]
        else:
            row0 = pl.multiple_of((grp * pps + i) * ROWS_PER_PAGE, ROWS_PER_PAGE)
            src = pages_hbm.at[seq, pl.ds(0, 2 * LANES), pl.ds(row0, ROWS_PER_PAGE)]
        return pltpu.make_async_copy(src, buf_ref.at[slot_, i], sem.at[slot_])

    @pl.when(step == 0)
    def _():
        for i in range(pps):
            page_copy(b, s, i, slot).start()

    @pl.when(step + 1 < pl.num_programs(0) * nsteps)
    def _():
        nxt = step + 1
        for i in range(pps):
            page_copy(nxt // nsteps, nxt % nsteps, i, 1 - slot).start()

    for i in range(pps):
        page_copy(b, s, i, slot).wait()

    for i in range(pps):
        tk_ref[i] = buf_ref[slot, i, 0:LANES, :].T
        tv_ref[i] = buf_ref[slot, i, LANES:2 * LANES, :].T
    lpair = 2 * LANES
    row0 = pl.multiple_of(s * step_chunks, step_chunks)
    for t_ref, w_ref, ab_ref in ((tk_ref, wk_ref, abk_ref), (tv_ref, wv_ref, abv_ref)):
        acc = None
        for lp in range(HALF_ROWS // 2):
            cols = [jnp.concatenate([t_ref[i, pl.ds(l, CHUNKS_PER_PAGE, stride=HALF_ROWS), :] for i in range(pps)], axis=0)
                    for l in (2 * lp, 2 * lp + 1)]
            x = jnp.concatenate(cols, axis=1).astype(BF16)
            part = _dot(x, w_ref[lp * lpair:(lp + 1) * lpair, :])
            acc = part if acc is None else acc + part
        ab_ref[pl.ds(row0, step_chunks), :] = acc

    @pl.when(s == pl.num_programs(1) - 1)
    def _():
        nch = abk_ref.shape[0]
        pk = _dot(posk_ref[...], w1k_ref[...])[0:1]
        pv = _dot(posv_ref[...], w1v_ref[...])[0:1]
        abk = abk_ref[...]
        abv = abv_ref[...]
        kc = None
        for g in range(G_NSA):
            o = g * 2 * CMP_HIDDEN
            hk = jax.nn.gelu(abk[:, o:o + CMP_HIDDEN] + pltpu.roll(abk[:, o + CMP_HIDDEN:o + 2 * CMP_HIDDEN], nch - 1, 0) + pk)
            hv = jax.nn.gelu(abv[:, o:o + CMP_HIDDEN] + pltpu.roll(abv[:, o + CMP_HIDDEN:o + 2 * CMP_HIDDEN], nch - 1, 0) + pv)
            tk = _dot(hk.astype(BF16), w2k_ref[g])
            kc = tk if kc is None else kc + tk
            vct_ref[0, g * D_NSA:(g + 1) * D_NSA, :] = _dot_nt(w2vt_ref[...], hv.astype(BF16)).astype(BF16)
        kc_ref[0] = kc.astype(BF16)


def _compress(pages_t, table, w, paged):
    nb, npages = table.shape
    pps = min(PAGES_PER_STEP, npages)
    nsteps = npages // pps
    nch = npages * CHUNKS_PER_PAGE

    names = ("cmp_wk", "cmp_wv", "cmp_posk", "cmp_w1k", "cmp_w2k", "cmp_posv", "cmp_w1v", "cmp_w2vt")
    consts = [w[n] for n in names]
    const_specs = [pl.BlockSpec(c.shape, functools.partial(lambda b, s, tbl, nd: (0,) * nd, nd=c.ndim)) for c in consts]
    grid_spec = pltpu.PrefetchScalarGridSpec(
        num_scalar_prefetch=1,
        grid=(nb, nsteps),
        in_specs=[pl.BlockSpec(memory_space=pl.ANY)] + const_specs,
        out_specs=[pl.BlockSpec((1, nch, LANES), lambda b, s, tbl: (b, 0, 0)),
                   pl.BlockSpec((1, LANES, nch), lambda b, s, tbl: (b, 0, 0))],
        scratch_shapes=[pltpu.VMEM((2, pps, 2 * LANES, ROWS_PER_PAGE), F32), pltpu.SemaphoreType.DMA((2,)),
                        pltpu.VMEM((pps, ROWS_PER_PAGE, LANES), F32), pltpu.VMEM((pps, ROWS_PER_PAGE, LANES), F32),
                        pltpu.VMEM((nch, 4 * CMP_HIDDEN), F32), pltpu.VMEM((nch, 4 * CMP_HIDDEN), F32)],
    )
    return pl.pallas_call(
        functools.partial(_compress_kernel, pps=pps, paged=paged),
        grid_spec=grid_spec,
        out_shape=[jax.ShapeDtypeStruct((nb, nch, LANES), BF16),
                   jax.ShapeDtypeStruct((nb, LANES, nch), BF16)],
        compiler_params=_params("arbitrary", "arbitrary"),
        name="compress",
    )(table, pages_t, *consts)


def _cmp_weights(pos, w1, w2):
    w = w1.reshape(2, HALF_ROWS, D_NSA, CMP_HIDDEN)
    z = jnp.zeros_like(w[0])
    blocks = []
    for g in range(G_NSA):
        cols = []
        for g2 in range(G_NSA):
            for half in range(2):
                cols.append(w[half] if g2 == g else z)
        blocks.append(jnp.concatenate(cols, axis=-1))
    wbig = jnp.stack(blocks, axis=1).reshape(HALF_ROWS * LANES, 4 * CMP_HIDDEN).astype(BF16)
    pos8 = jnp.broadcast_to(pos.reshape(1, CMP_LEN * D_NSA), (SUBLANES, CMP_LEN * D_NSA)).astype(BF16)
    z2 = jnp.zeros_like(w2)
    w2pad = jnp.stack([jnp.concatenate([w2, z2], axis=1), jnp.concatenate([z2, w2], axis=1)]).astype(BF16)
    return wbig, pos8, w1.astype(BF16), w2pad, w2.T.astype(BF16)


def _overlap_t(nc_pad, ns_pad, nc, ns):
    i = np.arange(nc_pad)[None, :] * CMP_STRIDE
    j = np.arange(ns_pad)[:, None] * SEL_BLOCK
    m = (i < j + SEL_BLOCK) & (i + CMP_LEN > j) & (np.arange(nc_pad)[None, :] < nc) & (np.arange(ns_pad)[:, None] < ns)
    return jnp.asarray(m.astype(np.float32), BF16)


def _rank_desc(x):
    nrow = x.shape[0]
    nblk = nrow // SUBLANES
    blocks = [x[v * SUBLANES:(v + 1) * SUBLANES] for v in range(nblk)]
    cnt = [jnp.zeros((SUBLANES, x.shape[1]), F32) for _ in range(nblk)]
    sub = lax.broadcasted_iota(jnp.int32, (SUBLANES, x.shape[1]), 0)
    for jp in range(nrow):
        row = x[jp:jp + 1]
        vb = jp // SUBLANES
        for v in range(nblk):
            ge = jnp.where(row >= blocks[v], 1.0, 0.0)
            gt = jnp.where(row > blocks[v], 1.0, 0.0)
            if v > vb:
                cnt[v] = cnt[v] + ge
            elif v < vb:
                cnt[v] = cnt[v] + gt
            else:
                cnt[v] = cnt[v] + jnp.where(sub > (jp % SUBLANES), ge, gt)
    return jnp.concatenate(cnt, axis=0)


def _nsa_prompt_kernel(qraw_ref, qrot_ref, gate_ref, kc_ref, vct_ref, ksel_ref, vselt_ref, kwin_ref, vwint_ref,
                       ovl_ref, o_ref, kaug_ref, kwb_ref, vsa_ref, vwa_ref):
    qb = pl.program_id(1)
    nlane = R_NSA * Q_TILE
    tpos = qb * Q_TILE + lax.broadcasted_iota(jnp.int32, (1, nlane), 1) % Q_TILE
    tpos1 = tpos[:, 0:Q_TILE]
    ncp = kc_ref.shape[1]
    nsb = ovl_ref.shape[0]
    seq = ksel_ref.shape[0]
    zeros_q = jnp.zeros((D_NSA, nlane), BF16)

    @pl.when(qb == 0)
    def _():
        ks = ksel_ref[...]
        lane = lax.broadcasted_iota(jnp.int32, (1, LANES), 1)
        blk = lax.broadcasted_iota(jnp.int32, (seq, 1), 0) // SEL_BLOCK
        kaug_ref[0] = jnp.where(lane < D_NSA, ks, jnp.where(lane - D_NSA == blk, 1.0, 0.0)).astype(BF16)
        kaug_ref[1] = jnp.where(lane >= D_NSA, ks, jnp.where(lane == blk, 1.0, 0.0)).astype(BF16)
        kwb_ref[...] = kwin_ref[...].astype(BF16)
        ones_row = jnp.where(lax.broadcasted_iota(jnp.int32, (V_PAD_ROWS, seq), 0) == 0, 1.0, 0.0)
        for g in range(G_NSA):
            gs = slice(g * D_NSA, (g + 1) * D_NSA)
            vsa_ref[g] = jnp.concatenate([vselt_ref[0, gs, :], ones_row], axis=0).astype(BF16)
            vwa_ref[g] = jnp.concatenate([vwint_ref[0, gs, :], ones_row], axis=0).astype(BF16)

    def group_q(ref, g):
        q = jnp.concatenate([ref[h * D_NSA:(h + 1) * D_NSA, :] for h in range(g * R_NSA, (g + 1) * R_NSA)], axis=1)
        return q, (jnp.concatenate([q, zeros_q], axis=0) if g == 0 else jnp.concatenate([zeros_q, q], axis=0))

    cmp_scores = [_dot(kc_ref[0], group_q(qraw_ref, g)[1]) for g in range(G_NSA)]
    o_cmps, qaugs, qpads = [], [], []
    for g in range(G_NSA):
        qrot, qpad = group_q(qrot_ref, g)
        gs = slice(g * D_NSA, (g + 1) * D_NSA)

        sc = cmp_scores[g]
        ci = lax.broadcasted_iota(jnp.int32, (ncp, 1), 0)
        p_cmp = _masked_softmax2_rows(sc, ci * CMP_STRIDE + (CMP_LEN - 1) <= tpos)
        o_cmp = _dot(vct_ref[0, gs, :], p_cmp.astype(BF16))
        psum = p_cmp[:, 0:Q_TILE]
        for r in range(1, R_NSA):
            psum = psum + p_cmp[:, r * Q_TILE:(r + 1) * Q_TILE]
        imp = _dot01_left(ovl_ref[...], psum, 2)
        bj = lax.broadcasted_iota(jnp.int32, (nsb, 1), 0)
        cur = tpos1 // SEL_BLOCK
        forced = (bj == 0) | (bj == cur) | (bj == cur - 1)
        imp = jnp.where(forced, FORCE_SCORE, jnp.where(bj * SEL_BLOCK <= tpos1, imp, -1.0))
        bias = jnp.where(_rank_desc(imp) < N_SELECT, 0.0, NEG_INF).astype(BF16)
        bias = jnp.concatenate([bias] * R_NSA, axis=1)
        if nsb < D_NSA:
            bias = jnp.concatenate([bias, jnp.zeros((D_NSA - nsb, nlane), BF16)], axis=0)
        o_cmps.append(o_cmp)
        qaugs.append(jnp.concatenate([qrot, bias], axis=0) if g == 0 else jnp.concatenate([bias, qrot], axis=0))
        qpads.append(qpad)

    ncol = nlane // SEL_COLS
    pairs = [(g, slice(j * SEL_COLS, (j + 1) * SEL_COLS)) for g in range(G_NSA) for j in range(ncol)]

    krow = lax.broadcasted_iota(jnp.int32, (K_TILE, 1), 0)

    def sel_step(c, carry, causal):
        start = pl.multiple_of(c * K_TILE, K_TILE)
        out = []
        scores = [_dot(kaug_ref[g, pl.ds(start, K_TILE), :], qaugs[g][:, cols]) for g, cols in pairs]
        for (g, cols), (m, acc), s in zip(pairs, carry, scores):
            if causal:
                s = jnp.where(start + krow <= tpos[:, cols], s, NEG_INF)
            m_new = jnp.maximum(m, jnp.max(s, axis=0, keepdims=True))
            e = jnp.exp2(s - m_new).astype(BF16)
            acc = jnp.exp2(m - m_new) * acc + _dot(vsa_ref[g, :, pl.ds(start, K_TILE)], e)
            out.append((m_new, acc))
        return tuple(out)

    n_full = (qb * Q_TILE) // K_TILE
    init = tuple((jnp.full((1, SEL_COLS), M_INIT, F32), jnp.zeros((D_NSA + V_PAD_ROWS, SEL_COLS), F32))
                 for _ in pairs)
    carry = lax.fori_loop(0, n_full, functools.partial(sel_step, causal=False), init)

    band = WINDOW + Q_TILE
    wstart = pl.multiple_of(jnp.maximum(qb * Q_TILE - WINDOW, 0), Q_TILE)
    win_scores = [_dot(kwb_ref[pl.ds(wstart, band), :], qpads[g]) for g in range(G_NSA)]

    carry = sel_step(n_full, carry, True)
    o_sels = []
    for g in range(G_NSA):
        acc_sel = jnp.concatenate([carry[g * ncol + j][1] for j in range(ncol)], axis=1)
        o_sels.append(acc_sel[0:D_NSA] / acc_sel[D_NSA:D_NSA + 1])

    diff = tpos - (wstart + lax.broadcasted_iota(jnp.int32, (band, 1), 0))
    old_edge = diff[0:Q_TILE] < WINDOW
    causal_ok = diff >= 0
    heads = []
    for g in range(G_NSA):
        sw = win_scores[g]
        sw = jnp.concatenate([jnp.where(old_edge, sw[0:Q_TILE], NEG_INF), sw[Q_TILE:]], axis=0)
        sw = jnp.where(causal_ok, sw, NEG_INF)
        e_win = jnp.exp2(sw - jnp.max(sw, axis=0, keepdims=True)).astype(BF16)
        acc_win = _dot(vwa_ref[g, :, pl.ds(wstart, band)], e_win)
        o_win = acc_win[0:D_NSA] / acc_win[D_NSA:D_NSA + 1]
        for r in range(R_NSA):
            h = g * R_NSA + r
            ls = slice(r * Q_TILE, (r + 1) * Q_TILE)
            heads.append(gate_ref[3 * h:3 * h + 1, :] * o_cmps[g][:, ls]
                         + gate_ref[3 * h + 1:3 * h + 2, :] * o_sels[g][:, ls]
                         + gate_ref[3 * h + 2:3 * h + 3, :] * o_win[:, ls])
    o_ref[...] = jnp.concatenate(heads, axis=0).T


def _nsa_prompt(qraw_t, qrot_t, gate_t, kc, vct, ksel, nsa_t, kwin, win_t, nb, t):
    nqb = t // Q_TILE
    ncp = kc.shape[1]
    ovl = _overlap_t(ncp, t // SEL_BLOCK, (t - CMP_LEN) // CMP_STRIDE + 1, t // SEL_BLOCK)
    qcol = lambda b, q: (0, b * nqb + q)
    m = nb * t
    return pl.pallas_call(
        _nsa_prompt_kernel,
        grid=(nb, nqb),
        in_specs=[
            pl.BlockSpec((N_Q_COLS, Q_TILE), qcol),
            pl.BlockSpec((N_Q_COLS, Q_TILE), qcol),
            pl.BlockSpec((N_GATE_ROWS, Q_TILE), qcol),
            pl.BlockSpec((1, ncp, LANES), lambda b, q: (b, 0, 0)),
            pl.BlockSpec((1, LANES, ncp), lambda b, q: (b, 0, 0)),
            pl.BlockSpec((t, LANES), lambda b, q: (b, 0)),
            pl.BlockSpec((1, LANES, t), lambda b, q: (b, 3, 0)),
            pl.BlockSpec((t, LANES), lambda b, q: (b, 0)),
            pl.BlockSpec((1, LANES, t), lambda b, q: (b, 1, 0)),
            pl.BlockSpec(ovl.shape, lambda b, q: (0, 0)),
        ],
        out_specs=pl.BlockSpec((Q_TILE, N_Q_COLS), lambda b, q: (b * nqb + q, 0)),
        out_shape=jax.ShapeDtypeStruct((m, N_Q_COLS), F32),
        scratch_shapes=[pltpu.VMEM((G_NSA, t, LANES), BF16), pltpu.VMEM((t, LANES), BF16),
                        pltpu.VMEM((G_NSA, D_NSA + V_PAD_ROWS, t), BF16),
                        pltpu.VMEM((G_NSA, D_NSA + V_PAD_ROWS, t), BF16)],
        compiler_params=_params("parallel", "arbitrary"),
        name="nsa_prompt",
    )(qraw_t, qrot_t, gate_t, kc, vct, ksel, nsa_t, kwin, win_t, ovl)


def _nsa_decode_select_kernel(qt_ref, kc_ref, vct_ref, ovl_ref, grp_ref, idx_ref, ocmp_ref, *, pos, n_blocks):
    ncp = kc_ref.shape[1]
    sc = _dot(kc_ref[0], qt_ref[0])
    ci = lax.broadcasted_iota(jnp.int32, (ncp, 1), 0)
    p = _masked_softmax2_rows(sc, ci * CMP_STRIDE + (CMP_LEN - 1) <= pos)
    ocmp_ref[0] = _dot(vct_ref[0], p.astype(BF16))
    psum = _dot01_right(p, grp_ref[...], 2)
    imp_t = _dot01_left(ovl_ref[...], psum, 2)
    imp_r = imp_t.T
    cur = pos // SEL_BLOCK

    def finish(v, j):
        forced = (j == 0) | (j == cur) | (j == cur - 1)
        v = jnp.where(forced, FORCE_SCORE, jnp.where(j * SEL_BLOCK <= pos, v, -1.0))
        return jnp.where(j < n_blocks, v, -3e38)

    jc = lax.broadcasted_iota(jnp.int32, (PAD_BLOCKS, 1), 0)
    jr = lax.broadcasted_iota(jnp.int32, (1, PAD_BLOCKS), 1)
    kk = lax.broadcasted_iota(jnp.int32, (N_SELECT, 1), 0).astype(F32)
    lane = lax.broadcasted_iota(jnp.int32, (N_SELECT, LANES), 1)
    out = jnp.zeros((N_SELECT, LANES), F32)
    for g in range(G_NSA):
        c0 = g * R_NSA
        col = finish(imp_t[:, c0:c0 + 1], jc)
        row = finish(imp_r[c0:c0 + 1, :], jr)
        beats = (col > row) | ((col == row) & (jc < jr))
        rank = jnp.sum(jnp.where(beats, 1.0, 0.0), axis=0, keepdims=True)
        hit = rank == kk
        idx = jnp.sum(jnp.where(hit, jr.astype(F32), 0.0), axis=1, keepdims=True)
        out = jnp.where(lane == g, idx, out)
    idx_ref[0] = out.astype(jnp.int32)


def _nsa_decode_select(q_t, kc, vct, pos, n_blocks):
    nb, ncp, _ = kc.shape
    nc = (pos + 1 - CMP_LEN) // CMP_STRIDE + 1
    ovl = _overlap_t(ncp, PAD_BLOCKS, nc, n_blocks)
    hh = np.arange(LANES)
    grp = ((hh[:, None] // R_NSA) == (hh[None, :] // R_NSA)) & (hh[:, None] < H_NSA) & (hh[None, :] < H_NSA)
    grp = jnp.asarray(grp.astype(np.float32), BF16)
    return pl.pallas_call(
        functools.partial(_nsa_decode_select_kernel, pos=pos, n_blocks=n_blocks),
        grid=(nb,),
        in_specs=[
            pl.BlockSpec((1, LANES, LANES), lambda b: (b, 0, 0)),
            pl.BlockSpec((1, ncp, LANES), lambda b: (b, 0, 0)),
            pl.BlockSpec((1, LANES, ncp), lambda b: (b, 0, 0)),
            pl.BlockSpec(ovl.shape, lambda b: (0, 0)),
            pl.BlockSpec(grp.shape, lambda b: (0, 0)),
        ],
        out_specs=[pl.BlockSpec((1, N_SELECT, LANES), lambda b: (b, 0, 0)),
                   pl.BlockSpec((1, LANES, LANES), lambda b: (b, 0, 0))],
        out_shape=[jax.ShapeDtypeStruct((nb, N_SELECT, LANES), jnp.int32),
                   jax.ShapeDtypeStruct((nb, LANES, LANES), F32)],
        compiler_params=_params("parallel"),
        name="nsa_decode_select",
    )(q_t, kc, vct, ovl, grp)


def _nsa_decode_attend_kernel(idx_ref, tbl_ref, cache_hbm, q_ref, win_ref, nsa_new_ref, win_new_ref, gate_ref, ocmp_ref,
                              o_ref, buf_ref, sem, *, n_past_blocks, win_buf):
    nblk = G_NSA * N_SELECT
    b = pl.program_id(0)
    slot = b % 2
    halves = ROWS_PER_PAGE // SEL_BLOCK

    def page_copy(seq, n, slot_):
        j = jnp.minimum(idx_ref[seq, n // N_SELECT, n % N_SELECT], n_past_blocks - 1)
        src = cache_hbm.at[tbl_ref[seq, j // halves], pl.ds(2 * LANES, 2 * LANES), :]
        return pltpu.make_async_copy(src, buf_ref.at[slot_, n], sem.at[slot_])

    @pl.when(b == 0)
    def _():
        for n in range(nblk):
            page_copy(b, n, slot).start()

    @pl.when(b + 1 < pl.num_programs(0))
    def _():
        for n in range(nblk):
            page_copy(b + 1, n, 1 - slot).start()

    for n in range(nblk):
        page_copy(b, n, slot).wait()
    blocks = [buf_ref.at[slot, n] for n in range(nblk)]
    q = q_ref[0]
    qf = q.astype(F32)
    rowg = lax.broadcasted_iota(jnp.int32, (H_NSA, 1), 0) // R_NSA
    ks_new = nsa_new_ref[0, :, 2 * LANES:3 * LANES]
    vs_new = nsa_new_ref[0, :, 3 * LANES:4 * LANES]
    s_new = jnp.sum(qf * ks_new, axis=1, keepdims=True)
    col = lax.broadcasted_iota(jnp.int32, (1, N_SELECT * ROWS_PER_PAGE), 1)
    colpage = col // ROWS_PER_PAGE
    colhalf = (col % ROWS_PER_PAGE) // SEL_BLOCK
    o_sel = None
    for g in range(G_NSA):
        ks_t = jnp.concatenate([blocks[g * N_SELECT + k][0:LANES, :] for k in range(N_SELECT)], axis=1)
        vs_t = jnp.concatenate([blocks[g * N_SELECT + k][LANES:2 * LANES, :] for k in range(N_SELECT)], axis=1)
        s = _dot(q, ks_t.astype(BF16))
        want = jnp.full(col.shape, -1, jnp.int32)
        for k in range(N_SELECT):
            j = idx_ref[b, g, k]
            half = jnp.where(j < n_past_blocks, j % halves, -1)
            want = jnp.where(colpage == k, half, want)
        valid = colhalf == want
        sm = jnp.where(valid, s, NEG_INF)
        m = jnp.maximum(jnp.max(sm, axis=1, keepdims=True), s_new)
        e = jnp.where(valid, jnp.exp2(sm - m), 0.0)
        e_new = jnp.exp2(s_new - m)
        l = jnp.sum(e, axis=1, keepdims=True) + e_new
        og = (_dot_nt(e.astype(BF16), vs_t.astype(BF16)) + e_new * vs_new) / l
        o_sel = og if o_sel is None else jnp.where(rowg == g, og, o_sel)

    kw_t = win_ref[0, 0:LANES, :]
    vw_t = win_ref[0, LANES:2 * LANES, :]
    kw_new = win_new_ref[0, :, 0:LANES]
    vw_new = win_new_ref[0, :, LANES:2 * LANES]
    sw = _dot(q, kw_t.astype(BF16))
    sw_new = jnp.sum(qf * kw_new, axis=1, keepdims=True)
    diff = win_buf - lax.broadcasted_iota(jnp.int32, (1, win_buf), 1)
    validw = (diff >= 0) & (diff < WINDOW)
    smw = jnp.where(validw, sw, NEG_INF)
    mw = jnp.maximum(jnp.max(smw, axis=1, keepdims=True), sw_new)
    ew = jnp.where(validw, jnp.exp2(smw - mw), 0.0)
    ew_new = jnp.exp2(sw_new - mw)
    lw = jnp.sum(ew, axis=1, keepdims=True) + ew_new
    o_win = (_dot_nt(ew.astype(BF16), vw_t.astype(BF16)) + ew_new * vw_new) / lw

    gt = gate_ref[0]
    o_ref[0] = gt[:, 0:1] * ocmp_ref[0] + gt[:, 1:2] * o_sel + gt[:, 2:3] * o_win


def _nsa_decode_attend(idx, table, cache_t, q2, win_t, nsa_new, win_new, gates, ocmp, n_past_blocks):
    nb = q2.shape[0]
    win_buf = win_t.shape[2]
    per_b = lambda shape: pl.BlockSpec((1,) + shape, lambda b, i, t: (b, 0, 0))
    grid_spec = pltpu.PrefetchScalarGridSpec(
        num_scalar_prefetch=2,
        grid=(nb,),
        in_specs=[pl.BlockSpec(memory_space=pl.ANY),
                  per_b((H_NSA, LANES)), per_b((2 * LANES, win_buf)), per_b((1, 4 * LANES)),
                  per_b((1, 2 * LANES)), per_b((H_NSA, LANES)), per_b((H_NSA, LANES))],
        out_specs=per_b((H_NSA, LANES)),
        scratch_shapes=[pltpu.VMEM((2, G_NSA * N_SELECT, 2 * LANES, ROWS_PER_PAGE), F32),
                        pltpu.SemaphoreType.DMA((2,))],
    )
    return pl.pallas_call(
        functools.partial(_nsa_decode_attend_kernel, n_past_blocks=n_past_blocks, win_buf=win_buf),
        grid_spec=grid_spec,
        out_shape=jax.ShapeDtypeStruct((nb, H_NSA, LANES), F32),
        compiler_params=_params("arbitrary"),
        name="nsa_decode_attend",
    )(idx, table, cache_t, q2, win_t, nsa_new, win_new, gates, ocmp)


def _post_mixer_kernel(h_ref, a_ref, b_ref, p_ref, wa_ref, wb_ref, gm_ref,
                       gpre_ref, wg_ref, wu_ref, wo_ref, gpost_ref,
                       ppre_ref, pg_ref, pp_ref, ppost_ref, o_ref):
    y = _dot(a_ref[...].astype(BF16), wa_ref[...]) + _dot(b_ref[...].astype(BF16), wb_ref[...])
    h = h_ref[...] + _rms(y, gm_ref[...])
    xn = _rms(h, gpre_ref[...]).astype(BF16)
    acc = jnp.zeros(h.shape, F32)
    for c in range(D_FF // FF_CHUNK):
        sl = slice(c * FF_CHUNK, (c + 1) * FF_CHUNK)
        g = _dot(xn, wg_ref[:, sl])
        u = _dot(xn, wu_ref[:, sl])
        acc = acc + _dot((jax.nn.silu(g) * u).astype(BF16), wo_ref[sl, :])
    h = h + 0.5 * _rms(acc, gpost_ref[...])
    gate = jax.nn.sigmoid(_dot(_rms(h, ppre_ref[...]).astype(BF16), pg_ref[...]))
    o_ref[...] = h + _rms(gate * _dot(p_ref[...].astype(BF16), pp_ref[...]), ppost_ref[...])


def _post_mixer(h, a, b, p, w, tm):
    m = h.shape[0]
    row = lambda i: (i, 0)
    once = dict(pipeline_mode=pl.Buffered(1))
    const = lambda shape, idx=(0, 0): pl.BlockSpec(shape, lambda i: idx, **once)
    vec = const((1, D_MODEL))
    return pl.pallas_call(
        _post_mixer_kernel,
        grid=(m // tm,),
        in_specs=[pl.BlockSpec((tm, D_MODEL), row), pl.BlockSpec((tm, a.shape[1]), row),
                  pl.BlockSpec((tm, b.shape[1]), row), pl.BlockSpec((tm, PLE_DIM), row),
                  const(w["wo_gla"].shape), const(w["wo_nsa"].shape), vec,
                  vec, const((D_MODEL, D_FF)), const((D_MODEL, D_FF), (0, 1)), const((D_FF, D_MODEL)), vec,
                  vec, const(w["ple_gate"].shape), const(w["ple_proj"].shape), vec],
        out_specs=pl.BlockSpec((tm, D_MODEL), row),
        out_shape=jax.ShapeDtypeStruct((m, D_MODEL), F32),
        compiler_params=_params("parallel"),
        name="post_mixer",
    )(h, a, b, p, w["wo_gla"], w["wo_nsa"], w["m_post"],
      w["f2_pre"], w["f2_in"], w["f2_in"], w["f2_out"], w["f2_post"],
      w["ple_pre"], w["ple_gate"], w["ple_proj"], w["ple_post"])


def _split_in_cols(w):
    outs, off = [], 0
    for n in IN_SPLITS:
        outs.append(w[:, off:off + n])
        off += n
    return outs


def _prep_mixer_weights(w_in, w_a2, b_a):
    q_g, k_g, v_g, r_g, a_lr, q_n, kv_n, gate_n = _split_in_cols(w_in)
    pad_cols = lambda w, n: jnp.pad(w, ((0, 0), (0, n - w.shape[1])))
    gd = G_NSA * D_NSA
    k_sel = kv_n[:, 2 * gd:3 * gd]
    k_win = kv_n[:, 4 * gd:5 * gd]
    wn = jnp.concatenate([q_g, k_g, v_g, r_g, pad_cols(a_lr, LANES), k_sel, k_win], axis=1).astype(BF16)
    wt = jnp.concatenate([q_n, kv_n, pad_cols(gate_n, N_GATE_ROWS)], axis=1).T.astype(BF16)
    wa2 = jnp.pad(w_a2, ((0, LANES - GLA_RANK), (0, 0))).astype(BF16)
    return wn, wt, wa2, b_a.reshape(1, -1)


def _row_tile(m):
    return ROW_TILE if m % ROW_TILE == 0 else m


def _layer(x2, p2, mixer, w):
    tm = _row_tile(x2.shape[0])
    h = _ffn(x2, w["f1_pre"], w["f1_in"], w["f1_out"], w["f1_post"])
    o_gla, o_nsa, extras = mixer(h)
    h = _post_mixer(h, o_gla, o_nsa, p2, w, tm)
    return h, extras


def _mixer_prompt(h, w, nb, t):
    tm = _row_tile(h.shape[0])
    tabs = _rope_tables(np.arange(t))
    (qk, v, r, la, ksel, kwin, qraw_t, qrot_t, gate_t, nsa_t, win_t) = _proj(
        h, w["m_pre"], w["wn"], w["wt"], w["wa2"], w["ba"], tabs, tm, t // tm)
    o_gla, s_fin = _gla_prompt(qk, la, v, r, w["gla_gain"], nb, t)
    table = jnp.zeros((nb, t // ROWS_PER_PAGE), jnp.int32)
    kc, vct = _compress(nsa_t, table, w, False)
    o_nsa = _nsa_prompt(qraw_t, qrot_t, gate_t, kc, vct, ksel, nsa_t, kwin, win_t, nb, t)
    s_t = s_fin.reshape(nb, H_GLA, DV_GLA, 2, DK_GLA)
    s_own = jnp.stack([s_t[:, hh, :, hh % 2, :] for hh in range(H_GLA)], axis=1)
    gla_state = jnp.swapaxes(s_own, -1, -2)
    keep = min(WINDOW, t)
    rows_first = lambda a, n: jnp.transpose(a.reshape(nb, n, G_NSA, D_NSA, a.shape[-1]), (0, 4, 1, 2, 3))
    nsa_rows = rows_first(nsa_t, 4)
    win_rows = rows_first(win_t[:, :, t - keep:], 2)
    return o_gla, o_nsa, (nsa_rows, win_rows, gla_state)


def _mixer_sample(h, w, cache_l, win_l, gla_l, page_table):
    nb = h.shape[0]
    n_pages = page_table.shape[1]
    past_len = n_pages * cache_l.shape[1]
    pos = past_len
    tabs = _rope_tables(np.full((nb,), pos))
    (qk, v, r, la, _, _, qraw_t, qrot_t, gate_t, nsa_new_t, win_new_t) = _proj(
        h, w["m_pre"], w["wn"], w["wt"], w["wa2"], w["ba"], tabs, nb, 1)
    nsa = nsa_new_t[0].T
    win = win_new_t[0].T

    o_gla, gla_state = _gla_step(qk.T, la.T, v, r, gla_l.astype(F32), w["gla_gain"])

    cache_t = jnp.transpose(cache_l.reshape(cache_l.shape[0], ROWS_PER_PAGE, 4 * LANES), (0, 2, 1))
    kc, vct = _compress(cache_t, page_table, w, True)

    hg = (jnp.arange(H_NSA) // R_NSA)[None, :, None]

    def group_pad(q_t):
        q8 = q_t.T.reshape(nb, H_NSA, D_NSA)
        return jnp.concatenate([jnp.where(hg == 0, q8, 0), jnp.where(hg == 1, q8, 0)], axis=-1)

    q2_raw = group_pad(qraw_t)
    q2_rot = group_pad(qrot_t)
    q2_raw_t = jnp.pad(jnp.swapaxes(q2_raw, 1, 2), ((0, 0), (0, 0), (0, LANES - H_NSA)))
    n_blocks = -(-(past_len + 1) // SEL_BLOCK)
    idx_pad, ocmp_t = _nsa_decode_select(q2_raw_t, kc, vct, pos, n_blocks)
    idx = jnp.stack([idx_pad[:, :, g] for g in range(G_NSA)], axis=1)
    ocmp = jnp.swapaxes(ocmp_t, 1, 2)[:, :H_NSA, :]
    gates = jnp.pad(gate_t[:3 * H_NSA].T.reshape(nb, H_NSA, 3), ((0, 0), (0, 0), (0, LANES - 3)))
    wb = win_l.shape[1]
    win_buf_t = jnp.transpose(win_l.reshape(nb, wb, 2 * LANES), (0, 2, 1))
    o8 = _nsa_decode_attend(idx, page_table, cache_t, q2_rot, win_buf_t,
                            nsa.reshape(nb, 1, 4 * LANES), win.reshape(nb, 1, 2 * LANES), gates, ocmp,
                            past_len // SEL_BLOCK)
    o8 = o8.reshape(nb, H_NSA, G_NSA, D_NSA)
    o_nsa = jnp.concatenate([o8[:, :R_NSA, 0], o8[:, R_NSA:, 1]], axis=1).reshape(nb, H_NSA * D_NSA)

    nsa_rows = nsa.reshape(nb, 1, 4, G_NSA, D_NSA)
    win_new = win.reshape(nb, 1, 2, G_NSA, D_NSA)
    kw = jnp.concatenate([win_l, win_new.astype(win_l.dtype)], axis=1)
    keep = min(WINDOW, wb + 1)
    return o_gla, o_nsa, (nsa_rows, kw[:, wb + 1 - keep:], gla_state.astype(gla_l.dtype))


def kernel(x_prompt, x_sample, cache_nsa, state_win, state_gla, page_table, p_prompt, p_sample,
           ffn1_norm_pre, ffn1_norm_post, ffn1_w_in, ffn1_w_out,
           mix_norm_pre, mix_norm_post, w_mix_in, w_gla_a2, b_gla_a, gla_out_norm,
           cmp_pos_k, w_cmp_k1, w_cmp_k2, cmp_pos_v, w_cmp_v1, w_cmp_v2, w_mix_out,
           ffn2_norm_pre, ffn2_norm_post, ffn2_w_in, ffn2_w_out,
           ple_norm_pre, ple_w_gate, ple_w_proj, ple_norm_post):
    nb, t, _ = x_prompt.shape
    ns = x_sample.shape[0]
    depth = ffn1_w_in.shape[0]
    hp = x_prompt.reshape(nb * t, D_MODEL)
    hs = x_sample.reshape(ns, D_MODEL)
    outs = [[] for _ in range(6)]
    for i in range(depth):
        wn, wt, wa2, ba = _prep_mixer_weights(w_mix_in[i], w_gla_a2[i], b_gla_a[i])
        cmp_wk, cmp_posk, cmp_w1k, cmp_w2k, _ = _cmp_weights(cmp_pos_k[i], w_cmp_k1[i], w_cmp_k2[i])
        cmp_wv, cmp_posv, cmp_w1v, _, cmp_w2vt = _cmp_weights(cmp_pos_v[i], w_cmp_v1[i], w_cmp_v2[i])
        gla_w = H_GLA * DV_GLA
        w = dict(
            f1_pre=ffn1_norm_pre[i][None], f1_post=ffn1_norm_post[i][None],
            f1_in=ffn1_w_in[i].astype(BF16), f1_out=ffn1_w_out[i].astype(BF16),
            m_pre=mix_norm_pre[i][None], m_post=mix_norm_post[i][None],
            wn=wn, wt=wt, wa2=wa2, ba=ba, gla_gain=gla_out_norm[i][None],
            cmp_wk=cmp_wk, cmp_posk=cmp_posk, cmp_w1k=cmp_w1k, cmp_w2k=cmp_w2k,
            cmp_wv=cmp_wv, cmp_posv=cmp_posv, cmp_w1v=cmp_w1v, cmp_w2vt=cmp_w2vt,
            wo_gla=w_mix_out[i][:gla_w].astype(BF16), wo_nsa=w_mix_out[i][gla_w:].astype(BF16),
            f2_pre=ffn2_norm_pre[i][None], f2_post=ffn2_norm_post[i][None],
            f2_in=ffn2_w_in[i].astype(BF16), f2_out=ffn2_w_out[i].astype(BF16),
            ple_pre=ple_norm_pre[i][None], ple_post=ple_norm_post[i][None],
            ple_gate=ple_w_gate[i].astype(BF16), ple_proj=ple_w_proj[i].astype(BF16),
        )
        hp, (r_p, w_p, s_p) = _layer(hp, p_prompt[i].reshape(nb * t, PLE_DIM),
                                     functools.partial(_mixer_prompt, w=w, nb=nb, t=t), w)
        hs, (r_s, w_s, s_s) = _layer(hs, p_sample[i].reshape(ns, PLE_DIM),
                                     functools.partial(_mixer_sample, w=w, cache_l=cache_nsa[i], win_l=state_win[i],
                                                       gla_l=state_gla[i], page_table=page_table), w)
        for lst, val in zip(outs, (r_p, w_p, s_p, r_s, w_s, s_s)):
            lst.append(val)
    return (hp.reshape(nb, t, D_MODEL), hs.reshape(ns, 1, D_MODEL), *[jnp.stack(o) for o in outs])
```

```python
import functools

import numpy as np
import jax
import jax.numpy as jnp
from jax import lax
from jax.experimental import pallas as pl
from jax.experimental.pallas import tpu as pltpu

F32 = jnp.float32
BF16 = jnp.bfloat16

D_MODEL = 1024
PLE_DIM = 256
D_FF = 2816
EPS = 1e-6
H_GLA = 4
DK_GLA = 64
DV_GLA = 128
GLA_RANK = 16
GLA_GATE_TEMP = 16.0
GLA_CHUNK = 64
H_NSA = 8
G_NSA = 2
R_NSA = H_NSA // G_NSA
D_NSA = 64
CMP_LEN = 32
CMP_STRIDE = 16
CMP_HIDDEN = 128
SEL_BLOCK = 64
N_SELECT = 16
WINDOW = 512
FORCE_SCORE = 1e4
NEG_INF = -1e30
M_INIT = -1e29
TINY = 1e-30
ATTN_SCALE = D_NSA ** -0.5
LOG2E = 1.4426950408889634
QK_SCALE = ATTN_SCALE * LOG2E
ROPE_THETA = 500000.0
ROPE_DIM = D_NSA // 4
ROPE_HALF = ROPE_DIM // 2
IN_SPLITS = (H_GLA * DK_GLA, H_GLA * DK_GLA, H_GLA * DV_GLA, H_GLA * DV_GLA, GLA_RANK,
             H_NSA * D_NSA, 6 * G_NSA * D_NSA, 3 * H_NSA)

LANES = 128
SUBLANES = 8
VMEM_LIMIT = 56 * 1024 * 1024

ROW_TILE = 512
FF_CHUNK = 256
Q_TILE = 256
K_TILE = 512
V_PAD_ROWS = 16
SEL_COLS = 256
PAGES_PER_STEP = 64
ROWS_PER_PAGE = 128
N_GLA_LEVELS = 6
GLA_CHUNKS_PER_STEP = 8
PAD_BLOCKS = 384


def _params(*sem):
    return pltpu.CompilerParams(dimension_semantics=sem, vmem_limit_bytes=VMEM_LIMIT)


def _rms(x, g):
    return x * lax.rsqrt(jnp.mean(x * x, axis=-1, keepdims=True) + EPS) * g


def _dot(a, b):
    return jnp.dot(a, b, preferred_element_type=F32)


def _dot_nt(a, b):
    return lax.dot_general(a, b, (((1,), (1,)), ((), ())), preferred_element_type=F32)


def _dot_tn(a, b):
    return lax.dot_general(a, b, (((0,), (0,)), ((), ())), preferred_element_type=F32)


def _split_bf16(x, n):
    parts = []
    r = x
    for _ in range(n):
        p = r.astype(BF16)
        parts.append(p)
        r = r - p.astype(F32)
    return parts


def _dot01_left(m01, x, n):
    out = None
    for p in _split_bf16(x, n):
        t = _dot(m01, p)
        out = t if out is None else out + t
    return out


def _dot01_right(x, m01, n):
    out = None
    for p in _split_bf16(x, n):
        t = _dot(p, m01)
        out = t if out is None else out + t
    return out


def _softmax2_rows(sm):
    m = jnp.maximum(jnp.max(sm, axis=0, keepdims=True), M_INIT)
    e = jnp.exp2(sm - m)
    return e / jnp.maximum(jnp.sum(e, axis=0, keepdims=True), TINY)


def _ffn_kernel(x_ref, gpre_ref, wg_ref, wu_ref, wo_ref, gpost_ref, o_ref):
    x = x_ref[...]
    xn = _rms(x, gpre_ref[...]).astype(BF16)
    acc = jnp.zeros(x.shape, F32)
    for c in range(D_FF // FF_CHUNK):
        sl = slice(c * FF_CHUNK, (c + 1) * FF_CHUNK)
        g = _dot(xn, wg_ref[:, sl])
        u = _dot(xn, wu_ref[:, sl])
        a = (jax.nn.silu(g) * u).astype(BF16)
        acc = acc + _dot(a, wo_ref[sl, :])
    o_ref[...] = x + 0.5 * _rms(acc, gpost_ref[...])


def _ffn(x, gpre, w_in, w_out, gpost):
    m = x.shape[0]
    tm = _row_tile(m)
    once = dict(pipeline_mode=pl.Buffered(1))
    const = lambda i: (0, 0)
    return pl.pallas_call(
        _ffn_kernel,
        grid=(m // tm,),
        in_specs=[
            pl.BlockSpec((tm, D_MODEL), lambda i: (i, 0)),
            pl.BlockSpec((1, D_MODEL), const, **once),
            pl.BlockSpec((D_MODEL, D_FF), const, **once),
            pl.BlockSpec((D_MODEL, D_FF), lambda i: (0, 1), **once),
            pl.BlockSpec((D_FF, D_MODEL), const, **once),
            pl.BlockSpec((1, D_MODEL), const, **once),
        ],
        out_specs=pl.BlockSpec((tm, D_MODEL), lambda i: (i, 0)),
        out_shape=jax.ShapeDtypeStruct((m, D_MODEL), F32),
        compiler_params=_params("parallel"),
        name="ffn",
    )(x, gpre, w_in, w_in, w_out, gpost)


N_GLA_COLS = 2 * H_GLA * DK_GLA + 2 * H_GLA * DV_GLA
N_KV_COLS = 6 * G_NSA * D_NSA
WN_COLS = N_GLA_COLS + 3 * LANES
N_Q_COLS = H_NSA * D_NSA
N_GATE_ROWS = 32
WT_ROWS = N_Q_COLS + N_KV_COLS + N_GATE_ROWS
N_CACHE_FEATS = 4 * G_NSA * D_NSA
N_WIN_FEATS = 2 * G_NSA * D_NSA


def _rope_rows(x, cos, sin):
    out = []
    for h in range(x.shape[0] // D_NSA):
        b = h * D_NSA
        x1 = x[b:b + ROPE_HALF]
        x2 = x[b + ROPE_HALF:b + ROPE_DIM]
        out += [x1 * cos - x2 * sin, x2 * cos + x1 * sin, x[b + ROPE_DIM:b + D_NSA]]
    return jnp.concatenate(out, axis=0)


def _rope_lanes(x, c, s1, s2):
    return x * c + pltpu.roll(x, LANES - ROPE_HALF, 1) * s1 + pltpu.roll(x, ROPE_HALF, 1) * s2


def _proj_kernel(h_ref, g_ref, wn_ref, wt_ref, wa2_ref, ba_ref, rc_ref, rs1_ref, rs2_ref, cos_ref, sin_ref,
                 qk_ref, v_ref, r_ref, la_ref, ksel_ref, kwin_ref,
                 qraw_ref, qrot_ref, gate_ref, nsat_ref, wint_ref):
    xn = _rms(h_ref[...], g_ref[...]).astype(BF16)
    nqk = 2 * H_GLA * DK_GLA
    nv = H_GLA * DV_GLA
    a_lr = _dot(xn, wn_ref[:, N_GLA_COLS:N_GLA_COLS + LANES])
    z = _dot(xn, wn_ref[:, 0:nqk])
    qk_ref[:, 0:nqk // 2] = z[:, 0:nqk // 2] * (DK_GLA ** -0.5)
    qk_ref[:, nqk // 2:nqk] = z[:, nqk // 2:nqk]
    v_ref[...] = _dot(xn, wn_ref[:, nqk:nqk + nv])
    r_ref[...] = _dot(xn, wn_ref[:, nqk + nv:N_GLA_COLS])
    k0 = N_GLA_COLS + LANES
    rc, rs1, rs2 = rc_ref[...], rs1_ref[...], rs2_ref[...]
    ksel_ref[...] = _rope_lanes(_dot(xn, wn_ref[:, k0:k0 + LANES]), rc, rs1, rs2)
    kwin_ref[...] = _rope_lanes(_dot(xn, wn_ref[:, k0 + LANES:k0 + 2 * LANES]), rc, rs1, rs2)
    zt = _dot_nt(wt_ref[...], xn)
    cos, sin = cos_ref[...], sin_ref[...]
    q = zt[0:N_Q_COLS] * QK_SCALE
    qraw_ref[...] = q.astype(BF16)
    qrot_ref[...] = _rope_rows(q, cos, sin).astype(BF16)
    kv = zt[N_Q_COLS:N_Q_COLS + N_KV_COLS]
    nsat_ref[0, 0:256, :] = kv[0:256]
    nsat_ref[0, 256:384, :] = _rope_rows(kv[256:384], cos, sin)
    nsat_ref[0, 384:512, :] = kv[384:512]
    wint_ref[0, 0:128, :] = _rope_rows(kv[512:640], cos, sin)
    wint_ref[0, 128:256, :] = kv[640:768]
    gate_ref[...] = jax.nn.sigmoid(zt[N_Q_COLS + N_KV_COLS:WT_ROWS])
    xa = _dot(a_lr.astype(BF16), wa2_ref[...]) + ba_ref[...]
    la_ref[...] = (jnp.minimum(xa, 0.0) - jnp.log1p(jnp.exp(-jnp.abs(xa)))) * (1.0 / GLA_GATE_TEMP)


def _proj(h, gain, wn, wt, wa2, ba, tabs, tm, tiles_per_seq):
    m = h.shape[0]
    rc, rs1, rs2, cos_t, sin_t = tabs
    const = lambda i: (0, 0)
    row = lambda i: (i, 0)
    col = lambda i: (0, i)
    tab_row = lambda i: (i % tiles_per_seq, 0)
    tab_col = lambda i: (0, i % tiles_per_seq)
    outs = [
        (2 * H_GLA * DK_GLA, F32), (H_GLA * DV_GLA, F32), (H_GLA * DV_GLA, F32), (H_GLA * DK_GLA, F32),
        (LANES, F32), (LANES, F32),
    ]
    outs_t = [(N_Q_COLS, BF16), (N_Q_COLS, BF16), (N_GATE_ROWS, F32)]
    nseq = m // (tm * tiles_per_seq)
    seq_len = tm * tiles_per_seq
    outs_seq = [N_CACHE_FEATS, N_WIN_FEATS]
    seq_map = lambda i: (i // tiles_per_seq, 0, i % tiles_per_seq)
    return pl.pallas_call(
        _proj_kernel,
        grid=(m // tm,),
        in_specs=[
            pl.BlockSpec((tm, D_MODEL), row),
            pl.BlockSpec((1, D_MODEL), const),
            pl.BlockSpec((D_MODEL, WN_COLS), const),
            pl.BlockSpec((WT_ROWS, D_MODEL), const),
            pl.BlockSpec((LANES, H_GLA * DK_GLA), const),
            pl.BlockSpec((1, H_GLA * DK_GLA), const),
            pl.BlockSpec((tm, LANES), tab_row),
            pl.BlockSpec((tm, LANES), tab_row),
            pl.BlockSpec((tm, LANES), tab_row),
            pl.BlockSpec((ROPE_HALF, tm), tab_col),
            pl.BlockSpec((ROPE_HALF, tm), tab_col),
        ],
        out_specs=[pl.BlockSpec((tm, n), row) for n, _ in outs] + [pl.BlockSpec((n, tm), col) for n, _ in outs_t]
        + [pl.BlockSpec((1, n, tm), seq_map) for n in outs_seq],
        out_shape=[jax.ShapeDtypeStruct((m, n), d) for n, d in outs]
        + [jax.ShapeDtypeStruct((n, m), d) for n, d in outs_t]
        + [jax.ShapeDtypeStruct((nseq, n, seq_len), F32) for n in outs_seq],
        compiler_params=_params("parallel"),
        name="mixer_proj",
    )(h, gain, wn, wt, wa2, ba, rc, rs1, rs2, cos_t, sin_t)


def _rope_tables(pos):
    pos = np.asarray(pos, np.float64)
    inv_freq = ROPE_THETA ** (-np.arange(ROPE_HALF, dtype=np.float64) * 2.0 / ROPE_DIM)
    ang = pos[:, None] * inv_freq[None, :]
    cos, sin = np.cos(ang), np.sin(ang)
    n = pos.shape[0]
    one = np.ones((n, D_NSA - ROPE_DIM))
    zero = np.zeros((n, D_NSA - ROPE_DIM))
    zh = np.zeros((n, ROPE_HALF))
    c = np.concatenate([cos, cos, one], axis=1)
    s1 = np.concatenate([-sin, zh, zero], axis=1)
    s2 = np.concatenate([zh, sin, zero], axis=1)
    dup = lambda t: np.concatenate([t, t], axis=1)
    return tuple(jnp.asarray(a, F32) for a in (dup(c), dup(s1), dup(s2), cos.T, sin.T))


def _gla_constants():
    c = GLA_CHUNK
    t = np.arange(c)
    low = (t[None, :] <= t[:, None]).astype(np.float32)
    mats, masks = [low], []
    for lev in range(1, N_GLA_LEVELS + 1):
        seg = (2 * c) >> lev
        half = seg // 2
        mid = (t // seg) * seg + half
        mats.append(low[mid])
        same = (t[:, None] // seg) == (t[None, :] // seg)
        masks.append((same & ((t[:, None] % seg) >= half) & ((t[None, :] % seg) < half)).astype(np.float32))
    masks.append(np.eye(c, dtype=np.float32))
    return np.concatenate(mats, axis=0), np.stack(masks)


def _gla_kernel(qk_ref, la_ref, v_ref, r_ref, gain_ref, big_ref, mask_ref, o_ref, sfin_ref, st_ref):
    c = pl.program_id(1)
    ch = GLA_CHUNK

    @pl.when(c == 0)
    def _():
        st_ref[...] = jnp.zeros(st_ref.shape, F32)

    lane = lax.broadcasted_iota(jnp.int32, (1, LANES), 1)
    head_mask = [jnp.where(lane < DK_GLA, 1.0, 0.0), jnp.where(lane >= DK_GLA, 1.0, 0.0)]
    big = big_ref[...]
    gain = gain_ref[...]
    nq = H_GLA * DK_GLA
    n_chunks = qk_ref.shape[0] // ch
    units = [(ci, p) for ci in range(n_chunks) for p in range(H_GLA // 2)]

    allbs = {}
    for ci, p in units:
        rows = slice(ci * ch, (ci + 1) * ch)
        allbs[ci, p] = _dot01_left(big, la_ref[rows, p * LANES:(p + 1) * LANES], 3)
    intra = {}
    for ci, p in units:
        rows = slice(ci * ch, (ci + 1) * ch)
        q = qk_ref[rows, p * LANES:(p + 1) * LANES]
        k = qk_ref[rows, nq + p * LANES:nq + (p + 1) * LANES]
        allb = allbs[ci, p]
        b = allb[0:ch]
        b_last = b[ch - 1:ch]
        attn = [jnp.zeros((ch, ch), F32), jnp.zeros((ch, ch), F32)]
        for lev in range(N_GLA_LEVELS + 1):
            if lev < N_GLA_LEVELS:
                ref = allb[(lev + 1) * ch:(lev + 2) * ch]
                ql = q * jnp.exp(jnp.minimum(b - ref, 0.0))
                kl = k * jnp.exp(jnp.minimum(ref - b, 0.0))
            else:
                ql, kl = q, k
            qq = jnp.concatenate([ql * head_mask[0], ql * head_mask[1]], axis=0).astype(BF16)
            s = _dot_nt(qq, kl.astype(BF16))
            mk = mask_ref[lev]
            attn[0] = attn[0] + mk * s[0:ch]
            attn[1] = attn[1] + mk * s[ch:2 * ch]
        q0 = q * jnp.exp(b)
        k_hat = (k * jnp.exp(b_last - b)).astype(BF16)
        for hh in range(2):
            hs = slice((2 * p + hh) * DV_GLA, (2 * p + hh + 1) * DV_GLA)
            vh = v_ref[rows, hs].astype(BF16)
            intra[ci, 2 * p + hh] = (_dot(attn[hh].astype(BF16), vh), (q0 * head_mask[hh]).astype(BF16),
                                     _dot_tn(vh, k_hat), jnp.exp(b_last))
    for ci in range(n_chunks):
        rows = slice(ci * ch, (ci + 1) * ch)
        for h in range(H_GLA):
            hs = slice(h * DV_GLA, (h + 1) * DV_GLA)
            o_intra, q0h, kv, decay = intra[ci, h]
            st = st_ref[h]
            o = o_intra + _dot_nt(q0h, st.astype(BF16))
            st_ref[h] = st * decay + kv
            o_ref[rows, hs] = _rms(o, gain) * jax.nn.silu(r_ref[rows, hs])

    @pl.when(c == pl.num_programs(1) - 1)
    def _():
        sfin_ref[0] = st_ref[...]


def _gla_prompt(qk, la, v, r, gain, nb, t):
    big, masks = _gla_constants()
    rows = GLA_CHUNK * GLA_CHUNKS_PER_STEP
    nc = t // rows
    row = lambda b, c: (b * nc + c, 0)
    const2 = lambda b, c: (0, 0)
    m = nb * t
    return pl.pallas_call(
        _gla_kernel,
        grid=(nb, nc),
        in_specs=[
            pl.BlockSpec((rows, 2 * H_GLA * DK_GLA), row),
            pl.BlockSpec((rows, H_GLA * DK_GLA), row),
            pl.BlockSpec((rows, H_GLA * DV_GLA), row),
            pl.BlockSpec((rows, H_GLA * DV_GLA), row),
            pl.BlockSpec((1, DV_GLA), const2),
            pl.BlockSpec(big.shape, const2),
            pl.BlockSpec(masks.shape, lambda b, c: (0, 0, 0)),
        ],
        out_specs=[
            pl.BlockSpec((rows, H_GLA * DV_GLA), row),
            pl.BlockSpec((1, H_GLA, DV_GLA, LANES), lambda b, c: (b, 0, 0, 0)),
        ],
        out_shape=[
            jax.ShapeDtypeStruct((m, H_GLA * DV_GLA), F32),
            jax.ShapeDtypeStruct((nb, H_GLA, DV_GLA, LANES), F32),
        ],
        scratch_shapes=[pltpu.VMEM((H_GLA, DV_GLA, LANES), F32)],
        compiler_params=_params("parallel", "arbitrary"),
        name="gla_scan",
    )(qk, la, v, r, gain, jnp.asarray(big, BF16), jnp.asarray(masks, F32))


def _gla_step_kernel(qk_t_ref, la_t_ref, v_ref, r_ref, s_ref, gain_ref, o_ref, sn_ref):
    nq = H_GLA * DK_GLA
    for b in range(s_ref.shape[0]):
        for h in range(H_GLA):
            ks = slice(h * DK_GLA, (h + 1) * DK_GLA)
            vs = slice(h * DV_GLA, (h + 1) * DV_GLA)
            q = qk_t_ref[ks, b:b + 1]
            k = qk_t_ref[nq + h * DK_GLA:nq + (h + 1) * DK_GLA, b:b + 1]
            s_new = jnp.exp(la_t_ref[ks, b:b + 1]) * s_ref[b, h] + k * v_ref[b:b + 1, vs]
            sn_ref[b, h] = s_new
            o = jnp.sum(q * s_new, axis=0, keepdims=True)
            o_ref[b:b + 1, vs] = _rms(o, gain_ref[...]) * jax.nn.silu(r_ref[b:b + 1, vs])


def _gla_step(qk_t, la_t, v, r, state, gain):
    nb = state.shape[0]
    full = lambda a: pl.BlockSpec(a.shape, functools.partial(lambda i, nd: (0,) * nd, nd=a.ndim))
    return pl.pallas_call(
        _gla_step_kernel,
        grid=(1,),
        in_specs=[full(qk_t), full(la_t), full(v), full(r), full(state), full(gain)],
        out_specs=[full(v), full(state)],
        out_shape=[jax.ShapeDtypeStruct((nb, H_GLA * DV_GLA), F32),
                   jax.ShapeDtypeStruct((nb, H_GLA, DK_GLA, DV_GLA), F32)],
        compiler_params=_params("arbitrary"),
        name="gla_step",
    )(qk_t, la_t, v, r, state, gain)


HALF_ROWS = CMP_STRIDE
CHUNKS_PER_PAGE = ROWS_PER_PAGE // HALF_ROWS


def _compress_kernel(tbl_ref, pages_hbm, wk_ref, wv_ref, posk_ref, w1k_ref, w2k_ref, posv_ref, w1v_ref, w2vt_ref,
                     kc_ref, vct_ref, buf_ref, sem, tk_ref, tv_ref, abk_ref, abv_ref, *, pps, paged):
    b = pl.program_id(0)
    s = pl.program_id(1)
    nsteps = pl.num_programs(1)
    step = b * nsteps + s
    slot = step % 2
    step_chunks = pps * CHUNKS_PER_PAGE

    def page_copy(seq, grp, i, slot_):
        if paged:
            src = pages_hbm.at[tbl_ref[seq, grp * pps + i], pl.ds(0, 2 * LANES), :]
        else:
            row0 = pl.multiple_of((grp * pps + i) * ROWS_PER_PAGE, ROWS_PER_PAGE)
            src = pages_hbm.at[seq, pl.ds(0, 2 * LANES), pl.ds(row0, ROWS_PER_PAGE)]
        return pltpu.make_async_copy(src, buf_ref.at[slot_, i], sem.at[slot_])

    @pl.when(step == 0)
    def _():
        for i in range(pps):
            page_copy(b, s, i, slot).start()

    @pl.when(step + 1 < pl.num_programs(0) * nsteps)
    def _():
        nxt = step + 1
        for i in range(pps):
            page_copy(nxt // nsteps, nxt % nsteps, i, 1 - slot).start()

    for i in range(pps):
        page_copy(b, s, i, slot).wait()

    for i in range(pps):
        tk_ref[i] = buf_ref[slot, i, 0:LANES, :].T
        tv_ref[i] = buf_ref[slot, i, LANES:2 * LANES, :].T
    lpair = 2 * LANES
    row0 = pl.multiple_of(s * step_chunks, step_chunks)
    for t_ref, w_ref, ab_ref in ((tk_ref, wk_ref, abk_ref), (tv_ref, wv_ref, abv_ref)):
        acc = None
        for lp in range(HALF_ROWS // 2):
            cols = [jnp.concatenate([t_ref[i, pl.ds(l, CHUNKS_PER_PAGE, stride=HALF_ROWS), :] for i in range(pps)], axis=0)
                    for l in (2 * lp, 2 * lp + 1)]
            x = jnp.concatenate(cols, axis=1).astype(BF16)
            part = _dot(x, w_ref[lp * lpair:(lp + 1) * lpair, :])
            acc = part if acc is None else acc + part
        ab_ref[pl.ds(row0, step_chunks), :] = acc

    @pl.when(s == pl.num_programs(1) - 1)
    def _():
        nch = abk_ref.shape[0]
        pk = _dot(posk_ref[...], w1k_ref[...])[0:1]
        pv = _dot(posv_ref[...], w1v_ref[...])[0:1]
        abk = abk_ref[...]
        abv = abv_ref[...]
        kc = None
        for g in range(G_NSA):
            o = g * 2 * CMP_HIDDEN
            hk = jax.nn.gelu(abk[:, o:o + CMP_HIDDEN] + pltpu.roll(abk[:, o + CMP_HIDDEN:o + 2 * CMP_HIDDEN], nch - 1, 0) + pk)
            hv = jax.nn.gelu(abv[:, o:o + CMP_HIDDEN] + pltpu.roll(abv[:, o + CMP_HIDDEN:o + 2 * CMP_HIDDEN], nch - 1, 0) + pv)
            tk = _dot(hk.astype(BF16), w2k_ref[g])
            kc = tk if kc is None else kc + tk
            vct_ref[0, g * D_NSA:(g + 1) * D_NSA, :] = _dot_nt(w2vt_ref[...], hv.astype(BF16)).astype(BF16)
        kc_ref[0] = kc.astype(BF16)


def _compress(pages_t, table, w, paged):
    nb, npages = table.shape
    pps = min(PAGES_PER_STEP, npages)
    nsteps = npages // pps
    nch = npages * CHUNKS_PER_PAGE

    names = ("cmp_wk", "cmp_wv", "cmp_posk", "cmp_w1k", "cmp_w2k", "cmp_posv", "cmp_w1v", "cmp_w2vt")
    consts = [w[n] for n in names]
    const_specs = [pl.BlockSpec(c.shape, functools.partial(lambda b, s, tbl, nd: (0,) * nd, nd=c.ndim)) for c in consts]
    grid_spec = pltpu.PrefetchScalarGridSpec(
        num_scalar_prefetch=1,
        grid=(nb, nsteps),
        in_specs=[pl.BlockSpec(memory_space=pl.ANY)] + const_specs,
        out_specs=[pl.BlockSpec((1, nch, LANES), lambda b, s, tbl: (b, 0, 0)),
                   pl.BlockSpec((1, LANES, nch), lambda b, s, tbl: (b, 0, 0))],
        scratch_shapes=[pltpu.VMEM((2, pps, 2 * LANES, ROWS_PER_PAGE), F32), pltpu.SemaphoreType.DMA((2,)),
                        pltpu.VMEM((pps, ROWS_PER_PAGE, LANES), F32), pltpu.VMEM((pps, ROWS_PER_PAGE, LANES), F32),
                        pltpu.VMEM((nch, 4 * CMP_HIDDEN), F32), pltpu.VMEM((nch, 4 * CMP_HIDDEN), F32)],
    )
    return pl.pallas_call(
        functools.partial(_compress_kernel, pps=pps, paged=paged),
        grid_spec=grid_spec,
        out_shape=[jax.ShapeDtypeStruct((nb, nch, LANES), BF16),
                   jax.ShapeDtypeStruct((nb, LANES, nch), BF16)],
        compiler_params=_params("arbitrary", "arbitrary"),
        name="compress",
    )(table, pages_t, *consts)


def _cmp_weights(pos, w1, w2):
    w = w1.reshape(2, HALF_ROWS, D_NSA, CMP_HIDDEN)
    z = jnp.zeros_like(w[0])
    blocks = []
    for g in range(G_NSA):
        cols = []
        for g2 in range(G_NSA):
            for half in range(2):
                cols.append(w[half] if g2 == g else z)
        blocks.append(jnp.concatenate(cols, axis=-1))
    wbig = jnp.stack(blocks, axis=1).reshape(HALF_ROWS * LANES, 4 * CMP_HIDDEN).astype(BF16)
    pos8 = jnp.broadcast_to(pos.reshape(1, CMP_LEN * D_NSA), (SUBLANES, CMP_LEN * D_NSA)).astype(BF16)
    z2 = jnp.zeros_like(w2)
    w2pad = jnp.stack([jnp.concatenate([w2, z2], axis=1), jnp.concatenate([z2, w2], axis=1)]).astype(BF16)
    return wbig, pos8, w1.astype(BF16), w2pad, w2.T.astype(BF16)


def _overlap_t(nc_pad, ns_pad, nc, ns):
    i = np.arange(nc_pad)[None, :] * CMP_STRIDE
    j = np.arange(ns_pad)[:, None] * SEL_BLOCK
    m = (i < j + SEL_BLOCK) & (i + CMP_LEN > j) & (np.arange(nc_pad)[None, :] < nc) & (np.arange(ns_pad)[:, None] < ns)
    return jnp.asarray(m.astype(np.float32), BF16)


def _rank_desc(x):
    nrow = x.shape[0]
    nblk = nrow // SUBLANES
    blocks = [x[v * SUBLANES:(v + 1) * SUBLANES] for v in range(nblk)]
    cnt = [jnp.zeros((SUBLANES, x.shape[1]), F32) for _ in range(nblk)]
    sub = lax.broadcasted_iota(jnp.int32, (SUBLANES, x.shape[1]), 0)
    for jp in range(nrow):
        row = x[jp:jp + 1]
        vb = jp // SUBLANES
        for v in range(nblk):
            ge = jnp.where(row >= blocks[v], 1.0, 0.0)
            gt = jnp.where(row > blocks[v], 1.0, 0.0)
            if v > vb:
                cnt[v] = cnt[v] + ge
            elif v < vb:
                cnt[v] = cnt[v] + gt
            else:
                cnt[v] = cnt[v] + jnp.where(sub > (jp % SUBLANES), ge, gt)
    return jnp.concatenate(cnt, axis=0)


def _nsa_prompt_kernel(qraw_ref, qrot_ref, gate_ref, kc_ref, vct_ref, ksel_ref, vselt_ref, kwin_ref, vwint_ref,
                       ovl_ref, o_ref, kaug_ref, kwb_ref, vsa_ref, vwa_ref):
    qb = pl.program_id(1)
    nlane = R_NSA * Q_TILE
    tpos = qb * Q_TILE + lax.broadcasted_iota(jnp.int32, (1, nlane), 1) % Q_TILE
    tpos1 = tpos[:, 0:Q_TILE]
    ncp = kc_ref.shape[1]
    nsb = ovl_ref.shape[0]
    seq = ksel_ref.shape[0]
    zeros_q = jnp.zeros((D_NSA, nlane), BF16)

    @pl.when(qb == 0)
    def _():
        ks = ksel_ref[...]
        lane = lax.broadcasted_iota(jnp.int32, (1, LANES), 1)
        blk = lax.broadcasted_iota(jnp.int32, (seq, 1), 0) // SEL_BLOCK
        kaug_ref[0] = jnp.where(lane < D_NSA, ks, jnp.where(lane - D_NSA == blk, 1.0, 0.0)).astype(BF16)
        kaug_ref[1] = jnp.where(lane >= D_NSA, ks, jnp.where(lane == blk, 1.0, 0.0)).astype(BF16)
        kwb_ref[...] = kwin_ref[...].astype(BF16)
        ones_row = jnp.where(lax.broadcasted_iota(jnp.int32, (V_PAD_ROWS, seq), 0) == 0, 1.0, 0.0)
        for g in range(G_NSA):
            gs = slice(g * D_NSA, (g + 1) * D_NSA)
            vsa_ref[g] = jnp.concatenate([vselt_ref[0, gs, :], ones_row], axis=0).astype(BF16)
            vwa_ref[g] = jnp.concatenate([vwint_ref[0, gs, :], ones_row], axis=0).astype(BF16)

    def group_q(ref, g):
        q = jnp.concatenate([ref[h * D_NSA:(h + 1) * D_NSA, :] for h in range(g * R_NSA, (g + 1) * R_NSA)], axis=1)
        return q, (jnp.concatenate([q, zeros_q], axis=0) if g == 0 else jnp.concatenate([zeros_q, q], axis=0))

    cmp_scores = [_dot(kc_ref[0], group_q(qraw_ref, g)[1]) for g in range(G_NSA)]
    o_cmps, qaugs, qpads = [], [], []
    for g in range(G_NSA):
        qrot, qpad = group_q(qrot_ref, g)
        gs = slice(g * D_NSA, (g + 1) * D_NSA)

        sc = cmp_scores[g]
        ci = lax.broadcasted_iota(jnp.int32, (ncp, 1), 0)
        ended = ci * CMP_STRIDE + (CMP_LEN - 1) <= tpos1
        p_cmp = _softmax2_rows(jnp.concatenate(
            [jnp.where(ended, sc[:, r * Q_TILE:(r + 1) * Q_TILE], NEG_INF) for r in range(R_NSA)], axis=1))
        o_cmp = _dot(vct_ref[0, gs, :], p_cmp.astype(BF16))
        psum = p_cmp[:, 0:Q_TILE]
        for r in range(1, R_NSA):
            psum = psum + p_cmp[:, r * Q_TILE:(r + 1) * Q_TILE]
        imp = _dot01_left(ovl_ref[...], psum, 2)
        bj = lax.broadcasted_iota(jnp.int32, (nsb, 1), 0)
        cur = tpos1 // SEL_BLOCK
        forced = (bj == 0) | (bj == cur) | (bj == cur - 1)
        imp = jnp.where(forced, FORCE_SCORE, jnp.where(bj * SEL_BLOCK <= tpos1, imp, -1.0))
        bias = jnp.where(_rank_desc(imp) < N_SELECT, 0.0, NEG_INF).astype(BF16)
        bias = jnp.concatenate([bias] * R_NSA, axis=1)
        if nsb < D_NSA:
            bias = jnp.concatenate([bias, jnp.zeros((D_NSA - nsb, nlane), BF16)], axis=0)
        o_cmps.append(o_cmp)
        qaugs.append(jnp.concatenate([qrot, bias], axis=0) if g == 0 else jnp.concatenate([bias, qrot], axis=0))
        qpads.append(qpad)

    ncol = nlane // SEL_COLS
    pairs = [(g, slice(j * SEL_COLS, (j + 1) * SEL_COLS)) for g in range(G_NSA) for j in range(ncol)]

    krow = lax.broadcasted_iota(jnp.int32, (K_TILE, 1), 0)

    def sel_step(c, carry, causal):
        start = pl.multiple_of(c * K_TILE, K_TILE)
        out = []
        scores = [_dot(kaug_ref[g, pl.ds(start, K_TILE), :], qaugs[g][:, cols]) for g, cols in pairs]
        for (g, cols), (m, acc), s in zip(pairs, carry, scores):
            if causal:
                s = jnp.where(start + krow <= tpos[:, cols], s, NEG_INF)
            m_new = jnp.maximum(m, jnp.max(s, axis=0, keepdims=True))
            e = jnp.exp2(s - m_new).astype(BF16)
            acc = jnp.exp2(m - m_new) * acc + _dot(vsa_ref[g, :, pl.ds(start, K_TILE)], e)
            out.append((m_new, acc))
        return tuple(out)

    n_full = (qb * Q_TILE) // K_TILE
    init = tuple((jnp.full((1, SEL_COLS), M_INIT, F32), jnp.zeros((D_NSA + V_PAD_ROWS, SEL_COLS), F32))
                 for _ in pairs)
    carry = lax.fori_loop(0, n_full, functools.partial(sel_step, causal=False), init)

    band = WINDOW + Q_TILE
    wstart = pl.multiple_of(jnp.maximum(qb * Q_TILE - WINDOW, 0), Q_TILE)
    win_scores = [_dot(kwb_ref[pl.ds(wstart, band), :], qpads[g]) for g in range(G_NSA)]

    carry = sel_step(n_full, carry, True)
    o_sels = []
    for g in range(G_NSA):
        acc_sel = jnp.concatenate([carry[g * ncol + j][1] for j in range(ncol)], axis=1)
        o_sels.append(acc_sel[0:D_NSA] / acc_sel[D_NSA:D_NSA + 1])

    brow = lax.broadcasted_iota(jnp.int32, (band, 1), 0)
    diff = tpos1 - (wstart + brow)
    age_limit = jnp.where(brow < Q_TILE, WINDOW, band + WINDOW)
    in_window = (diff >= 0) & (diff < age_limit)
    heads = []
    for g in range(G_NSA):
        sw = jnp.concatenate([jnp.where(in_window, win_scores[g][:, r * Q_TILE:(r + 1) * Q_TILE], NEG_INF)
                              for r in range(R_NSA)], axis=1)
        e_win = jnp.exp2(sw - jnp.max(sw, axis=0, keepdims=True)).astype(BF16)
        acc_win = _dot(vwa_ref[g, :, pl.ds(wstart, band)], e_win)
        o_win = acc_win[0:D_NSA] / acc_win[D_NSA:D_NSA + 1]
        for r in range(R_NSA):
            h = g * R_NSA + r
            ls = slice(r * Q_TILE, (r + 1) * Q_TILE)
            heads.append(gate_ref[3 * h:3 * h + 1, :] * o_cmps[g][:, ls]
                         + gate_ref[3 * h + 1:3 * h + 2, :] * o_sels[g][:, ls]
                         + gate_ref[3 * h + 2:3 * h + 3, :] * o_win[:, ls])
    o_ref[...] = jnp.concatenate(heads, axis=0).T


def _nsa_prompt(qraw_t, qrot_t, gate_t, kc, vct, ksel, nsa_t, kwin, win_t, nb, t):
    nqb = t // Q_TILE
    ncp = kc.shape[1]
    ovl = _overlap_t(ncp, t // SEL_BLOCK, (t - CMP_LEN) // CMP_STRIDE + 1, t // SEL_BLOCK)
    qcol = lambda b, q: (0, b * nqb + q)
    m = nb * t
    return pl.pallas_call(
        _nsa_prompt_kernel,
        grid=(nb, nqb),
        in_specs=[
            pl.BlockSpec((N_Q_COLS, Q_TILE), qcol),
            pl.BlockSpec((N_Q_COLS, Q_TILE), qcol),
            pl.BlockSpec((N_GATE_ROWS, Q_TILE), qcol),
            pl.BlockSpec((1, ncp, LANES), lambda b, q: (b, 0, 0)),
            pl.BlockSpec((1, LANES, ncp), lambda b, q: (b, 0, 0)),
            pl.BlockSpec((t, LANES), lambda b, q: (b, 0)),
            pl.BlockSpec((1, LANES, t), lambda b, q: (b, 3, 0)),
            pl.BlockSpec((t, LANES), lambda b, q: (b, 0)),
            pl.BlockSpec((1, LANES, t), lambda b, q: (b, 1, 0)),
            pl.BlockSpec(ovl.shape, lambda b, q: (0, 0)),
        ],
        out_specs=pl.BlockSpec((Q_TILE, N_Q_COLS), lambda b, q: (b * nqb + q, 0)),
        out_shape=jax.ShapeDtypeStruct((m, N_Q_COLS), F32),
        scratch_shapes=[pltpu.VMEM((G_NSA, t, LANES), BF16), pltpu.VMEM((t, LANES), BF16),
                        pltpu.VMEM((G_NSA, D_NSA + V_PAD_ROWS, t), BF16),
                        pltpu.VMEM((G_NSA, D_NSA + V_PAD_ROWS, t), BF16)],
        compiler_params=_params("parallel", "arbitrary"),
        name="nsa_prompt",
    )(qraw_t, qrot_t, gate_t, kc, vct, ksel, nsa_t, kwin, win_t, ovl)


def _nsa_decode_select_kernel(qt_ref, kc_ref, vct_ref, ovl_ref, grp_ref, idx_ref, ocmp_ref, *, pos, n_blocks):
    ncp = kc_ref.shape[1]
    sc = _dot(kc_ref[0], qt_ref[0])
    ci = lax.broadcasted_iota(jnp.int32, (ncp, 1), 0)
    p = _softmax2_rows(jnp.where(ci * CMP_STRIDE + (CMP_LEN - 1) <= pos, sc, NEG_INF))
    ocmp_ref[0] = _dot(vct_ref[0], p.astype(BF16))
    psum = _dot01_right(p, grp_ref[...], 2)
    imp_t = _dot01_left(ovl_ref[...], psum, 2)
    imp_r = imp_t.T
    cur = pos // SEL_BLOCK

    def finish(v, j):
        forced = (j == 0) | (j == cur) | (j == cur - 1)
        v = jnp.where(forced, FORCE_SCORE, jnp.where(j * SEL_BLOCK <= pos, v, -1.0))
        return jnp.where(j < n_blocks, v, -3e38)

    jc = lax.broadcasted_iota(jnp.int32, (PAD_BLOCKS, 1), 0)
    jr = lax.broadcasted_iota(jnp.int32, (1, PAD_BLOCKS), 1)
    kk = lax.broadcasted_iota(jnp.int32, (N_SELECT, 1), 0).astype(F32)
    lane = lax.broadcasted_iota(jnp.int32, (N_SELECT, LANES), 1)
    out = jnp.zeros((N_SELECT, LANES), F32)
    for g in range(G_NSA):
        c0 = g * R_NSA
        col = finish(imp_t[:, c0:c0 + 1], jc)
        row = finish(imp_r[c0:c0 + 1, :], jr)
        beats = (col > row) | ((col == row) & (jc < jr))
        rank = jnp.sum(jnp.where(beats, 1.0, 0.0), axis=0, keepdims=True)
        hit = rank == kk
        idx = jnp.sum(jnp.where(hit, jr.astype(F32), 0.0), axis=1, keepdims=True)
        out = jnp.where(lane == g, idx, out)
    idx_ref[0] = out.astype(jnp.int32)


def _nsa_decode_select(q_t, kc, vct, pos, n_blocks):
    nb, ncp, _ = kc.shape
    nc = (pos + 1 - CMP_LEN) // CMP_STRIDE + 1
    ovl = _overlap_t(ncp, PAD_BLOCKS, nc, n_blocks)
    hh = np.arange(LANES)
    grp = ((hh[:, None] // R_NSA) == (hh[None, :] // R_NSA)) & (hh[:, None] < H_NSA) & (hh[None, :] < H_NSA)
    grp = jnp.asarray(grp.astype(np.float32), BF16)
    return pl.pallas_call(
        functools.partial(_nsa_decode_select_kernel, pos=pos, n_blocks=n_blocks),
        grid=(nb,),
        in_specs=[
            pl.BlockSpec((1, LANES, LANES), lambda b: (b, 0, 0)),
            pl.BlockSpec((1, ncp, LANES), lambda b: (b, 0, 0)),
            pl.BlockSpec((1, LANES, ncp), lambda b: (b, 0, 0)),
            pl.BlockSpec(ovl.shape, lambda b: (0, 0)),
            pl.BlockSpec(grp.shape, lambda b: (0, 0)),
        ],
        out_specs=[pl.BlockSpec((1, N_SELECT, LANES), lambda b: (b, 0, 0)),
                   pl.BlockSpec((1, LANES, LANES), lambda b: (b, 0, 0))],
        out_shape=[jax.ShapeDtypeStruct((nb, N_SELECT, LANES), jnp.int32),
                   jax.ShapeDtypeStruct((nb, LANES, LANES), F32)],
        compiler_params=_params("parallel"),
        name="nsa_decode_select",
    )(q_t, kc, vct, ovl, grp)


def _nsa_decode_attend_kernel(idx_ref, tbl_ref, cache_hbm, q_ref, win_ref, nsa_new_ref, win_new_ref, gate_ref, ocmp_ref,
                              o_ref, buf_ref, sem, *, n_past_blocks, win_buf):
    nblk = G_NSA * N_SELECT
    b = pl.program_id(0)
    slot = b % 2
    halves = ROWS_PER_PAGE // SEL_BLOCK

    def page_copy(seq, n, slot_):
        j = jnp.minimum(idx_ref[seq, n // N_SELECT, n % N_SELECT], n_past_blocks - 1)
        src = cache_hbm.at[tbl_ref[seq, j // halves], pl.ds(2 * LANES, 2 * LANES), :]
        return pltpu.make_async_copy(src, buf_ref.at[slot_, n], sem.at[slot_])

    @pl.when(b == 0)
    def _():
        for n in range(nblk):
            page_copy(b, n, slot).start()

    @pl.when(b + 1 < pl.num_programs(0))
    def _():
        for n in range(nblk):
            page_copy(b + 1, n, 1 - slot).start()

    for n in range(nblk):
        page_copy(b, n, slot).wait()
    blocks = [buf_ref.at[slot, n] for n in range(nblk)]
    q = q_ref[0]
    qf = q.astype(F32)
    rowg = lax.broadcasted_iota(jnp.int32, (H_NSA, 1), 0) // R_NSA
    ks_new = nsa_new_ref[0, :, 2 * LANES:3 * LANES]
    vs_new = nsa_new_ref[0, :, 3 * LANES:4 * LANES]
    s_new = jnp.sum(qf * ks_new, axis=1, keepdims=True)
    col = lax.broadcasted_iota(jnp.int32, (1, N_SELECT * ROWS_PER_PAGE), 1)
    colpage = col // ROWS_PER_PAGE
    colhalf = (col % ROWS_PER_PAGE) // SEL_BLOCK
    o_sel = None
    for g in range(G_NSA):
        ks_t = jnp.concatenate([blocks[g * N_SELECT + k][0:LANES, :] for k in range(N_SELECT)], axis=1)
        vs_t = jnp.concatenate([blocks[g * N_SELECT + k][LANES:2 * LANES, :] for k in range(N_SELECT)], axis=1)
        s = _dot(q, ks_t.astype(BF16))
        want = jnp.full(col.shape, -1, jnp.int32)
        for k in range(N_SELECT):
            j = idx_ref[b, g, k]
            half = jnp.where(j < n_past_blocks, j % halves, -1)
            want = jnp.where(colpage == k, half, want)
        valid = colhalf == want
        sm = jnp.where(valid, s, NEG_INF)
        m = jnp.maximum(jnp.max(sm, axis=1, keepdims=True), s_new)
        e = jnp.where(valid, jnp.exp2(sm - m), 0.0)
        e_new = jnp.exp2(s_new - m)
        l = jnp.sum(e, axis=1, keepdims=True) + e_new
        og = (_dot_nt(e.astype(BF16), vs_t.astype(BF16)) + e_new * vs_new) / l
        o_sel = og if o_sel is None else jnp.where(rowg == g, og, o_sel)

    kw_t = win_ref[0, 0:LANES, :]
    vw_t = win_ref[0, LANES:2 * LANES, :]
    kw_new = win_new_ref[0, :, 0:LANES]
    vw_new = win_new_ref[0, :, LANES:2 * LANES]
    sw = _dot(q, kw_t.astype(BF16))
    sw_new = jnp.sum(qf * kw_new, axis=1, keepdims=True)
    diff = win_buf - lax.broadcasted_iota(jnp.int32, (1, win_buf), 1)
    validw = (diff >= 0) & (diff < WINDOW)
    smw = jnp.where(validw, sw, NEG_INF)
    mw = jnp.maximum(jnp.max(smw, axis=1, keepdims=True), sw_new)
    ew = jnp.where(validw, jnp.exp2(smw - mw), 0.0)
    ew_new = jnp.exp2(sw_new - mw)
    lw = jnp.sum(ew, axis=1, keepdims=True) + ew_new
    o_win = (_dot_nt(ew.astype(BF16), vw_t.astype(BF16)) + ew_new * vw_new) / lw

    gt = gate_ref[0]
    o_ref[0] = gt[:, 0:1] * ocmp_ref[0] + gt[:, 1:2] * o_sel + gt[:, 2:3] * o_win


def _nsa_decode_attend(idx, table, cache_t, q2, win_t, nsa_new, win_new, gates, ocmp, n_past_blocks):
    nb = q2.shape[0]
    win_buf = win_t.shape[2]
    per_b = lambda shape: pl.BlockSpec((1,) + shape, lambda b, i, t: (b, 0, 0))
    grid_spec = pltpu.PrefetchScalarGridSpec(
        num_scalar_prefetch=2,
        grid=(nb,),
        in_specs=[pl.BlockSpec(memory_space=pl.ANY),
                  per_b((H_NSA, LANES)), per_b((2 * LANES, win_buf)), per_b((1, 4 * LANES)),
                  per_b((1, 2 * LANES)), per_b((H_NSA, LANES)), per_b((H_NSA, LANES))],
        out_specs=per_b((H_NSA, LANES)),
        scratch_shapes=[pltpu.VMEM((2, G_NSA * N_SELECT, 2 * LANES, ROWS_PER_PAGE), F32),
                        pltpu.SemaphoreType.DMA((2,))],
    )
    return pl.pallas_call(
        functools.partial(_nsa_decode_attend_kernel, n_past_blocks=n_past_blocks, win_buf=win_buf),
        grid_spec=grid_spec,
        out_shape=jax.ShapeDtypeStruct((nb, H_NSA, LANES), F32),
        compiler_params=_params("arbitrary"),
        name="nsa_decode_attend",
    )(idx, table, cache_t, q2, win_t, nsa_new, win_new, gates, ocmp)


def _post_mixer_kernel(h_ref, a_ref, b_ref, p_ref, wa_ref, wb_ref, gm_ref,
                       gpre_ref, wg_ref, wu_ref, wo_ref, gpost_ref,
                       ppre_ref, pg_ref, pp_ref, ppost_ref, o_ref):
    y = _dot(a_ref[...].astype(BF16), wa_ref[...]) + _dot(b_ref[...].astype(BF16), wb_ref[...])
    h = h_ref[...] + _rms(y, gm_ref[...])
    xn = _rms(h, gpre_ref[...]).astype(BF16)
    acc = jnp.zeros(h.shape, F32)
    for c in range(D_FF // FF_CHUNK):
        sl = slice(c * FF_CHUNK, (c + 1) * FF_CHUNK)
        g = _dot(xn, wg_ref[:, sl])
        u = _dot(xn, wu_ref[:, sl])
        acc = acc + _dot((jax.nn.silu(g) * u).astype(BF16), wo_ref[sl, :])
    h = h + 0.5 * _rms(acc, gpost_ref[...])
    gate = jax.nn.sigmoid(_dot(_rms(h, ppre_ref[...]).astype(BF16), pg_ref[...]))
    o_ref[...] = h + _rms(gate * _dot(p_ref[...].astype(BF16), pp_ref[...]), ppost_ref[...])


def _post_mixer(h, a, b, p, w, tm):
    m = h.shape[0]
    row = lambda i: (i, 0)
    once = dict(pipeline_mode=pl.Buffered(1))
    const = lambda shape, idx=(0, 0): pl.BlockSpec(shape, lambda i: idx, **once)
    vec = const((1, D_MODEL))
    return pl.pallas_call(
        _post_mixer_kernel,
        grid=(m // tm,),
        in_specs=[pl.BlockSpec((tm, D_MODEL), row), pl.BlockSpec((tm, a.shape[1]), row),
                  pl.BlockSpec((tm, b.shape[1]), row), pl.BlockSpec((tm, PLE_DIM), row),
                  const(w["wo_gla"].shape), const(w["wo_nsa"].shape), vec,
                  vec, const((D_MODEL, D_FF)), const((D_MODEL, D_FF), (0, 1)), const((D_FF, D_MODEL)), vec,
                  vec, const(w["ple_gate"].shape), const(w["ple_proj"].shape), vec],
        out_specs=pl.BlockSpec((tm, D_MODEL), row),
        out_shape=jax.ShapeDtypeStruct((m, D_MODEL), F32),
        compiler_params=_params("parallel"),
        name="post_mixer",
    )(h, a, b, p, w["wo_gla"], w["wo_nsa"], w["m_post"],
      w["f2_pre"], w["f2_in"], w["f2_in"], w["f2_out"], w["f2_post"],
      w["ple_pre"], w["ple_gate"], w["ple_proj"], w["ple_post"])


def _split_in_cols(w):
    outs, off = [], 0
    for n in IN_SPLITS:
        outs.append(w[:, off:off + n])
        off += n
    return outs


def _prep_mixer_weights(w_in, w_a2, b_a):
    q_g, k_g, v_g, r_g, a_lr, q_n, kv_n, gate_n = _split_in_cols(w_in)
    pad_cols = lambda w, n: jnp.pad(w, ((0, 0), (0, n - w.shape[1])))
    gd = G_NSA * D_NSA
    k_sel = kv_n[:, 2 * gd:3 * gd]
    k_win = kv_n[:, 4 * gd:5 * gd]
    wn = jnp.concatenate([q_g, k_g, v_g, r_g, pad_cols(a_lr, LANES), k_sel, k_win], axis=1).astype(BF16)
    wt = jnp.concatenate([q_n, kv_n, pad_cols(gate_n, N_GATE_ROWS)], axis=1).T.astype(BF16)
    wa2 = jnp.pad(w_a2, ((0, LANES - GLA_RANK), (0, 0))).astype(BF16)
    return wn, wt, wa2, b_a.reshape(1, -1)


def _row_tile(m):
    return ROW_TILE if m % ROW_TILE == 0 else m


def _layer(x2, p2, mixer, w):
    tm = _row_tile(x2.shape[0])
    h = _ffn(x2, w["f1_pre"], w["f1_in"], w["f1_out"], w["f1_post"])
    o_gla, o_nsa, extras = mixer(h)
    h = _post_mixer(h, o_gla, o_nsa, p2, w, tm)
    return h, extras


def _mixer_prompt(h, w, nb, t):
    tm = _row_tile(h.shape[0])
    tabs = _rope_tables(np.arange(t))
    (qk, v, r, la, ksel, kwin, qraw_t, qrot_t, gate_t, nsa_t, win_t) = _proj(
        h, w["m_pre"], w["wn"], w["wt"], w["wa2"], w["ba"], tabs, tm, t // tm)
    o_gla, s_fin = _gla_prompt(qk, la, v, r, w["gla_gain"], nb, t)
    table = jnp.zeros((nb, t // ROWS_PER_PAGE), jnp.int32)
    kc, vct = _compress(nsa_t, table, w, False)
    o_nsa = _nsa_prompt(qraw_t, qrot_t, gate_t, kc, vct, ksel, nsa_t, kwin, win_t, nb, t)
    s_t = s_fin.reshape(nb, H_GLA, DV_GLA, 2, DK_GLA)
    s_own = jnp.stack([s_t[:, hh, :, hh % 2, :] for hh in range(H_GLA)], axis=1)
    gla_state = jnp.swapaxes(s_own, -1, -2)
    keep = min(WINDOW, t)
    rows_first = lambda a, n: jnp.transpose(a.reshape(nb, n, G_NSA, D_NSA, a.shape[-1]), (0, 4, 1, 2, 3))
    nsa_rows = rows_first(nsa_t, 4)
    win_rows = rows_first(win_t[:, :, t - keep:], 2)
    return o_gla, o_nsa, (nsa_rows, win_rows, gla_state)


def _mixer_sample(h, w, cache_l, win_l, gla_l, page_table):
    nb = h.shape[0]
    n_pages = page_table.shape[1]
    past_len = n_pages * cache_l.shape[1]
    pos = past_len
    tabs = _rope_tables(np.full((nb,), pos))
    (qk, v, r, la, _, _, qraw_t, qrot_t, gate_t, nsa_new_t, win_new_t) = _proj(
        h, w["m_pre"], w["wn"], w["wt"], w["wa2"], w["ba"], tabs, nb, 1)
    nsa = nsa_new_t[0].T
    win = win_new_t[0].T

    o_gla, gla_state = _gla_step(qk.T, la.T, v, r, gla_l.astype(F32), w["gla_gain"])

    cache_t = jnp.transpose(cache_l.reshape(cache_l.shape[0], ROWS_PER_PAGE, 4 * LANES), (0, 2, 1))
    kc, vct = _compress(cache_t, page_table, w, True)

    hg = (jnp.arange(H_NSA) // R_NSA)[None, :, None]

    def group_pad(q_t):
        q8 = q_t.T.reshape(nb, H_NSA, D_NSA)
        return jnp.concatenate([jnp.where(hg == 0, q8, 0), jnp.where(hg == 1, q8, 0)], axis=-1)

    q2_raw = group_pad(qraw_t)
    q2_rot = group_pad(qrot_t)
    q2_raw_t = jnp.pad(jnp.swapaxes(q2_raw, 1, 2), ((0, 0), (0, 0), (0, LANES - H_NSA)))
    n_blocks = -(-(past_len + 1) // SEL_BLOCK)
    idx_pad, ocmp_t = _nsa_decode_select(q2_raw_t, kc, vct, pos, n_blocks)
    idx = jnp.stack([idx_pad[:, :, g] for g in range(G_NSA)], axis=1)
    ocmp = jnp.swapaxes(ocmp_t, 1, 2)[:, :H_NSA, :]
    gates = jnp.pad(gate_t[:3 * H_NSA].T.reshape(nb, H_NSA, 3), ((0, 0), (0, 0), (0, LANES - 3)))
    wb = win_l.shape[1]
    win_buf_t = jnp.transpose(win_l.reshape(nb, wb, 2 * LANES), (0, 2, 1))
    o8 = _nsa_decode_attend(idx, page_table, cache_t, q2_rot, win_buf_t,
                            nsa.reshape(nb, 1, 4 * LANES), win.reshape(nb, 1, 2 * LANES), gates, ocmp,
                            past_len // SEL_BLOCK)
    o8 = o8.reshape(nb, H_NSA, G_NSA, D_NSA)
    o_nsa = jnp.concatenate([o8[:, :R_NSA, 0], o8[:, R_NSA:, 1]], axis=1).reshape(nb, H_NSA * D_NSA)

    nsa_rows = nsa.reshape(nb, 1, 4, G_NSA, D_NSA)
    win_new = win.reshape(nb, 1, 2, G_NSA, D_NSA)
    kw = jnp.concatenate([win_l, win_new.astype(win_l.dtype)], axis=1)
    keep = min(WINDOW, wb + 1)
    return o_gla, o_nsa, (nsa_rows, kw[:, wb + 1 - keep:], gla_state.astype(gla_l.dtype))


def kernel(x_prompt, x_sample, cache_nsa, state_win, state_gla, page_table, p_prompt, p_sample,
           ffn1_norm_pre, ffn1_norm_post, ffn1_w_in, ffn1_w_out,
           mix_norm_pre, mix_norm_post, w_mix_in, w_gla_a2, b_gla_a, gla_out_norm,
           cmp_pos_k, w_cmp_k1, w_cmp_k2, cmp_pos_v, w_cmp_v1, w_cmp_v2, w_mix_out,
           ffn2_norm_pre, ffn2_norm_post, ffn2_w_in, ffn2_w_out,
           ple_norm_pre, ple_w_gate, ple_w_proj, ple_norm_post):
    nb, t, _ = x_prompt.shape
    ns = x_sample.shape[0]
    depth = ffn1_w_in.shape[0]
    hp = x_prompt.reshape(nb * t, D_MODEL)
    hs = x_sample.reshape(ns, D_MODEL)
    outs = [[] for _ in range(6)]
    for i in range(depth):
        wn, wt, wa2, ba = _prep_mixer_weights(w_mix_in[i], w_gla_a2[i], b_gla_a[i])
        cmp_wk, cmp_posk, cmp_w1k, cmp_w2k, _ = _cmp_weights(cmp_pos_k[i], w_cmp_k1[i], w_cmp_k2[i])
        cmp_wv, cmp_posv, cmp_w1v, _, cmp_w2vt = _cmp_weights(cmp_pos_v[i], w_cmp_v1[i], w_cmp_v2[i])
        gla_w = H_GLA * DV_GLA
        w = dict(
            f1_pre=ffn1_norm_pre[i][None], f1_post=ffn1_norm_post[i][None],
            f1_in=ffn1_w_in[i].astype(BF16), f1_out=ffn1_w_out[i].astype(BF16),
            m_pre=mix_norm_pre[i][None], m_post=mix_norm_post[i][None],
            wn=wn, wt=wt, wa2=wa2, ba=ba, gla_gain=gla_out_norm[i][None],
            cmp_wk=cmp_wk, cmp_posk=cmp_posk, cmp_w1k=cmp_w1k, cmp_w2k=cmp_w2k,
            cmp_wv=cmp_wv, cmp_posv=cmp_posv, cmp_w1v=cmp_w1v, cmp_w2vt=cmp_w2vt,
            wo_gla=w_mix_out[i][:gla_w].astype(BF16), wo_nsa=w_mix_out[i][gla_w:].astype(BF16),
            f2_pre=ffn2_norm_pre[i][None], f2_post=ffn2_norm_post[i][None],
            f2_in=ffn2_w_in[i].astype(BF16), f2_out=ffn2_w_out[i].astype(BF16),
            ple_pre=ple_norm_pre[i][None], ple_post=ple_norm_post[i][None],
            ple_gate=ple_w_gate[i].astype(BF16), ple_proj=ple_w_proj[i].astype(BF16),
        )
        hp, (r_p, w_p, s_p) = _layer(hp, p_prompt[i].reshape(nb * t, PLE_DIM),
                                     functools.partial(_mixer_prompt, w=w, nb=nb, t=t), w)
        hs, (r_s, w_s, s_s) = _layer(hs, p_sample[i].reshape(ns, PLE_DIM),
                                     functools.partial(_mixer_sample, w=w, cache_l=cache_nsa[i], win_l=state_win[i],
                                                       gla_l=state_gla[i], page_table=page_table), w)
        for lst, val in zip(outs, (r_p, w_p, s_p, r_s, w_s, s_s)):
            lst.append(val)
    return (hp.reshape(nb, t, D_MODEL), hs.reshape(ns, 1, D_MODEL), *[jnp.stack(o) for o in outs])
```

```python
import functools

import numpy as np
import jax
import jax.numpy as jnp
from jax import lax
from jax.experimental import pallas as pl
from jax.experimental.pallas import tpu as pltpu

F32 = jnp.float32
BF16 = jnp.bfloat16

D_MODEL = 1024
PLE_DIM = 256
D_FF = 2816
EPS = 1e-6
H_GLA = 4
DK_GLA = 64
DV_GLA = 128
GLA_RANK = 16
GLA_GATE_TEMP = 16.0
GLA_CHUNK = 64
H_NSA = 8
G_NSA = 2
R_NSA = H_NSA // G_NSA
D_NSA = 64
CMP_LEN = 32
CMP_STRIDE = 16
CMP_HIDDEN = 128
SEL_BLOCK = 64
N_SELECT = 16
WINDOW = 512
FORCE_SCORE = 1e4
NEG_INF = -1e30
M_INIT = -1e29
TINY = 1e-30
ATTN_SCALE = D_NSA ** -0.5
LOG2E = 1.4426950408889634
QK_SCALE = ATTN_SCALE * LOG2E
ROPE_THETA = 500000.0
ROPE_DIM = D_NSA // 4
ROPE_HALF = ROPE_DIM // 2
IN_SPLITS = (H_GLA * DK_GLA, H_GLA * DK_GLA, H_GLA * DV_GLA, H_GLA * DV_GLA, GLA_RANK,
             H_NSA * D_NSA, 6 * G_NSA * D_NSA, 3 * H_NSA)

LANES = 128
SUBLANES = 8
VMEM_LIMIT = 56 * 1024 * 1024

ROW_TILE = 512
FF_CHUNK = 256
Q_TILE = 256
K_TILE = 512
V_PAD_ROWS = 16
SEL_COLS = 256
PAGES_PER_STEP = 64
ROWS_PER_PAGE = 128
N_GLA_LEVELS = 6
GLA_CHUNKS_PER_STEP = 8
PAD_BLOCKS = 384
SELECT_SEQS = 4


def _params(*sem):
    return pltpu.CompilerParams(dimension_semantics=sem, vmem_limit_bytes=VMEM_LIMIT)


def _rms(x, g):
    return x * lax.rsqrt(jnp.mean(x * x, axis=-1, keepdims=True) + EPS) * g


def _dot(a, b):
    return jnp.dot(a, b, preferred_element_type=F32)


def _dot_nt(a, b):
    return lax.dot_general(a, b, (((1,), (1,)), ((), ())), preferred_element_type=F32)


def _dot_tn(a, b):
    return lax.dot_general(a, b, (((0,), (0,)), ((), ())), preferred_element_type=F32)


def _split_bf16(x, n):
    parts = []
    r = x
    for _ in range(n):
        p = r.astype(BF16)
        parts.append(p)
        r = r - p.astype(F32)
    return parts


def _dot01_left(m01, x, n):
    out = None
    for p in _split_bf16(x, n):
        t = _dot(m01, p)
        out = t if out is None else out + t
    return out


def _dot01_right(x, m01, n):
    out = None
    for p in _split_bf16(x, n):
        t = _dot(p, m01)
        out = t if out is None else out + t
    return out


def _softmax2_rows(sm):
    m = jnp.maximum(jnp.max(sm, axis=0, keepdims=True), M_INIT)
    e = jnp.exp2(sm - m)
    return e / jnp.maximum(jnp.sum(e, axis=0, keepdims=True), TINY)


def _ffn_kernel(x_ref, gpre_ref, wg_ref, wu_ref, wo_ref, gpost_ref, o_ref):
    x = x_ref[...]
    xn = _rms(x, gpre_ref[...]).astype(BF16)
    acc = jnp.zeros(x.shape, F32)
    for c in range(D_FF // FF_CHUNK):
        sl = slice(c * FF_CHUNK, (c + 1) * FF_CHUNK)
        g = _dot(xn, wg_ref[:, sl])
        u = _dot(xn, wu_ref[:, sl])
        a = (jax.nn.silu(g) * u).astype(BF16)
        acc = acc + _dot(a, wo_ref[sl, :])
    o_ref[...] = x + 0.5 * _rms(acc, gpost_ref[...])


def _ffn(x, gpre, w_in, w_out, gpost):
    m = x.shape[0]
    tm = _row_tile(m)
    once = dict(pipeline_mode=pl.Buffered(1))
    const = lambda i: (0, 0)
    return pl.pallas_call(
        _ffn_kernel,
        grid=(m // tm,),
        in_specs=[
            pl.BlockSpec((tm, D_MODEL), lambda i: (i, 0)),
            pl.BlockSpec((1, D_MODEL), const, **once),
            pl.BlockSpec((D_MODEL, D_FF), const, **once),
            pl.BlockSpec((D_MODEL, D_FF), lambda i: (0, 1), **once),
            pl.BlockSpec((D_FF, D_MODEL), const, **once),
            pl.BlockSpec((1, D_MODEL), const, **once),
        ],
        out_specs=pl.BlockSpec((tm, D_MODEL), lambda i: (i, 0)),
        out_shape=jax.ShapeDtypeStruct((m, D_MODEL), F32),
        compiler_params=_params("parallel"),
        name="ffn",
    )(x, gpre, w_in, w_in, w_out, gpost)


N_GLA_COLS = 2 * H_GLA * DK_GLA + 2 * H_GLA * DV_GLA
N_KV_COLS = 6 * G_NSA * D_NSA
WN_COLS = N_GLA_COLS + 3 * LANES
N_Q_COLS = H_NSA * D_NSA
N_GATE_ROWS = 32
WT_ROWS = N_Q_COLS + N_KV_COLS + N_GATE_ROWS
N_CACHE_FEATS = 4 * G_NSA * D_NSA
N_WIN_FEATS = 2 * G_NSA * D_NSA


def _rope_rows(x, cos, sin):
    out = []
    for h in range(x.shape[0] // D_NSA):
        b = h * D_NSA
        x1 = x[b:b + ROPE_HALF]
        x2 = x[b + ROPE_HALF:b + ROPE_DIM]
        out += [x1 * cos - x2 * sin, x2 * cos + x1 * sin, x[b + ROPE_DIM:b + D_NSA]]
    return jnp.concatenate(out, axis=0)


def _rope_lanes(x, c, s1, s2):
    return x * c + pltpu.roll(x, LANES - ROPE_HALF, 1) * s1 + pltpu.roll(x, ROPE_HALF, 1) * s2


def _proj_kernel(h_ref, g_ref, wn_ref, wt_ref, wa2_ref, ba_ref, rc_ref, rs1_ref, rs2_ref, cos_ref, sin_ref,
                 qk_ref, v_ref, r_ref, la_ref, ksel_ref, kwin_ref,
                 qraw_ref, qrot_ref, gate_ref, nsat_ref, wint_ref):
    xn = _rms(h_ref[...], g_ref[...]).astype(BF16)
    nqk = 2 * H_GLA * DK_GLA
    nv = H_GLA * DV_GLA
    a_lr = _dot(xn, wn_ref[:, N_GLA_COLS:N_GLA_COLS + LANES])
    z = _dot(xn, wn_ref[:, 0:nqk])
    qk_ref[:, 0:nqk // 2] = z[:, 0:nqk // 2] * (DK_GLA ** -0.5)
    qk_ref[:, nqk // 2:nqk] = z[:, nqk // 2:nqk]
    v_ref[...] = _dot(xn, wn_ref[:, nqk:nqk + nv])
    r_ref[...] = _dot(xn, wn_ref[:, nqk + nv:N_GLA_COLS])
    k0 = N_GLA_COLS + LANES
    rc, rs1, rs2 = rc_ref[...], rs1_ref[...], rs2_ref[...]
    ksel_ref[...] = _rope_lanes(_dot(xn, wn_ref[:, k0:k0 + LANES]), rc, rs1, rs2)
    kwin_ref[...] = _rope_lanes(_dot(xn, wn_ref[:, k0 + LANES:k0 + 2 * LANES]), rc, rs1, rs2)
    zt = _dot_nt(wt_ref[...], xn)
    cos, sin = cos_ref[...], sin_ref[...]
    q = zt[0:N_Q_COLS] * QK_SCALE
    qraw_ref[...] = q.astype(BF16)
    qrot_ref[...] = _rope_rows(q, cos, sin).astype(BF16)
    kv = zt[N_Q_COLS:N_Q_COLS + N_KV_COLS]
    nsat_ref[0, 0:256, :] = kv[0:256]
    nsat_ref[0, 256:384, :] = _rope_rows(kv[256:384], cos, sin)
    nsat_ref[0, 384:512, :] = kv[384:512]
    wint_ref[0, 0:128, :] = _rope_rows(kv[512:640], cos, sin)
    wint_ref[0, 128:256, :] = kv[640:768]
    gate_ref[...] = jax.nn.sigmoid(zt[N_Q_COLS + N_KV_COLS:WT_ROWS])
    xa = _dot(a_lr.astype(BF16), wa2_ref[...]) + ba_ref[...]
    la_ref[...] = (jnp.minimum(xa, 0.0) - jnp.log1p(jnp.exp(-jnp.abs(xa)))) * (1.0 / GLA_GATE_TEMP)


def _proj(h, gain, wn, wt, wa2, ba, tabs, tm, tiles_per_seq):
    m = h.shape[0]
    rc, rs1, rs2, cos_t, sin_t = tabs
    const = lambda i: (0, 0)
    row = lambda i: (i, 0)
    col = lambda i: (0, i)
    tab_row = lambda i: (i % tiles_per_seq, 0)
    tab_col = lambda i: (0, i % tiles_per_seq)
    outs = [
        (2 * H_GLA * DK_GLA, F32), (H_GLA * DV_GLA, F32), (H_GLA * DV_GLA, F32), (H_GLA * DK_GLA, F32),
        (LANES, F32), (LANES, F32),
    ]
    outs_t = [(N_Q_COLS, BF16), (N_Q_COLS, BF16), (N_GATE_ROWS, F32)]
    nseq = m // (tm * tiles_per_seq)
    seq_len = tm * tiles_per_seq
    outs_seq = [N_CACHE_FEATS, N_WIN_FEATS]
    seq_map = lambda i: (i // tiles_per_seq, 0, i % tiles_per_seq)
    return pl.pallas_call(
        _proj_kernel,
        grid=(m // tm,),
        in_specs=[
            pl.BlockSpec((tm, D_MODEL), row),
            pl.BlockSpec((1, D_MODEL), const),
            pl.BlockSpec((D_MODEL, WN_COLS), const),
            pl.BlockSpec((WT_ROWS, D_MODEL), const),
            pl.BlockSpec((LANES, H_GLA * DK_GLA), const),
            pl.BlockSpec((1, H_GLA * DK_GLA), const),
            pl.BlockSpec((tm, LANES), tab_row),
            pl.BlockSpec((tm, LANES), tab_row),
            pl.BlockSpec((tm, LANES), tab_row),
            pl.BlockSpec((ROPE_HALF, tm), tab_col),
            pl.BlockSpec((ROPE_HALF, tm), tab_col),
        ],
        out_specs=[pl.BlockSpec((tm, n), row) for n, _ in outs] + [pl.BlockSpec((n, tm), col) for n, _ in outs_t]
        + [pl.BlockSpec((1, n, tm), seq_map) for n in outs_seq],
        out_shape=[jax.ShapeDtypeStruct((m, n), d) for n, d in outs]
        + [jax.ShapeDtypeStruct((n, m), d) for n, d in outs_t]
        + [jax.ShapeDtypeStruct((nseq, n, seq_len), F32) for n in outs_seq],
        compiler_params=_params("parallel"),
        name="mixer_proj",
    )(h, gain, wn, wt, wa2, ba, rc, rs1, rs2, cos_t, sin_t)


def _rope_tables(pos):
    pos = np.asarray(pos, np.float64)
    inv_freq = ROPE_THETA ** (-np.arange(ROPE_HALF, dtype=np.float64) * 2.0 / ROPE_DIM)
    ang = pos[:, None] * inv_freq[None, :]
    cos, sin = np.cos(ang), np.sin(ang)
    n = pos.shape[0]
    one = np.ones((n, D_NSA - ROPE_DIM))
    zero = np.zeros((n, D_NSA - ROPE_DIM))
    zh = np.zeros((n, ROPE_HALF))
    c = np.concatenate([cos, cos, one], axis=1)
    s1 = np.concatenate([-sin, zh, zero], axis=1)
    s2 = np.concatenate([zh, sin, zero], axis=1)
    dup = lambda t: np.concatenate([t, t], axis=1)
    return tuple(jnp.asarray(a, F32) for a in (dup(c), dup(s1), dup(s2), cos.T, sin.T))


def _gla_constants():
    c = GLA_CHUNK
    t = np.arange(c)
    low = (t[None, :] <= t[:, None]).astype(np.float32)
    mats, masks = [low], []
    for lev in range(1, N_GLA_LEVELS + 1):
        seg = (2 * c) >> lev
        half = seg // 2
        mid = (t // seg) * seg + half
        mats.append(low[mid])
        same = (t[:, None] // seg) == (t[None, :] // seg)
        masks.append((same & ((t[:, None] % seg) >= half) & ((t[None, :] % seg) < half)).astype(np.float32))
    masks.append(np.eye(c, dtype=np.float32))
    return np.concatenate(mats, axis=0), np.stack(masks)


def _gla_kernel(qk_ref, la_ref, v_ref, r_ref, gain_ref, big_ref, mask_ref, o_ref, sfin_ref, st_ref):
    c = pl.program_id(1)
    ch = GLA_CHUNK

    @pl.when(c == 0)
    def _():
        st_ref[...] = jnp.zeros(st_ref.shape, F32)

    lane = lax.broadcasted_iota(jnp.int32, (1, LANES), 1)
    head_mask = [jnp.where(lane < DK_GLA, 1.0, 0.0), jnp.where(lane >= DK_GLA, 1.0, 0.0)]
    big = big_ref[...]
    gain = gain_ref[...]
    nq = H_GLA * DK_GLA
    n_chunks = qk_ref.shape[0] // ch
    units = [(ci, p) for ci in range(n_chunks) for p in range(H_GLA // 2)]

    allbs = {}
    for ci, p in units:
        rows = slice(ci * ch, (ci + 1) * ch)
        allbs[ci, p] = _dot01_left(big, la_ref[rows, p * LANES:(p + 1) * LANES], 3)
    intra = {}
    for ci, p in units:
        rows = slice(ci * ch, (ci + 1) * ch)
        q = qk_ref[rows, p * LANES:(p + 1) * LANES]
        k = qk_ref[rows, nq + p * LANES:nq + (p + 1) * LANES]
        allb = allbs[ci, p]
        b = allb[0:ch]
        b_last = b[ch - 1:ch]
        attn = [jnp.zeros((ch, ch), F32), jnp.zeros((ch, ch), F32)]
        for lev in range(N_GLA_LEVELS + 1):
            if lev < N_GLA_LEVELS:
                ref = allb[(lev + 1) * ch:(lev + 2) * ch]
                ql = q * jnp.exp(jnp.minimum(b - ref, 0.0))
                kl = k * jnp.exp(jnp.minimum(ref - b, 0.0))
            else:
                ql, kl = q, k
            qq = jnp.concatenate([ql * head_mask[0], ql * head_mask[1]], axis=0).astype(BF16)
            s = _dot_nt(qq, kl.astype(BF16))
            mk = mask_ref[lev]
            attn[0] = attn[0] + mk * s[0:ch]
            attn[1] = attn[1] + mk * s[ch:2 * ch]
        q0 = q * jnp.exp(b)
        k_hat = (k * jnp.exp(b_last - b)).astype(BF16)
        for hh in range(2):
            hs = slice((2 * p + hh) * DV_GLA, (2 * p + hh + 1) * DV_GLA)
            vh = v_ref[rows, hs].astype(BF16)
            intra[ci, 2 * p + hh] = (_dot(attn[hh].astype(BF16), vh), (q0 * head_mask[hh]).astype(BF16),
                                     _dot_tn(vh, k_hat), jnp.exp(b_last))
    for ci in range(n_chunks):
        rows = slice(ci * ch, (ci + 1) * ch)
        for h in range(H_GLA):
            hs = slice(h * DV_GLA, (h + 1) * DV_GLA)
            o_intra, q0h, kv, decay = intra[ci, h]
            st = st_ref[h]
            o = o_intra + _dot_nt(q0h, st.astype(BF16))
            st_ref[h] = st * decay + kv
            o_ref[rows, hs] = _rms(o, gain) * jax.nn.silu(r_ref[rows, hs])

    @pl.when(c == pl.num_programs(1) - 1)
    def _():
        sfin_ref[0] = st_ref[...]


def _gla_prompt(qk, la, v, r, gain, nb, t):
    big, masks = _gla_constants()
    rows = GLA_CHUNK * GLA_CHUNKS_PER_STEP
    nc = t // rows
    row = lambda b, c: (b * nc + c, 0)
    const2 = lambda b, c: (0, 0)
    m = nb * t
    return pl.pallas_call(
        _gla_kernel,
        grid=(nb, nc),
        in_specs=[
            pl.BlockSpec((rows, 2 * H_GLA * DK_GLA), row),
            pl.BlockSpec((rows, H_GLA * DK_GLA), row),
            pl.BlockSpec((rows, H_GLA * DV_GLA), row),
            pl.BlockSpec((rows, H_GLA * DV_GLA), row),
            pl.BlockSpec((1, DV_GLA), const2),
            pl.BlockSpec(big.shape, const2),
            pl.BlockSpec(masks.shape, lambda b, c: (0, 0, 0)),
        ],
        out_specs=[
            pl.BlockSpec((rows, H_GLA * DV_GLA), row),
            pl.BlockSpec((1, H_GLA, DV_GLA, LANES), lambda b, c: (b, 0, 0, 0)),
        ],
        out_shape=[
            jax.ShapeDtypeStruct((m, H_GLA * DV_GLA), F32),
            jax.ShapeDtypeStruct((nb, H_GLA, DV_GLA, LANES), F32),
        ],
        scratch_shapes=[pltpu.VMEM((H_GLA, DV_GLA, LANES), F32)],
        compiler_params=_params("parallel", "arbitrary"),
        name="gla_scan",
    )(qk, la, v, r, gain, jnp.asarray(big, BF16), jnp.asarray(masks, F32))


def _gla_step_kernel(qk_t_ref, la_t_ref, v_ref, r_ref, s_ref, gain_ref, o_ref, sn_ref):
    nq = H_GLA * DK_GLA
    for b in range(s_ref.shape[0]):
        for h in range(H_GLA):
            ks = slice(h * DK_GLA, (h + 1) * DK_GLA)
            vs = slice(h * DV_GLA, (h + 1) * DV_GLA)
            q = qk_t_ref[ks, b:b + 1]
            k = qk_t_ref[nq + h * DK_GLA:nq + (h + 1) * DK_GLA, b:b + 1]
            s_new = jnp.exp(la_t_ref[ks, b:b + 1]) * s_ref[b, h] + k * v_ref[b:b + 1, vs]
            sn_ref[b, h] = s_new
            o = jnp.sum(q * s_new, axis=0, keepdims=True)
            o_ref[b:b + 1, vs] = _rms(o, gain_ref[...]) * jax.nn.silu(r_ref[b:b + 1, vs])


def _gla_step(qk_t, la_t, v, r, state, gain):
    nb = state.shape[0]
    full = lambda a: pl.BlockSpec(a.shape, functools.partial(lambda i, nd: (0,) * nd, nd=a.ndim))
    return pl.pallas_call(
        _gla_step_kernel,
        grid=(1,),
        in_specs=[full(qk_t), full(la_t), full(v), full(r), full(state), full(gain)],
        out_specs=[full(v), full(state)],
        out_shape=[jax.ShapeDtypeStruct((nb, H_GLA * DV_GLA), F32),
                   jax.ShapeDtypeStruct((nb, H_GLA, DK_GLA, DV_GLA), F32)],
        compiler_params=_params("arbitrary"),
        name="gla_step",
    )(qk_t, la_t, v, r, state, gain)


HALF_ROWS = CMP_STRIDE
CHUNKS_PER_PAGE = ROWS_PER_PAGE // HALF_ROWS


def _compress_kernel(tbl_ref, pages_hbm, wk_ref, wv_ref, posk_ref, w1k_ref, w2k_ref, posv_ref, w1v_ref, w2vt_ref,
                     kc_ref, vct_ref, buf_ref, sem, tk_ref, tv_ref, abk_ref, abv_ref, *, pps, paged):
    b = pl.program_id(0)
    s = pl.program_id(1)
    nsteps = pl.num_programs(1)
    step = b * nsteps + s
    slot = step % 2
    step_chunks = pps * CHUNKS_PER_PAGE

    def page_copy(seq, grp, i, slot_):
        if paged:
            src = pages_hbm.at[tbl_ref[seq, grp * pps + i], pl.ds(0, 2 * LANES), :]
        else:
            row0 = pl.multiple_of((grp * pps + i) * ROWS_PER_PAGE, ROWS_PER_PAGE)
            src = pages_hbm.at[seq, pl.ds(0, 2 * LANES), pl.ds(row0, ROWS_PER_PAGE)]
        return pltpu.make_async_copy(src, buf_ref.at[slot_, i], sem.at[slot_])

    @pl.when(step == 0)
    def _():
        for i in range(pps):
            page_copy(b, s, i, slot).start()

    @pl.when(step + 1 < pl.num_programs(0) * nsteps)
    def _():
        nxt = step + 1
        for i in range(pps):
            page_copy(nxt // nsteps, nxt % nsteps, i, 1 - slot).start()

    for i in range(pps):
        page_copy(b, s, i, slot).wait()

    for i in range(pps):
        tk_ref[i] = buf_ref[slot, i, 0:LANES, :].T
        tv_ref[i] = buf_ref[slot, i, LANES:2 * LANES, :].T
    lpair = 2 * LANES
    row0 = pl.multiple_of(s * step_chunks, step_chunks)
    for t_ref, w_ref, ab_ref in ((tk_ref, wk_ref, abk_ref), (tv_ref, wv_ref, abv_ref)):
        acc = None
        for lp in range(HALF_ROWS // 2):
            cols = [jnp.concatenate([t_ref[i, pl.ds(l, CHUNKS_PER_PAGE, stride=HALF_ROWS), :] for i in range(pps)], axis=0)
                    for l in (2 * lp, 2 * lp + 1)]
            x = jnp.concatenate(cols, axis=1).astype(BF16)
            part = _dot(x, w_ref[lp * lpair:(lp + 1) * lpair, :])
            acc = part if acc is None else acc + part
        ab_ref[pl.ds(row0, step_chunks), :] = acc

    @pl.when(s == pl.num_programs(1) - 1)
    def _():
        nch = abk_ref.shape[0]
        pk = _dot(posk_ref[...], w1k_ref[...])[0:1]
        pv = _dot(posv_ref[...], w1v_ref[...])[0:1]
        abk = abk_ref[...]
        abv = abv_ref[...]
        kc = None
        for g in range(G_NSA):
            o = g * 2 * CMP_HIDDEN
            hk = jax.nn.gelu(abk[:, o:o + CMP_HIDDEN] + pltpu.roll(abk[:, o + CMP_HIDDEN:o + 2 * CMP_HIDDEN], nch - 1, 0) + pk)
            hv = jax.nn.gelu(abv[:, o:o + CMP_HIDDEN] + pltpu.roll(abv[:, o + CMP_HIDDEN:o + 2 * CMP_HIDDEN], nch - 1, 0) + pv)
            tk = _dot(hk.astype(BF16), w2k_ref[g])
            kc = tk if kc is None else kc + tk
            vct_ref[0, g * D_NSA:(g + 1) * D_NSA, :] = _dot_nt(w2vt_ref[...], hv.astype(BF16)).astype(BF16)
        kc_ref[0] = kc.astype(BF16)


def _compress(pages_t, table, w, paged):
    nb, npages = table.shape
    pps = min(PAGES_PER_STEP, npages)
    nsteps = npages // pps
    nch = npages * CHUNKS_PER_PAGE

    names = ("cmp_wk", "cmp_wv", "cmp_posk", "cmp_w1k", "cmp_w2k", "cmp_posv", "cmp_w1v", "cmp_w2vt")
    consts = [w[n] for n in names]
    const_specs = [pl.BlockSpec(c.shape, functools.partial(lambda b, s, tbl, nd: (0,) * nd, nd=c.ndim)) for c in consts]
    grid_spec = pltpu.PrefetchScalarGridSpec(
        num_scalar_prefetch=1,
        grid=(nb, nsteps),
        in_specs=[pl.BlockSpec(memory_space=pl.ANY)] + const_specs,
        out_specs=[pl.BlockSpec((1, nch, LANES), lambda b, s, tbl: (b, 0, 0)),
                   pl.BlockSpec((1, LANES, nch), lambda b, s, tbl: (b, 0, 0))],
        scratch_shapes=[pltpu.VMEM((2, pps, 2 * LANES, ROWS_PER_PAGE), F32), pltpu.SemaphoreType.DMA((2,)),
                        pltpu.VMEM((pps, ROWS_PER_PAGE, LANES), F32), pltpu.VMEM((pps, ROWS_PER_PAGE, LANES), F32),
                        pltpu.VMEM((nch, 4 * CMP_HIDDEN), F32), pltpu.VMEM((nch, 4 * CMP_HIDDEN), F32)],
    )
    return pl.pallas_call(
        functools.partial(_compress_kernel, pps=pps, paged=paged),
        grid_spec=grid_spec,
        out_shape=[jax.ShapeDtypeStruct((nb, nch, LANES), BF16),
                   jax.ShapeDtypeStruct((nb, LANES, nch), BF16)],
        compiler_params=_params("arbitrary", "arbitrary"),
        name="compress",
    )(table, pages_t, *consts)


def _cmp_weights(pos, w1, w2):
    w = w1.reshape(2, HALF_ROWS, D_NSA, CMP_HIDDEN)
    z = jnp.zeros_like(w[0])
    blocks = []
    for g in range(G_NSA):
        cols = []
        for g2 in range(G_NSA):
            for half in range(2):
                cols.append(w[half] if g2 == g else z)
        blocks.append(jnp.concatenate(cols, axis=-1))
    wbig = jnp.stack(blocks, axis=1).reshape(HALF_ROWS * LANES, 4 * CMP_HIDDEN).astype(BF16)
    pos8 = jnp.broadcast_to(pos.reshape(1, CMP_LEN * D_NSA), (SUBLANES, CMP_LEN * D_NSA)).astype(BF16)
    z2 = jnp.zeros_like(w2)
    w2pad = jnp.stack([jnp.concatenate([w2, z2], axis=1), jnp.concatenate([z2, w2], axis=1)]).astype(BF16)
    return wbig, pos8, w1.astype(BF16), w2pad, w2.T.astype(BF16)


def _overlap_t(nc_pad, ns_pad, nc, ns):
    i = np.arange(nc_pad)[None, :] * CMP_STRIDE
    j = np.arange(ns_pad)[:, None] * SEL_BLOCK
    m = (i < j + SEL_BLOCK) & (i + CMP_LEN > j) & (np.arange(nc_pad)[None, :] < nc) & (np.arange(ns_pad)[:, None] < ns)
    return jnp.asarray(m.astype(np.float32), BF16)


def _rank_desc(x):
    nrow = x.shape[0]
    nblk = nrow // SUBLANES
    blocks = [x[v * SUBLANES:(v + 1) * SUBLANES] for v in range(nblk)]
    cnt = [jnp.zeros((SUBLANES, x.shape[1]), F32) for _ in range(nblk)]
    sub = lax.broadcasted_iota(jnp.int32, (SUBLANES, x.shape[1]), 0)
    for jp in range(nrow):
        row = x[jp:jp + 1]
        vb = jp // SUBLANES
        for v in range(nblk):
            ge = jnp.where(row >= blocks[v], 1.0, 0.0)
            gt = jnp.where(row > blocks[v], 1.0, 0.0)
            if v > vb:
                cnt[v] = cnt[v] + ge
            elif v < vb:
                cnt[v] = cnt[v] + gt
            else:
                cnt[v] = cnt[v] + jnp.where(sub > (jp % SUBLANES), ge, gt)
    return jnp.concatenate(cnt, axis=0)


def _nsa_prompt_kernel(qraw_ref, qrot_ref, gate_ref, kc_ref, vct_ref, ksel_ref, vselt_ref, kwin_ref, vwint_ref,
                       ovl_ref, o_ref, kaug_ref, kwb_ref, vsa_ref, vwa_ref):
    qb = pl.program_id(1)
    nlane = R_NSA * Q_TILE
    tpos = qb * Q_TILE + lax.broadcasted_iota(jnp.int32, (1, nlane), 1) % Q_TILE
    tpos1 = tpos[:, 0:Q_TILE]
    ncp = kc_ref.shape[1]
    nsb = ovl_ref.shape[0]
    seq = ksel_ref.shape[0]
    zeros_q = jnp.zeros((D_NSA, nlane), BF16)

    @pl.when(qb == 0)
    def _():
        ks = ksel_ref[...]
        lane = lax.broadcasted_iota(jnp.int32, (1, LANES), 1)
        blk = lax.broadcasted_iota(jnp.int32, (seq, 1), 0) // SEL_BLOCK
        kaug_ref[0] = jnp.where(lane < D_NSA, ks, jnp.where(lane - D_NSA == blk, 1.0, 0.0)).astype(BF16)
        kaug_ref[1] = jnp.where(lane >= D_NSA, ks, jnp.where(lane == blk, 1.0, 0.0)).astype(BF16)
        kwb_ref[...] = kwin_ref[...].astype(BF16)
        ones_row = jnp.where(lax.broadcasted_iota(jnp.int32, (V_PAD_ROWS, seq), 0) == 0, 1.0, 0.0)
        for g in range(G_NSA):
            gs = slice(g * D_NSA, (g + 1) * D_NSA)
            vsa_ref[g] = jnp.concatenate([vselt_ref[0, gs, :], ones_row], axis=0).astype(BF16)
            vwa_ref[g] = jnp.concatenate([vwint_ref[0, gs, :], ones_row], axis=0).astype(BF16)

    def group_q(ref, g):
        q = jnp.concatenate([ref[h * D_NSA:(h + 1) * D_NSA, :] for h in range(g * R_NSA, (g + 1) * R_NSA)], axis=1)
        return q, (jnp.concatenate([q, zeros_q], axis=0) if g == 0 else jnp.concatenate([zeros_q, q], axis=0))

    cmp_scores = [_dot(kc_ref[0], group_q(qraw_ref, g)[1]) for g in range(G_NSA)]
    o_cmps, qaugs, qpads = [], [], []
    for g in range(G_NSA):
        qrot, qpad = group_q(qrot_ref, g)
        gs = slice(g * D_NSA, (g + 1) * D_NSA)

        sc = cmp_scores[g]
        ci = lax.broadcasted_iota(jnp.int32, (ncp, 1), 0)
        ended = ci * CMP_STRIDE + (CMP_LEN - 1) <= tpos1
        p_cmp = _softmax2_rows(jnp.concatenate(
            [jnp.where(ended, sc[:, r * Q_TILE:(r + 1) * Q_TILE], NEG_INF) for r in range(R_NSA)], axis=1))
        o_cmp = _dot(vct_ref[0, gs, :], p_cmp.astype(BF16))
        psum = p_cmp[:, 0:Q_TILE]
        for r in range(1, R_NSA):
            psum = psum + p_cmp[:, r * Q_TILE:(r + 1) * Q_TILE]
        imp = _dot01_left(ovl_ref[...], psum, 2)
        bj = lax.broadcasted_iota(jnp.int32, (nsb, 1), 0)
        cur = tpos1 // SEL_BLOCK
        forced = (bj == 0) | (bj == cur) | (bj == cur - 1)
        imp = jnp.where(forced, FORCE_SCORE, jnp.where(bj * SEL_BLOCK <= tpos1, imp, -1.0))
        bias = jnp.where(_rank_desc(imp) < N_SELECT, 0.0, NEG_INF).astype(BF16)
        bias = jnp.concatenate([bias] * R_NSA, axis=1)
        if nsb < D_NSA:
            bias = jnp.concatenate([bias, jnp.zeros((D_NSA - nsb, nlane), BF16)], axis=0)
        o_cmps.append(o_cmp)
        qaugs.append(jnp.concatenate([qrot, bias], axis=0) if g == 0 else jnp.concatenate([bias, qrot], axis=0))
        qpads.append(qpad)

    ncol = nlane // SEL_COLS
    pairs = [(g, slice(j * SEL_COLS, (j + 1) * SEL_COLS)) for g in range(G_NSA) for j in range(ncol)]

    krow = lax.broadcasted_iota(jnp.int32, (K_TILE, 1), 0)

    def sel_step(c, carry, causal):
        start = pl.multiple_of(c * K_TILE, K_TILE)
        out = []
        scores = [_dot(kaug_ref[g, pl.ds(start, K_TILE), :], qaugs[g][:, cols]) for g, cols in pairs]
        for (g, cols), (m, acc), s in zip(pairs, carry, scores):
            if causal:
                s = jnp.where(start + krow <= tpos[:, cols], s, NEG_INF)
            m_new = jnp.maximum(m, jnp.max(s, axis=0, keepdims=True))
            e = jnp.exp2(s - m_new).astype(BF16)
            acc = jnp.exp2(m - m_new) * acc + _dot(vsa_ref[g, :, pl.ds(start, K_TILE)], e)
            out.append((m_new, acc))
        return tuple(out)

    n_full = (qb * Q_TILE) // K_TILE
    init = tuple((jnp.full((1, SEL_COLS), M_INIT, F32), jnp.zeros((D_NSA + V_PAD_ROWS, SEL_COLS), F32))
                 for _ in pairs)
    carry = lax.fori_loop(0, n_full, functools.partial(sel_step, causal=False), init)

    band = WINDOW + Q_TILE
    wstart = pl.multiple_of(jnp.maximum(qb * Q_TILE - WINDOW, 0), Q_TILE)
    win_scores = [_dot(kwb_ref[pl.ds(wstart, band), :], qpads[g]) for g in range(G_NSA)]

    carry = sel_step(n_full, carry, True)
    o_sels = []
    for g in range(G_NSA):
        acc_sel = jnp.concatenate([carry[g * ncol + j][1] for j in range(ncol)], axis=1)
        o_sels.append(acc_sel[0:D_NSA] / acc_sel[D_NSA:D_NSA + 1])

    brow = lax.broadcasted_iota(jnp.int32, (band, 1), 0)
    diff = tpos1 - (wstart + brow)
    age_limit = jnp.where(brow < Q_TILE, WINDOW, band + WINDOW)
    in_window = (diff >= 0) & (diff < age_limit)
    heads = []
    for g in range(G_NSA):
        sw = jnp.concatenate([jnp.where(in_window, win_scores[g][:, r * Q_TILE:(r + 1) * Q_TILE], NEG_INF)
                              for r in range(R_NSA)], axis=1)
        e_win = jnp.exp2(sw - jnp.max(sw, axis=0, keepdims=True)).astype(BF16)
        acc_win = _dot(vwa_ref[g, :, pl.ds(wstart, band)], e_win)
        o_win = acc_win[0:D_NSA] / acc_win[D_NSA:D_NSA + 1]
        for r in range(R_NSA):
            h = g * R_NSA + r
            ls = slice(r * Q_TILE, (r + 1) * Q_TILE)
            heads.append(gate_ref[3 * h:3 * h + 1, :] * o_cmps[g][:, ls]
                         + gate_ref[3 * h + 1:3 * h + 2, :] * o_sels[g][:, ls]
                         + gate_ref[3 * h + 2:3 * h + 3, :] * o_win[:, ls])
    o_ref[...] = jnp.concatenate(heads, axis=0).T


def _nsa_prompt(qraw_t, qrot_t, gate_t, kc, vct, ksel, nsa_t, kwin, win_t, nb, t):
    nqb = t // Q_TILE
    ncp = kc.shape[1]
    ovl = _overlap_t(ncp, t // SEL_BLOCK, (t - CMP_LEN) // CMP_STRIDE + 1, t // SEL_BLOCK)
    qcol = lambda b, q: (0, b * nqb + q)
    m = nb * t
    return pl.pallas_call(
        _nsa_prompt_kernel,
        grid=(nb, nqb),
        in_specs=[
            pl.BlockSpec((N_Q_COLS, Q_TILE), qcol),
            pl.BlockSpec((N_Q_COLS, Q_TILE), qcol),
            pl.BlockSpec((N_GATE_ROWS, Q_TILE), qcol),
            pl.BlockSpec((1, ncp, LANES), lambda b, q: (b, 0, 0)),
            pl.BlockSpec((1, LANES, ncp), lambda b, q: (b, 0, 0)),
            pl.BlockSpec((t, LANES), lambda b, q: (b, 0)),
            pl.BlockSpec((1, LANES, t), lambda b, q: (b, 3, 0)),
            pl.BlockSpec((t, LANES), lambda b, q: (b, 0)),
            pl.BlockSpec((1, LANES, t), lambda b, q: (b, 1, 0)),
            pl.BlockSpec(ovl.shape, lambda b, q: (0, 0)),
        ],
        out_specs=pl.BlockSpec((Q_TILE, N_Q_COLS), lambda b, q: (b * nqb + q, 0)),
        out_shape=jax.ShapeDtypeStruct((m, N_Q_COLS), F32),
        scratch_shapes=[pltpu.VMEM((G_NSA, t, LANES), BF16), pltpu.VMEM((t, LANES), BF16),
                        pltpu.VMEM((G_NSA, D_NSA + V_PAD_ROWS, t), BF16),
                        pltpu.VMEM((G_NSA, D_NSA + V_PAD_ROWS, t), BF16)],
        compiler_params=_params("parallel", "arbitrary"),
        name="nsa_prompt",
    )(qraw_t, qrot_t, gate_t, kc, vct, ksel, nsa_t, kwin, win_t, ovl)


def _nsa_decode_select_kernel(qt_ref, kc_ref, vct_ref, ovl_ref, grp_ref, idx_ref, ocmp_ref, *, pos, n_blocks):
    nseq, ncp, _ = kc_ref.shape
    seqs = range(nseq)
    ci = lax.broadcasted_iota(jnp.int32, (ncp, 1), 0)
    ended = ci * CMP_STRIDE + (CMP_LEN - 1) <= pos
    scores = [_dot(kc_ref[i], qt_ref[i]) for i in seqs]
    probs = [_softmax2_rows(jnp.where(ended, sc, NEG_INF)) for sc in scores]
    for i in seqs:
        ocmp_ref[i] = _dot(vct_ref[i], probs[i].astype(BF16))
    psums = [_dot01_right(p, grp_ref[...], 2) for p in probs]
    imps = [_dot01_left(ovl_ref[...], ps, 2) for ps in psums]
    for i in seqs:
        idx_ref[i] = _decode_topk(imps[i], pos, n_blocks)


def _decode_topk(imp_t, pos, n_blocks):
    imp_r = imp_t.T
    cur = pos // SEL_BLOCK

    def finish(v, j):
        forced = (j == 0) | (j == cur) | (j == cur - 1)
        v = jnp.where(forced, FORCE_SCORE, jnp.where(j * SEL_BLOCK <= pos, v, -1.0))
        return jnp.where(j < n_blocks, v, -3e38)

    jc = lax.broadcasted_iota(jnp.int32, (PAD_BLOCKS, 1), 0)
    jr = lax.broadcasted_iota(jnp.int32, (1, PAD_BLOCKS), 1)
    kk = lax.broadcasted_iota(jnp.int32, (N_SELECT, 1), 0).astype(F32)
    lane = lax.broadcasted_iota(jnp.int32, (N_SELECT, LANES), 1)
    out = jnp.zeros((N_SELECT, LANES), F32)
    for g in range(G_NSA):
        c0 = g * R_NSA
        col = finish(imp_t[:, c0:c0 + 1], jc)
        row = finish(imp_r[c0:c0 + 1, :], jr)
        beats = (col > row) | ((col == row) & (jc < jr))
        rank = jnp.sum(jnp.where(beats, 1.0, 0.0), axis=0, keepdims=True)
        hit = rank == kk
        idx = jnp.sum(jnp.where(hit, jr.astype(F32), 0.0), axis=1, keepdims=True)
        out = jnp.where(lane == g, idx, out)
    return out.astype(jnp.int32)


def _nsa_decode_select(q_t, kc, vct, pos, n_blocks):
    nb, ncp, _ = kc.shape
    nc = (pos + 1 - CMP_LEN) // CMP_STRIDE + 1
    ovl = _overlap_t(ncp, PAD_BLOCKS, nc, n_blocks)
    hh = np.arange(LANES)
    grp = ((hh[:, None] // R_NSA) == (hh[None, :] // R_NSA)) & (hh[:, None] < H_NSA) & (hh[None, :] < H_NSA)
    grp = jnp.asarray(grp.astype(np.float32), BF16)
    ns = SELECT_SEQS if nb % SELECT_SEQS == 0 else 1
    return pl.pallas_call(
        functools.partial(_nsa_decode_select_kernel, pos=pos, n_blocks=n_blocks),
        grid=(nb // ns,),
        in_specs=[
            pl.BlockSpec((ns, LANES, LANES), lambda b: (b, 0, 0)),
            pl.BlockSpec((ns, ncp, LANES), lambda b: (b, 0, 0)),
            pl.BlockSpec((ns, LANES, ncp), lambda b: (b, 0, 0)),
            pl.BlockSpec(ovl.shape, lambda b: (0, 0)),
            pl.BlockSpec(grp.shape, lambda b: (0, 0)),
        ],
        out_specs=[pl.BlockSpec((ns, N_SELECT, LANES), lambda b: (b, 0, 0)),
                   pl.BlockSpec((ns, LANES, LANES), lambda b: (b, 0, 0))],
        out_shape=[jax.ShapeDtypeStruct((nb, N_SELECT, LANES), jnp.int32),
                   jax.ShapeDtypeStruct((nb, LANES, LANES), F32)],
        compiler_params=_params("parallel"),
        name="nsa_decode_select",
    )(q_t, kc, vct, ovl, grp)


def _nsa_decode_attend_kernel(idx_ref, tbl_ref, cache_hbm, q_ref, win_ref, nsa_new_ref, win_new_ref, gate_ref, ocmp_ref,
                              o_ref, buf_ref, sem, *, n_past_blocks, win_buf):
    nblk = G_NSA * N_SELECT
    b = pl.program_id(0)
    slot = b % 2
    halves = ROWS_PER_PAGE // SEL_BLOCK

    def page_copy(seq, n, slot_):
        j = jnp.minimum(idx_ref[seq, n // N_SELECT, n % N_SELECT], n_past_blocks - 1)
        src = cache_hbm.at[tbl_ref[seq, j // halves], pl.ds(2 * LANES, 2 * LANES), :]
        return pltpu.make_async_copy(src, buf_ref.at[slot_, n], sem.at[slot_])

    @pl.when(b == 0)
    def _():
        for n in range(nblk):
            page_copy(b, n, slot).start()

    @pl.when(b + 1 < pl.num_programs(0))
    def _():
        for n in range(nblk):
            page_copy(b + 1, n, 1 - slot).start()

    for n in range(nblk):
        page_copy(b, n, slot).wait()
    blocks = [buf_ref.at[slot, n] for n in range(nblk)]
    q = q_ref[0]
    qf = q.astype(F32)
    rowg = lax.broadcasted_iota(jnp.int32, (H_NSA, 1), 0) // R_NSA
    ks_new = nsa_new_ref[0, :, 2 * LANES:3 * LANES]
    vs_new = nsa_new_ref[0, :, 3 * LANES:4 * LANES]
    s_new = jnp.sum(qf * ks_new, axis=1, keepdims=True)
    col = lax.broadcasted_iota(jnp.int32, (1, N_SELECT * ROWS_PER_PAGE), 1)
    colpage = col // ROWS_PER_PAGE
    colhalf = (col % ROWS_PER_PAGE) // SEL_BLOCK
    o_sel = None
    for g in range(G_NSA):
        ks_t = jnp.concatenate([blocks[g * N_SELECT + k][0:LANES, :] for k in range(N_SELECT)], axis=1)
        vs_t = jnp.concatenate([blocks[g * N_SELECT + k][LANES:2 * LANES, :] for k in range(N_SELECT)], axis=1)
        s = _dot(q, ks_t.astype(BF16))
        want = jnp.full(col.shape, -1, jnp.int32)
        for k in range(N_SELECT):
            j = idx_ref[b, g, k]
            half = jnp.where(j < n_past_blocks, j % halves, -1)
            want = jnp.where(colpage == k, half, want)
        valid = colhalf == want
        sm = jnp.where(valid, s, NEG_INF)
        m = jnp.maximum(jnp.max(sm, axis=1, keepdims=True), s_new)
        e = jnp.where(valid, jnp.exp2(sm - m), 0.0)
        e_new = jnp.exp2(s_new - m)
        l = jnp.sum(e, axis=1, keepdims=True) + e_new
        og = (_dot_nt(e.astype(BF16), vs_t.astype(BF16)) + e_new * vs_new) / l
        o_sel = og if o_sel is None else jnp.where(rowg == g, og, o_sel)

    kw_t = win_ref[0, 0:LANES, :]
    vw_t = win_ref[0, LANES:2 * LANES, :]
    kw_new = win_new_ref[0, :, 0:LANES]
    vw_new = win_new_ref[0, :, LANES:2 * LANES]
    sw = _dot(q, kw_t.astype(BF16))
    sw_new = jnp.sum(qf * kw_new, axis=1, keepdims=True)
    diff = win_buf - lax.broadcasted_iota(jnp.int32, (1, win_buf), 1)
    validw = (diff >= 0) & (diff < WINDOW)
    smw = jnp.where(validw, sw, NEG_INF)
    mw = jnp.maximum(jnp.max(smw, axis=1, keepdims=True), sw_new)
    ew = jnp.where(validw, jnp.exp2(smw - mw), 0.0)
    ew_new = jnp.exp2(sw_new - mw)
    lw = jnp.sum(ew, axis=1, keepdims=True) + ew_new
    o_win = (_dot_nt(ew.astype(BF16), vw_t.astype(BF16)) + ew_new * vw_new) / lw

    gt = gate_ref[0]
    o_ref[0] = gt[:, 0:1] * ocmp_ref[0] + gt[:, 1:2] * o_sel + gt[:, 2:3] * o_win


def _nsa_decode_attend(idx, table, cache_t, q2, win_t, nsa_new, win_new, gates, ocmp, n_past_blocks):
    nb = q2.shape[0]
    win_buf = win_t.shape[2]
    per_b = lambda shape: pl.BlockSpec((1,) + shape, lambda b, i, t: (b, 0, 0))
    grid_spec = pltpu.PrefetchScalarGridSpec(
        num_scalar_prefetch=2,
        grid=(nb,),
        in_specs=[pl.BlockSpec(memory_space=pl.ANY),
                  per_b((H_NSA, LANES)), per_b((2 * LANES, win_buf)), per_b((1, 4 * LANES)),
                  per_b((1, 2 * LANES)), per_b((H_NSA, LANES)), per_b((H_NSA, LANES))],
        out_specs=per_b((H_NSA, LANES)),
        scratch_shapes=[pltpu.VMEM((2, G_NSA * N_SELECT, 2 * LANES, ROWS_PER_PAGE), F32),
                        pltpu.SemaphoreType.DMA((2,))],
    )
    return pl.pallas_call(
        functools.partial(_nsa_decode_attend_kernel, n_past_blocks=n_past_blocks, win_buf=win_buf),
        grid_spec=grid_spec,
        out_shape=jax.ShapeDtypeStruct((nb, H_NSA, LANES), F32),
        compiler_params=_params("arbitrary"),
        name="nsa_decode_attend",
    )(idx, table, cache_t, q2, win_t, nsa_new, win_new, gates, ocmp)


def _post_mixer_kernel(h_ref, a_ref, b_ref, p_ref, wa_ref, wb_ref, gm_ref,
                       gpre_ref, wg_ref, wu_ref, wo_ref, gpost_ref,
                       ppre_ref, pg_ref, pp_ref, ppost_ref, o_ref):
    y = _dot(a_ref[...].astype(BF16), wa_ref[...]) + _dot(b_ref[...].astype(BF16), wb_ref[...])
    h = h_ref[...] + _rms(y, gm_ref[...])
    xn = _rms(h, gpre_ref[...]).astype(BF16)
    acc = jnp.zeros(h.shape, F32)
    for c in range(D_FF // FF_CHUNK):
        sl = slice(c * FF_CHUNK, (c + 1) * FF_CHUNK)
        g = _dot(xn, wg_ref[:, sl])
        u = _dot(xn, wu_ref[:, sl])
        acc = acc + _dot((jax.nn.silu(g) * u).astype(BF16), wo_ref[sl, :])
    h = h + 0.5 * _rms(acc, gpost_ref[...])
    gate = jax.nn.sigmoid(_dot(_rms(h, ppre_ref[...]).astype(BF16), pg_ref[...]))
    o_ref[...] = h + _rms(gate * _dot(p_ref[...].astype(BF16), pp_ref[...]), ppost_ref[...])


def _post_mixer(h, a, b, p, w, tm):
    m = h.shape[0]
    row = lambda i: (i, 0)
    once = dict(pipeline_mode=pl.Buffered(1))
    const = lambda shape, idx=(0, 0): pl.BlockSpec(shape, lambda i: idx, **once)
    vec = const((1, D_MODEL))
    return pl.pallas_call(
        _post_mixer_kernel,
        grid=(m // tm,),
        in_specs=[pl.BlockSpec((tm, D_MODEL), row), pl.BlockSpec((tm, a.shape[1]), row),
                  pl.BlockSpec((tm, b.shape[1]), row), pl.BlockSpec((tm, PLE_DIM), row),
                  const(w["wo_gla"].shape), const(w["wo_nsa"].shape), vec,
                  vec, const((D_MODEL, D_FF)), const((D_MODEL, D_FF), (0, 1)), const((D_FF, D_MODEL)), vec,
                  vec, const(w["ple_gate"].shape), const(w["ple_proj"].shape), vec],
        out_specs=pl.BlockSpec((tm, D_MODEL), row),
        out_shape=jax.ShapeDtypeStruct((m, D_MODEL), F32),
        compiler_params=_params("parallel"),
        name="post_mixer",
    )(h, a, b, p, w["wo_gla"], w["wo_nsa"], w["m_post"],
      w["f2_pre"], w["f2_in"], w["f2_in"], w["f2_out"], w["f2_post"],
      w["ple_pre"], w["ple_gate"], w["ple_proj"], w["ple_post"])


def _split_in_cols(w):
    outs, off = [], 0
    for n in IN_SPLITS:
        outs.append(w[:, off:off + n])
        off += n
    return outs


def _prep_mixer_weights(w_in, w_a2, b_a):
    q_g, k_g, v_g, r_g, a_lr, q_n, kv_n, gate_n = _split_in_cols(w_in)
    pad_cols = lambda w, n: jnp.pad(w, ((0, 0), (0, n - w.shape[1])))
    gd = G_NSA * D_NSA
    k_sel = kv_n[:, 2 * gd:3 * gd]
    k_win = kv_n[:, 4 * gd:5 * gd]
    wn = jnp.concatenate([q_g, k_g, v_g, r_g, pad_cols(a_lr, LANES), k_sel, k_win], axis=1).astype(BF16)
    wt = jnp.concatenate([q_n, kv_n, pad_cols(gate_n, N_GATE_ROWS)], axis=1).T.astype(BF16)
    wa2 = jnp.pad(w_a2, ((0, LANES - GLA_RANK), (0, 0))).astype(BF16)
    return wn, wt, wa2, b_a.reshape(1, -1)


def _row_tile(m):
    return ROW_TILE if m % ROW_TILE == 0 else m


def _layer(x2, p2, mixer, w):
    tm = _row_tile(x2.shape[0])
    h = _ffn(x2, w["f1_pre"], w["f1_in"], w["f1_out"], w["f1_post"])
    o_gla, o_nsa, extras = mixer(h)
    h = _post_mixer(h, o_gla, o_nsa, p2, w, tm)
    return h, extras


def _mixer_prompt(h, w, nb, t):
    tm = _row_tile(h.shape[0])
    tabs = _rope_tables(np.arange(t))
    (qk, v, r, la, ksel, kwin, qraw_t, qrot_t, gate_t, nsa_t, win_t) = _proj(
        h, w["m_pre"], w["wn"], w["wt"], w["wa2"], w["ba"], tabs, tm, t // tm)
    o_gla, s_fin = _gla_prompt(qk, la, v, r, w["gla_gain"], nb, t)
    table = jnp.zeros((nb, t // ROWS_PER_PAGE), jnp.int32)
    kc, vct = _compress(nsa_t, table, w, False)
    o_nsa = _nsa_prompt(qraw_t, qrot_t, gate_t, kc, vct, ksel, nsa_t, kwin, win_t, nb, t)
    s_t = s_fin.reshape(nb, H_GLA, DV_GLA, 2, DK_GLA)
    s_own = jnp.stack([s_t[:, hh, :, hh % 2, :] for hh in range(H_GLA)], axis=1)
    gla_state = jnp.swapaxes(s_own, -1, -2)
    keep = min(WINDOW, t)
    rows_first = lambda a, n: jnp.transpose(a.reshape(nb, n, G_NSA, D_NSA, a.shape[-1]), (0, 4, 1, 2, 3))
    nsa_rows = rows_first(nsa_t, 4)
    win_rows = rows_first(win_t[:, :, t - keep:], 2)
    return o_gla, o_nsa, (nsa_rows, win_rows, gla_state)


def _mixer_sample(h, w, cache_l, win_l, gla_l, page_table):
    nb = h.shape[0]
    n_pages = page_table.shape[1]
    past_len = n_pages * cache_l.shape[1]
    pos = past_len
    tabs = _rope_tables(np.full((nb,), pos))
    (qk, v, r, la, _, _, qraw_t, qrot_t, gate_t, nsa_new_t, win_new_t) = _proj(
        h, w["m_pre"], w["wn"], w["wt"], w["wa2"], w["ba"], tabs, nb, 1)
    nsa = nsa_new_t[0].T
    win = win_new_t[0].T

    o_gla, gla_state = _gla_step(qk.T, la.T, v, r, gla_l.astype(F32), w["gla_gain"])

    cache_t = jnp.transpose(cache_l.reshape(cache_l.shape[0], ROWS_PER_PAGE, 4 * LANES), (0, 2, 1))
    kc, vct = _compress(cache_t, page_table, w, True)

    hg = (jnp.arange(H_NSA) // R_NSA)[None, :, None]

    def group_pad(q_t):
        q8 = q_t.T.reshape(nb, H_NSA, D_NSA)
        return jnp.concatenate([jnp.where(hg == 0, q8, 0), jnp.where(hg == 1, q8, 0)], axis=-1)

    q2_raw = group_pad(qraw_t)
    q2_rot = group_pad(qrot_t)
    q2_raw_t = jnp.pad(jnp.swapaxes(q2_raw, 1, 2), ((0, 0), (0, 0), (0, LANES - H_NSA)))
    n_blocks = -(-(past_len + 1) // SEL_BLOCK)
    idx_pad, ocmp_t = _nsa_decode_select(q2_raw_t, kc, vct, pos, n_blocks)
    idx = jnp.stack([idx_pad[:, :, g] for g in range(G_NSA)], axis=1)
    ocmp = jnp.swapaxes(ocmp_t, 1, 2)[:, :H_NSA, :]
    gates = jnp.pad(gate_t[:3 * H_NSA].T.reshape(nb, H_NSA, 3), ((0, 0), (0, 0), (0, LANES - 3)))
    wb = win_l.shape[1]
    win_buf_t = jnp.transpose(win_l.reshape(nb, wb, 2 * LANES), (0, 2, 1))
    o8 = _nsa_decode_attend(idx, page_table, cache_t, q2_rot, win_buf_t,
                            nsa.reshape(nb, 1, 4 * LANES), win.reshape(nb, 1, 2 * LANES), gates, ocmp,
                            past_len // SEL_BLOCK)
    o8 = o8.reshape(nb, H_NSA, G_NSA, D_NSA)
    o_nsa = jnp.concatenate([o8[:, :R_NSA, 0], o8[:, R_NSA:, 1]], axis=1).reshape(nb, H_NSA * D_NSA)

    nsa_rows = nsa.reshape(nb, 1, 4, G_NSA, D_NSA)
    win_new = win.reshape(nb, 1, 2, G_NSA, D_NSA)
    kw = jnp.concatenate([win_l, win_new.astype(win_l.dtype)], axis=1)
    keep = min(WINDOW, wb + 1)
    return o_gla, o_nsa, (nsa_rows, kw[:, wb + 1 - keep:], gla_state.astype(gla_l.dtype))


def kernel(x_prompt, x_sample, cache_nsa, state_win, state_gla, page_table, p_prompt, p_sample,
           ffn1_norm_pre, ffn1_norm_post, ffn1_w_in, ffn1_w_out,
           mix_norm_pre, mix_norm_post, w_mix_in, w_gla_a2, b_gla_a, gla_out_norm,
           cmp_pos_k, w_cmp_k1, w_cmp_k2, cmp_pos_v, w_cmp_v1, w_cmp_v2, w_mix_out,
           ffn2_norm_pre, ffn2_norm_post, ffn2_w_in, ffn2_w_out,
           ple_norm_pre, ple_w_gate, ple_w_proj, ple_norm_post):
    nb, t, _ = x_prompt.shape
    ns = x_sample.shape[0]
    depth = ffn1_w_in.shape[0]
    hp = x_prompt.reshape(nb * t, D_MODEL)
    hs = x_sample.reshape(ns, D_MODEL)
    outs = [[] for _ in range(6)]
    for i in range(depth):
        wn, wt, wa2, ba = _prep_mixer_weights(w_mix_in[i], w_gla_a2[i], b_gla_a[i])
        cmp_wk, cmp_posk, cmp_w1k, cmp_w2k, _ = _cmp_weights(cmp_pos_k[i], w_cmp_k1[i], w_cmp_k2[i])
        cmp_wv, cmp_posv, cmp_w1v, _, cmp_w2vt = _cmp_weights(cmp_pos_v[i], w_cmp_v1[i], w_cmp_v2[i])
        gla_w = H_GLA * DV_GLA
        w = dict(
            f1_pre=ffn1_norm_pre[i][None], f1_post=ffn1_norm_post[i][None],
            f1_in=ffn1_w_in[i].astype(BF16), f1_out=ffn1_w_out[i].astype(BF16),
            m_pre=mix_norm_pre[i][None], m_post=mix_norm_post[i][None],
            wn=wn, wt=wt, wa2=wa2, ba=ba, gla_gain=gla_out_norm[i][None],
            cmp_wk=cmp_wk, cmp_posk=cmp_posk, cmp_w1k=cmp_w1k, cmp_w2k=cmp_w2k,
            cmp_wv=cmp_wv, cmp_posv=cmp_posv, cmp_w1v=cmp_w1v, cmp_w2vt=cmp_w2vt,
            wo_gla=w_mix_out[i][:gla_w].astype(BF16), wo_nsa=w_mix_out[i][gla_w:].astype(BF16),
            f2_pre=ffn2_norm_pre[i][None], f2_post=ffn2_norm_post[i][None],
            f2_in=ffn2_w_in[i].astype(BF16), f2_out=ffn2_w_out[i].astype(BF16),
            ple_pre=ple_norm_pre[i][None], ple_post=ple_norm_post[i][None],
            ple_gate=ple_w_gate[i].astype(BF16), ple_proj=ple_w_proj[i].astype(BF16),
        )
        hp, (r_p, w_p, s_p) = _layer(hp, p_prompt[i].reshape(nb * t, PLE_DIM),
                                     functools.partial(_mixer_prompt, w=w, nb=nb, t=t), w)
        hs, (r_s, w_s, s_s) = _layer(hs, p_sample[i].reshape(ns, PLE_DIM),
                                     functools.partial(_mixer_sample, w=w, cache_l=cache_nsa[i], win_l=state_win[i],
                                                       gla_l=state_gla[i], page_table=page_table), w)
        for lst, val in zip(outs, (r_p, w_p, s_p, r_s, w_s, s_s)):
            lst.append(val)
    return (hp.reshape(nb, t, D_MODEL), hs.reshape(ns, 1, D_MODEL), *[jnp.stack(o) for o in outs])
```

```python
import functools

import numpy as np
import jax
import jax.numpy as jnp
from jax import lax
from jax.experimental import pallas as pl
from jax.experimental.pallas import tpu as pltpu

F32 = jnp.float32
BF16 = jnp.bfloat16

D_MODEL = 1024
PLE_DIM = 256
D_FF = 2816
EPS = 1e-6
H_GLA = 4
DK_GLA = 64
DV_GLA = 128
GLA_RANK = 16
GLA_GATE_TEMP = 16.0
GLA_CHUNK = 64
H_NSA = 8
G_NSA = 2
R_NSA = H_NSA // G_NSA
D_NSA = 64
CMP_LEN = 32
CMP_STRIDE = 16
CMP_HIDDEN = 128
SEL_BLOCK = 64
N_SELECT = 16
WINDOW = 512
FORCE_SCORE = 1e4
NEG_INF = -1e30
M_INIT = -1e29
TINY = 1e-30
ATTN_SCALE = D_NSA ** -0.5
LOG2E = 1.4426950408889634
QK_SCALE = ATTN_SCALE * LOG2E
ROPE_THETA = 500000.0
ROPE_DIM = D_NSA // 4
ROPE_HALF = ROPE_DIM // 2
IN_SPLITS = (H_GLA * DK_GLA, H_GLA * DK_GLA, H_GLA * DV_GLA, H_GLA * DV_GLA, GLA_RANK,
             H_NSA * D_NSA, 6 * G_NSA * D_NSA, 3 * H_NSA)

LANES = 128
SUBLANES = 8
VMEM_LIMIT = 56 * 1024 * 1024

ROW_TILE = 512
FF_CHUNK = 256
Q_TILE = 256
K_TILE = 512
V_PAD_ROWS = 16
SEL_COLS = 256
PAGES_PER_STEP = 64
ROWS_PER_PAGE = 128
N_GLA_LEVELS = 6
GLA_CHUNKS_PER_STEP = 8
PAD_BLOCKS = 384
SELECT_SEQS = 4
ATTEND_SEQS = 2


def _params(*sem):
    return pltpu.CompilerParams(dimension_semantics=sem, vmem_limit_bytes=VMEM_LIMIT)


def _rms(x, g):
    return x * lax.rsqrt(jnp.mean(x * x, axis=-1, keepdims=True) + EPS) * g


def _dot(a, b):
    return jnp.dot(a, b, preferred_element_type=F32)


def _dot_nt(a, b):
    return lax.dot_general(a, b, (((1,), (1,)), ((), ())), preferred_element_type=F32)


def _dot_tn(a, b):
    return lax.dot_general(a, b, (((0,), (0,)), ((), ())), preferred_element_type=F32)


def _split_bf16(x, n):
    parts = []
    r = x
    for _ in range(n):
        p = r.astype(BF16)
        parts.append(p)
        r = r - p.astype(F32)
    return parts


def _dot01_left(m01, x, n):
    out = None
    for p in _split_bf16(x, n):
        t = _dot(m01, p)
        out = t if out is None else out + t
    return out


def _dot01_right(x, m01, n):
    out = None
    for p in _split_bf16(x, n):
        t = _dot(p, m01)
        out = t if out is None else out + t
    return out


def _softmax2_rows(sm):
    m = jnp.maximum(jnp.max(sm, axis=0, keepdims=True), M_INIT)
    e = jnp.exp2(sm - m)
    return e / jnp.maximum(jnp.sum(e, axis=0, keepdims=True), TINY)


def _ffn_kernel(x_ref, gpre_ref, wg_ref, wu_ref, wo_ref, gpost_ref, o_ref):
    x = x_ref[...]
    xn = _rms(x, gpre_ref[...]).astype(BF16)
    acc = jnp.zeros(x.shape, F32)
    for c in range(D_FF // FF_CHUNK):
        sl = slice(c * FF_CHUNK, (c + 1) * FF_CHUNK)
        g = _dot(xn, wg_ref[:, sl])
        u = _dot(xn, wu_ref[:, sl])
        a = (jax.nn.silu(g) * u).astype(BF16)
        acc = acc + _dot(a, wo_ref[sl, :])
    o_ref[...] = x + 0.5 * _rms(acc, gpost_ref[...])


def _ffn(x, gpre, w_in, w_out, gpost):
    m = x.shape[0]
    tm = _row_tile(m)
    once = dict(pipeline_mode=pl.Buffered(1))
    const = lambda i: (0, 0)
    return pl.pallas_call(
        _ffn_kernel,
        grid=(m // tm,),
        in_specs=[
            pl.BlockSpec((tm, D_MODEL), lambda i: (i, 0)),
            pl.BlockSpec((1, D_MODEL), const, **once),
            pl.BlockSpec((D_MODEL, D_FF), const, **once),
            pl.BlockSpec((D_MODEL, D_FF), lambda i: (0, 1), **once),
            pl.BlockSpec((D_FF, D_MODEL), const, **once),
            pl.BlockSpec((1, D_MODEL), const, **once),
        ],
        out_specs=pl.BlockSpec((tm, D_MODEL), lambda i: (i, 0)),
        out_shape=jax.ShapeDtypeStruct((m, D_MODEL), F32),
        compiler_params=_params("parallel"),
        name="ffn",
    )(x, gpre, w_in, w_in, w_out, gpost)


N_GLA_COLS = 2 * H_GLA * DK_GLA + 2 * H_GLA * DV_GLA
N_KV_COLS = 6 * G_NSA * D_NSA
WN_COLS = N_GLA_COLS + 3 * LANES
N_Q_COLS = H_NSA * D_NSA
N_GATE_ROWS = 32
WT_ROWS = N_Q_COLS + N_KV_COLS + N_GATE_ROWS
N_CACHE_FEATS = 4 * G_NSA * D_NSA
N_WIN_FEATS = 2 * G_NSA * D_NSA


def _rope_rows(x, cos, sin):
    out = []
    for h in range(x.shape[0] // D_NSA):
        b = h * D_NSA
        x1 = x[b:b + ROPE_HALF]
        x2 = x[b + ROPE_HALF:b + ROPE_DIM]
        out += [x1 * cos - x2 * sin, x2 * cos + x1 * sin, x[b + ROPE_DIM:b + D_NSA]]
    return jnp.concatenate(out, axis=0)


def _rope_lanes(x, c, s1, s2):
    return x * c + pltpu.roll(x, LANES - ROPE_HALF, 1) * s1 + pltpu.roll(x, ROPE_HALF, 1) * s2


def _proj_kernel(h_ref, g_ref, wn_ref, wt_ref, wa2_ref, ba_ref, rc_ref, rs1_ref, rs2_ref, cos_ref, sin_ref,
                 qk_ref, v_ref, r_ref, la_ref, ksel_ref, kwin_ref,
                 qraw_ref, qrot_ref, gate_ref, nsat_ref, wint_ref):
    xn = _rms(h_ref[...], g_ref[...]).astype(BF16)
    nqk = 2 * H_GLA * DK_GLA
    nv = H_GLA * DV_GLA
    a_lr = _dot(xn, wn_ref[:, N_GLA_COLS:N_GLA_COLS + LANES])
    z = _dot(xn, wn_ref[:, 0:nqk])
    qk_ref[:, 0:nqk // 2] = z[:, 0:nqk // 2] * (DK_GLA ** -0.5)
    qk_ref[:, nqk // 2:nqk] = z[:, nqk // 2:nqk]
    v_ref[...] = _dot(xn, wn_ref[:, nqk:nqk + nv])
    r_ref[...] = _dot(xn, wn_ref[:, nqk + nv:N_GLA_COLS])
    k0 = N_GLA_COLS + LANES
    rc, rs1, rs2 = rc_ref[...], rs1_ref[...], rs2_ref[...]
    ksel_ref[...] = _rope_lanes(_dot(xn, wn_ref[:, k0:k0 + LANES]), rc, rs1, rs2)
    kwin_ref[...] = _rope_lanes(_dot(xn, wn_ref[:, k0 + LANES:k0 + 2 * LANES]), rc, rs1, rs2)
    zt = _dot_nt(wt_ref[...], xn)
    cos, sin = cos_ref[...], sin_ref[...]
    q = zt[0:N_Q_COLS] * QK_SCALE
    qraw_ref[...] = q.astype(BF16)
    qrot_ref[...] = _rope_rows(q, cos, sin).astype(BF16)
    kv = zt[N_Q_COLS:N_Q_COLS + N_KV_COLS]
    nsat_ref[0, 0:256, :] = kv[0:256]
    nsat_ref[0, 256:384, :] = _rope_rows(kv[256:384], cos, sin)
    nsat_ref[0, 384:512, :] = kv[384:512]
    wint_ref[0, 0:128, :] = _rope_rows(kv[512:640], cos, sin)
    wint_ref[0, 128:256, :] = kv[640:768]
    gate_ref[...] = jax.nn.sigmoid(zt[N_Q_COLS + N_KV_COLS:WT_ROWS])
    xa = _dot(a_lr.astype(BF16), wa2_ref[...]) + ba_ref[...]
    la_ref[...] = (jnp.minimum(xa, 0.0) - jnp.log1p(jnp.exp(-jnp.abs(xa)))) * (1.0 / GLA_GATE_TEMP)


def _proj(h, gain, wn, wt, wa2, ba, tabs, tm, tiles_per_seq):
    m = h.shape[0]
    rc, rs1, rs2, cos_t, sin_t = tabs
    const = lambda i: (0, 0)
    row = lambda i: (i, 0)
    col = lambda i: (0, i)
    tab_row = lambda i: (i % tiles_per_seq, 0)
    tab_col = lambda i: (0, i % tiles_per_seq)
    outs = [
        (2 * H_GLA * DK_GLA, F32), (H_GLA * DV_GLA, F32), (H_GLA * DV_GLA, F32), (H_GLA * DK_GLA, F32),
        (LANES, F32), (LANES, F32),
    ]
    outs_t = [(N_Q_COLS, BF16), (N_Q_COLS, BF16), (N_GATE_ROWS, F32)]
    nseq = m // (tm * tiles_per_seq)
    seq_len = tm * tiles_per_seq
    outs_seq = [N_CACHE_FEATS, N_WIN_FEATS]
    seq_map = lambda i: (i // tiles_per_seq, 0, i % tiles_per_seq)
    return pl.pallas_call(
        _proj_kernel,
        grid=(m // tm,),
        in_specs=[
            pl.BlockSpec((tm, D_MODEL), row),
            pl.BlockSpec((1, D_MODEL), const),
            pl.BlockSpec((D_MODEL, WN_COLS), const),
            pl.BlockSpec((WT_ROWS, D_MODEL), const),
            pl.BlockSpec((LANES, H_GLA * DK_GLA), const),
            pl.BlockSpec((1, H_GLA * DK_GLA), const),
            pl.BlockSpec((tm, LANES), tab_row),
            pl.BlockSpec((tm, LANES), tab_row),
            pl.BlockSpec((tm, LANES), tab_row),
            pl.BlockSpec((ROPE_HALF, tm), tab_col),
            pl.BlockSpec((ROPE_HALF, tm), tab_col),
        ],
        out_specs=[pl.BlockSpec((tm, n), row) for n, _ in outs] + [pl.BlockSpec((n, tm), col) for n, _ in outs_t]
        + [pl.BlockSpec((1, n, tm), seq_map) for n in outs_seq],
        out_shape=[jax.ShapeDtypeStruct((m, n), d) for n, d in outs]
        + [jax.ShapeDtypeStruct((n, m), d) for n, d in outs_t]
        + [jax.ShapeDtypeStruct((nseq, n, seq_len), F32) for n in outs_seq],
        compiler_params=_params("parallel"),
        name="mixer_proj",
    )(h, gain, wn, wt, wa2, ba, rc, rs1, rs2, cos_t, sin_t)


def _rope_tables(pos):
    pos = np.asarray(pos, np.float64)
    inv_freq = ROPE_THETA ** (-np.arange(ROPE_HALF, dtype=np.float64) * 2.0 / ROPE_DIM)
    ang = pos[:, None] * inv_freq[None, :]
    cos, sin = np.cos(ang), np.sin(ang)
    n = pos.shape[0]
    one = np.ones((n, D_NSA - ROPE_DIM))
    zero = np.zeros((n, D_NSA - ROPE_DIM))
    zh = np.zeros((n, ROPE_HALF))
    c = np.concatenate([cos, cos, one], axis=1)
    s1 = np.concatenate([-sin, zh, zero], axis=1)
    s2 = np.concatenate([zh, sin, zero], axis=1)
    dup = lambda t: np.concatenate([t, t], axis=1)
    return tuple(jnp.asarray(a, F32) for a in (dup(c), dup(s1), dup(s2), cos.T, sin.T))


def _gla_constants():
    c = GLA_CHUNK
    t = np.arange(c)
    low = (t[None, :] <= t[:, None]).astype(np.float32)
    mats, masks = [low], []
    for lev in range(1, N_GLA_LEVELS + 1):
        seg = (2 * c) >> lev
        half = seg // 2
        mid = (t // seg) * seg + half
        mats.append(low[mid])
        same = (t[:, None] // seg) == (t[None, :] // seg)
        masks.append((same & ((t[:, None] % seg) >= half) & ((t[None, :] % seg) < half)).astype(np.float32))
    masks.append(np.eye(c, dtype=np.float32))
    return np.concatenate(mats, axis=0), np.stack(masks)


def _gla_kernel(qk_ref, la_ref, v_ref, r_ref, gain_ref, big_ref, mask_ref, o_ref, sfin_ref, st_ref):
    c = pl.program_id(1)
    ch = GLA_CHUNK

    @pl.when(c == 0)
    def _():
        st_ref[...] = jnp.zeros(st_ref.shape, F32)

    lane = lax.broadcasted_iota(jnp.int32, (1, LANES), 1)
    head_mask = [jnp.where(lane < DK_GLA, 1.0, 0.0), jnp.where(lane >= DK_GLA, 1.0, 0.0)]
    big = big_ref[...]
    gain = gain_ref[...]
    nq = H_GLA * DK_GLA
    n_chunks = qk_ref.shape[0] // ch
    units = [(ci, p) for ci in range(n_chunks) for p in range(H_GLA // 2)]

    allbs = {}
    for ci, p in units:
        rows = slice(ci * ch, (ci + 1) * ch)
        allbs[ci, p] = _dot01_left(big, la_ref[rows, p * LANES:(p + 1) * LANES], 3)
    intra = {}
    for ci, p in units:
        rows = slice(ci * ch, (ci + 1) * ch)
        q = qk_ref[rows, p * LANES:(p + 1) * LANES]
        k = qk_ref[rows, nq + p * LANES:nq + (p + 1) * LANES]
        allb = allbs[ci, p]
        b = allb[0:ch]
        b_last = b[ch - 1:ch]
        attn = [jnp.zeros((ch, ch), F32), jnp.zeros((ch, ch), F32)]
        for lev in range(N_GLA_LEVELS + 1):
            if lev < N_GLA_LEVELS:
                ref = allb[(lev + 1) * ch:(lev + 2) * ch]
                ql = q * jnp.exp(jnp.minimum(b - ref, 0.0))
                kl = k * jnp.exp(jnp.minimum(ref - b, 0.0))
            else:
                ql, kl = q, k
            qq = jnp.concatenate([ql * head_mask[0], ql * head_mask[1]], axis=0).astype(BF16)
            s = _dot_nt(qq, kl.astype(BF16))
            mk = mask_ref[lev]
            attn[0] = attn[0] + mk * s[0:ch]
            attn[1] = attn[1] + mk * s[ch:2 * ch]
        q0 = q * jnp.exp(b)
        k_hat = (k * jnp.exp(b_last - b)).astype(BF16)
        for hh in range(2):
            hs = slice((2 * p + hh) * DV_GLA, (2 * p + hh + 1) * DV_GLA)
            vh = v_ref[rows, hs].astype(BF16)
            intra[ci, 2 * p + hh] = (_dot(attn[hh].astype(BF16), vh), (q0 * head_mask[hh]).astype(BF16),
                                     _dot_tn(vh, k_hat), jnp.exp(b_last))
    for ci in range(n_chunks):
        rows = slice(ci * ch, (ci + 1) * ch)
        for h in range(H_GLA):
            hs = slice(h * DV_GLA, (h + 1) * DV_GLA)
            o_intra, q0h, kv, decay = intra[ci, h]
            st = st_ref[h]
            o = o_intra + _dot_nt(q0h, st.astype(BF16))
            st_ref[h] = st * decay + kv
            o_ref[rows, hs] = _rms(o, gain) * jax.nn.silu(r_ref[rows, hs])

    @pl.when(c == pl.num_programs(1) - 1)
    def _():
        sfin_ref[0] = st_ref[...]


def _gla_prompt(qk, la, v, r, gain, nb, t):
    big, masks = _gla_constants()
    rows = GLA_CHUNK * GLA_CHUNKS_PER_STEP
    nc = t // rows
    row = lambda b, c: (b * nc + c, 0)
    const2 = lambda b, c: (0, 0)
    m = nb * t
    return pl.pallas_call(
        _gla_kernel,
        grid=(nb, nc),
        in_specs=[
            pl.BlockSpec((rows, 2 * H_GLA * DK_GLA), row),
            pl.BlockSpec((rows, H_GLA * DK_GLA), row),
            pl.BlockSpec((rows, H_GLA * DV_GLA), row),
            pl.BlockSpec((rows, H_GLA * DV_GLA), row),
            pl.BlockSpec((1, DV_GLA), const2),
            pl.BlockSpec(big.shape, const2),
            pl.BlockSpec(masks.shape, lambda b, c: (0, 0, 0)),
        ],
        out_specs=[
            pl.BlockSpec((rows, H_GLA * DV_GLA), row),
            pl.BlockSpec((1, H_GLA, DV_GLA, LANES), lambda b, c: (b, 0, 0, 0)),
        ],
        out_shape=[
            jax.ShapeDtypeStruct((m, H_GLA * DV_GLA), F32),
            jax.ShapeDtypeStruct((nb, H_GLA, DV_GLA, LANES), F32),
        ],
        scratch_shapes=[pltpu.VMEM((H_GLA, DV_GLA, LANES), F32)],
        compiler_params=_params("parallel", "arbitrary"),
        name="gla_scan",
    )(qk, la, v, r, gain, jnp.asarray(big, BF16), jnp.asarray(masks, F32))


def _gla_step_kernel(qk_t_ref, la_t_ref, v_ref, r_ref, s_ref, gain_ref, o_ref, sn_ref):
    nq = H_GLA * DK_GLA
    for b in range(s_ref.shape[0]):
        for h in range(H_GLA):
            ks = slice(h * DK_GLA, (h + 1) * DK_GLA)
            vs = slice(h * DV_GLA, (h + 1) * DV_GLA)
            q = qk_t_ref[ks, b:b + 1]
            k = qk_t_ref[nq + h * DK_GLA:nq + (h + 1) * DK_GLA, b:b + 1]
            s_new = jnp.exp(la_t_ref[ks, b:b + 1]) * s_ref[b, h] + k * v_ref[b:b + 1, vs]
            sn_ref[b, h] = s_new
            o = jnp.sum(q * s_new, axis=0, keepdims=True)
            o_ref[b:b + 1, vs] = _rms(o, gain_ref[...]) * jax.nn.silu(r_ref[b:b + 1, vs])


def _gla_step(qk_t, la_t, v, r, state, gain):
    nb = state.shape[0]
    full = lambda a: pl.BlockSpec(a.shape, functools.partial(lambda i, nd: (0,) * nd, nd=a.ndim))
    return pl.pallas_call(
        _gla_step_kernel,
        grid=(1,),
        in_specs=[full(qk_t), full(la_t), full(v), full(r), full(state), full(gain)],
        out_specs=[full(v), full(state)],
        out_shape=[jax.ShapeDtypeStruct((nb, H_GLA * DV_GLA), F32),
                   jax.ShapeDtypeStruct((nb, H_GLA, DK_GLA, DV_GLA), F32)],
        compiler_params=_params("arbitrary"),
        name="gla_step",
    )(qk_t, la_t, v, r, state, gain)


HALF_ROWS = CMP_STRIDE
CHUNKS_PER_PAGE = ROWS_PER_PAGE // HALF_ROWS


def _compress_kernel(tbl_ref, pages_hbm, wk_ref, wv_ref, posk_ref, w1k_ref, w2k_ref, posv_ref, w1v_ref, w2vt_ref,
                     kc_ref, vct_ref, buf_ref, sem, tk_ref, tv_ref, abk_ref, abv_ref, *, pps, paged):
    b = pl.program_id(0)
    s = pl.program_id(1)
    nsteps = pl.num_programs(1)
    step = b * nsteps + s
    slot = step % 2
    step_chunks = pps * CHUNKS_PER_PAGE

    def page_copy(seq, grp, i, slot_):
        if paged:
            src = pages_hbm.at[tbl_ref[seq, grp * pps + i], pl.ds(0, 2 * LANES), :]
        else:
            row0 = pl.multiple_of((grp * pps + i) * ROWS_PER_PAGE, ROWS_PER_PAGE)
            src = pages_hbm.at[seq, pl.ds(0, 2 * LANES), pl.ds(row0, ROWS_PER_PAGE)]
        return pltpu.make_async_copy(src, buf_ref.at[slot_, i], sem.at[slot_])

    @pl.when(step == 0)
    def _():
        for i in range(pps):
            page_copy(b, s, i, slot).start()

    @pl.when(step + 1 < pl.num_programs(0) * nsteps)
    def _():
        nxt = step + 1
        for i in range(pps):
            page_copy(nxt // nsteps, nxt % nsteps, i, 1 - slot).start()

    for i in range(pps):
        page_copy(b, s, i, slot).wait()

    for i in range(pps):
        tk_ref[i] = buf_ref[slot, i, 0:LANES, :].T
        tv_ref[i] = buf_ref[slot, i, LANES:2 * LANES, :].T
    lpair = 2 * LANES
    row0 = pl.multiple_of(s * step_chunks, step_chunks)
    for t_ref, w_ref, ab_ref in ((tk_ref, wk_ref, abk_ref), (tv_ref, wv_ref, abv_ref)):
        acc = None
        for lp in range(HALF_ROWS // 2):
            cols = [jnp.concatenate([t_ref[i, pl.ds(l, CHUNKS_PER_PAGE, stride=HALF_ROWS), :] for i in range(pps)], axis=0)
                    for l in (2 * lp, 2 * lp + 1)]
            x = jnp.concatenate(cols, axis=1).astype(BF16)
            part = _dot(x, w_ref[lp * lpair:(lp + 1) * lpair, :])
            acc = part if acc is None else acc + part
        ab_ref[pl.ds(row0, step_chunks), :] = acc

    @pl.when(s == pl.num_programs(1) - 1)
    def _():
        nch = abk_ref.shape[0]
        pk = _dot(posk_ref[...], w1k_ref[...])[0:1]
        pv = _dot(posv_ref[...], w1v_ref[...])[0:1]
        abk = abk_ref[...]
        abv = abv_ref[...]
        kc = None
        for g in range(G_NSA):
            o = g * 2 * CMP_HIDDEN
            hk = jax.nn.gelu(abk[:, o:o + CMP_HIDDEN] + pltpu.roll(abk[:, o + CMP_HIDDEN:o + 2 * CMP_HIDDEN], nch - 1, 0) + pk)
            hv = jax.nn.gelu(abv[:, o:o + CMP_HIDDEN] + pltpu.roll(abv[:, o + CMP_HIDDEN:o + 2 * CMP_HIDDEN], nch - 1, 0) + pv)
            tk = _dot(hk.astype(BF16), w2k_ref[g])
            kc = tk if kc is None else kc + tk
            vct_ref[0, g * D_NSA:(g + 1) * D_NSA, :] = _dot_nt(w2vt_ref[...], hv.astype(BF16)).astype(BF16)
        kc_ref[0] = kc.astype(BF16)


def _compress(pages_t, table, w, paged):
    nb, npages = table.shape
    pps = min(PAGES_PER_STEP, npages)
    nsteps = npages // pps
    nch = npages * CHUNKS_PER_PAGE

    names = ("cmp_wk", "cmp_wv", "cmp_posk", "cmp_w1k", "cmp_w2k", "cmp_posv", "cmp_w1v", "cmp_w2vt")
    consts = [w[n] for n in names]
    const_specs = [pl.BlockSpec(c.shape, functools.partial(lambda b, s, tbl, nd: (0,) * nd, nd=c.ndim)) for c in consts]
    grid_spec = pltpu.PrefetchScalarGridSpec(
        num_scalar_prefetch=1,
        grid=(nb, nsteps),
        in_specs=[pl.BlockSpec(memory_space=pl.ANY)] + const_specs,
        out_specs=[pl.BlockSpec((1, nch, LANES), lambda b, s, tbl: (b, 0, 0)),
                   pl.BlockSpec((1, LANES, nch), lambda b, s, tbl: (b, 0, 0))],
        scratch_shapes=[pltpu.VMEM((2, pps, 2 * LANES, ROWS_PER_PAGE), F32), pltpu.SemaphoreType.DMA((2,)),
                        pltpu.VMEM((pps, ROWS_PER_PAGE, LANES), F32), pltpu.VMEM((pps, ROWS_PER_PAGE, LANES), F32),
                        pltpu.VMEM((nch, 4 * CMP_HIDDEN), F32), pltpu.VMEM((nch, 4 * CMP_HIDDEN), F32)],
    )
    return pl.pallas_call(
        functools.partial(_compress_kernel, pps=pps, paged=paged),
        grid_spec=grid_spec,
        out_shape=[jax.ShapeDtypeStruct((nb, nch, LANES), BF16),
                   jax.ShapeDtypeStruct((nb, LANES, nch), BF16)],
        compiler_params=_params("arbitrary", "arbitrary"),
        name="compress",
    )(table, pages_t, *consts)


def _cmp_weights(pos, w1, w2):
    w = w1.reshape(2, HALF_ROWS, D_NSA, CMP_HIDDEN)
    z = jnp.zeros_like(w[0])
    blocks = []
    for g in range(G_NSA):
        cols = []
        for g2 in range(G_NSA):
            for half in range(2):
                cols.append(w[half] if g2 == g else z)
        blocks.append(jnp.concatenate(cols, axis=-1))
    wbig = jnp.stack(blocks, axis=1).reshape(HALF_ROWS * LANES, 4 * CMP_HIDDEN).astype(BF16)
    pos8 = jnp.broadcast_to(pos.reshape(1, CMP_LEN * D_NSA), (SUBLANES, CMP_LEN * D_NSA)).astype(BF16)
    z2 = jnp.zeros_like(w2)
    w2pad = jnp.stack([jnp.concatenate([w2, z2], axis=1), jnp.concatenate([z2, w2], axis=1)]).astype(BF16)
    return wbig, pos8, w1.astype(BF16), w2pad, w2.T.astype(BF16)


def _overlap_t(nc_pad, ns_pad, nc, ns):
    i = np.arange(nc_pad)[None, :] * CMP_STRIDE
    j = np.arange(ns_pad)[:, None] * SEL_BLOCK
    m = (i < j + SEL_BLOCK) & (i + CMP_LEN > j) & (np.arange(nc_pad)[None, :] < nc) & (np.arange(ns_pad)[:, None] < ns)
    return jnp.asarray(m.astype(np.float32), BF16)


def _rank_desc(x):
    nrow = x.shape[0]
    nblk = nrow // SUBLANES
    blocks = [x[v * SUBLANES:(v + 1) * SUBLANES] for v in range(nblk)]
    cnt = [jnp.zeros((SUBLANES, x.shape[1]), F32) for _ in range(nblk)]
    sub = lax.broadcasted_iota(jnp.int32, (SUBLANES, x.shape[1]), 0)
    for jp in range(nrow):
        row = x[jp:jp + 1]
        vb = jp // SUBLANES
        for v in range(nblk):
            ge = jnp.where(row >= blocks[v], 1.0, 0.0)
            gt = jnp.where(row > blocks[v], 1.0, 0.0)
            if v > vb:
                cnt[v] = cnt[v] + ge
            elif v < vb:
                cnt[v] = cnt[v] + gt
            else:
                cnt[v] = cnt[v] + jnp.where(sub > (jp % SUBLANES), ge, gt)
    return jnp.concatenate(cnt, axis=0)


def _nsa_prompt_kernel(qraw_ref, qrot_ref, gate_ref, kc_ref, vct_ref, ksel_ref, vselt_ref, kwin_ref, vwint_ref,
                       ovl_ref, o_ref, kaug_ref, kwb_ref, vsa_ref, vwa_ref):
    qb = pl.program_id(1)
    nlane = R_NSA * Q_TILE
    tpos = qb * Q_TILE + lax.broadcasted_iota(jnp.int32, (1, nlane), 1) % Q_TILE
    tpos1 = tpos[:, 0:Q_TILE]
    ncp = kc_ref.shape[1]
    nsb = ovl_ref.shape[0]
    seq = ksel_ref.shape[0]
    zeros_q = jnp.zeros((D_NSA, nlane), BF16)

    @pl.when(qb == 0)
    def _():
        ks = ksel_ref[...]
        lane = lax.broadcasted_iota(jnp.int32, (1, LANES), 1)
        blk = lax.broadcasted_iota(jnp.int32, (seq, 1), 0) // SEL_BLOCK
        kaug_ref[0] = jnp.where(lane < D_NSA, ks, jnp.where(lane - D_NSA == blk, 1.0, 0.0)).astype(BF16)
        kaug_ref[1] = jnp.where(lane >= D_NSA, ks, jnp.where(lane == blk, 1.0, 0.0)).astype(BF16)
        kwb_ref[...] = kwin_ref[...].astype(BF16)
        ones_row = jnp.where(lax.broadcasted_iota(jnp.int32, (V_PAD_ROWS, seq), 0) == 0, 1.0, 0.0)
        for g in range(G_NSA):
            gs = slice(g * D_NSA, (g + 1) * D_NSA)
            vsa_ref[g] = jnp.concatenate([vselt_ref[0, gs, :], ones_row], axis=0).astype(BF16)
            vwa_ref[g] = jnp.concatenate([vwint_ref[0, gs, :], ones_row], axis=0).astype(BF16)

    def group_q(ref, g):
        q = jnp.concatenate([ref[h * D_NSA:(h + 1) * D_NSA, :] for h in range(g * R_NSA, (g + 1) * R_NSA)], axis=1)
        return q, (jnp.concatenate([q, zeros_q], axis=0) if g == 0 else jnp.concatenate([zeros_q, q], axis=0))

    cmp_scores = [_dot(kc_ref[0], group_q(qraw_ref, g)[1]) for g in range(G_NSA)]
    o_cmps, qaugs, qpads = [], [], []
    for g in range(G_NSA):
        qrot, qpad = group_q(qrot_ref, g)
        gs = slice(g * D_NSA, (g + 1) * D_NSA)

        sc = cmp_scores[g]
        ci = lax.broadcasted_iota(jnp.int32, (ncp, 1), 0)
        ended = ci * CMP_STRIDE + (CMP_LEN - 1) <= tpos1
        p_cmp = _softmax2_rows(jnp.concatenate(
            [jnp.where(ended, sc[:, r * Q_TILE:(r + 1) * Q_TILE], NEG_INF) for r in range(R_NSA)], axis=1))
        o_cmp = _dot(vct_ref[0, gs, :], p_cmp.astype(BF16))
        psum = p_cmp[:, 0:Q_TILE]
        for r in range(1, R_NSA):
            psum = psum + p_cmp[:, r * Q_TILE:(r + 1) * Q_TILE]
        imp = _dot01_left(ovl_ref[...], psum, 2)
        bj = lax.broadcasted_iota(jnp.int32, (nsb, 1), 0)
        cur = tpos1 // SEL_BLOCK
        forced = (bj == 0) | (bj == cur) | (bj == cur - 1)
        imp = jnp.where(forced, FORCE_SCORE, jnp.where(bj * SEL_BLOCK <= tpos1, imp, -1.0))
        bias = jnp.where(_rank_desc(imp) < N_SELECT, 0.0, NEG_INF).astype(BF16)
        bias = jnp.concatenate([bias] * R_NSA, axis=1)
        if nsb < D_NSA:
            bias = jnp.concatenate([bias, jnp.zeros((D_NSA - nsb, nlane), BF16)], axis=0)
        o_cmps.append(o_cmp)
        qaugs.append(jnp.concatenate([qrot, bias], axis=0) if g == 0 else jnp.concatenate([bias, qrot], axis=0))
        qpads.append(qpad)

    ncol = nlane // SEL_COLS
    pairs = [(g, slice(j * SEL_COLS, (j + 1) * SEL_COLS)) for g in range(G_NSA) for j in range(ncol)]

    krow = lax.broadcasted_iota(jnp.int32, (K_TILE, 1), 0)

    def sel_step(c, carry, causal):
        start = pl.multiple_of(c * K_TILE, K_TILE)
        out = []
        scores = [_dot(kaug_ref[g, pl.ds(start, K_TILE), :], qaugs[g][:, cols]) for g, cols in pairs]
        for (g, cols), (m, acc), s in zip(pairs, carry, scores):
            if causal:
                s = jnp.where(start + krow <= tpos[:, cols], s, NEG_INF)
            m_new = jnp.maximum(m, jnp.max(s, axis=0, keepdims=True))
            e = jnp.exp2(s - m_new).astype(BF16)
            acc = jnp.exp2(m - m_new) * acc + _dot(vsa_ref[g, :, pl.ds(start, K_TILE)], e)
            out.append((m_new, acc))
        return tuple(out)

    n_full = (qb * Q_TILE) // K_TILE
    init = tuple((jnp.full((1, SEL_COLS), M_INIT, F32), jnp.zeros((D_NSA + V_PAD_ROWS, SEL_COLS), F32))
                 for _ in pairs)
    carry = lax.fori_loop(0, n_full, functools.partial(sel_step, causal=False), init)

    band = WINDOW + Q_TILE
    wstart = pl.multiple_of(jnp.maximum(qb * Q_TILE - WINDOW, 0), Q_TILE)
    win_scores = [_dot(kwb_ref[pl.ds(wstart, band), :], qpads[g]) for g in range(G_NSA)]

    carry = sel_step(n_full, carry, True)
    o_sels = []
    for g in range(G_NSA):
        acc_sel = jnp.concatenate([carry[g * ncol + j][1] for j in range(ncol)], axis=1)
        o_sels.append(acc_sel[0:D_NSA] / acc_sel[D_NSA:D_NSA + 1])

    brow = lax.broadcasted_iota(jnp.int32, (band, 1), 0)
    diff = tpos1 - (wstart + brow)
    age_limit = jnp.where(brow < Q_TILE, WINDOW, band + WINDOW)
    in_window = (diff >= 0) & (diff < age_limit)
    heads = []
    for g in range(G_NSA):
        sw = jnp.concatenate([jnp.where(in_window, win_scores[g][:, r * Q_TILE:(r + 1) * Q_TILE], NEG_INF)
                              for r in range(R_NSA)], axis=1)
        e_win = jnp.exp2(sw - jnp.max(sw, axis=0, keepdims=True)).astype(BF16)
        acc_win = _dot(vwa_ref[g, :, pl.ds(wstart, band)], e_win)
        o_win = acc_win[0:D_NSA] / acc_win[D_NSA:D_NSA + 1]
        for r in range(R_NSA):
            h = g * R_NSA + r
            ls = slice(r * Q_TILE, (r + 1) * Q_TILE)
            heads.append(gate_ref[3 * h:3 * h + 1, :] * o_cmps[g][:, ls]
                         + gate_ref[3 * h + 1:3 * h + 2, :] * o_sels[g][:, ls]
                         + gate_ref[3 * h + 2:3 * h + 3, :] * o_win[:, ls])
    o_ref[...] = jnp.concatenate(heads, axis=0).T


def _nsa_prompt(qraw_t, qrot_t, gate_t, kc, vct, ksel, nsa_t, kwin, win_t, nb, t):
    nqb = t // Q_TILE
    ncp = kc.shape[1]
    ovl = _overlap_t(ncp, t // SEL_BLOCK, (t - CMP_LEN) // CMP_STRIDE + 1, t // SEL_BLOCK)
    qcol = lambda b, q: (0, b * nqb + q)
    m = nb * t
    return pl.pallas_call(
        _nsa_prompt_kernel,
        grid=(nb, nqb),
        in_specs=[
            pl.BlockSpec((N_Q_COLS, Q_TILE), qcol),
            pl.BlockSpec((N_Q_COLS, Q_TILE), qcol),
            pl.BlockSpec((N_GATE_ROWS, Q_TILE), qcol),
            pl.BlockSpec((1, ncp, LANES), lambda b, q: (b, 0, 0)),
            pl.BlockSpec((1, LANES, ncp), lambda b, q: (b, 0, 0)),
            pl.BlockSpec((t, LANES), lambda b, q: (b, 0)),
            pl.BlockSpec((1, LANES, t), lambda b, q: (b, 3, 0)),
            pl.BlockSpec((t, LANES), lambda b, q: (b, 0)),
            pl.BlockSpec((1, LANES, t), lambda b, q: (b, 1, 0)),
            pl.BlockSpec(ovl.shape, lambda b, q: (0, 0)),
        ],
        out_specs=pl.BlockSpec((Q_TILE, N_Q_COLS), lambda b, q: (b * nqb + q, 0)),
        out_shape=jax.ShapeDtypeStruct((m, N_Q_COLS), F32),
        scratch_shapes=[pltpu.VMEM((G_NSA, t, LANES), BF16), pltpu.VMEM((t, LANES), BF16),
                        pltpu.VMEM((G_NSA, D_NSA + V_PAD_ROWS, t), BF16),
                        pltpu.VMEM((G_NSA, D_NSA + V_PAD_ROWS, t), BF16)],
        compiler_params=_params("parallel", "arbitrary"),
        name="nsa_prompt",
    )(qraw_t, qrot_t, gate_t, kc, vct, ksel, nsa_t, kwin, win_t, ovl)


def _nsa_decode_select_kernel(qt_ref, kc_ref, vct_ref, ovl_ref, grp_ref, idx_ref, ocmp_ref, *, pos, n_blocks):
    nseq, ncp, _ = kc_ref.shape
    seqs = range(nseq)
    ci = lax.broadcasted_iota(jnp.int32, (ncp, 1), 0)
    ended = ci * CMP_STRIDE + (CMP_LEN - 1) <= pos
    scores = [_dot(kc_ref[i], qt_ref[i]) for i in seqs]
    probs = [_softmax2_rows(jnp.where(ended, sc, NEG_INF)) for sc in scores]
    for i in seqs:
        ocmp_ref[i] = _dot(vct_ref[i], probs[i].astype(BF16))
    psums = [_dot01_right(p, grp_ref[...], 2) for p in probs]
    imps = [_dot01_left(ovl_ref[...], ps, 2) for ps in psums]
    for i in seqs:
        idx_ref[i] = _decode_topk(imps[i], pos, n_blocks)


def _decode_topk(imp_t, pos, n_blocks):
    imp_r = imp_t.T
    cur = pos // SEL_BLOCK

    def finish(v, j):
        forced = (j == 0) | (j == cur) | (j == cur - 1)
        v = jnp.where(forced, FORCE_SCORE, jnp.where(j * SEL_BLOCK <= pos, v, -1.0))
        return jnp.where(j < n_blocks, v, -3e38)

    jc = lax.broadcasted_iota(jnp.int32, (PAD_BLOCKS, 1), 0)
    jr = lax.broadcasted_iota(jnp.int32, (1, PAD_BLOCKS), 1)
    kk = lax.broadcasted_iota(jnp.int32, (N_SELECT, 1), 0).astype(F32)
    lane = lax.broadcasted_iota(jnp.int32, (N_SELECT, LANES), 1)
    out = jnp.zeros((N_SELECT, LANES), F32)
    for g in range(G_NSA):
        c0 = g * R_NSA
        col = finish(imp_t[:, c0:c0 + 1], jc)
        row = finish(imp_r[c0:c0 + 1, :], jr)
        beats = (col > row) | ((col == row) & (jc < jr))
        rank = jnp.sum(jnp.where(beats, 1.0, 0.0), axis=0, keepdims=True)
        hit = rank == kk
        idx = jnp.sum(jnp.where(hit, jr.astype(F32), 0.0), axis=1, keepdims=True)
        out = jnp.where(lane == g, idx, out)
    return out.astype(jnp.int32)


def _nsa_decode_select(q_t, kc, vct, pos, n_blocks):
    nb, ncp, _ = kc.shape
    nc = (pos + 1 - CMP_LEN) // CMP_STRIDE + 1
    ovl = _overlap_t(ncp, PAD_BLOCKS, nc, n_blocks)
    hh = np.arange(LANES)
    grp = ((hh[:, None] // R_NSA) == (hh[None, :] // R_NSA)) & (hh[:, None] < H_NSA) & (hh[None, :] < H_NSA)
    grp = jnp.asarray(grp.astype(np.float32), BF16)
    ns = SELECT_SEQS if nb % SELECT_SEQS == 0 else 1
    return pl.pallas_call(
        functools.partial(_nsa_decode_select_kernel, pos=pos, n_blocks=n_blocks),
        grid=(nb // ns,),
        in_specs=[
            pl.BlockSpec((ns, LANES, LANES), lambda b: (b, 0, 0)),
            pl.BlockSpec((ns, ncp, LANES), lambda b: (b, 0, 0)),
            pl.BlockSpec((ns, LANES, ncp), lambda b: (b, 0, 0)),
            pl.BlockSpec(ovl.shape, lambda b: (0, 0)),
            pl.BlockSpec(grp.shape, lambda b: (0, 0)),
        ],
        out_specs=[pl.BlockSpec((ns, N_SELECT, LANES), lambda b: (b, 0, 0)),
                   pl.BlockSpec((ns, LANES, LANES), lambda b: (b, 0, 0))],
        out_shape=[jax.ShapeDtypeStruct((nb, N_SELECT, LANES), jnp.int32),
                   jax.ShapeDtypeStruct((nb, LANES, LANES), F32)],
        compiler_params=_params("parallel"),
        name="nsa_decode_select",
    )(q_t, kc, vct, ovl, grp)


def _nsa_decode_attend_kernel(idx_ref, tbl_ref, cache_hbm, q_ref, win_ref, nsa_new_ref, win_new_ref, gate_ref, ocmp_ref,
                              o_ref, buf_ref, sem, *, n_past_blocks, win_buf):
    nblk = G_NSA * N_SELECT
    nseq = q_ref.shape[0]
    step = pl.program_id(0)
    slot = step % 2
    halves = ROWS_PER_PAGE // SEL_BLOCK

    def page_copy(step_, i, n, slot_):
        seq = step_ * nseq + i
        j = jnp.minimum(idx_ref[seq, n // N_SELECT, n % N_SELECT], n_past_blocks - 1)
        src = cache_hbm.at[tbl_ref[seq, j // halves], pl.ds(2 * LANES, 2 * LANES), :]
        return pltpu.make_async_copy(src, buf_ref.at[slot_, i * nblk + n], sem.at[slot_])

    copies = [(i, n) for i in range(nseq) for n in range(nblk)]

    @pl.when(step == 0)
    def _():
        for i, n in copies:
            page_copy(step, i, n, slot).start()

    @pl.when(step + 1 < pl.num_programs(0))
    def _():
        for i, n in copies:
            page_copy(step + 1, i, n, 1 - slot).start()

    for i, n in copies:
        page_copy(step, i, n, slot).wait()

    rowg = lax.broadcasted_iota(jnp.int32, (H_NSA, 1), 0) // R_NSA
    col = lax.broadcasted_iota(jnp.int32, (1, N_SELECT * ROWS_PER_PAGE), 1)
    colpage = col // ROWS_PER_PAGE
    colhalf = (col % ROWS_PER_PAGE) // SEL_BLOCK
    diff = win_buf - lax.broadcasted_iota(jnp.int32, (1, win_buf), 1)
    validw = (diff >= 0) & (diff < WINDOW)

    def block_rows(i, g, rows):
        return jnp.concatenate([buf_ref[slot, i * nblk + g * N_SELECT + k, rows, :] for k in range(N_SELECT)], axis=1)

    qs = [q_ref[i] for i in range(nseq)]
    sel_scores = [[_dot(qs[i], block_rows(i, g, slice(0, LANES)).astype(BF16)) for g in range(G_NSA)]
                  for i in range(nseq)]
    win_scores = [_dot(qs[i], win_ref[i, 0:LANES, :].astype(BF16)) for i in range(nseq)]
    for i in range(nseq):
        seq = step * nseq + i
        qf = qs[i].astype(F32)
        ks_new = nsa_new_ref[i, :, 2 * LANES:3 * LANES]
        vs_new = nsa_new_ref[i, :, 3 * LANES:4 * LANES]
        s_new = jnp.sum(qf * ks_new, axis=1, keepdims=True)
        o_sel = None
        for g in range(G_NSA):
            vs_t = block_rows(i, g, slice(LANES, 2 * LANES))
            want = jnp.full(col.shape, -1, jnp.int32)
            for k in range(N_SELECT):
                j = idx_ref[seq, g, k]
                half = jnp.where(j < n_past_blocks, j % halves, -1)
                want = jnp.where(colpage == k, half, want)
            valid = colhalf == want
            sm = jnp.where(valid, sel_scores[i][g], NEG_INF)
            m = jnp.maximum(jnp.max(sm, axis=1, keepdims=True), s_new)
            e = jnp.where(valid, jnp.exp2(sm - m), 0.0)
            e_new = jnp.exp2(s_new - m)
            l = jnp.sum(e, axis=1, keepdims=True) + e_new
            og = (_dot_nt(e.astype(BF16), vs_t.astype(BF16)) + e_new * vs_new) / l
            o_sel = og if o_sel is None else jnp.where(rowg == g, og, o_sel)

        vw_t = win_ref[i, LANES:2 * LANES, :]
        kw_new = win_new_ref[i, :, 0:LANES]
        vw_new = win_new_ref[i, :, LANES:2 * LANES]
        sw_new = jnp.sum(qf * kw_new, axis=1, keepdims=True)
        smw = jnp.where(validw, win_scores[i], NEG_INF)
        mw = jnp.maximum(jnp.max(smw, axis=1, keepdims=True), sw_new)
        ew = jnp.where(validw, jnp.exp2(smw - mw), 0.0)
        ew_new = jnp.exp2(sw_new - mw)
        lw = jnp.sum(ew, axis=1, keepdims=True) + ew_new
        o_win = (_dot_nt(ew.astype(BF16), vw_t.astype(BF16)) + ew_new * vw_new) / lw

        gt = gate_ref[i]
        o_ref[i] = gt[:, 0:1] * ocmp_ref[i] + gt[:, 1:2] * o_sel + gt[:, 2:3] * o_win


def _nsa_decode_attend(idx, table, cache_t, q2, win_t, nsa_new, win_new, gates, ocmp, n_past_blocks):
    nb = q2.shape[0]
    win_buf = win_t.shape[2]
    ns = ATTEND_SEQS if nb % ATTEND_SEQS == 0 else 1
    per_b = lambda shape: pl.BlockSpec((ns,) + shape, lambda b, i, t: (b, 0, 0))
    grid_spec = pltpu.PrefetchScalarGridSpec(
        num_scalar_prefetch=2,
        grid=(nb // ns,),
        in_specs=[pl.BlockSpec(memory_space=pl.ANY),
                  per_b((H_NSA, LANES)), per_b((2 * LANES, win_buf)), per_b((1, 4 * LANES)),
                  per_b((1, 2 * LANES)), per_b((H_NSA, LANES)), per_b((H_NSA, LANES))],
        out_specs=per_b((H_NSA, LANES)),
        scratch_shapes=[pltpu.VMEM((2, ns * G_NSA * N_SELECT, 2 * LANES, ROWS_PER_PAGE), F32),
                        pltpu.SemaphoreType.DMA((2,))],
    )
    return pl.pallas_call(
        functools.partial(_nsa_decode_attend_kernel, n_past_blocks=n_past_blocks, win_buf=win_buf),
        grid_spec=grid_spec,
        out_shape=jax.ShapeDtypeStruct((nb, H_NSA, LANES), F32),
        compiler_params=_params("arbitrary"),
        name="nsa_decode_attend",
    )(idx, table, cache_t, q2, win_t, nsa_new, win_new, gates, ocmp)


def _post_mixer_kernel(h_ref, a_ref, b_ref, p_ref, wa_ref, wb_ref, gm_ref,
                       gpre_ref, wg_ref, wu_ref, wo_ref, gpost_ref,
                       ppre_ref, pg_ref, pp_ref, ppost_ref, o_ref):
    y = _dot(a_ref[...].astype(BF16), wa_ref[...]) + _dot(b_ref[...].astype(BF16), wb_ref[...])
    h = h_ref[...] + _rms(y, gm_ref[...])
    xn = _rms(h, gpre_ref[...]).astype(BF16)
    acc = jnp.zeros(h.shape, F32)
    for c in range(D_FF // FF_CHUNK):
        sl = slice(c * FF_CHUNK, (c + 1) * FF_CHUNK)
        g = _dot(xn, wg_ref[:, sl])
        u = _dot(xn, wu_ref[:, sl])
        acc = acc + _dot((jax.nn.silu(g) * u).astype(BF16), wo_ref[sl, :])
    h = h + 0.5 * _rms(acc, gpost_ref[...])
    gate = jax.nn.sigmoid(_dot(_rms(h, ppre_ref[...]).astype(BF16), pg_ref[...]))
    o_ref[...] = h + _rms(gate * _dot(p_ref[...].astype(BF16), pp_ref[...]), ppost_ref[...])


def _post_mixer(h, a, b, p, w, tm):
    m = h.shape[0]
    row = lambda i: (i, 0)
    once = dict(pipeline_mode=pl.Buffered(1))
    const = lambda shape, idx=(0, 0): pl.BlockSpec(shape, lambda i: idx, **once)
    vec = const((1, D_MODEL))
    return pl.pallas_call(
        _post_mixer_kernel,
        grid=(m // tm,),
        in_specs=[pl.BlockSpec((tm, D_MODEL), row), pl.BlockSpec((tm, a.shape[1]), row),
                  pl.BlockSpec((tm, b.shape[1]), row), pl.BlockSpec((tm, PLE_DIM), row),
                  const(w["wo_gla"].shape), const(w["wo_nsa"].shape), vec,
                  vec, const((D_MODEL, D_FF)), const((D_MODEL, D_FF), (0, 1)), const((D_FF, D_MODEL)), vec,
                  vec, const(w["ple_gate"].shape), const(w["ple_proj"].shape), vec],
        out_specs=pl.BlockSpec((tm, D_MODEL), row),
        out_shape=jax.ShapeDtypeStruct((m, D_MODEL), F32),
        compiler_params=_params("parallel"),
        name="post_mixer",
    )(h, a, b, p, w["wo_gla"], w["wo_nsa"], w["m_post"],
      w["f2_pre"], w["f2_in"], w["f2_in"], w["f2_out"], w["f2_post"],
      w["ple_pre"], w["ple_gate"], w["ple_proj"], w["ple_post"])


def _split_in_cols(w):
    outs, off = [], 0
    for n in IN_SPLITS:
        outs.append(w[:, off:off + n])
        off += n
    return outs


def _prep_mixer_weights(w_in, w_a2, b_a):
    q_g, k_g, v_g, r_g, a_lr, q_n, kv_n, gate_n = _split_in_cols(w_in)
    pad_cols = lambda w, n: jnp.pad(w, ((0, 0), (0, n - w.shape[1])))
    gd = G_NSA * D_NSA
    k_sel = kv_n[:, 2 * gd:3 * gd]
    k_win = kv_n[:, 4 * gd:5 * gd]
    wn = jnp.concatenate([q_g, k_g, v_g, r_g, pad_cols(a_lr, LANES), k_sel, k_win], axis=1).astype(BF16)
    wt = jnp.concatenate([q_n, kv_n, pad_cols(gate_n, N_GATE_ROWS)], axis=1).T.astype(BF16)
    wa2 = jnp.pad(w_a2, ((0, LANES - GLA_RANK), (0, 0))).astype(BF16)
    return wn, wt, wa2, b_a.reshape(1, -1)


def _row_tile(m):
    return ROW_TILE if m % ROW_TILE == 0 else m


def _layer(x2, p2, mixer, w):
    tm = _row_tile(x2.shape[0])
    h = _ffn(x2, w["f1_pre"], w["f1_in"], w["f1_out"], w["f1_post"])
    o_gla, o_nsa, extras = mixer(h)
    h = _post_mixer(h, o_gla, o_nsa, p2, w, tm)
    return h, extras


def _mixer_prompt(h, w, nb, t):
    tm = _row_tile(h.shape[0])
    tabs = _rope_tables(np.arange(t))
    (qk, v, r, la, ksel, kwin, qraw_t, qrot_t, gate_t, nsa_t, win_t) = _proj(
        h, w["m_pre"], w["wn"], w["wt"], w["wa2"], w["ba"], tabs, tm, t // tm)
    o_gla, s_fin = _gla_prompt(qk, la, v, r, w["gla_gain"], nb, t)
    table = jnp.zeros((nb, t // ROWS_PER_PAGE), jnp.int32)
    kc, vct = _compress(nsa_t, table, w, False)
    o_nsa = _nsa_prompt(qraw_t, qrot_t, gate_t, kc, vct, ksel, nsa_t, kwin, win_t, nb, t)
    s_t = s_fin.reshape(nb, H_GLA, DV_GLA, 2, DK_GLA)
    s_own = jnp.stack([s_t[:, hh, :, hh % 2, :] for hh in range(H_GLA)], axis=1)
    gla_state = jnp.swapaxes(s_own, -1, -2)
    keep = min(WINDOW, t)
    rows_first = lambda a, n: jnp.transpose(a.reshape(nb, n, G_NSA, D_NSA, a.shape[-1]), (0, 4, 1, 2, 3))
    nsa_rows = rows_first(nsa_t, 4)
    win_rows = rows_first(win_t[:, :, t - keep:], 2)
    return o_gla, o_nsa, (nsa_rows, win_rows, gla_state)


def _mixer_sample(h, w, cache_l, win_l, gla_l, page_table):
    nb = h.shape[0]
    n_pages = page_table.shape[1]
    past_len = n_pages * cache_l.shape[1]
    pos = past_len
    tabs = _rope_tables(np.full((nb,), pos))
    (qk, v, r, la, _, _, qraw_t, qrot_t, gate_t, nsa_new_t, win_new_t) = _proj(
        h, w["m_pre"], w["wn"], w["wt"], w["wa2"], w["ba"], tabs, nb, 1)
    nsa = nsa_new_t[0].T
    win = win_new_t[0].T

    o_gla, gla_state = _gla_step(qk.T, la.T, v, r, gla_l.astype(F32), w["gla_gain"])

    cache_t = jnp.transpose(cache_l.reshape(cache_l.shape[0], ROWS_PER_PAGE, 4 * LANES), (0, 2, 1))
    kc, vct = _compress(cache_t, page_table, w, True)

    hg = (jnp.arange(H_NSA) // R_NSA)[None, :, None]

    def group_pad(q_t):
        q8 = q_t.T.reshape(nb, H_NSA, D_NSA)
        return jnp.concatenate([jnp.where(hg == 0, q8, 0), jnp.where(hg == 1, q8, 0)], axis=-1)

    q2_raw = group_pad(qraw_t)
    q2_rot = group_pad(qrot_t)
    q2_raw_t = jnp.pad(jnp.swapaxes(q2_raw, 1, 2), ((0, 0), (0, 0), (0, LANES - H_NSA)))
    n_blocks = -(-(past_len + 1) // SEL_BLOCK)
    idx_pad, ocmp_t = _nsa_decode_select(q2_raw_t, kc, vct, pos, n_blocks)
    idx = jnp.stack([idx_pad[:, :, g] for g in range(G_NSA)], axis=1)
    ocmp = jnp.swapaxes(ocmp_t, 1, 2)[:, :H_NSA, :]
    gates = jnp.pad(gate_t[:3 * H_NSA].T.reshape(nb, H_NSA, 3), ((0, 0), (0, 0), (0, LANES - 3)))
    wb = win_l.shape[1]
    win_buf_t = jnp.transpose(win_l.reshape(nb, wb, 2 * LANES), (0, 2, 1))
    o8 = _nsa_decode_attend(idx, page_table, cache_t, q2_rot, win_buf_t,
                            nsa.reshape(nb, 1, 4 * LANES), win.reshape(nb, 1, 2 * LANES), gates, ocmp,
                            past_len // SEL_BLOCK)
    o8 = o8.reshape(nb, H_NSA, G_NSA, D_NSA)
    o_nsa = jnp.concatenate([o8[:, :R_NSA, 0], o8[:, R_NSA:, 1]], axis=1).reshape(nb, H_NSA * D_NSA)

    nsa_rows = nsa.reshape(nb, 1, 4, G_NSA, D_NSA)
    win_new = win.reshape(nb, 1, 2, G_NSA, D_NSA)
    kw = jnp.concatenate([win_l, win_new.astype(win_l.dtype)], axis=1)
    keep = min(WINDOW, wb + 1)
    return o_gla, o_nsa, (nsa_rows, kw[:, wb + 1 - keep:], gla_state.astype(gla_l.dtype))


def kernel(x_prompt, x_sample, cache_nsa, state_win, state_gla, page_table, p_prompt, p_sample,
           ffn1_norm_pre, ffn1_norm_post, ffn1_w_in, ffn1_w_out,
           mix_norm_pre, mix_norm_post, w_mix_in, w_gla_a2, b_gla_a, gla_out_norm,
           cmp_pos_k, w_cmp_k1, w_cmp_k2, cmp_pos_v, w_cmp_v1, w_cmp_v2, w_mix_out,
           ffn2_norm_pre, ffn2_norm_post, ffn2_w_in, ffn2_w_out,
           ple_norm_pre, ple_w_gate, ple_w_proj, ple_norm_post):
    nb, t, _ = x_prompt.shape
    ns = x_sample.shape[0]
    depth = ffn1_w_in.shape[0]
    hp = x_prompt.reshape(nb * t, D_MODEL)
    hs = x_sample.reshape(ns, D_MODEL)
    outs = [[] for _ in range(6)]
    for i in range(depth):
        wn, wt, wa2, ba = _prep_mixer_weights(w_mix_in[i], w_gla_a2[i], b_gla_a[i])
        cmp_wk, cmp_posk, cmp_w1k, cmp_w2k, _ = _cmp_weights(cmp_pos_k[i], w_cmp_k1[i], w_cmp_k2[i])
        cmp_wv, cmp_posv, cmp_w1v, _, cmp_w2vt = _cmp_weights(cmp_pos_v[i], w_cmp_v1[i], w_cmp_v2[i])
        gla_w = H_GLA * DV_GLA
        w = dict(
            f1_pre=ffn1_norm_pre[i][None], f1_post=ffn1_norm_post[i][None],
            f1_in=ffn1_w_in[i].astype(BF16), f1_out=ffn1_w_out[i].astype(BF16),
            m_pre=mix_norm_pre[i][None], m_post=mix_norm_post[i][None],
            wn=wn, wt=wt, wa2=wa2, ba=ba, gla_gain=gla_out_norm[i][None],
            cmp_wk=cmp_wk, cmp_posk=cmp_posk, cmp_w1k=cmp_w1k, cmp_w2k=cmp_w2k,
            cmp_wv=cmp_wv, cmp_posv=cmp_posv, cmp_w1v=cmp_w1v, cmp_w2vt=cmp_w2vt,
            wo_gla=w_mix_out[i][:gla_w].astype(BF16), wo_nsa=w_mix_out[i][gla_w:].astype(BF16),
            f2_pre=ffn2_norm_pre[i][None], f2_post=ffn2_norm_post[i][None],
            f2_in=ffn2_w_in[i].astype(BF16), f2_out=ffn2_w_out[i].astype(BF16),
            ple_pre=ple_norm_pre[i][None], ple_post=ple_norm_post[i][None],
            ple_gate=ple_w_gate[i].astype(BF16), ple_proj=ple_w_proj[i].astype(BF16),
        )
        hp, (r_p, w_p, s_p) = _layer(hp, p_prompt[i].reshape(nb * t, PLE_DIM),
                                     functools.partial(_mixer_prompt, w=w, nb=nb, t=t), w)
        hs, (r_s, w_s, s_s) = _layer(hs, p_sample[i].reshape(ns, PLE_DIM),
                                     functools.partial(_mixer_sample, w=w, cache_l=cache_nsa[i], win_l=state_win[i],
                                                       gla_l=state_gla[i], page_table=page_table), w)
        for lst, val in zip(outs, (r_p, w_p, s_p, r_s, w_s, s_s)):
            lst.append(val)
    return (hp.reshape(nb, t, D_MODEL), hs.reshape(ns, 1, D_MODEL), *[jnp.stack(o) for o in outs])
```
